```python
import math
import jax, jax.numpy as jnp
from jax import lax
import numpy as np

D_MODEL = 1024
BATCH = 4
SEQ = 4096
DEPTH = 2

CHUNK = 64
Q_BLOCK = 128
A_HEAD_DIM = 64
A_WIDTH = D_MODEL // 2
A_HEADS = A_WIDTH // A_HEAD_DIM
IDX_HEADS = 8
IDX_DIM = 64
TOPK_MAX = 256
N_BUCKETS = 32
MAX_DISTANCE = 128
B_HEAD_DIM = 64
B_WIDTH = D_MODEL - A_WIDTH
B_HEADS = B_WIDTH // B_HEAD_DIM
DECAY_LORA = 64
ICLR_LORA = 64
GATE_LORA = 128
A_SIZES = (A_WIDTH, A_WIDTH, A_WIDTH, IDX_HEADS * IDX_DIM, IDX_DIM, IDX_HEADS)
B_SIZES = (B_WIDTH, B_WIDTH, B_WIDTH, DECAY_LORA, ICLR_LORA, GATE_LORA)
A_COLS = sum(A_SIZES)
B_COLS = sum(B_SIZES)
IN_COLS = A_COLS + B_COLS
MIX_WIDTH = A_WIDTH + B_WIDTH
N_EXPERTS = 16
N_GROUPS = 4
EXPERTS_PER_GROUP = N_EXPERTS // N_GROUPS
TOP_K = 2
D_EXPERT = 256
PLE_DIM = 256
ALPHA = (2 * DEPTH) ** 0.25
BETA = (8 * DEPTH) ** -0.25
LN_EPS = 1e-5
GN_EPS = 64e-5
NEG = -1e30

kernel_name = 'hybrid_dsa_rwkv7_grouped_moe_deepnorm'


def split_cols(a, sizes):
    cuts = [int(c) for c in np.cumsum(sizes)[:-1]]
    return jnp.split(a, cuts, axis=-1)


def layer_norm(x, g, b):
    xf = x.astype(jnp.float32)
    mu = xf.mean(-1, keepdims=True)
    var = jnp.square(xf - mu).mean(-1, keepdims=True)
    return ((xf - mu) * lax.rsqrt(var + LN_EPS) * g + b).astype(x.dtype)


def t5_bucket(rel):
    nb = N_BUCKETS // 2
    max_exact = nb // 2
    ret = jnp.where(rel > 0, nb, 0)
    n = jnp.abs(rel)
    nf = jnp.maximum(n, 1).astype(jnp.float32)
    large = max_exact + (jnp.log(nf / max_exact) / math.log(MAX_DISTANCE / max_exact)
                         * (nb - max_exact)).astype(jnp.int32)
    large = jnp.minimum(large, nb - 1)
    return ret + jnp.where(n < max_exact, n, large)


def dsa_attention(q, k, v, qi, ki, wi, rel_bias):
    bsz, seq = q.shape[0], q.shape[1]
    n_sel = min(TOPK_MAX, seq // 4)
    n_blk = seq // Q_BLOCK
    key_chunk = jnp.arange(seq, dtype=jnp.int32) // CHUNK
    ki32 = ki.astype(jnp.float32)
    idx_scale = (IDX_HEADS ** -0.5) * (IDX_DIM ** -0.5)
    att_scale = A_HEAD_DIM ** -0.5
    gather = jax.vmap(lambda a, i: a[i])

    def to_blocks(a):
        a = a.reshape((bsz, n_blk, Q_BLOCK) + a.shape[2:])
        return jnp.moveaxis(a, 1, 0)

    def one_block(args):
        qb, qib, wib, start = args
        t = start + jnp.arange(Q_BLOCK, dtype=jnp.int32)
        q_chunk = t // CHUNK
        dots = jnp.einsum('bqhd,bsd->bqhs', qib.astype(jnp.float32), ki32)
        score = jnp.einsum('bqh,bqhs->bqs', wib.astype(jnp.float32) * idx_scale, jax.nn.relu(dots))
        admissible = key_chunk[None, :] <= q_chunk[:, None]
        score = jnp.where(admissible[None], score, NEG)
        _, sel = lax.top_k(score, n_sel)
        kg = gather(k, sel)
        vg = gather(v, sel)
        logits = jnp.einsum('bqhd,bqkhd->bqkh', qb, kg).astype(jnp.float32) * att_scale
        logits = logits + rel_bias[t5_bucket(sel - t[None, :, None])].astype(jnp.float32)
        valid = (sel // CHUNK) <= q_chunk[None, :, None]
        logits = jnp.where(valid[..., None], logits, NEG)
        probs = jax.nn.softmax(logits, axis=2).astype(vg.dtype)
        return jnp.einsum('bqkh,bqkhd->bqhd', probs, vg)

    starts = jnp.arange(n_blk, dtype=jnp.int32) * Q_BLOCK
    out = lax.map(one_block, (to_blocks(q), to_blocks(qi), to_blocks(wi), starts))
    return jnp.moveaxis(out, 0, 1).reshape(bsz, seq, A_WIDTH)


def rwkv7_time_mix(h, mu, w0, w_decay, a0, w_iclr, w_gate_up, k_k, k_a, r_k, gn_g, gn_b):
    bsz, seq = h.shape[0], h.shape[1]
    f32 = jnp.float32
    h_prev = jnp.pad(h, ((0, 0), (1, 0), (0, 0)))[:, :-1]
    hs = h + (h_prev - h) * mu
    r, k, v, wd, ad, gd = split_cols(hs, B_SIZES)
    w_log = -jax.nn.softplus(-(w0 + jnp.tanh(wd) @ w_decay)) - 0.5
    decay = jnp.exp(-jnp.exp(w_log.astype(f32)))
    a = jax.nn.sigmoid(a0 + ad @ w_iclr)
    g = jax.nn.sigmoid(gd) @ w_gate_up
    heads = lambda t: t.astype(f32).reshape(bsz, seq, B_HEADS, B_HEAD_DIM)
    kk = heads(k * k_k)
    kk = kk / jnp.maximum(jnp.sqrt(jnp.sum(kk * kk, -1, keepdims=True)), 1e-12)
    k = k * (1 + (a - 1) * k_a)
    rh, kh, vh, wh, ah = heads(r), heads(k), heads(v), heads(decay), heads(a)

    def step(state, inp):
        r_t, k_t, v_t, w_t, kk_t, a_t = inp
        sa = jnp.einsum('bhvk,bhk->bhv', state, -kk_t)
        state = (state * w_t[:, :, None, :] + sa[..., None] * (kk_t * a_t)[:, :, None, :]
                 + v_t[..., None] * k_t[:, :, None, :])
        return state, jnp.einsum('bhvk,bhk->bhv', state, r_t)

    tm = lambda t: jnp.moveaxis(t, 1, 0)
    s0 = jnp.zeros((bsz, B_HEADS, B_HEAD_DIM, B_HEAD_DIM), f32)
    _, y = lax.scan(step, s0, (tm(rh), tm(kh), tm(vh), tm(wh), tm(kk), tm(ah)))
    y = jnp.moveaxis(y, 0, 1)
    m = y.mean(-1, keepdims=True)
    var = jnp.square(y - m).mean(-1, keepdims=True)
    yn = ((y - m) * lax.rsqrt(var + GN_EPS)).reshape(bsz, seq, B_WIDTH) * gn_g + gn_b
    bonus = (jnp.sum(rh * kh * r_k.astype(f32), -1, keepdims=True) * vh).reshape(bsz, seq, B_WIDTH)
    return ((yn + bonus) * g).astype(h.dtype)


def grouped_moe(h, w_router, b_router, w_gate, w_up, w_down):
    f32 = jnp.float32
    scores = jax.nn.sigmoid(h.astype(f32) @ w_router.astype(f32))
    sel = scores + b_router.astype(f32)
    grouped = sel.reshape(sel.shape[:-1] + (N_GROUPS, EXPERTS_PER_GROUP))
    group_score = lax.top_k(grouped, TOP_K)[0].sum(-1)
    best = jnp.argmax(group_score, axis=-1)
    in_group = (jnp.arange(N_EXPERTS) // EXPERTS_PER_GROUP) == best[..., None]
    _, idx = lax.top_k(jnp.where(in_group, sel, NEG), TOP_K)
    w_sel = jnp.take_along_axis(scores, idx, axis=-1)
    w_sel = w_sel / jnp.sum(w_sel, -1, keepdims=True)
    gates = jnp.sum(jax.nn.one_hot(idx, N_EXPERTS, dtype=f32) * w_sel[..., None], axis=-2)
    hidden = (jax.nn.silu(jnp.einsum('bsd,edf->bsef', h, w_gate))
              * jnp.einsum('bsd,edf->bsef', h, w_up))
    hidden = hidden * gates[..., None].astype(hidden.dtype)
    return jnp.einsum('bsef,efd->bsd', hidden, w_down)


def setup_inputs(seed: int = 0) -> dict:
    key = jax.random.key(seed)
    ks = iter(jax.random.split(key, 40))
    nrm = lambda shape, s: jax.random.normal(next(ks), shape, jnp.float32) * s
    gain = lambda shape: 1.0 + nrm(shape, 0.02)
    L = DEPTH
    return {
        'x': nrm((BATCH, SEQ, D_MODEL), 1.0),
        'p': nrm((DEPTH, BATCH, SEQ, PLE_DIM), 1.0),
        'ln_in_g': gain((D_MODEL,)),
        'ln_in_b': nrm((D_MODEL,), 0.02),
        'w_in': nrm((L, D_MODEL, IN_COLS), D_MODEL ** -0.5),
        'w_out': nrm((L, MIX_WIDTH, D_MODEL), MIX_WIDTH ** -0.5 * BETA),
        'mu_shift': jax.random.uniform(next(ks), (L, B_COLS), jnp.float32),
        'w0': nrm((L, B_WIDTH), 1.0) - 2.0,
        'w_decay': nrm((L, DECAY_LORA, B_WIDTH), 0.1),
        'a0': nrm((L, B_WIDTH), 0.1),
        'w_iclr': nrm((L, ICLR_LORA, B_WIDTH), 0.1),
        'w_gate_up': nrm((L, GATE_LORA, B_WIDTH), GATE_LORA ** -0.5),
        'k_k': 0.85 + nrm((L, B_WIDTH), 0.02),
        'k_a': gain((L, B_WIDTH)),
        'r_k': nrm((L, B_HEADS, B_HEAD_DIM), 0.1),
        'gn_g': gain((L, B_WIDTH)),
        'gn_b': nrm((L, B_WIDTH), 0.02),
        'rel_bias': nrm((N_BUCKETS, A_HEADS), 0.5),
        'w_router': nrm((D_MODEL, N_EXPERTS), D_MODEL ** -0.5),
        'b_router': nrm((N_EXPERTS,), 0.01),
        'w_exp_gate': nrm((L, N_EXPERTS, D_MODEL, D_EXPERT), D_MODEL ** -0.5),
        'w_exp_up': nrm((L, N_EXPERTS, D_MODEL, D_EXPERT), D_MODEL ** -0.5),
        'w_exp_down': nrm((L, N_EXPERTS, D_EXPERT, D_MODEL), D_EXPERT ** -0.5 * BETA),
        'w_ple': nrm((L, PLE_DIM, D_MODEL), PLE_DIM ** -0.5 * BETA),
        'w_ple_gate': nrm((L, D_MODEL, D_MODEL), D_MODEL ** -0.5),
        'ln_mix_g': gain((L, D_MODEL)),
        'ln_mix_b': nrm((L, D_MODEL), 0.02),
        'ln_ffn_g': gain((L, D_MODEL)),
        'ln_ffn_b': nrm((L, D_MODEL), 0.02),
        'ln_ple_g': gain((L, D_MODEL)),
        'ln_ple_b': nrm((L, D_MODEL), 0.02),
    }


def reference(x, p, ln_in_g, ln_in_b, w_in, w_out, mu_shift, w0, w_decay, a0, w_iclr,
              w_gate_up, k_k, k_a, r_k, gn_g, gn_b, rel_bias, w_router, b_router,
              w_exp_gate, w_exp_up, w_exp_down, w_ple, w_ple_gate, ln_mix_g, ln_mix_b,
              ln_ffn_g, ln_ffn_b, ln_ple_g, ln_ple_b):
    bsz, seq = x.shape[0], x.shape[1]
    x = layer_norm(x, ln_in_g, ln_in_b)
    for i in range(DEPTH):
        proj = x @ w_in[i]
        h_a, h_b = proj[..., :A_COLS], proj[..., A_COLS:]
        q, k, v, qi, ki, wi = split_cols(h_a, A_SIZES)
        hsplit = lambda t: t.reshape(bsz, seq, A_HEADS, A_HEAD_DIM)
        a_out = dsa_attention(hsplit(q), hsplit(k), hsplit(v),
                              qi.reshape(bsz, seq, IDX_HEADS, IDX_DIM), ki, wi, rel_bias)
        b_out = rwkv7_time_mix(h_b, mu_shift[i], w0[i], w_decay[i], a0[i], w_iclr[i],
                               w_gate_up[i], k_k[i], k_a[i], r_k[i], gn_g[i], gn_b[i])
        mix = jnp.concatenate([a_out, b_out.astype(a_out.dtype)], axis=-1) @ w_out[i]
        x = layer_norm(ALPHA * x + mix, ln_mix_g[i], ln_mix_b[i])
        ffn = grouped_moe(x, w_router, b_router, w_exp_gate[i], w_exp_up[i], w_exp_down[i])
        x = layer_norm(ALPHA * x + ffn, ln_ffn_g[i], ln_ffn_b[i])
        ple = jax.nn.sigmoid(x @ w_ple_gate[i]) * (p[i] @ w_ple[i])
        x = layer_norm(ALPHA * x + ple, ln_ple_g[i], ln_ple_b[i])
    return x
```

```python
import functools
import math

import numpy as np
import jax
import jax.numpy as jnp
from jax import lax
from jax.experimental import pallas as pl
from jax.experimental.pallas import tpu as pltpu

D_MODEL = 1024
DEPTH = 2
CHUNK = 64
A_HEAD_DIM = 64
A_WIDTH = D_MODEL // 2
A_HEADS = A_WIDTH // A_HEAD_DIM
IDX_HEADS = 8
IDX_DIM = 64
TOPK_MAX = 256
N_BUCKETS = 32
MAX_DISTANCE = 128
B_HEAD_DIM = 64
B_WIDTH = D_MODEL - A_WIDTH
B_HEADS = B_WIDTH // B_HEAD_DIM
DECAY_LORA = 64
ICLR_LORA = 64
GATE_LORA = 128
A_SIZES = (A_WIDTH, A_WIDTH, A_WIDTH, IDX_HEADS * IDX_DIM, IDX_DIM, IDX_HEADS)
B_SIZES = (B_WIDTH, B_WIDTH, B_WIDTH, DECAY_LORA, ICLR_LORA, GATE_LORA)
A_COLS = sum(A_SIZES)
B_COLS = sum(B_SIZES)
N_EXPERTS = 16
N_GROUPS = 4
EXPERTS_PER_GROUP = N_EXPERTS // N_GROUPS
D_EXPERT = 256
PLE_DIM = 256
ALPHA = (2 * DEPTH) ** 0.25
LN_EPS = 1e-5
GN_EPS = 64e-5
NEG = -1e30

LANES = 128
VMEM_LIMIT = 56 * 1024 * 1024
F32 = jnp.float32
BF16 = jnp.bfloat16
HI = lax.Precision.HIGHEST
NT_DIMS = (((1,), (1,)), ((), ()))


def _cparams(sem):
    return pltpu.CompilerParams(dimension_semantics=sem, vmem_limit_bytes=VMEM_LIMIT)


def _ln(x, g, b):
    mu = jnp.mean(x, axis=-1, keepdims=True)
    xc = x - mu
    var = jnp.mean(xc * xc, axis=-1, keepdims=True)
    return xc * lax.rsqrt(var + LN_EPS) * g + b


def _dot(a, b):
    return jnp.dot(a, b, preferred_element_type=F32)


def _dot_hi(a, b):
    return jnp.dot(a, b, preferred_element_type=F32, precision=HI)


def _dot_nt(a, b):
    return lax.dot_general(a, b, NT_DIMS, preferred_element_type=F32)


def _dot_nt_hi(a, b):
    return lax.dot_general(a, b, NT_DIMS, preferred_element_type=F32, precision=HI)


IN_NCHUNK = 512


def _inproj_kernel(x_ref, g_ref, b_ref, w_ref, *out_refs, apply_ln):
    x = x_ref[...]
    if apply_ln:
        x = _ln(x, g_ref[...], b_ref[...])
        out_refs[4][...] = x
    xb = x.astype(BF16)
    qkv_ref, kk_ref, wi_ref, hb_ref = out_refs[:4]
    for c in range(4):
        sl = slice(c * IN_NCHUNK, (c + 1) * IN_NCHUNK)
        qkv_ref[:, sl] = _dot(xb, w_ref[:, sl]).astype(BF16)
    aux = _dot(xb, w_ref[:, 2048:2304])
    kk_ref[...] = aux[:, :LANES].astype(BF16)
    wi_ref[...] = aux[:, LANES:]
    nb = hb_ref.shape[1]
    for c0 in range(0, nb, IN_NCHUNK):
        c1 = min(c0 + IN_NCHUNK, nb)
        hb_ref[:, c0:c1] = _dot(xb, w_ref[:, 2304 + c0:2304 + c1])


def _inproj(x, g, b, w_all, apply_ln, tm=512):
    t, d = x.shape
    nc = w_all.shape[1]
    nb = nc - 2304
    out_shape = [jax.ShapeDtypeStruct((t, 2048), BF16), jax.ShapeDtypeStruct((t, LANES), BF16),
                 jax.ShapeDtypeStruct((t, LANES), F32), jax.ShapeDtypeStruct((t, nb), F32)]
    out_specs = [pl.BlockSpec((tm, 2048), lambda i: (i, 0)), pl.BlockSpec((tm, LANES), lambda i: (i, 0)),
                 pl.BlockSpec((tm, LANES), lambda i: (i, 0)), pl.BlockSpec((tm, nb), lambda i: (i, 0))]
    if apply_ln:
        out_shape.append(jax.ShapeDtypeStruct((t, d), F32))
        out_specs.append(pl.BlockSpec((tm, d), lambda i: (i, 0)))
    return pl.pallas_call(
        functools.partial(_inproj_kernel, apply_ln=apply_ln),
        grid=(t // tm,),
        in_specs=[pl.BlockSpec((tm, d), lambda i: (i, 0)), pl.BlockSpec((1, d), lambda i: (0, 0)),
                  pl.BlockSpec((1, d), lambda i: (0, 0)), pl.BlockSpec((d, nc), lambda i: (0, 0))],
        out_specs=out_specs, out_shape=out_shape,
        compiler_params=_cparams(("arbitrary",)), name="inproj_ln" if apply_ln else "inproj",
    )(x, g, b, w_all)


def _pack_w_in(w):
    d = w.shape[0]
    cuts = np.cumsum(A_SIZES)
    wq = w[:, :cuts[3]]
    wki = w[:, cuts[3]:cuts[4]]
    wwi = w[:, cuts[4]:cuts[5]]
    wb = w[:, A_COLS:]
    pad = jnp.zeros((d, LANES - IDX_HEADS), w.dtype)
    return jnp.concatenate([wq, wki, wki, wwi, pad, wb], axis=1).astype(BF16)


TQ = 256
ROW_GROUP = 32
BIGF = 3.0e38


def _t5_bucket_np(rel):
    nb = N_BUCKETS // 2
    max_exact = nb // 2
    ret = np.where(rel > 0, nb, 0)
    n = np.abs(rel)
    nf = np.maximum(n, 1).astype(np.float64)
    large = max_exact + np.floor(np.log(nf / max_exact) / math.log(MAX_DISTANCE / max_exact)
                                 * (nb - max_exact) + 1e-9).astype(np.int64)
    large = np.minimum(large, nb - 1)
    return ret + np.where(n < max_exact, n, large)


def _near_bias(rel_bias):
    qpos = np.arange(TQ)[:, None]
    kpos = np.arange(TQ)[None, :]
    bk = np.stack([_t5_bucket_np(kpos - qpos), _t5_bucket_np(kpos - TQ - qpos)])
    far = N_BUCKETS // 2 - 1
    tab = rel_bias - rel_bias[far][None, :]
    return jnp.transpose(tab[jnp.asarray(bk)], (0, 3, 1, 2)).astype(F32)


def _dsa_kernel(q_ref, qi_ref, wi_ref, k_ref, v_ref, kk_ref, bias_ref, o_ref,
                s_ref, m_ref, l_ref, acc_ref, *, ksel):
    i = pl.program_id(1)
    nt = i + 1
    tq = q_ref.shape[0]
    idx_scale = (IDX_HEADS ** -0.5) * (IDX_DIM ** -0.5)
    att_scale = A_HEAD_DIM ** -0.5

    lane = lax.broadcasted_iota(jnp.int32, (tq, LANES), 1)
    half0 = lane < (LANES // 2)
    zero_b = jnp.zeros((tq, LANES), BF16)

    def head_parts(ref, scale):
        parts = []
        for h in range(A_HEADS):
            j, s = divmod(h, 2)
            blk = ref[:, j * LANES:(j + 1) * LANES]
            if scale != 1.0:
                blk = (blk.astype(F32) * scale).astype(BF16)
            parts.append(jnp.where(half0 if s == 0 else jnp.logical_not(half0), blk, zero_b))
        return parts

    qi_parts = head_parts(qi_ref, 1.0)
    wi = wi_ref[...] * idx_scale
    wcols = [wi[:, h:h + 1] for h in range(IDX_HEADS)]
    row = lax.broadcasted_iota(jnp.int32, (tq, tq), 0)
    col = lax.broadcasted_iota(jnp.int32, (tq, tq), 1)
    adm_diag = (col // CHUNK) <= (row // CHUNK)

    def score_tile(kt):
        kk = kk_ref[pl.ds(pl.multiple_of(kt * tq, tq), tq), :]
        acc = jnp.zeros((tq, tq), F32)
        for h in range(IDX_HEADS):
            acc = acc + wcols[h] * jnp.maximum(_dot_nt(qi_parts[h], kk), 0.0)
        return acc

    def p1_body(kt, c):
        s_ref[kt] = score_tile(kt)
        return c

    lax.fori_loop(0, i, p1_body, 0)
    s_ref[i] = jnp.where(adm_diag, score_tile(i), NEG)

    ngrp = tq // ROW_GROUP
    groups = range(ngrp)

    def fold_tiles(rg, fn, inits, reducers):
        rows = slice(rg * ROW_GROUP, (rg + 1) * ROW_GROUP)

        def body(kt, carry):
            return fn(s_ref[kt, rows, :], kt, carry)

        carry = lax.fori_loop(0, nt, body, tuple(jnp.full((ROW_GROUP, tq), v, F32) for v in inits))
        return [red(cv, axis=1, keepdims=True) for red, cv in zip(reducers, carry)]

    def minmax(s, kt, carry):
        mn, mx = carry
        return jnp.minimum(mn, jnp.where(s > 0.5 * NEG, s, BIGF)), jnp.maximum(mx, s)

    kself = float(ksel)
    qrow = lax.broadcasted_iota(jnp.int32, (ROW_GROUP, 1), 0)
    lo0, hi0, clo0, chi0 = [], [], [], []
    for rg in groups:
        rowmin, rowmax = fold_tiles(rg, minmax, (BIGF, -BIGF), (jnp.min, jnp.max))
        n_adm = ((i * tq + rg * ROW_GROUP + qrow) // CHUNK + 1) * CHUNK
        trivial = n_adm <= ksel
        lo0.append(jnp.where(trivial, 0.1 * NEG, rowmin))
        hi0.append(jnp.where(trivial, 0.1 * NEG, rowmax))
        clo0.append(n_adm.astype(F32))
        chi0.append(jnp.zeros((ROW_GROUP, 1), F32))

    def any_open(lo, hi):
        flags = [jnp.max(jnp.where(lo[rg] < hi[rg], 1.0, 0.0)) for rg in groups]
        return functools.reduce(jnp.maximum, flags) > 0.0

    def bis_cond(st):
        return any_open(st[0], st[1])

    def bis_body(st):
        lo, hi, clo, chi = st
        out = ([], [], [], [])
        for rg in groups:
            mid = 0.5 * (lo[rg] + hi[rg])
            mid = jnp.where(mid > lo[rg], mid, hi[rg])

            def cnt(s, kt, carry, mid=mid):
                c, vge, vlt = carry
                ge = s >= mid
                return (c + jnp.where(ge, 1.0, 0.0), jnp.minimum(vge, jnp.where(ge, s, BIGF)),
                        jnp.maximum(vlt, jnp.where(ge, -BIGF, s)))

            c, vge, vlt = fold_tiles(rg, cnt, (0.0, BIGF, -BIGF), (jnp.sum, jnp.min, jnp.max))
            ok = c >= kself
            out[0].append(jnp.where(ok, vge, lo[rg]))
            out[1].append(jnp.where(ok, hi[rg], vlt))
            out[2].append(jnp.where(ok, c, clo[rg]))
            out[3].append(jnp.where(ok, chi[rg], c))
        return tuple(tuple(o) for o in out)

    thr_g, _, c_ge, c_gt = lax.while_loop(bis_cond, bis_body, (tuple(lo0), tuple(hi0), tuple(clo0), tuple(chi0)))

    s_total = float(s_ref.shape[0] * tq)
    kcol = lax.broadcasted_iota(jnp.int32, (tq, tq), 1).astype(F32)
    kcol_g = lax.broadcasted_iota(jnp.int32, (ROW_GROUP, tq), 1).astype(F32)
    need = [kself - c_gt[rg] for rg in groups]
    jlo0 = tuple(jnp.where(c_ge[rg] > kself, 0.0, s_total - 1.0) for rg in groups)
    jhi0 = tuple(jnp.where(c_ge[rg] > kself, (nt * tq).astype(F32), s_total) for rg in groups)

    def tie_cond(st):
        jlo, jhi = st
        return functools.reduce(jnp.maximum, [jnp.max(jhi[rg] - jlo[rg]) for rg in groups]) > 1.0

    def tie_body(st):
        jlo, jhi = st
        out = ([], [])
        for rg in groups:
            mid = jnp.floor(0.5 * (jlo[rg] + jhi[rg]))

            def cnt(s, kt, carry, mid=mid, thr=thr_g[rg]):
                idx = kcol_g + (kt * tq).astype(F32)
                return (carry[0] + jnp.where(jnp.logical_and(s == thr, idx < mid), 1.0, 0.0),)

            (c,) = fold_tiles(rg, cnt, (0.0,), (jnp.sum,))
            ok = c >= need[rg]
            done = (jhi[rg] - jlo[rg]) <= 1.0
            out[0].append(jnp.where(jnp.logical_or(ok, done), jlo[rg], mid))
            out[1].append(jnp.where(jnp.logical_and(ok, jnp.logical_not(done)), mid, jhi[rg]))
        return tuple(tuple(o) for o in out)

    _, jcut_g = lax.while_loop(tie_cond, tie_body, (jlo0, jhi0))
    thr = jnp.concatenate(list(thr_g), axis=0)
    jcut = jnp.concatenate(list(jcut_g), axis=0)

    q_parts = head_parts(q_ref, att_scale)
    m_ref[...] = jnp.full(m_ref.shape, NEG, F32)
    l_ref[...] = jnp.zeros(l_ref.shape, F32)
    acc_ref[...] = jnp.zeros(acc_ref.shape, F32)

    def attend(kt, near):
        s = s_ref[kt]
        idx = kcol + (kt * tq).astype(F32)
        sel = jnp.logical_or(s > thr, jnp.logical_and(s == thr, idx < jcut))
        mb = jnp.where(sel, 0.0, NEG)
        rows = pl.ds(pl.multiple_of(kt * tq, tq), tq)
        for j in range(A_HEADS // 2):
            kp = k_ref[rows, j * LANES:(j + 1) * LANES]
            vp = v_ref[rows, j * LANES:(j + 1) * LANES]
            new = []
            for s_half in range(2):
                h = 2 * j + s_half
                lg = _dot_nt(q_parts[h], kp) + mb
                if near is not None:
                    lg = lg + bias_ref[near, h]
                m_old = m_ref[h]
                m_new = jnp.maximum(m_old, jnp.max(lg, axis=1, keepdims=True))
                p = jnp.exp(lg - m_new[:, :1])
                alpha = jnp.exp(m_old - m_new)
                l_ref[h] = alpha * l_ref[h] + jnp.sum(p, axis=1, keepdims=True)
                m_ref[h] = m_new
                new.append(alpha * acc_ref[j] + _dot(p.astype(BF16), vp))
            acc_ref[j] = jnp.where(half0, new[0], new[1])

    def p3_body(kt, c):
        attend(kt, None)
        return c

    lax.fori_loop(0, jnp.maximum(i - 1, 0), p3_body, 0)

    @pl.when(i >= 1)
    def _():
        attend(i - 1, 1)

    attend(i, 0)

    for j in range(A_HEADS // 2):
        linv = jnp.where(half0, 1.0 / l_ref[2 * j], 1.0 / l_ref[2 * j + 1])
        o_ref[:, j * LANES:(j + 1) * LANES] = (acc_ref[j] * linv).astype(o_ref.dtype)


def _dsa(qkv, kiki, wi, near_bias, bsz, seq):
    nq = seq // TQ
    ksel = min(TOPK_MAX, seq // 4)
    return pl.pallas_call(
        functools.partial(_dsa_kernel, ksel=ksel),
        grid=(bsz, nq),
        in_specs=[
            pl.BlockSpec((TQ, A_WIDTH), lambda b, i: (b * nq + i, 0)),
            pl.BlockSpec((TQ, A_WIDTH), lambda b, i: (b * nq + i, 3)),
            pl.BlockSpec((TQ, LANES), lambda b, i: (b * nq + i, 0)),
            pl.BlockSpec((seq, A_WIDTH), lambda b, i: (b, 1)),
            pl.BlockSpec((seq, A_WIDTH), lambda b, i: (b, 2)),
            pl.BlockSpec((seq, LANES), lambda b, i: (b, 0)),
            pl.BlockSpec((2, A_HEADS, TQ, TQ), lambda b, i: (0, 0, 0, 0)),
        ],
        out_specs=pl.BlockSpec((TQ, A_WIDTH), lambda b, i: (b * nq + i, 0)),
        out_shape=jax.ShapeDtypeStruct((bsz * seq, A_WIDTH), BF16),
        scratch_shapes=[pltpu.VMEM((nq, TQ, TQ), F32), pltpu.VMEM((A_HEADS, TQ, LANES), F32),
                        pltpu.VMEM((A_HEADS, TQ, LANES), F32), pltpu.VMEM((A_HEADS // 2, TQ, LANES), F32)],
        compiler_params=_cparams(("arbitrary", "arbitrary")), name="dsa_attention",
    )(qkv, qkv, wi, qkv, qkv, kiki, near_bias)


RC = 64
RTS = 128


def _rwkv_kernel(r_ref, k_ref, v_ref, wa_ref, gd_ref, rp_ref, kp_ref, vp_ref, wap_ref, gdp_ref,
                 par_ref, muw_ref, mug_ref, wdec_ref, wic_ref, wg_ref, o_ref, z_ref):
    t = pl.program_id(2)
    ts = r_ref.shape[0]

    @pl.when(t == 0)
    def _():
        z_ref[...] = jnp.zeros(z_ref.shape, F32)

    row1 = lax.broadcasted_iota(jnp.int32, (ts, LANES), 0)
    first = jnp.where(t == 0, 0.0, 1.0)

    def shifted(ref, pref, mu):
        x = ref[...]
        prev = pltpu.roll(x, 1, 0)
        prev = jnp.where(row1 == 0, pref[7:8, :] * first, prev)
        return x + (prev - x) * mu

    par = par_ref[...]
    w0, a0, k_k, k_a, r_k, gn_g, gn_b = (par[n:n + 1] for n in range(7))
    r = shifted(r_ref, rp_ref, par[8:9])
    k = shifted(k_ref, kp_ref, par[9:10])
    v = shifted(v_ref, vp_ref, par[10:11])
    wa = shifted(wa_ref, wap_ref, muw_ref[...])
    gd = shifted(gd_ref, gdp_ref, mug_ref[...])

    lane = lax.broadcasted_iota(jnp.int32, (LANES, LANES), 1)
    rowl = lax.broadcasted_iota(jnp.int32, (LANES, LANES), 0)
    same_head = (lane // B_HEAD_DIM) == (rowl // B_HEAD_DIM)
    gones = jnp.where(same_head, 1.0, 0.0)
    gmean = gones * (1.0 / B_HEAD_DIM)

    z = w0 + _dot_hi(jnp.tanh(wa), wdec_ref[...])
    w_log = -(jnp.maximum(-z, 0.0) + jnp.log1p(jnp.exp(-jnp.abs(z)))) - 0.5
    e = jnp.exp(w_log)
    a = jax.nn.sigmoid(a0 + _dot_hi(wa, wic_ref[...]))
    g = _dot_hi(jax.nn.sigmoid(gd), wg_ref[...])
    kkr = k * k_k
    kk = kkr / jnp.maximum(jnp.sqrt(_dot_hi(kkr * kkr, gones)), 1e-12)
    k2 = k * (1.0 + (a - 1.0) * k_a)
    bonus = _dot_hi(r * k2 * r_k, gones) * v
    kka = kk * a

    cr = lax.broadcasted_iota(jnp.int32, (RC, RC), 0)
    cc = lax.broadcasted_iota(jnp.int32, (RC, RC), 1)
    tri_incl = jnp.where(cr >= cc, 1.0, 0.0)
    strict = cr > cc
    incl = cr >= cc
    eye = jnp.where(cr == cc, 1.0, 0.0)
    hl = lax.broadcasted_iota(jnp.int32, (RC, LANES), 1)
    hmask = [hl < B_HEAD_DIM, hl >= B_HEAD_DIM]

    for c in range(ts // RC):
        sl = slice(c * RC, (c + 1) * RC)
        e_c, r_c, v_c, kk_c, kka_c, k2_c = e[sl], r[sl], v[sl], kk[sl], kka[sl], k2[sl]
        cum = _dot_hi(tri_incl, e_c)
        tot = cum[RC - 1:RC, :]
        dec_in = jnp.exp(-cum)
        gro_in = jnp.exp(cum)
        a_hat = -kk_c * jnp.exp(e_c - cum)
        r_hat = r_c * dec_in
        b_til = kka_c * gro_in
        k_til = k2_c * gro_in
        rem = jnp.exp(cum - tot)
        b_bar = kka_c * rem
        k_bar = k2_c * rem
        w_c = jnp.exp(-tot)

        a_t = jnp.zeros((RC, LANES), F32)
        v_t = jnp.zeros((RC, LANES), F32)
        r_p = r_hat
        y0 = jnp.zeros((RC, LANES), F32)
        for s in range(2):
            a_s = jnp.where(hmask[s], a_hat, 0.0)
            r_s = jnp.where(hmask[s], r_hat, 0.0)
            v_s = jnp.where(hmask[s], v_c, 0.0)
            l_ab = jnp.where(strict, _dot_nt_hi(a_s, b_til), 0.0)
            l_ak = jnp.where(strict, _dot_nt_hi(a_s, k_til), 0.0)
            m_rb = jnp.where(incl, _dot_nt_hi(r_s, b_til), 0.0)
            m_rk = jnp.where(incl, _dot_nt_hi(r_s, k_til), 0.0)
            inv = eye + l_ab
            pw = l_ab
            for _ in range(5):
                pw = _dot_hi(pw, pw)
                inv = inv + _dot_hi(inv, pw)
            at_s = _dot_hi(inv, a_s)
            vt_s = _dot_hi(inv, _dot_hi(l_ak, v_s))
            a_t = a_t + at_s
            v_t = v_t + vt_s
            r_p = r_p + _dot_hi(m_rb, at_s)
            y0 = y0 + _dot_hi(m_rk, v_s) + _dot_hi(m_rb, vt_s)
        b_bar_t = b_bar.T
        k_bar_t = k_bar.T
        diag_w = jnp.where(lane == rowl, 1.0, 0.0) * w_c
        g_mat = diag_w + gones * _dot_hi(b_bar_t, a_t)
        h_mat = gones * (_dot_hi(b_bar_t, v_t) + _dot_hi(k_bar_t, v_c))

        zc = z_ref[...]
        y = _dot_hi(r_p, zc) + y0
        z_ref[...] = _dot_hi(g_mat, zc) + h_mat

        mean = _dot_hi(y, gmean)
        yc = y - mean
        var = _dot_hi(yc * yc, gmean)
        yn = yc * lax.rsqrt(var + GN_EPS) * gn_g + gn_b
        o_ref[sl, :] = ((yn + bonus[sl]) * g[sl]).astype(o_ref.dtype)


def _rwkv(hb, par, mu_wa, mu_gd, wdec, wic, wg, bsz, seq):
    nt = seq // RTS
    npair = B_WIDTH // LANES
    cb = B_WIDTH // LANES

    def main(colblk):
        return pl.BlockSpec((RTS, LANES), lambda b, p, t: (b * nt + t, colblk(p)))

    def prev(colblk):
        return pl.BlockSpec((8, LANES), lambda b, p, t: (jnp.maximum((b * nt + t) * (RTS // 8) - 1, 0), colblk(p)))

    cols = [lambda p: p, lambda p: cb + p, lambda p: 2 * cb + p, lambda p: 3 * cb, lambda p: 3 * cb + 1]
    in_specs = [main(c) for c in cols] + [prev(c) for c in cols] + [
        pl.BlockSpec((16, LANES), lambda b, p, t: (0, p)),
        pl.BlockSpec((1, LANES), lambda b, p, t: (0, 0)),
        pl.BlockSpec((1, LANES), lambda b, p, t: (0, 0)),
        pl.BlockSpec((LANES, LANES), lambda b, p, t: (0, p)),
        pl.BlockSpec((LANES, LANES), lambda b, p, t: (0, p)),
        pl.BlockSpec((LANES, LANES), lambda b, p, t: (0, p)),
    ]
    return pl.pallas_call(
        _rwkv_kernel,
        grid=(bsz, npair, nt),
        in_specs=in_specs,
        out_specs=pl.BlockSpec((RTS, LANES), lambda b, p, t: (b * nt + t, p)),
        out_shape=jax.ShapeDtypeStruct((bsz * seq, B_WIDTH), BF16),
        scratch_shapes=[pltpu.VMEM((LANES, LANES), F32)],
        compiler_params=_cparams(("arbitrary", "arbitrary", "arbitrary")), name="rwkv7",
    )(*([hb] * 10), par, mu_wa, mu_gd, wdec, wic, wg)


def _pack_rwkv_params(mu, w0, w_decay, a0, w_iclr, k_k, k_a, r_k, gn_g, gn_b):
    bw = B_WIDTH
    zero = jnp.zeros((bw,), F32)
    rows = [w0, a0, k_k, k_a, r_k.reshape(bw), gn_g, gn_b, zero,
            mu[:bw], mu[bw:2 * bw], mu[2 * bw:3 * bw], zero, zero, zero, zero, zero]
    par = jnp.stack(rows)
    mu_wa = mu[3 * bw:3 * bw + LANES][None]
    mu_gd = mu[3 * bw + LANES:][None]
    zpad = jnp.zeros((DECAY_LORA, bw), F32)
    wdec = jnp.concatenate([w_decay, zpad], axis=0)
    wic = jnp.concatenate([zpad, w_iclr], axis=0)
    return par, mu_wa, mu_gd, wdec, wic


def _outproj_kernel(x_ref, a_ref, b_ref, wa_ref, wb_ref, g_ref, be_ref, o_ref):
    mix = _dot(a_ref[...], wa_ref[...]) + _dot(b_ref[...], wb_ref[...])
    o_ref[...] = _ln(ALPHA * x_ref[...] + mix, g_ref[...], be_ref[...])


def _outproj(x, a_out, b_out, w_out, g, b, tm=512):
    t, d = x.shape
    wa = w_out[:A_WIDTH].astype(BF16)
    wb = w_out[A_WIDTH:].astype(BF16)
    row = lambda i: (i, 0)
    const = lambda i: (0, 0)
    return pl.pallas_call(
        _outproj_kernel, grid=(t // tm,),
        in_specs=[pl.BlockSpec((tm, d), row), pl.BlockSpec((tm, A_WIDTH), row), pl.BlockSpec((tm, B_WIDTH), row),
                  pl.BlockSpec((A_WIDTH, d), const), pl.BlockSpec((B_WIDTH, d), const),
                  pl.BlockSpec((1, d), const), pl.BlockSpec((1, d), const)],
        out_specs=pl.BlockSpec((tm, d), row), out_shape=jax.ShapeDtypeStruct((t, d), F32),
        compiler_params=_cparams(("arbitrary",)), name="outproj_ln",
    )(x, a_out, b_out, wa, wb, g, b)


def _xor_partner(x, lane, bit):
    up = pltpu.roll(x, bit, 1)
    down = pltpu.roll(x, LANES - bit, 1)
    return jnp.where((lane & bit) != 0, up, down)


def _moe_gates(xf, wr, br):
    tm = xf.shape[0]
    lane = lax.broadcasted_iota(jnp.int32, (tm, LANES), 1)
    valid = lane < N_EXPERTS
    scores = jax.nn.sigmoid(_dot_hi(xf, wr))
    sel = jnp.where(valid, scores + br, NEG)
    p1 = _xor_partner(sel, lane, 1)
    hi1, lo1 = jnp.maximum(sel, p1), jnp.minimum(sel, p1)
    hi2, lo2 = _xor_partner(hi1, lane, 2), _xor_partner(lo1, lane, 2)
    gscore = jnp.maximum(hi1, hi2) + jnp.maximum(jnp.minimum(hi1, hi2), jnp.maximum(lo1, lo2))
    gscore = jnp.where(valid, gscore, NEG)
    lanef = lane.astype(F32)
    gbest = jnp.max(gscore, axis=1, keepdims=True)
    first = jnp.min(jnp.where(gscore == gbest, lanef, float(LANES)), axis=1, keepdims=True)
    in_group = jnp.floor(lanef * (1.0 / EXPERTS_PER_GROUP)) == jnp.floor(first * (1.0 / EXPERTS_PER_GROUP))
    masked = jnp.where(in_group, sel, NEG)
    m1 = jnp.max(masked, axis=1, keepdims=True)
    i1 = jnp.min(jnp.where(masked == m1, lanef, float(LANES)), axis=1, keepdims=True)
    pick1 = lanef == i1
    masked2 = jnp.where(pick1, NEG, masked)
    m2 = jnp.max(masked2, axis=1, keepdims=True)
    i2 = jnp.min(jnp.where(masked2 == m2, lanef, float(LANES)), axis=1, keepdims=True)
    pick2 = lanef == i2
    s1 = jnp.sum(jnp.where(pick1, scores, 0.0), axis=1, keepdims=True)
    s2 = jnp.sum(jnp.where(pick2, scores, 0.0), axis=1, keepdims=True)
    tot = s1 + s2
    return jnp.where(pick1, s1 / tot, 0.0) + jnp.where(pick2, s2 / tot, 0.0)


def _moe_kernel(x_ref, wr_ref, br_ref, wgu_ref, wd_ref, g_ref, b_ref, o_ref, gate_ref, acc_ref, xb_ref):
    e = pl.program_id(1)

    @pl.when(e == 0)
    def _():
        xf = x_ref[...]
        gate_ref[...] = _moe_gates(xf, wr_ref[...], br_ref[...])
        xb_ref[...] = xf.astype(BF16)
        acc_ref[...] = jnp.zeros(acc_ref.shape, F32)

    gates = gate_ref[...]
    lane = lax.broadcasted_iota(jnp.int32, gates.shape, 1)
    gcol = jnp.sum(jnp.where(lane == e, gates, 0.0), axis=1, keepdims=True)
    hu = _dot(xb_ref[...], wgu_ref[0])
    hg, up = hu[:, :D_EXPERT], hu[:, D_EXPERT:]
    hid = (hg * jax.nn.sigmoid(hg)) * up * gcol
    acc_ref[...] += _dot(hid.astype(BF16), wd_ref[0])

    @pl.when(e == pl.num_programs(1) - 1)
    def _():
        o_ref[...] = _ln(ALPHA * x_ref[...] + acc_ref[...], g_ref[...], b_ref[...])


def _moe(x, w_router, b_router, w_gate, w_up, w_down, g, b, tm=1024):
    t, d = x.shape
    ne = w_gate.shape[0]
    wgu = jnp.concatenate([w_gate, w_up], axis=2).astype(BF16)
    wd = w_down.astype(BF16)
    wr = jnp.pad(w_router, ((0, 0), (0, LANES - ne)))
    br = jnp.pad(b_router, (0, LANES - ne))[None]
    row = lambda i, e: (i, 0)
    const = lambda i, e: (0, 0)
    return pl.pallas_call(
        _moe_kernel, grid=(t // tm, ne),
        in_specs=[pl.BlockSpec((tm, d), row), pl.BlockSpec((d, LANES), const), pl.BlockSpec((1, LANES), const),
                  pl.BlockSpec((1, d, 2 * D_EXPERT), lambda i, e: (e, 0, 0)),
                  pl.BlockSpec((1, D_EXPERT, d), lambda i, e: (e, 0, 0)),
                  pl.BlockSpec((1, d), const), pl.BlockSpec((1, d), const)],
        out_specs=pl.BlockSpec((tm, d), row), out_shape=jax.ShapeDtypeStruct((t, d), F32),
        scratch_shapes=[pltpu.VMEM((tm, LANES), F32), pltpu.VMEM((tm, d), F32), pltpu.VMEM((tm, d), BF16)],
        compiler_params=_cparams(("arbitrary", "arbitrary")), name="moe_ln",
    )(x, wr, br, wgu, wd, g, b)


def _ple_kernel(x_ref, p_ref, wg_ref, wp_ref, g_ref, b_ref, o_ref):
    x = x_ref[...]
    gate = jax.nn.sigmoid(_dot(x.astype(BF16), wg_ref[...]))
    pe = _dot(p_ref[...].astype(BF16), wp_ref[...])
    o_ref[...] = _ln(ALPHA * x + gate * pe, g_ref[...], b_ref[...])


def _ple(x, p, w_gate, w_ple, g, b, tm=512):
    t, d = x.shape
    row = lambda i: (i, 0)
    const = lambda i: (0, 0)
    return pl.pallas_call(
        _ple_kernel, grid=(t // tm,),
        in_specs=[pl.BlockSpec((tm, d), row), pl.BlockSpec((tm, PLE_DIM), row),
                  pl.BlockSpec((d, d), const), pl.BlockSpec((PLE_DIM, d), const),
                  pl.BlockSpec((1, d), const), pl.BlockSpec((1, d), const)],
        out_specs=pl.BlockSpec((tm, d), row), out_shape=jax.ShapeDtypeStruct((t, d), F32),
        compiler_params=_cparams(("arbitrary",)), name="ple_ln",
    )(x, p, w_gate.astype(BF16), w_ple.astype(BF16), g, b)


def kernel(x, p, ln_in_g, ln_in_b, w_in, w_out, mu_shift, w0, w_decay, a0, w_iclr, w_gate_up, k_k, k_a, r_k,
           gn_g, gn_b, rel_bias, w_router, b_router, w_exp_gate, w_exp_up, w_exp_down, w_ple, w_ple_gate,
           ln_mix_g, ln_mix_b, ln_ffn_g, ln_ffn_b, ln_ple_g, ln_ple_b):
    bsz, seq, d = x.shape
    t = bsz * seq
    depth = w_in.shape[0]
    xf = x.reshape(t, d)
    near_bias = _near_bias(rel_bias)
    for i in range(depth):
        w_all = _pack_w_in(w_in[i])
        outs = _inproj(xf, ln_in_g[None], ln_in_b[None], w_all, apply_ln=(i == 0))
        qkv, kiki, wi, hb = outs[:4]
        if i == 0:
            xf = outs[4]
        a_out = _dsa(qkv, kiki, wi, near_bias, bsz, seq)
        par, mu_wa, mu_gd, wdec, wic = _pack_rwkv_params(mu_shift[i], w0[i], w_decay[i], a0[i], w_iclr[i],
                                                         k_k[i], k_a[i], r_k[i], gn_g[i], gn_b[i])
        b_out = _rwkv(hb, par, mu_wa, mu_gd, wdec, wic, w_gate_up[i], bsz, seq)
        xf = _outproj(xf, a_out, b_out, w_out[i], ln_mix_g[i][None], ln_mix_b[i][None])
        xf = _moe(xf, w_router, b_router, w_exp_gate[i], w_exp_up[i], w_exp_down[i],
                  ln_ffn_g[i][None], ln_ffn_b[i][None])
        xf = _ple(xf, p[i].reshape(t, PLE_DIM), w_ple_gate[i], w_ple[i], ln_ple_g[i][None], ln_ple_b[i][None])
    return xf.reshape(bsz, seq, d)
```

```python
import functools
import math

import numpy as np
import jax
import jax.numpy as jnp
from jax import lax
from jax.experimental import pallas as pl
from jax.experimental.pallas import tpu as pltpu

D_MODEL = 1024
DEPTH = 2
CHUNK = 64
A_HEAD_DIM = 64
A_WIDTH = D_MODEL // 2
A_HEADS = A_WIDTH // A_HEAD_DIM
IDX_HEADS = 8
IDX_DIM = 64
TOPK_MAX = 256
N_BUCKETS = 32
MAX_DISTANCE = 128
B_HEAD_DIM = 64
B_WIDTH = D_MODEL - A_WIDTH
B_HEADS = B_WIDTH // B_HEAD_DIM
DECAY_LORA = 64
ICLR_LORA = 64
GATE_LORA = 128
A_SIZES = (A_WIDTH, A_WIDTH, A_WIDTH, IDX_HEADS * IDX_DIM, IDX_DIM, IDX_HEADS)
B_SIZES = (B_WIDTH, B_WIDTH, B_WIDTH, DECAY_LORA, ICLR_LORA, GATE_LORA)
A_COLS = sum(A_SIZES)
B_COLS = sum(B_SIZES)
N_EXPERTS = 16
N_GROUPS = 4
EXPERTS_PER_GROUP = N_EXPERTS // N_GROUPS
D_EXPERT = 256
PLE_DIM = 256
ALPHA = (2 * DEPTH) ** 0.25
LN_EPS = 1e-5
GN_EPS = 64e-5
NEG = -1e30

LANES = 128
VMEM_LIMIT = 56 * 1024 * 1024
F32 = jnp.float32
BF16 = jnp.bfloat16
HI = lax.Precision.HIGHEST
NT_DIMS = (((1,), (1,)), ((), ()))


def _cparams(sem):
    return pltpu.CompilerParams(dimension_semantics=sem, vmem_limit_bytes=VMEM_LIMIT)


def _ln(x, g, b):
    mu = jnp.mean(x, axis=-1, keepdims=True)
    xc = x - mu
    var = jnp.mean(xc * xc, axis=-1, keepdims=True)
    return xc * lax.rsqrt(var + LN_EPS) * g + b


def _dot(a, b):
    return jnp.dot(a, b, preferred_element_type=F32)


def _dot_hi(a, b):
    return jnp.dot(a, b, preferred_element_type=F32, precision=HI)


def _dot_nt(a, b):
    return lax.dot_general(a, b, NT_DIMS, preferred_element_type=F32)


def _dot_nt_hi(a, b):
    return lax.dot_general(a, b, NT_DIMS, preferred_element_type=F32, precision=HI)


IN_NCHUNK = 512


def _inproj_kernel(x_ref, g_ref, b_ref, w_ref, *out_refs, apply_ln):
    x = x_ref[...]
    if apply_ln:
        x = _ln(x, g_ref[...], b_ref[...])
        out_refs[4][...] = x
    xb = x.astype(BF16)
    qkv_ref, kk_ref, wi_ref, hb_ref = out_refs[:4]
    for c in range(4):
        sl = slice(c * IN_NCHUNK, (c + 1) * IN_NCHUNK)
        qkv_ref[:, sl] = _dot(xb, w_ref[:, sl]).astype(BF16)
    aux = _dot(xb, w_ref[:, 2048:2304])
    kk_ref[...] = aux[:, :LANES].astype(BF16)
    wi_ref[...] = aux[:, LANES:]
    nb = hb_ref.shape[1]
    for c0 in range(0, nb, IN_NCHUNK):
        c1 = min(c0 + IN_NCHUNK, nb)
        hb_ref[:, c0:c1] = _dot(xb, w_ref[:, 2304 + c0:2304 + c1])


def _inproj(x, g, b, w_all, apply_ln, tm=512):
    t, d = x.shape
    nc = w_all.shape[1]
    nb = nc - 2304
    out_shape = [jax.ShapeDtypeStruct((t, 2048), BF16), jax.ShapeDtypeStruct((t, LANES), BF16),
                 jax.ShapeDtypeStruct((t, LANES), F32), jax.ShapeDtypeStruct((t, nb), F32)]
    out_specs = [pl.BlockSpec((tm, 2048), lambda i: (i, 0)), pl.BlockSpec((tm, LANES), lambda i: (i, 0)),
                 pl.BlockSpec((tm, LANES), lambda i: (i, 0)), pl.BlockSpec((tm, nb), lambda i: (i, 0))]
    if apply_ln:
        out_shape.append(jax.ShapeDtypeStruct((t, d), F32))
        out_specs.append(pl.BlockSpec((tm, d), lambda i: (i, 0)))
    return pl.pallas_call(
        functools.partial(_inproj_kernel, apply_ln=apply_ln),
        grid=(t // tm,),
        in_specs=[pl.BlockSpec((tm, d), lambda i: (i, 0)), pl.BlockSpec((1, d), lambda i: (0, 0)),
                  pl.BlockSpec((1, d), lambda i: (0, 0)), pl.BlockSpec((d, nc), lambda i: (0, 0))],
        out_specs=out_specs, out_shape=out_shape,
        compiler_params=_cparams(("arbitrary",)), name="inproj_ln" if apply_ln else "inproj",
    )(x, g, b, w_all)


def _pack_w_in(w):
    d = w.shape[0]
    cuts = np.cumsum(A_SIZES)
    wq = w[:, :cuts[3]]
    wki = w[:, cuts[3]:cuts[4]]
    wwi = w[:, cuts[4]:cuts[5]]
    wb = w[:, A_COLS:]
    pad = jnp.zeros((d, LANES - IDX_HEADS), w.dtype)
    return jnp.concatenate([wq, wki, wki, wwi, pad, wb], axis=1).astype(BF16)


TQ = 256
ROW_GROUP = 32
BIGF = 3.0e38


def _t5_bucket_np(rel):
    nb = N_BUCKETS // 2
    max_exact = nb // 2
    ret = np.where(rel > 0, nb, 0)
    n = np.abs(rel)
    nf = np.maximum(n, 1).astype(np.float64)
    large = max_exact + np.floor(np.log(nf / max_exact) / math.log(MAX_DISTANCE / max_exact)
                                 * (nb - max_exact) + 1e-9).astype(np.int64)
    large = np.minimum(large, nb - 1)
    return ret + np.where(n < max_exact, n, large)


def _near_bias(rel_bias):
    qpos = np.arange(TQ)[:, None]
    kpos = np.arange(TQ)[None, :]
    bk = np.stack([_t5_bucket_np(kpos - qpos), _t5_bucket_np(kpos - TQ - qpos)])
    far = N_BUCKETS // 2 - 1
    tab = rel_bias - rel_bias[far][None, :]
    onehot = jnp.asarray(bk[..., None] == np.arange(N_BUCKETS)).astype(F32)
    return jnp.einsum('nqkb,bh->nhqk', onehot, tab.astype(F32), precision=HI)


def _dsa_kernel(q_ref, qi_ref, wi_ref, k_ref, v_ref, kk_ref, bias_ref, o_ref,
                s_ref, m_ref, l_ref, acc_ref, *, ksel):
    i = pl.program_id(1)
    nt = i + 1
    tq = q_ref.shape[0]
    idx_scale = (IDX_HEADS ** -0.5) * (IDX_DIM ** -0.5)
    att_scale = A_HEAD_DIM ** -0.5

    lane = lax.broadcasted_iota(jnp.int32, (tq, LANES), 1)
    half0 = lane < (LANES // 2)
    zero_b = jnp.zeros((tq, LANES), BF16)

    def head_parts(ref, scale):
        parts = []
        for h in range(A_HEADS):
            j, s = divmod(h, 2)
            blk = ref[:, j * LANES:(j + 1) * LANES]
            if scale != 1.0:
                blk = (blk.astype(F32) * scale).astype(BF16)
            parts.append(jnp.where(half0 if s == 0 else jnp.logical_not(half0), blk, zero_b))
        return parts

    qi_parts = head_parts(qi_ref, 1.0)
    wi = wi_ref[...] * idx_scale
    wcols = [wi[:, h:h + 1] for h in range(IDX_HEADS)]
    row = lax.broadcasted_iota(jnp.int32, (tq, tq), 0)
    col = lax.broadcasted_iota(jnp.int32, (tq, tq), 1)
    adm_diag = (col // CHUNK) <= (row // CHUNK)

    def score_tile(kt):
        kk = kk_ref[pl.ds(pl.multiple_of(kt * tq, tq), tq), :]
        acc = jnp.zeros((tq, tq), F32)
        for h in range(IDX_HEADS):
            acc = acc + wcols[h] * jnp.maximum(_dot_nt(qi_parts[h], kk), 0.0)
        return acc

    def p1_body(kt, c):
        s_ref[kt] = score_tile(kt)
        return c

    lax.fori_loop(0, i, p1_body, 0)
    s_ref[i] = jnp.where(adm_diag, score_tile(i), NEG)

    ngrp = tq // ROW_GROUP
    groups = range(ngrp)

    def fold_tiles(rg, fn, inits, reducers):
        rows = slice(rg * ROW_GROUP, (rg + 1) * ROW_GROUP)

        def body(kt, carry):
            return fn(s_ref[kt, rows, :], kt, carry)

        carry = lax.fori_loop(0, nt, body, tuple(jnp.full((ROW_GROUP, tq), v, F32) for v in inits))
        return [red(cv, axis=1, keepdims=True) for red, cv in zip(reducers, carry)]

    def minmax(s, kt, carry):
        mn, mx = carry
        return jnp.minimum(mn, jnp.where(s > 0.5 * NEG, s, BIGF)), jnp.maximum(mx, s)

    kself = float(ksel)
    qrow = lax.broadcasted_iota(jnp.int32, (ROW_GROUP, 1), 0)
    lo0, hi0, clo0, chi0 = [], [], [], []
    for rg in groups:
        rowmin, rowmax = fold_tiles(rg, minmax, (BIGF, -BIGF), (jnp.min, jnp.max))
        n_adm = ((i * tq + rg * ROW_GROUP + qrow) // CHUNK + 1) * CHUNK
        trivial = n_adm <= ksel
        lo0.append(jnp.where(trivial, 0.1 * NEG, rowmin))
        hi0.append(jnp.where(trivial, 0.1 * NEG, rowmax))
        clo0.append(n_adm.astype(F32))
        chi0.append(jnp.zeros((ROW_GROUP, 1), F32))

    def any_open(lo, hi):
        flags = [jnp.max(jnp.where(lo[rg] < hi[rg], 1.0, 0.0)) for rg in groups]
        return functools.reduce(jnp.maximum, flags) > 0.0

    def bis_cond(st):
        return any_open(st[0], st[1])

    def bis_body(st):
        lo, hi, clo, chi = st
        out = ([], [], [], [])
        for rg in groups:
            mid = 0.5 * (lo[rg] + hi[rg])
            mid = jnp.where(mid > lo[rg], mid, hi[rg])

            def cnt(s, kt, carry, mid=mid):
                c, vge, vlt = carry
                ge = s >= mid
                return (c + jnp.where(ge, 1.0, 0.0), jnp.minimum(vge, jnp.where(ge, s, BIGF)),
                        jnp.maximum(vlt, jnp.where(ge, -BIGF, s)))

            c, vge, vlt = fold_tiles(rg, cnt, (0.0, BIGF, -BIGF), (jnp.sum, jnp.min, jnp.max))
            ok = c >= kself
            out[0].append(jnp.where(ok, vge, lo[rg]))
            out[1].append(jnp.where(ok, hi[rg], vlt))
            out[2].append(jnp.where(ok, c, clo[rg]))
            out[3].append(jnp.where(ok, chi[rg], c))
        return tuple(tuple(o) for o in out)

    thr_g, _, c_ge, c_gt = lax.while_loop(bis_cond, bis_body, (tuple(lo0), tuple(hi0), tuple(clo0), tuple(chi0)))

    s_total = float(s_ref.shape[0] * tq)
    kcol = lax.broadcasted_iota(jnp.int32, (tq, tq), 1).astype(F32)
    kcol_g = lax.broadcasted_iota(jnp.int32, (ROW_GROUP, tq), 1).astype(F32)
    need = [kself - c_gt[rg] for rg in groups]
    jlo0 = tuple(jnp.where(c_ge[rg] > kself, 0.0, s_total - 1.0) for rg in groups)
    jhi0 = tuple(jnp.where(c_ge[rg] > kself, (nt * tq).astype(F32), s_total) for rg in groups)

    def tie_cond(st):
        jlo, jhi = st
        return functools.reduce(jnp.maximum, [jnp.max(jhi[rg] - jlo[rg]) for rg in groups]) > 1.0

    def tie_body(st):
        jlo, jhi = st
        out = ([], [])
        for rg in groups:
            mid = jnp.floor(0.5 * (jlo[rg] + jhi[rg]))

            def cnt(s, kt, carry, mid=mid, thr=thr_g[rg]):
                idx = kcol_g + (kt * tq).astype(F32)
                return (carry[0] + jnp.where(jnp.logical_and(s == thr, idx < mid), 1.0, 0.0),)

            (c,) = fold_tiles(rg, cnt, (0.0,), (jnp.sum,))
            ok = c >= need[rg]
            done = (jhi[rg] - jlo[rg]) <= 1.0
            out[0].append(jnp.where(jnp.logical_or(ok, done), jlo[rg], mid))
            out[1].append(jnp.where(jnp.logical_and(ok, jnp.logical_not(done)), mid, jhi[rg]))
        return tuple(tuple(o) for o in out)

    _, jcut_g = lax.while_loop(tie_cond, tie_body, (jlo0, jhi0))
    thr = jnp.concatenate(list(thr_g), axis=0)
    jcut = jnp.concatenate(list(jcut_g), axis=0)

    q_parts = head_parts(q_ref, att_scale)
    m_ref[...] = jnp.full(m_ref.shape, NEG, F32)
    l_ref[...] = jnp.zeros(l_ref.shape, F32)
    acc_ref[...] = jnp.zeros(acc_ref.shape, F32)

    def attend(kt, near):
        s = s_ref[kt]
        idx = kcol + (kt * tq).astype(F32)
        sel = jnp.logical_or(s > thr, jnp.logical_and(s == thr, idx < jcut))
        mb = jnp.where(sel, 0.0, NEG)
        rows = pl.ds(pl.multiple_of(kt * tq, tq), tq)
        for j in range(A_HEADS // 2):
            kp = k_ref[rows, j * LANES:(j + 1) * LANES]
            vp = v_ref[rows, j * LANES:(j + 1) * LANES]
            new = []
            for s_half in range(2):
                h = 2 * j + s_half
                lg = _dot_nt(q_parts[h], kp) + mb
                if near is not None:
                    lg = lg + bias_ref[near, h]
                m_old = m_ref[h]
                m_new = jnp.maximum(m_old, jnp.max(lg, axis=1, keepdims=True))
                p = jnp.exp(lg - m_new[:, :1])
                alpha = jnp.exp(m_old - m_new)
                l_ref[h] = alpha * l_ref[h] + jnp.sum(p, axis=1, keepdims=True)
                m_ref[h] = m_new
                new.append(alpha * acc_ref[j] + _dot(p.astype(BF16), vp))
            acc_ref[j] = jnp.where(half0, new[0], new[1])

    def p3_body(kt, c):
        attend(kt, None)
        return c

    lax.fori_loop(0, jnp.maximum(i - 1, 0), p3_body, 0)

    @pl.when(i >= 1)
    def _():
        attend(i - 1, 1)

    attend(i, 0)

    for j in range(A_HEADS // 2):
        linv = jnp.where(half0, 1.0 / l_ref[2 * j], 1.0 / l_ref[2 * j + 1])
        o_ref[:, j * LANES:(j + 1) * LANES] = (acc_ref[j] * linv).astype(o_ref.dtype)


def _dsa(qkv, kiki, wi, near_bias, bsz, seq):
    nq = seq // TQ
    ksel = min(TOPK_MAX, seq // 4)
    return pl.pallas_call(
        functools.partial(_dsa_kernel, ksel=ksel),
        grid=(bsz, nq),
        in_specs=[
            pl.BlockSpec((TQ, A_WIDTH), lambda b, i: (b * nq + i, 0)),
            pl.BlockSpec((TQ, A_WIDTH), lambda b, i: (b * nq + i, 3)),
            pl.BlockSpec((TQ, LANES), lambda b, i: (b * nq + i, 0)),
            pl.BlockSpec((seq, A_WIDTH), lambda b, i: (b, 1)),
            pl.BlockSpec((seq, A_WIDTH), lambda b, i: (b, 2)),
            pl.BlockSpec((seq, LANES), lambda b, i: (b, 0)),
            pl.BlockSpec((2, A_HEADS, TQ, TQ), lambda b, i: (0, 0, 0, 0)),
        ],
        out_specs=pl.BlockSpec((TQ, A_WIDTH), lambda b, i: (b * nq + i, 0)),
        out_shape=jax.ShapeDtypeStruct((bsz * seq, A_WIDTH), BF16),
        scratch_shapes=[pltpu.VMEM((nq, TQ, TQ), F32), pltpu.VMEM((A_HEADS, TQ, LANES), F32),
                        pltpu.VMEM((A_HEADS, TQ, LANES), F32), pltpu.VMEM((A_HEADS // 2, TQ, LANES), F32)],
        compiler_params=_cparams(("arbitrary", "arbitrary")), name="dsa_attention",
    )(qkv, qkv, wi, qkv, qkv, kiki, near_bias)


RC = 64
RTS = 256


def _rwkv_kernel(r_ref, k_ref, v_ref, wa_ref, gd_ref, rp_ref, kp_ref, vp_ref, wap_ref, gdp_ref,
                 par_ref, muw_ref, mug_ref, wdec_ref, wic_ref, wg_ref, o_ref, z_ref):
    t = pl.program_id(2)
    ts = r_ref.shape[0]

    @pl.when(t == 0)
    def _():
        z_ref[...] = jnp.zeros(z_ref.shape, F32)

    row1 = lax.broadcasted_iota(jnp.int32, (ts, LANES), 0)
    first = jnp.where(t == 0, 0.0, 1.0)

    def shifted(ref, pref, mu):
        x = ref[...]
        prev = pltpu.roll(x, 1, 0)
        prev = jnp.where(row1 == 0, pref[7:8, :] * first, prev)
        return x + (prev - x) * mu

    par = par_ref[...]
    w0, a0, k_k, k_a, r_k, gn_g, gn_b = (par[n:n + 1] for n in range(7))
    r = shifted(r_ref, rp_ref, par[8:9])
    k = shifted(k_ref, kp_ref, par[9:10])
    v = shifted(v_ref, vp_ref, par[10:11])
    wa = shifted(wa_ref, wap_ref, muw_ref[...])
    gd = shifted(gd_ref, gdp_ref, mug_ref[...])

    lane = lax.broadcasted_iota(jnp.int32, (LANES, LANES), 1)
    rowl = lax.broadcasted_iota(jnp.int32, (LANES, LANES), 0)
    same_head = (lane // B_HEAD_DIM) == (rowl // B_HEAD_DIM)
    gones = jnp.where(same_head, 1.0, 0.0)
    gmean = gones * (1.0 / B_HEAD_DIM)

    z = w0 + _dot_hi(jnp.tanh(wa), wdec_ref[...])
    w_log = -(jnp.maximum(-z, 0.0) + jnp.log1p(jnp.exp(-jnp.abs(z)))) - 0.5
    e = jnp.exp(w_log)
    a = jax.nn.sigmoid(a0 + _dot_hi(wa, wic_ref[...]))
    g = _dot_hi(jax.nn.sigmoid(gd), wg_ref[...])
    kkr = k * k_k
    kk = kkr / jnp.maximum(jnp.sqrt(_dot_hi(kkr * kkr, gones)), 1e-12)
    k2 = k * (1.0 + (a - 1.0) * k_a)
    bonus = _dot_hi(r * k2 * r_k, gones) * v
    kka = kk * a

    hl = lax.broadcasted_iota(jnp.int32, (RC, LANES), 1)
    head0 = hl < B_HEAD_DIM
    strict = jnp.logical_and(same_head, rowl > lane)
    incl = jnp.logical_and(same_head, rowl >= lane)
    eye = jnp.where(lane == rowl, 1.0, 0.0)

    def stack_heads(x):
        return jnp.concatenate([jnp.where(head0, x, 0.0), jnp.where(head0, 0.0, x)], axis=0)

    def fold_heads(x):
        return x[:RC] + x[RC:]

    def twice(x):
        return jnp.concatenate([x, x], axis=0)

    rt = lax.broadcasted_iota(jnp.int32, (ts, ts), 0)
    ct = lax.broadcasted_iota(jnp.int32, (ts, ts), 1)
    same_chunk = (rt // RC) == (ct // RC)
    cum = _dot_hi(jnp.where(jnp.logical_and(same_chunk, rt >= ct), 1.0, 0.0), e)
    tot = _dot_hi(jnp.where(same_chunk, 1.0, 0.0), e)
    gro_in = jnp.exp(cum)
    rem = jnp.exp(cum - tot)
    a_hat = -kk * jnp.exp(e - cum)
    r_hat = r * jnp.exp(-cum)
    b_til = (kka * gro_in).astype(BF16)
    k_til = (k2 * gro_in).astype(BF16)
    b_bar = kka * rem
    k_bar = k2 * rem
    w_all = jnp.exp(-tot)
    v16 = v.astype(BF16)

    chunks = range(ts // RC)
    sls = [slice(c * RC, (c + 1) * RC) for c in chunks]
    a_s = [stack_heads(a_hat[sl]).astype(BF16) for sl in sls]
    r_s = [stack_heads(r_hat[sl]).astype(BF16) for sl in sls]
    v_s = [stack_heads(v[sl]).astype(BF16) for sl in sls]
    b_t2 = [twice(b_til[sl]) for sl in sls]
    k_t2 = [twice(k_til[sl]) for sl in sls]
    l_ab = [jnp.where(strict, _dot_nt(a_s[c], b_t2[c]), 0.0) for c in chunks]
    l_ak = [jnp.where(strict, _dot_nt(a_s[c], k_t2[c]), 0.0).astype(BF16) for c in chunks]
    m_rb = [jnp.where(incl, _dot_nt(r_s[c], b_t2[c]), 0.0).astype(BF16) for c in chunks]
    m_rk = [jnp.where(incl, _dot_nt(r_s[c], k_t2[c]), 0.0).astype(BF16) for c in chunks]
    inv = [eye + l_ab[c] for c in chunks]
    pw = [l_ab[c].astype(BF16) for c in chunks]
    for _ in range(5):
        pw = [_dot(pw[c], pw[c]).astype(BF16) for c in chunks]
        inv = [inv[c] + _dot(inv[c].astype(BF16), pw[c]) for c in chunks]
    lv = [_dot(l_ak[c], v_s[c]).astype(BF16) for c in chunks]
    x_av = [_dot(inv[c].astype(BF16), jnp.concatenate([a_s[c], lv[c]], axis=1)) for c in chunks]
    y_av = [_dot(m_rb[c], x_av[c].astype(BF16)) for c in chunks]
    y_kv = [_dot(m_rk[c], v_s[c]) for c in chunks]
    r_p = [r_hat[sls[c]] + fold_heads(y_av[c][:, :LANES]) for c in chunks]
    y0 = [fold_heads(y_av[c][:, LANES:] + y_kv[c]) for c in chunks]
    gh = [_dot(b_bar[sls[c]].T.astype(BF16), fold_heads(x_av[c]).astype(BF16)) for c in chunks]
    kv = [_dot(k_bar[sls[c]].T.astype(BF16), v16[sls[c]]) for c in chunks]
    g_mat = [eye * w_all[c * RC:c * RC + 1] + gones * gh[c][:, :LANES] for c in chunks]
    h_mat = [gones * (gh[c][:, LANES:] + kv[c]) for c in chunks]
    rg = [jnp.concatenate([r_p[c], g_mat[c]], axis=0).astype(BF16) for c in chunks]

    ys = []
    zc = z_ref[...]
    for c in chunks:
        yz = _dot(rg[c], zc.astype(BF16))
        ys.append(yz[:RC] + y0[c])
        zc = yz[RC:] + h_mat[c]
    z_ref[...] = zc

    y = jnp.concatenate(ys, axis=0)
    mean = _dot_hi(y, gmean)
    yc = y - mean
    var = _dot_hi(yc * yc, gmean)
    yn = yc * lax.rsqrt(var + GN_EPS) * gn_g + gn_b
    o_ref[...] = ((yn + bonus) * g).astype(o_ref.dtype)


def _rwkv(hb, par, mu_wa, mu_gd, wdec, wic, wg, bsz, seq):
    nt = seq // RTS
    npair = B_WIDTH // LANES
    cb = B_WIDTH // LANES

    def main(colblk):
        return pl.BlockSpec((RTS, LANES), lambda b, p, t: (b * nt + t, colblk(p)))

    def prev(colblk):
        return pl.BlockSpec((8, LANES), lambda b, p, t: (jnp.maximum((b * nt + t) * (RTS // 8) - 1, 0), colblk(p)))

    cols = [lambda p: p, lambda p: cb + p, lambda p: 2 * cb + p, lambda p: 3 * cb, lambda p: 3 * cb + 1]
    in_specs = [main(c) for c in cols] + [prev(c) for c in cols] + [
        pl.BlockSpec((16, LANES), lambda b, p, t: (0, p)),
        pl.BlockSpec((1, LANES), lambda b, p, t: (0, 0)),
        pl.BlockSpec((1, LANES), lambda b, p, t: (0, 0)),
        pl.BlockSpec((LANES, LANES), lambda b, p, t: (0, p)),
        pl.BlockSpec((LANES, LANES), lambda b, p, t: (0, p)),
        pl.BlockSpec((LANES, LANES), lambda b, p, t: (0, p)),
    ]
    return pl.pallas_call(
        _rwkv_kernel,
        grid=(bsz, npair, nt),
        in_specs=in_specs,
        out_specs=pl.BlockSpec((RTS, LANES), lambda b, p, t: (b * nt + t, p)),
        out_shape=jax.ShapeDtypeStruct((bsz * seq, B_WIDTH), BF16),
        scratch_shapes=[pltpu.VMEM((LANES, LANES), F32)],
        compiler_params=_cparams(("arbitrary", "arbitrary", "arbitrary")), name="rwkv7",
    )(*([hb] * 10), par, mu_wa, mu_gd, wdec, wic, wg)


def _pack_rwkv_params(mu, w0, w_decay, a0, w_iclr, k_k, k_a, r_k, gn_g, gn_b):
    bw = B_WIDTH
    zero = jnp.zeros((bw,), F32)
    rows = [w0, a0, k_k, k_a, r_k.reshape(bw), gn_g, gn_b, zero,
            mu[:bw], mu[bw:2 * bw], mu[2 * bw:3 * bw], zero, zero, zero, zero, zero]
    par = jnp.stack(rows)
    mu_wa = mu[3 * bw:3 * bw + LANES][None]
    mu_gd = mu[3 * bw + LANES:][None]
    zpad = jnp.zeros((DECAY_LORA, bw), F32)
    wdec = jnp.concatenate([w_decay, zpad], axis=0)
    wic = jnp.concatenate([zpad, w_iclr], axis=0)
    return par, mu_wa, mu_gd, wdec, wic


def _outproj_kernel(x_ref, a_ref, b_ref, wa_ref, wb_ref, g_ref, be_ref, o_ref):
    mix = _dot(a_ref[...], wa_ref[...]) + _dot(b_ref[...], wb_ref[...])
    o_ref[...] = _ln(ALPHA * x_ref[...] + mix, g_ref[...], be_ref[...])


def _outproj(x, a_out, b_out, w_out, g, b, tm=512):
    t, d = x.shape
    wa = w_out[:A_WIDTH].astype(BF16)
    wb = w_out[A_WIDTH:].astype(BF16)
    row = lambda i: (i, 0)
    const = lambda i: (0, 0)
    return pl.pallas_call(
        _outproj_kernel, grid=(t // tm,),
        in_specs=[pl.BlockSpec((tm, d), row), pl.BlockSpec((tm, A_WIDTH), row), pl.BlockSpec((tm, B_WIDTH), row),
                  pl.BlockSpec((A_WIDTH, d), const), pl.BlockSpec((B_WIDTH, d), const),
                  pl.BlockSpec((1, d), const), pl.BlockSpec((1, d), const)],
        out_specs=pl.BlockSpec((tm, d), row), out_shape=jax.ShapeDtypeStruct((t, d), F32),
        compiler_params=_cparams(("arbitrary",)), name="outproj_ln",
    )(x, a_out, b_out, wa, wb, g, b)


def _xor_partner(x, lane, bit):
    up = pltpu.roll(x, bit, 1)
    down = pltpu.roll(x, LANES - bit, 1)
    return jnp.where((lane & bit) != 0, up, down)


def _moe_gates(xf, wr, br):
    tm = xf.shape[0]
    lane = lax.broadcasted_iota(jnp.int32, (tm, LANES), 1)
    valid = lane < N_EXPERTS
    scores = jax.nn.sigmoid(_dot_hi(xf, wr))
    sel = jnp.where(valid, scores + br, NEG)
    p1 = _xor_partner(sel, lane, 1)
    hi1, lo1 = jnp.maximum(sel, p1), jnp.minimum(sel, p1)
    hi2, lo2 = _xor_partner(hi1, lane, 2), _xor_partner(lo1, lane, 2)
    gscore = jnp.maximum(hi1, hi2) + jnp.maximum(jnp.minimum(hi1, hi2), jnp.maximum(lo1, lo2))
    gscore = jnp.where(valid, gscore, NEG)
    lanef = lane.astype(F32)
    gbest = jnp.max(gscore, axis=1, keepdims=True)
    first = jnp.min(jnp.where(gscore == gbest, lanef, float(LANES)), axis=1, keepdims=True)
    in_group = jnp.floor(lanef * (1.0 / EXPERTS_PER_GROUP)) == jnp.floor(first * (1.0 / EXPERTS_PER_GROUP))
    masked = jnp.where(in_group, sel, NEG)
    m1 = jnp.max(masked, axis=1, keepdims=True)
    i1 = jnp.min(jnp.where(masked == m1, lanef, float(LANES)), axis=1, keepdims=True)
    pick1 = lanef == i1
    masked2 = jnp.where(pick1, NEG, masked)
    m2 = jnp.max(masked2, axis=1, keepdims=True)
    i2 = jnp.min(jnp.where(masked2 == m2, lanef, float(LANES)), axis=1, keepdims=True)
    pick2 = lanef == i2
    s1 = jnp.sum(jnp.where(pick1, scores, 0.0), axis=1, keepdims=True)
    s2 = jnp.sum(jnp.where(pick2, scores, 0.0), axis=1, keepdims=True)
    tot = s1 + s2
    return jnp.where(pick1, s1 / tot, 0.0) + jnp.where(pick2, s2 / tot, 0.0)


def _moe_kernel(x_ref, wr_ref, br_ref, wgu_ref, wd_ref, g_ref, b_ref, o_ref, gate_ref, acc_ref, xb_ref):
    e = pl.program_id(1)

    @pl.when(e == 0)
    def _():
        xf = x_ref[...]
        gate_ref[...] = _moe_gates(xf, wr_ref[...], br_ref[...])
        xb_ref[...] = xf.astype(BF16)
        acc_ref[...] = jnp.zeros(acc_ref.shape, F32)

    gates = gate_ref[...]
    lane = lax.broadcasted_iota(jnp.int32, gates.shape, 1)
    gcol = jnp.sum(jnp.where(lane == e, gates, 0.0), axis=1, keepdims=True)
    hu = _dot(xb_ref[...], wgu_ref[0])
    hg, up = hu[:, :D_EXPERT], hu[:, D_EXPERT:]
    hid = (hg * jax.nn.sigmoid(hg)) * up * gcol
    acc_ref[...] += _dot(hid.astype(BF16), wd_ref[0])

    @pl.when(e == pl.num_programs(1) - 1)
    def _():
        o_ref[...] = _ln(ALPHA * x_ref[...] + acc_ref[...], g_ref[...], b_ref[...])


def _moe(x, w_router, b_router, w_gate, w_up, w_down, g, b, tm=1024):
    t, d = x.shape
    ne = w_gate.shape[0]
    wgu = jnp.concatenate([w_gate, w_up], axis=2).astype(BF16)
    wd = w_down.astype(BF16)
    wr = jnp.pad(w_router, ((0, 0), (0, LANES - ne)))
    br = jnp.pad(b_router, (0, LANES - ne))[None]
    row = lambda i, e: (i, 0)
    const = lambda i, e: (0, 0)
    return pl.pallas_call(
        _moe_kernel, grid=(t // tm, ne),
        in_specs=[pl.BlockSpec((tm, d), row), pl.BlockSpec((d, LANES), const), pl.BlockSpec((1, LANES), const),
                  pl.BlockSpec((1, d, 2 * D_EXPERT), lambda i, e: (e, 0, 0)),
                  pl.BlockSpec((1, D_EXPERT, d), lambda i, e: (e, 0, 0)),
                  pl.BlockSpec((1, d), const), pl.BlockSpec((1, d), const)],
        out_specs=pl.BlockSpec((tm, d), row), out_shape=jax.ShapeDtypeStruct((t, d), F32),
        scratch_shapes=[pltpu.VMEM((tm, LANES), F32), pltpu.VMEM((tm, d), F32), pltpu.VMEM((tm, d), BF16)],
        compiler_params=_cparams(("arbitrary", "arbitrary")), name="moe_ln",
    )(x, wr, br, wgu, wd, g, b)


def _ple_kernel(x_ref, p_ref, wg_ref, wp_ref, g_ref, b_ref, o_ref):
    x = x_ref[...]
    gate = jax.nn.sigmoid(_dot(x.astype(BF16), wg_ref[...]))
    pe = _dot(p_ref[...].astype(BF16), wp_ref[...])
    o_ref[...] = _ln(ALPHA * x + gate * pe, g_ref[...], b_ref[...])


def _ple(x, p, w_gate, w_ple, g, b, tm=512):
    t, d = x.shape
    row = lambda i: (i, 0)
    const = lambda i: (0, 0)
    return pl.pallas_call(
        _ple_kernel, grid=(t // tm,),
        in_specs=[pl.BlockSpec((tm, d), row), pl.BlockSpec((tm, PLE_DIM), row),
                  pl.BlockSpec((d, d), const), pl.BlockSpec((PLE_DIM, d), const),
                  pl.BlockSpec((1, d), const), pl.BlockSpec((1, d), const)],
        out_specs=pl.BlockSpec((tm, d), row), out_shape=jax.ShapeDtypeStruct((t, d), F32),
        compiler_params=_cparams(("arbitrary",)), name="ple_ln",
    )(x, p, w_gate.astype(BF16), w_ple.astype(BF16), g, b)


def kernel(x, p, ln_in_g, ln_in_b, w_in, w_out, mu_shift, w0, w_decay, a0, w_iclr, w_gate_up, k_k, k_a, r_k,
           gn_g, gn_b, rel_bias, w_router, b_router, w_exp_gate, w_exp_up, w_exp_down, w_ple, w_ple_gate,
           ln_mix_g, ln_mix_b, ln_ffn_g, ln_ffn_b, ln_ple_g, ln_ple_b):
    bsz, seq, d = x.shape
    t = bsz * seq
    depth = w_in.shape[0]
    xf = x.reshape(t, d)
    near_bias = _near_bias(rel_bias)
    for i in range(depth):
        w_all = _pack_w_in(w_in[i])
        outs = _inproj(xf, ln_in_g[None], ln_in_b[None], w_all, apply_ln=(i == 0))
        qkv, kiki, wi, hb = outs[:4]
        if i == 0:
            xf = outs[4]
        a_out = _dsa(qkv, kiki, wi, near_bias, bsz, seq)
        par, mu_wa, mu_gd, wdec, wic = _pack_rwkv_params(mu_shift[i], w0[i], w_decay[i], a0[i], w_iclr[i],
                                                         k_k[i], k_a[i], r_k[i], gn_g[i], gn_b[i])
        b_out = _rwkv(hb, par, mu_wa, mu_gd, wdec, wic, w_gate_up[i], bsz, seq)
        xf = _outproj(xf, a_out, b_out, w_out[i], ln_mix_g[i][None], ln_mix_b[i][None])
        xf = _moe(xf, w_router, b_router, w_exp_gate[i], w_exp_up[i], w_exp_down[i],
                  ln_ffn_g[i][None], ln_ffn_b[i][None])
        xf = _ple(xf, p[i].reshape(t, PLE_DIM), w_ple_gate[i], w_ple[i], ln_ple_g[i][None], ln_ple_b[i][None])
    return xf.reshape(bsz, seq, d)
```

```python
import functools
import math

import numpy as np
import jax
import jax.numpy as jnp
from jax import lax
from jax.experimental import pallas as pl
from jax.experimental.pallas import tpu as pltpu

D_MODEL = 1024
DEPTH = 2
CHUNK = 64
A_HEAD_DIM = 64
A_WIDTH = D_MODEL // 2
A_HEADS = A_WIDTH // A_HEAD_DIM
IDX_HEADS = 8
IDX_DIM = 64
TOPK_MAX = 256
N_BUCKETS = 32
MAX_DISTANCE = 128
B_HEAD_DIM = 64
B_WIDTH = D_MODEL - A_WIDTH
B_HEADS = B_WIDTH // B_HEAD_DIM
DECAY_LORA = 64
ICLR_LORA = 64
GATE_LORA = 128
A_SIZES = (A_WIDTH, A_WIDTH, A_WIDTH, IDX_HEADS * IDX_DIM, IDX_DIM, IDX_HEADS)
B_SIZES = (B_WIDTH, B_WIDTH, B_WIDTH, DECAY_LORA, ICLR_LORA, GATE_LORA)
A_COLS = sum(A_SIZES)
B_COLS = sum(B_SIZES)
N_EXPERTS = 16
N_GROUPS = 4
EXPERTS_PER_GROUP = N_EXPERTS // N_GROUPS
D_EXPERT = 256
PLE_DIM = 256
ALPHA = (2 * DEPTH) ** 0.25
LN_EPS = 1e-5
GN_EPS = 64e-5
NEG = -1e30

LANES = 128
VMEM_LIMIT = 56 * 1024 * 1024
F32 = jnp.float32
BF16 = jnp.bfloat16
HI = lax.Precision.HIGHEST
NT_DIMS = (((1,), (1,)), ((), ()))


def _cparams(sem):
    return pltpu.CompilerParams(dimension_semantics=sem, vmem_limit_bytes=VMEM_LIMIT)


def _ln(x, g, b):
    mu = jnp.mean(x, axis=-1, keepdims=True)
    xc = x - mu
    var = jnp.mean(xc * xc, axis=-1, keepdims=True)
    return xc * lax.rsqrt(var + LN_EPS) * g + b


def _dot(a, b):
    return jnp.dot(a, b, preferred_element_type=F32)


def _dot_hi(a, b):
    return jnp.dot(a, b, preferred_element_type=F32, precision=HI)


def _dot_nt(a, b):
    return lax.dot_general(a, b, NT_DIMS, preferred_element_type=F32)


def _dot_nt_hi(a, b):
    return lax.dot_general(a, b, NT_DIMS, preferred_element_type=F32, precision=HI)


IN_NCHUNK = 512


WI_ROWS = 16


def _inproj_kernel(x_ref, g_ref, b_ref, wr_ref, wf_ref, *out_refs, apply_ln):
    x = x_ref[...]
    if apply_ln:
        x = _ln(x, g_ref[...], b_ref[...])
        out_refs[7][...] = x
    xb = x.astype(BF16)
    k_ref, kk_ref, hb_ref, qt_ref, qit_ref, vt_ref, wit_ref = out_refs[:7]
    aw = A_WIDTH
    k_ref[...] = _dot(xb, wr_ref[:, :aw]).astype(BF16)
    kk_ref[...] = _dot(xb, wr_ref[:, aw:aw + LANES]).astype(BF16)
    nb = hb_ref.shape[1]
    for c0 in range(0, nb, IN_NCHUNK):
        c1 = min(c0 + IN_NCHUNK, nb)
        hb_ref[:, c0:c1] = _dot(xb, wr_ref[:, aw + LANES + c0:aw + LANES + c1])
    qt_ref[...] = _dot_nt(wf_ref[:aw], xb).astype(BF16)
    qit_ref[...] = _dot_nt(wf_ref[aw:2 * aw], xb).astype(BF16)
    vt = _dot_nt(wf_ref[2 * aw:3 * aw], xb).astype(BF16)
    for n in range(vt_ref.shape[0]):
        vt_ref[n] = vt[:, n * TQ:(n + 1) * TQ]
    wit_ref[...] = _dot_nt(wf_ref[3 * aw:], xb)


def _inproj(x, g, b, w_rm, w_fm, apply_ln, tm=512):
    t, d = x.shape
    aw = A_WIDTH
    nb = w_rm.shape[1] - aw - LANES
    row = lambda i: (i, 0)
    col = lambda i: (0, i)
    const = lambda i: (0, 0)
    out_shape = [jax.ShapeDtypeStruct((t, aw), BF16), jax.ShapeDtypeStruct((t, LANES), BF16),
                 jax.ShapeDtypeStruct((t, nb), F32), jax.ShapeDtypeStruct((aw, t), BF16),
                 jax.ShapeDtypeStruct((aw, t), BF16), jax.ShapeDtypeStruct((t // TQ, aw, TQ), BF16),
                 jax.ShapeDtypeStruct((WI_ROWS, t), F32)]
    out_specs = [pl.BlockSpec((tm, aw), row), pl.BlockSpec((tm, LANES), row), pl.BlockSpec((tm, nb), row),
                 pl.BlockSpec((aw, tm), col), pl.BlockSpec((aw, tm), col),
                 pl.BlockSpec((tm // TQ, aw, TQ), lambda i: (i, 0, 0)), pl.BlockSpec((WI_ROWS, tm), col)]
    if apply_ln:
        out_shape.append(jax.ShapeDtypeStruct((t, d), F32))
        out_specs.append(pl.BlockSpec((tm, d), row))
    return pl.pallas_call(
        functools.partial(_inproj_kernel, apply_ln=apply_ln),
        grid=(t // tm,),
        in_specs=[pl.BlockSpec((tm, d), row), pl.BlockSpec((1, d), const), pl.BlockSpec((1, d), const),
                  pl.BlockSpec(w_rm.shape, const), pl.BlockSpec(w_fm.shape, const)],
        out_specs=out_specs, out_shape=out_shape,
        compiler_params=_cparams(("arbitrary",)), name="inproj_ln" if apply_ln else "inproj",
    )(x, g, b, w_rm, w_fm)


def _pack_w_in(w):
    d = w.shape[0]
    aw = A_WIDTH
    cuts = np.cumsum(A_SIZES)
    wq, wk, wv, wqi = w[:, :aw], w[:, aw:2 * aw], w[:, 2 * aw:3 * aw], w[:, cuts[2]:cuts[3]]
    wki = w[:, cuts[3]:cuts[4]]
    wwi = w[:, cuts[4]:cuts[5]]
    w_rm = jnp.concatenate([wk, wki, wki, w[:, A_COLS:]], axis=1).astype(BF16)
    pad = jnp.zeros((d, WI_ROWS - IDX_HEADS), w.dtype)
    w_fm = jnp.concatenate([wq, wqi, wv, wwi, pad], axis=1).T.astype(BF16)
    return w_rm, w_fm


TQ = 256
ROW_GROUP = 32
BIGF = 3.0e38


def _t5_bucket_np(rel):
    nb = N_BUCKETS // 2
    max_exact = nb // 2
    ret = np.where(rel > 0, nb, 0)
    n = np.abs(rel)
    nf = np.maximum(n, 1).astype(np.float64)
    large = max_exact + np.floor(np.log(nf / max_exact) / math.log(MAX_DISTANCE / max_exact)
                                 * (nb - max_exact) + 1e-9).astype(np.int64)
    large = np.minimum(large, nb - 1)
    return ret + np.where(n < max_exact, n, large)


def _near_bias(rel_bias):
    qpos = np.arange(TQ)[None, :]
    kpos = np.arange(TQ)[:, None]
    bk = np.stack([_t5_bucket_np(kpos - qpos), _t5_bucket_np(kpos - TQ - qpos)])
    far = N_BUCKETS // 2 - 1
    tab = rel_bias - rel_bias[far][None, :]
    onehot = jnp.asarray(bk[..., None] == np.arange(N_BUCKETS)).astype(F32)
    return jnp.einsum('nkqb,bh->nhkq', onehot, tab.astype(F32), precision=HI)


def _dsa_kernel(q_ref, qi_ref, wi_ref, k_ref, v_ref, kk_ref, bias_ref, o_ref,
                s_ref, m_ref, l_ref, acc_ref, *, ksel):
    i = pl.program_id(1)
    nt = i + 1
    tq = q_ref.shape[0]
    idx_scale = (IDX_HEADS ** -0.5) * (IDX_DIM ** -0.5)
    att_scale = A_HEAD_DIM ** -0.5

    lane = lax.broadcasted_iota(jnp.int32, (tq, LANES), 1)
    half0 = lane < (LANES // 2)
    zero_b = jnp.zeros((tq, LANES), BF16)

    def head_parts(ref, scale):
        parts = []
        for h in range(A_HEADS):
            j, s = divmod(h, 2)
            blk = ref[:, j * LANES:(j + 1) * LANES]
            if scale != 1.0:
                blk = (blk.astype(F32) * scale).astype(BF16)
            parts.append(jnp.where(half0 if s == 0 else jnp.logical_not(half0), blk, zero_b))
        return parts

    qi_parts = head_parts(qi_ref, 1.0)
    wi = wi_ref[...] * idx_scale
    wcols = [wi[:, h:h + 1] for h in range(IDX_HEADS)]
    row = lax.broadcasted_iota(jnp.int32, (tq, tq), 0)
    col = lax.broadcasted_iota(jnp.int32, (tq, tq), 1)
    adm_diag = (col // CHUNK) <= (row // CHUNK)

    def score_tile(kt):
        kk = kk_ref[pl.ds(pl.multiple_of(kt * tq, tq), tq), :]
        acc = jnp.zeros((tq, tq), F32)
        for h in range(IDX_HEADS):
            acc = acc + wcols[h] * jnp.maximum(_dot_nt(qi_parts[h], kk), 0.0)
        return acc

    def p1_body(kt, c):
        s_ref[kt] = score_tile(kt)
        return c

    lax.fori_loop(0, i, p1_body, 0)
    s_ref[i] = jnp.where(adm_diag, score_tile(i), NEG)

    ngrp = tq // ROW_GROUP
    groups = range(ngrp)

    def fold_tiles(rg, fn, inits, reducers):
        rows = slice(rg * ROW_GROUP, (rg + 1) * ROW_GROUP)

        def body(kt, carry):
            return fn(s_ref[kt, rows, :], kt, carry)

        carry = lax.fori_loop(0, nt, body, tuple(jnp.full((ROW_GROUP, tq), v, F32) for v in inits))
        return [red(cv, axis=1, keepdims=True) for red, cv in zip(reducers, carry)]

    def minmax(s, kt, carry):
        mn, mx = carry
        return jnp.minimum(mn, jnp.where(s > 0.5 * NEG, s, BIGF)), jnp.maximum(mx, s)

    kself = float(ksel)
    qrow = lax.broadcasted_iota(jnp.int32, (ROW_GROUP, 1), 0)
    lo0, hi0, clo0, chi0 = [], [], [], []
    for rg in groups:
        rowmin, rowmax = fold_tiles(rg, minmax, (BIGF, -BIGF), (jnp.min, jnp.max))
        n_adm = ((i * tq + rg * ROW_GROUP + qrow) // CHUNK + 1) * CHUNK
        trivial = n_adm <= ksel
        lo0.append(jnp.where(trivial, 0.1 * NEG, rowmin))
        hi0.append(jnp.where(trivial, 0.1 * NEG, rowmax))
        clo0.append(n_adm.astype(F32))
        chi0.append(jnp.zeros((ROW_GROUP, 1), F32))

    def any_open(lo, hi):
        flags = [jnp.max(jnp.where(lo[rg] < hi[rg], 1.0, 0.0)) for rg in groups]
        return functools.reduce(jnp.maximum, flags) > 0.0

    def bis_cond(st):
        return any_open(st[0], st[1])

    def bis_body(st):
        lo, hi, clo, chi = st
        out = ([], [], [], [])
        for rg in groups:
            mid = 0.5 * (lo[rg] + hi[rg])
            mid = jnp.where(mid > lo[rg], mid, hi[rg])

            def cnt(s, kt, carry, mid=mid):
                c, vge, vlt = carry
                ge = s >= mid
                return (c + jnp.where(ge, 1.0, 0.0), jnp.minimum(vge, jnp.where(ge, s, BIGF)),
                        jnp.maximum(vlt, jnp.where(ge, -BIGF, s)))

            c, vge, vlt = fold_tiles(rg, cnt, (0.0, BIGF, -BIGF), (jnp.sum, jnp.min, jnp.max))
            ok = c >= kself
            out[0].append(jnp.where(ok, vge, lo[rg]))
            out[1].append(jnp.where(ok, hi[rg], vlt))
            out[2].append(jnp.where(ok, c, clo[rg]))
            out[3].append(jnp.where(ok, chi[rg], c))
        return tuple(tuple(o) for o in out)

    thr_g, _, c_ge, c_gt = lax.while_loop(bis_cond, bis_body, (tuple(lo0), tuple(hi0), tuple(clo0), tuple(chi0)))

    s_total = float(s_ref.shape[0] * tq)
    kcol = lax.broadcasted_iota(jnp.int32, (tq, tq), 1).astype(F32)
    kcol_g = lax.broadcasted_iota(jnp.int32, (ROW_GROUP, tq), 1).astype(F32)
    need = [kself - c_gt[rg] for rg in groups]
    jlo0 = tuple(jnp.where(c_ge[rg] > kself, 0.0, s_total - 1.0) for rg in groups)
    jhi0 = tuple(jnp.where(c_ge[rg] > kself, (nt * tq).astype(F32), s_total) for rg in groups)

    def tie_cond(st):
        jlo, jhi = st
        return functools.reduce(jnp.maximum, [jnp.max(jhi[rg] - jlo[rg]) for rg in groups]) > 1.0

    def tie_body(st):
        jlo, jhi = st
        out = ([], [])
        for rg in groups:
            mid = jnp.floor(0.5 * (jlo[rg] + jhi[rg]))

            def cnt(s, kt, carry, mid=mid, thr=thr_g[rg]):
                idx = kcol_g + (kt * tq).astype(F32)
                return (carry[0] + jnp.where(jnp.logical_and(s == thr, idx < mid), 1.0, 0.0),)

            (c,) = fold_tiles(rg, cnt, (0.0,), (jnp.sum,))
            ok = c >= need[rg]
            done = (jhi[rg] - jlo[rg]) <= 1.0
            out[0].append(jnp.where(jnp.logical_or(ok, done), jlo[rg], mid))
            out[1].append(jnp.where(jnp.logical_and(ok, jnp.logical_not(done)), mid, jhi[rg]))
        return tuple(tuple(o) for o in out)

    _, jcut_g = lax.while_loop(tie_cond, tie_body, (jlo0, jhi0))
    thr = jnp.concatenate(list(thr_g), axis=0)
    jcut = jnp.concatenate(list(jcut_g), axis=0)

    q_parts = head_parts(q_ref, att_scale)
    m_ref[...] = jnp.full(m_ref.shape, NEG, F32)
    l_ref[...] = jnp.zeros(l_ref.shape, F32)
    acc_ref[...] = jnp.zeros(acc_ref.shape, F32)

    def attend(kt, near):
        s = s_ref[kt]
        idx = kcol + (kt * tq).astype(F32)
        sel = jnp.logical_or(s > thr, jnp.logical_and(s == thr, idx < jcut))
        mb = jnp.where(sel, 0.0, NEG)
        rows = pl.ds(pl.multiple_of(kt * tq, tq), tq)
        for j in range(A_HEADS // 2):
            kp = k_ref[rows, j * LANES:(j + 1) * LANES]
            vp = v_ref[rows, j * LANES:(j + 1) * LANES]
            new = []
            for s_half in range(2):
                h = 2 * j + s_half
                lg = _dot_nt(q_parts[h], kp) + mb
                if near is not None:
                    lg = lg + bias_ref[near, h]
                m_old = m_ref[h]
                m_new = jnp.maximum(m_old, jnp.max(lg, axis=1, keepdims=True))
                p = jnp.exp(lg - m_new[:, :1])
                alpha = jnp.exp(m_old - m_new)
                l_ref[h] = alpha * l_ref[h] + jnp.sum(p, axis=1, keepdims=True)
                m_ref[h] = m_new
                new.append(alpha * acc_ref[j] + _dot(p.astype(BF16), vp))
            acc_ref[j] = jnp.where(half0, new[0], new[1])

    def p3_body(kt, c):
        attend(kt, None)
        return c

    lax.fori_loop(0, jnp.maximum(i - 1, 0), p3_body, 0)

    @pl.when(i >= 1)
    def _():
        attend(i - 1, 1)

    attend(i, 0)

    for j in range(A_HEADS // 2):
        linv = jnp.where(half0, 1.0 / l_ref[2 * j], 1.0 / l_ref[2 * j + 1])
        o_ref[:, j * LANES:(j + 1) * LANES] = (acc_ref[j] * linv).astype(o_ref.dtype)


def _dsa(qkv, kiki, wi, near_bias, bsz, seq):
    nq = seq // TQ
    ksel = min(TOPK_MAX, seq // 4)
    return pl.pallas_call(
        functools.partial(_dsa_kernel, ksel=ksel),
        grid=(bsz, nq),
        in_specs=[
            pl.BlockSpec((TQ, A_WIDTH), lambda b, i: (b * nq + i, 0)),
            pl.BlockSpec((TQ, A_WIDTH), lambda b, i: (b * nq + i, 3)),
            pl.BlockSpec((TQ, LANES), lambda b, i: (b * nq + i, 0)),
            pl.BlockSpec((seq, A_WIDTH), lambda b, i: (b, 1)),
            pl.BlockSpec((seq, A_WIDTH), lambda b, i: (b, 2)),
            pl.BlockSpec((seq, LANES), lambda b, i: (b, 0)),
            pl.BlockSpec((2, A_HEADS, TQ, TQ), lambda b, i: (0, 0, 0, 0)),
        ],
        out_specs=pl.BlockSpec((TQ, A_WIDTH), lambda b, i: (b * nq + i, 0)),
        out_shape=jax.ShapeDtypeStruct((bsz * seq, A_WIDTH), BF16),
        scratch_shapes=[pltpu.VMEM((nq, TQ, TQ), F32), pltpu.VMEM((A_HEADS, TQ, LANES), F32),
                        pltpu.VMEM((A_HEADS, TQ, LANES), F32), pltpu.VMEM((A_HEADS // 2, TQ, LANES), F32)],
        compiler_params=_cparams(("arbitrary", "arbitrary")), name="dsa_attention",
    )(qkv, qkv, wi, qkv, qkv, kiki, near_bias)


BISECT_STEPS = 18
SUBLANES = 8


def _fold8(x, op):
    return op(x.reshape(x.shape[0] // SUBLANES, SUBLANES, x.shape[1]), axis=0)


def _dsa_kernel(qt_ref, qit_ref, wit_ref, k_ref, vt_ref, kk_ref, bias_ref, o_ref,
                s_ref, m_ref, l_ref, acc_ref, p_ref, *, ksel):
    i = pl.program_id(1)
    nt = i + 1
    tq = qt_ref.shape[1]
    kself = float(ksel)
    idx_scale = (IDX_HEADS ** -0.5) * (IDX_DIM ** -0.5)
    att_scale = A_HEAD_DIM ** -0.5

    sub = lax.broadcasted_iota(jnp.int32, (LANES, tq), 0)
    lower = sub < (LANES // 2)
    zero_b = jnp.zeros((LANES, tq), BF16)

    def head_parts(ref, scale):
        parts = []
        for h in range(A_HEADS):
            j, s = divmod(h, 2)
            blk = ref[j * LANES:(j + 1) * LANES, :]
            if scale != 1.0:
                blk = (blk.astype(F32) * scale).astype(BF16)
            parts.append(jnp.where(lower if s == 0 else jnp.logical_not(lower), blk, zero_b))
        return parts

    def key_rows(kt):
        return pl.ds(pl.multiple_of(kt * tq, tq), tq)

    qi_parts = head_parts(qit_ref, 1.0)
    wi = wit_ref[...] * idx_scale
    wrows = [wi[h:h + 1, :] for h in range(IDX_HEADS)]
    krow = lax.broadcasted_iota(jnp.int32, (tq, tq), 0)
    qcol = lax.broadcasted_iota(jnp.int32, (tq, tq), 1)
    adm_diag = (krow // CHUNK) <= (qcol // CHUNK)

    def score_tile(kt):
        kk = kk_ref[key_rows(kt), :]
        acc = jnp.zeros((tq, tq), F32)
        for h in range(IDX_HEADS):
            acc = acc + wrows[h] * jnp.maximum(_dot(kk, qi_parts[h]), 0.0)
        return acc

    def p1_body(kt, c):
        s_ref[kt] = score_tile(kt)
        return c

    lax.fori_loop(0, i, p1_body, 0)
    s_ref[i] = jnp.where(adm_diag, score_tile(i), NEG)

    def reduce_tiles(fn, inits, ops):
        def body(kt, carry):
            return fn(s_ref[kt], kt, carry)

        carry = lax.fori_loop(0, nt, body, tuple(jnp.full((SUBLANES, tq), v, F32) for v in inits))
        return [op(c, axis=0, keepdims=True) for op, c in zip(ops, carry)]

    def minmax(s, kt, carry):
        mn, mx = carry
        return (jnp.minimum(mn, _fold8(jnp.where(s > 0.5 * NEG, s, BIGF), jnp.min)),
                jnp.maximum(mx, _fold8(s, jnp.max)))

    rowmin, rowmax = reduce_tiles(minmax, (BIGF, -BIGF), (jnp.min, jnp.max))
    qpos = i * tq + lax.broadcasted_iota(jnp.int32, (1, tq), 1)
    n_adm = (qpos // CHUNK + 1) * CHUNK
    trivial = n_adm <= ksel
    lo0 = jnp.where(trivial, 0.1 * NEG, rowmin)
    hi0 = jnp.where(trivial, 0.1 * NEG, rowmax)

    def count_ge(mid):
        def cnt(s, kt, carry):
            return (carry[0] + _fold8(jnp.where(s >= mid, 1.0, 0.0), jnp.sum),)

        return reduce_tiles(cnt, (0.0,), (jnp.sum,))[0]

    def a_cond(st):
        it, _, _, clo = st
        open_rows = jnp.logical_and(clo != kself, jnp.logical_not(trivial))
        return jnp.logical_and(it < BISECT_STEPS, jnp.max(jnp.where(open_rows, 1.0, 0.0)) > 0.0)

    def a_body(st):
        it, lo, hi, clo = st
        mid = 0.5 * (lo + hi)
        c = count_ge(mid)
        ok = c >= kself
        return it + 1, jnp.where(ok, mid, lo), jnp.where(ok, hi, mid), jnp.where(ok, c, clo)

    _, lo1, hi1, clo1 = lax.while_loop(a_cond, a_body, (jnp.int32(0), lo0, hi0, n_adm.astype(F32)))
    hi1 = jnp.where(jnp.logical_or(clo1 == kself, trivial), lo1, hi1)

    def b_cond(st):
        lo, hi = st
        return jnp.max(jnp.where(lo < hi, 1.0, 0.0)) > 0.0

    def b_body(st):
        lo, hi = st
        mid = 0.5 * (lo + hi)
        mid = jnp.where(mid > lo, mid, hi)

        def cnt(s, kt, carry):
            c, vge, vlt = carry
            ge = s >= mid
            return (c + _fold8(jnp.where(ge, 1.0, 0.0), jnp.sum),
                    jnp.minimum(vge, _fold8(jnp.where(ge, s, BIGF), jnp.min)),
                    jnp.maximum(vlt, _fold8(jnp.where(ge, -BIGF, s), jnp.max)))

        c, vge, vlt = reduce_tiles(cnt, (0.0, BIGF, -BIGF), (jnp.sum, jnp.min, jnp.max))
        ok = c >= kself
        return jnp.where(ok, vge, lo), jnp.where(ok, hi, vlt)

    thr, _ = lax.while_loop(b_cond, b_body, (lo1, hi1))

    def count_ties(s, kt, carry):
        return (carry[0] + _fold8(jnp.where(s >= thr, 1.0, 0.0), jnp.sum),
                carry[1] + _fold8(jnp.where(s > thr, 1.0, 0.0), jnp.sum))

    c_ge, c_gt = reduce_tiles(count_ties, (0.0, 0.0), (jnp.sum, jnp.sum))
    need = kself - c_gt
    tied = c_ge > kself
    s_total = float(s_ref.shape[0] * tq)
    jlo0 = jnp.where(tied, 0.0, s_total - 1.0)
    jhi0 = jnp.where(tied, (nt * tq).astype(F32), s_total)
    kidx = krow.astype(F32)

    def tie_cond(st):
        jlo, jhi = st
        return jnp.max(jhi - jlo) > 1.0

    def tie_body(st):
        jlo, jhi = st
        mid = jnp.floor(0.5 * (jlo + jhi))

        def cnt(s, kt, carry):
            idx = kidx + (kt * tq).astype(F32)
            return (carry[0] + _fold8(jnp.where(jnp.logical_and(s == thr, idx < mid), 1.0, 0.0), jnp.sum),)

        (c,) = reduce_tiles(cnt, (0.0,), (jnp.sum,))
        ok = c >= need
        done = (jhi - jlo) <= 1.0
        return (jnp.where(jnp.logical_or(ok, done), jlo, mid),
                jnp.where(jnp.logical_and(ok, jnp.logical_not(done)), mid, jhi))

    _, jcut = lax.while_loop(tie_cond, tie_body, (jlo0, jhi0))

    q_parts = head_parts(qt_ref, att_scale)
    m_ref[...] = jnp.full(m_ref.shape, NEG, F32)
    l_ref[...] = jnp.zeros(l_ref.shape, F32)
    acc_ref[...] = jnp.zeros(acc_ref.shape, F32)

    def logits(kt, near):
        mb = s_ref[kt]
        for j in range(A_HEADS // 2):
            kp = k_ref[key_rows(kt), j * LANES:(j + 1) * LANES]
            for s_half in range(2):
                h = 2 * j + s_half
                lg = _dot(kp, q_parts[h]) + mb
                if near is not None:
                    lg = lg + bias_ref[near, h]
                yield h, lg

    def sweep_max(kt, near):
        s = s_ref[kt]
        idx = kidx + (kt * tq).astype(F32)
        sel = jnp.logical_or(s > thr, jnp.logical_and(s == thr, idx < jcut))
        s_ref[kt] = jnp.where(sel, 0.0, NEG)
        for h, lg in logits(kt, near):
            m_ref[h] = jnp.maximum(m_ref[h], _fold8(lg, jnp.max))

    def sweep_sum(kt, near):
        for h, lg in logits(kt, near):
            p = jnp.exp(lg - m_ref[h][:1])
            l_ref[h] += _fold8(p, jnp.sum)
            p_ref[h] = p.astype(BF16)
        for h in range(A_HEADS):
            acc_ref[h] += _dot(vt_ref[kt, h * A_HEAD_DIM:(h + 1) * A_HEAD_DIM, :], p_ref[h])

    def all_tiles(sweep):
        def body(kt, c):
            sweep(kt, None)
            return c

        lax.fori_loop(0, jnp.maximum(i - 1, 0), body, 0)

        @pl.when(i >= 1)
        def _():
            sweep(i - 1, 1)

        sweep(i, 0)

    all_tiles(sweep_max)
    for h in range(A_HEADS):
        m_ref[h] = jnp.broadcast_to(jnp.max(m_ref[h], axis=0, keepdims=True), (SUBLANES, tq))
    all_tiles(sweep_sum)

    for j in range(A_HEADS // 2):
        outs = [acc_ref[h] / jnp.sum(l_ref[h], axis=0, keepdims=True) for h in (2 * j, 2 * j + 1)]
        o_ref[:, j * LANES:(j + 1) * LANES] = jnp.concatenate(outs, axis=0).T.astype(o_ref.dtype)


def _dsa(qt, qit, wit, k, vt, kiki, near_bias, bsz, seq):
    nq = seq // TQ
    ksel = min(TOPK_MAX, seq // 4)
    qcol = lambda b, i: (0, b * nq + i)
    return pl.pallas_call(
        functools.partial(_dsa_kernel, ksel=ksel),
        grid=(bsz, nq),
        in_specs=[
            pl.BlockSpec((A_WIDTH, TQ), qcol),
            pl.BlockSpec((A_WIDTH, TQ), qcol),
            pl.BlockSpec((WI_ROWS, TQ), qcol),
            pl.BlockSpec((seq, A_WIDTH), lambda b, i: (b, 0)),
            pl.BlockSpec((nq, A_WIDTH, TQ), lambda b, i: (b, 0, 0)),
            pl.BlockSpec((seq, LANES), lambda b, i: (b, 0)),
            pl.BlockSpec((2, A_HEADS, TQ, TQ), lambda b, i: (0, 0, 0, 0)),
        ],
        out_specs=pl.BlockSpec((TQ, A_WIDTH), lambda b, i: (b * nq + i, 0)),
        out_shape=jax.ShapeDtypeStruct((bsz * seq, A_WIDTH), BF16),
        scratch_shapes=[pltpu.VMEM((nq, TQ, TQ), F32), pltpu.VMEM((A_HEADS, SUBLANES, TQ), F32),
                        pltpu.VMEM((A_HEADS, SUBLANES, TQ), F32), pltpu.VMEM((A_HEADS, A_HEAD_DIM, TQ), F32),
                        pltpu.VMEM((A_HEADS, TQ, TQ), BF16)],
        compiler_params=_cparams(("arbitrary", "arbitrary")), name="dsa_attention",
    )(qt, qit, wit, k, vt, kiki, near_bias)


RC = 64
RTS = 256


def _rwkv_kernel(r_ref, k_ref, v_ref, wa_ref, gd_ref, rp_ref, kp_ref, vp_ref, wap_ref, gdp_ref,
                 par_ref, muw_ref, mug_ref, wdec_ref, wic_ref, wg_ref, o_ref, z_ref):
    t = pl.program_id(2)
    ts = r_ref.shape[0]

    @pl.when(t == 0)
    def _():
        z_ref[...] = jnp.zeros(z_ref.shape, F32)

    row1 = lax.broadcasted_iota(jnp.int32, (ts, LANES), 0)
    first = jnp.where(t == 0, 0.0, 1.0)

    def shifted(ref, pref, mu):
        x = ref[...]
        prev = pltpu.roll(x, 1, 0)
        prev = jnp.where(row1 == 0, pref[7:8, :] * first, prev)
        return x + (prev - x) * mu

    par = par_ref[...]
    w0, a0, k_k, k_a, r_k, gn_g, gn_b = (par[n:n + 1] for n in range(7))
    r = shifted(r_ref, rp_ref, par[8:9])
    k = shifted(k_ref, kp_ref, par[9:10])
    v = shifted(v_ref, vp_ref, par[10:11])
    wa = shifted(wa_ref, wap_ref, muw_ref[...])
    gd = shifted(gd_ref, gdp_ref, mug_ref[...])

    lane = lax.broadcasted_iota(jnp.int32, (LANES, LANES), 1)
    rowl = lax.broadcasted_iota(jnp.int32, (LANES, LANES), 0)
    same_head = (lane // B_HEAD_DIM) == (rowl // B_HEAD_DIM)
    gones = jnp.where(same_head, 1.0, 0.0)
    gmean = gones * (1.0 / B_HEAD_DIM)

    z = w0 + _dot_hi(jnp.tanh(wa), wdec_ref[...])
    w_log = -(jnp.maximum(-z, 0.0) + jnp.log1p(jnp.exp(-jnp.abs(z)))) - 0.5
    e = jnp.exp(w_log)
    a = jax.nn.sigmoid(a0 + _dot_hi(wa, wic_ref[...]))
    g = _dot_hi(jax.nn.sigmoid(gd), wg_ref[...])
    kkr = k * k_k
    kk = kkr / jnp.maximum(jnp.sqrt(_dot_hi(kkr * kkr, gones)), 1e-12)
    k2 = k * (1.0 + (a - 1.0) * k_a)
    bonus = _dot_hi(r * k2 * r_k, gones) * v
    kka = kk * a

    hl = lax.broadcasted_iota(jnp.int32, (RC, LANES), 1)
    head0 = hl < B_HEAD_DIM
    strict = jnp.logical_and(same_head, rowl > lane)
    incl = jnp.logical_and(same_head, rowl >= lane)
    eye = jnp.where(lane == rowl, 1.0, 0.0)

    def stack_heads(x):
        return jnp.concatenate([jnp.where(head0, x, 0.0), jnp.where(head0, 0.0, x)], axis=0)

    def fold_heads(x):
        return x[:RC] + x[RC:]

    def twice(x):
        return jnp.concatenate([x, x], axis=0)

    rt = lax.broadcasted_iota(jnp.int32, (ts, ts), 0)
    ct = lax.broadcasted_iota(jnp.int32, (ts, ts), 1)
    same_chunk = (rt // RC) == (ct // RC)
    cum = _dot_hi(jnp.where(jnp.logical_and(same_chunk, rt >= ct), 1.0, 0.0), e)
    tot = _dot_hi(jnp.where(same_chunk, 1.0, 0.0), e)
    gro_in = jnp.exp(cum)
    rem = jnp.exp(cum - tot)
    a_hat = -kk * jnp.exp(e - cum)
    r_hat = r * jnp.exp(-cum)
    b_til = (kka * gro_in).astype(BF16)
    k_til = (k2 * gro_in).astype(BF16)
    b_bar = kka * rem
    k_bar = k2 * rem
    w_all = jnp.exp(-tot)
    v16 = v.astype(BF16)

    chunks = range(ts // RC)
    sls = [slice(c * RC, (c + 1) * RC) for c in chunks]
    a_s = [stack_heads(a_hat[sl]).astype(BF16) for sl in sls]
    r_s = [stack_heads(r_hat[sl]).astype(BF16) for sl in sls]
    v_s = [stack_heads(v[sl]).astype(BF16) for sl in sls]
    b_t2 = [twice(b_til[sl]) for sl in sls]
    k_t2 = [twice(k_til[sl]) for sl in sls]
    l_ab = [jnp.where(strict, _dot_nt(a_s[c], b_t2[c]), 0.0) for c in chunks]
    l_ak = [jnp.where(strict, _dot_nt(a_s[c], k_t2[c]), 0.0).astype(BF16) for c in chunks]
    m_rb = [jnp.where(incl, _dot_nt(r_s[c], b_t2[c]), 0.0).astype(BF16) for c in chunks]
    m_rk = [jnp.where(incl, _dot_nt(r_s[c], k_t2[c]), 0.0).astype(BF16) for c in chunks]
    inv = [eye + l_ab[c] for c in chunks]
    pw = [l_ab[c].astype(BF16) for c in chunks]
    for _ in range(5):
        pw = [_dot(pw[c], pw[c]).astype(BF16) for c in chunks]
        inv = [inv[c] + _dot(inv[c].astype(BF16), pw[c]) for c in chunks]
    lv = [_dot(l_ak[c], v_s[c]).astype(BF16) for c in chunks]
    x_av = [_dot(inv[c].astype(BF16), jnp.concatenate([a_s[c], lv[c]], axis=1)) for c in chunks]
    y_av = [_dot(m_rb[c], x_av[c].astype(BF16)) for c in chunks]
    y_kv = [_dot(m_rk[c], v_s[c]) for c in chunks]
    r_p = [r_hat[sls[c]] + fold_heads(y_av[c][:, :LANES]) for c in chunks]
    y0 = [fold_heads(y_av[c][:, LANES:] + y_kv[c]) for c in chunks]
    gh = [_dot(b_bar[sls[c]].T.astype(BF16), fold_heads(x_av[c]).astype(BF16)) for c in chunks]
    kv = [_dot(k_bar[sls[c]].T.astype(BF16), v16[sls[c]]) for c in chunks]
    g_mat = [eye * w_all[c * RC:c * RC + 1] + gones * gh[c][:, :LANES] for c in chunks]
    h_mat = [gones * (gh[c][:, LANES:] + kv[c]) for c in chunks]
    rg = [jnp.concatenate([r_p[c], g_mat[c]], axis=0).astype(BF16) for c in chunks]

    ys = []
    zc = z_ref[...]
    for c in chunks:
        yz = _dot(rg[c], zc.astype(BF16))
        ys.append(yz[:RC] + y0[c])
        zc = yz[RC:] + h_mat[c]
    z_ref[...] = zc

    y = jnp.concatenate(ys, axis=0)
    mean = _dot_hi(y, gmean)
    yc = y - mean
    var = _dot_hi(yc * yc, gmean)
    yn = yc * lax.rsqrt(var + GN_EPS) * gn_g + gn_b
    o_ref[...] = ((yn + bonus) * g).astype(o_ref.dtype)


def _rwkv(hb, par, mu_wa, mu_gd, wdec, wic, wg, bsz, seq):
    nt = seq // RTS
    npair = B_WIDTH // LANES
    cb = B_WIDTH // LANES

    def main(colblk):
        return pl.BlockSpec((RTS, LANES), lambda b, p, t: (b * nt + t, colblk(p)))

    def prev(colblk):
        return pl.BlockSpec((8, LANES), lambda b, p, t: (jnp.maximum((b * nt + t) * (RTS // 8) - 1, 0), colblk(p)))

    cols = [lambda p: p, lambda p: cb + p, lambda p: 2 * cb + p, lambda p: 3 * cb, lambda p: 3 * cb + 1]
    in_specs = [main(c) for c in cols] + [prev(c) for c in cols] + [
        pl.BlockSpec((16, LANES), lambda b, p, t: (0, p)),
        pl.BlockSpec((1, LANES), lambda b, p, t: (0, 0)),
        pl.BlockSpec((1, LANES), lambda b, p, t: (0, 0)),
        pl.BlockSpec((LANES, LANES), lambda b, p, t: (0, p)),
        pl.BlockSpec((LANES, LANES), lambda b, p, t: (0, p)),
        pl.BlockSpec((LANES, LANES), lambda b, p, t: (0, p)),
    ]
    return pl.pallas_call(
        _rwkv_kernel,
        grid=(bsz, npair, nt),
        in_specs=in_specs,
        out_specs=pl.BlockSpec((RTS, LANES), lambda b, p, t: (b * nt + t, p)),
        out_shape=jax.ShapeDtypeStruct((bsz * seq, B_WIDTH), BF16),
        scratch_shapes=[pltpu.VMEM((LANES, LANES), F32)],
        compiler_params=_cparams(("arbitrary", "arbitrary", "arbitrary")), name="rwkv7",
    )(*([hb] * 10), par, mu_wa, mu_gd, wdec, wic, wg)


def _pack_rwkv_params(mu, w0, w_decay, a0, w_iclr, k_k, k_a, r_k, gn_g, gn_b):
    bw = B_WIDTH
    zero = jnp.zeros((bw,), F32)
    rows = [w0, a0, k_k, k_a, r_k.reshape(bw), gn_g, gn_b, zero,
            mu[:bw], mu[bw:2 * bw], mu[2 * bw:3 * bw], zero, zero, zero, zero, zero]
    par = jnp.stack(rows)
    mu_wa = mu[3 * bw:3 * bw + LANES][None]
    mu_gd = mu[3 * bw + LANES:][None]
    zpad = jnp.zeros((DECAY_LORA, bw), F32)
    wdec = jnp.concatenate([w_decay, zpad], axis=0)
    wic = jnp.concatenate([zpad, w_iclr], axis=0)
    return par, mu_wa, mu_gd, wdec, wic


def _outproj_kernel(x_ref, a_ref, b_ref, wa_ref, wb_ref, g_ref, be_ref, o_ref):
    mix = _dot(a_ref[...], wa_ref[...]) + _dot(b_ref[...], wb_ref[...])
    o_ref[...] = _ln(ALPHA * x_ref[...] + mix, g_ref[...], be_ref[...])


def _outproj(x, a_out, b_out, w_out, g, b, tm=512):
    t, d = x.shape
    wa = w_out[:A_WIDTH].astype(BF16)
    wb = w_out[A_WIDTH:].astype(BF16)
    row = lambda i: (i, 0)
    const = lambda i: (0, 0)
    return pl.pallas_call(
        _outproj_kernel, grid=(t // tm,),
        in_specs=[pl.BlockSpec((tm, d), row), pl.BlockSpec((tm, A_WIDTH), row), pl.BlockSpec((tm, B_WIDTH), row),
                  pl.BlockSpec((A_WIDTH, d), const), pl.BlockSpec((B_WIDTH, d), const),
                  pl.BlockSpec((1, d), const), pl.BlockSpec((1, d), const)],
        out_specs=pl.BlockSpec((tm, d), row), out_shape=jax.ShapeDtypeStruct((t, d), F32),
        compiler_params=_cparams(("arbitrary",)), name="outproj_ln",
    )(x, a_out, b_out, wa, wb, g, b)


def _xor_partner(x, lane, bit):
    up = pltpu.roll(x, bit, 1)
    down = pltpu.roll(x, LANES - bit, 1)
    return jnp.where((lane & bit) != 0, up, down)


def _moe_gates(xf, wr, br):
    tm = xf.shape[0]
    lane = lax.broadcasted_iota(jnp.int32, (tm, LANES), 1)
    valid = lane < N_EXPERTS
    scores = jax.nn.sigmoid(_dot_hi(xf, wr))
    sel = jnp.where(valid, scores + br, NEG)
    p1 = _xor_partner(sel, lane, 1)
    hi1, lo1 = jnp.maximum(sel, p1), jnp.minimum(sel, p1)
    hi2, lo2 = _xor_partner(hi1, lane, 2), _xor_partner(lo1, lane, 2)
    gscore = jnp.maximum(hi1, hi2) + jnp.maximum(jnp.minimum(hi1, hi2), jnp.maximum(lo1, lo2))
    gscore = jnp.where(valid, gscore, NEG)
    lanef = lane.astype(F32)
    gbest = jnp.max(gscore, axis=1, keepdims=True)
    first = jnp.min(jnp.where(gscore == gbest, lanef, float(LANES)), axis=1, keepdims=True)
    in_group = jnp.floor(lanef * (1.0 / EXPERTS_PER_GROUP)) == jnp.floor(first * (1.0 / EXPERTS_PER_GROUP))
    masked = jnp.where(in_group, sel, NEG)
    m1 = jnp.max(masked, axis=1, keepdims=True)
    i1 = jnp.min(jnp.where(masked == m1, lanef, float(LANES)), axis=1, keepdims=True)
    pick1 = lanef == i1
    masked2 = jnp.where(pick1, NEG, masked)
    m2 = jnp.max(masked2, axis=1, keepdims=True)
    i2 = jnp.min(jnp.where(masked2 == m2, lanef, float(LANES)), axis=1, keepdims=True)
    pick2 = lanef == i2
    s1 = jnp.sum(jnp.where(pick1, scores, 0.0), axis=1, keepdims=True)
    s2 = jnp.sum(jnp.where(pick2, scores, 0.0), axis=1, keepdims=True)
    tot = s1 + s2
    return jnp.where(pick1, s1 / tot, 0.0) + jnp.where(pick2, s2 / tot, 0.0)


def _moe_kernel(x_ref, wr_ref, br_ref, wgu_ref, wd_ref, g_ref, b_ref, o_ref, gate_ref, acc_ref, xb_ref):
    e = pl.program_id(1)

    @pl.when(e == 0)
    def _():
        xf = x_ref[...]
        gate_ref[...] = _moe_gates(xf, wr_ref[...], br_ref[...])
        xb_ref[...] = xf.astype(BF16)
        acc_ref[...] = jnp.zeros(acc_ref.shape, F32)

    gates = gate_ref[...]
    lane = lax.broadcasted_iota(jnp.int32, gates.shape, 1)
    gcol = jnp.sum(jnp.where(lane == e, gates, 0.0), axis=1, keepdims=True)
    hu = _dot(xb_ref[...], wgu_ref[0])
    hg, up = hu[:, :D_EXPERT], hu[:, D_EXPERT:]
    hid = (hg * jax.nn.sigmoid(hg)) * up * gcol
    acc_ref[...] += _dot(hid.astype(BF16), wd_ref[0])

    @pl.when(e == pl.num_programs(1) - 1)
    def _():
        o_ref[...] = _ln(ALPHA * x_ref[...] + acc_ref[...], g_ref[...], b_ref[...])


def _moe(x, w_router, b_router, w_gate, w_up, w_down, g, b, tm=1024):
    t, d = x.shape
    ne = w_gate.shape[0]
    wgu = jnp.concatenate([w_gate, w_up], axis=2).astype(BF16)
    wd = w_down.astype(BF16)
    wr = jnp.pad(w_router, ((0, 0), (0, LANES - ne)))
    br = jnp.pad(b_router, (0, LANES - ne))[None]
    row = lambda i, e: (i, 0)
    const = lambda i, e: (0, 0)
    return pl.pallas_call(
        _moe_kernel, grid=(t // tm, ne),
        in_specs=[pl.BlockSpec((tm, d), row), pl.BlockSpec((d, LANES), const), pl.BlockSpec((1, LANES), const),
                  pl.BlockSpec((1, d, 2 * D_EXPERT), lambda i, e: (e, 0, 0)),
                  pl.BlockSpec((1, D_EXPERT, d), lambda i, e: (e, 0, 0)),
                  pl.BlockSpec((1, d), const), pl.BlockSpec((1, d), const)],
        out_specs=pl.BlockSpec((tm, d), row), out_shape=jax.ShapeDtypeStruct((t, d), F32),
        scratch_shapes=[pltpu.VMEM((tm, LANES), F32), pltpu.VMEM((tm, d), F32), pltpu.VMEM((tm, d), BF16)],
        compiler_params=_cparams(("arbitrary", "arbitrary")), name="moe_ln",
    )(x, wr, br, wgu, wd, g, b)


def _ple_kernel(x_ref, p_ref, wg_ref, wp_ref, g_ref, b_ref, o_ref):
    x = x_ref[...]
    gate = jax.nn.sigmoid(_dot(x.astype(BF16), wg_ref[...]))
    pe = _dot(p_ref[...].astype(BF16), wp_ref[...])
    o_ref[...] = _ln(ALPHA * x + gate * pe, g_ref[...], b_ref[...])


def _ple(x, p, w_gate, w_ple, g, b, tm=512):
    t, d = x.shape
    row = lambda i: (i, 0)
    const = lambda i: (0, 0)
    return pl.pallas_call(
        _ple_kernel, grid=(t // tm,),
        in_specs=[pl.BlockSpec((tm, d), row), pl.BlockSpec((tm, PLE_DIM), row),
                  pl.BlockSpec((d, d), const), pl.BlockSpec((PLE_DIM, d), const),
                  pl.BlockSpec((1, d), const), pl.BlockSpec((1, d), const)],
        out_specs=pl.BlockSpec((tm, d), row), out_shape=jax.ShapeDtypeStruct((t, d), F32),
        compiler_params=_cparams(("arbitrary",)), name="ple_ln",
    )(x, p, w_gate.astype(BF16), w_ple.astype(BF16), g, b)


def kernel(x, p, ln_in_g, ln_in_b, w_in, w_out, mu_shift, w0, w_decay, a0, w_iclr, w_gate_up, k_k, k_a, r_k,
           gn_g, gn_b, rel_bias, w_router, b_router, w_exp_gate, w_exp_up, w_exp_down, w_ple, w_ple_gate,
           ln_mix_g, ln_mix_b, ln_ffn_g, ln_ffn_b, ln_ple_g, ln_ple_b):
    bsz, seq, d = x.shape
    t = bsz * seq
    depth = w_in.shape[0]
    xf = x.reshape(t, d)
    near_bias = _near_bias(rel_bias)
    for i in range(depth):
        w_rm, w_fm = _pack_w_in(w_in[i])
        outs = _inproj(xf, ln_in_g[None], ln_in_b[None], w_rm, w_fm, apply_ln=(i == 0))
        k_rm, kiki, hb, qt, qit, vt, wit = outs[:7]
        if i == 0:
            xf = outs[7]
        a_out = _dsa(qt, qit, wit, k_rm, vt, kiki, near_bias, bsz, seq)
        par, mu_wa, mu_gd, wdec, wic = _pack_rwkv_params(mu_shift[i], w0[i], w_decay[i], a0[i], w_iclr[i],
                                                         k_k[i], k_a[i], r_k[i], gn_g[i], gn_b[i])
        b_out = _rwkv(hb, par, mu_wa, mu_gd, wdec, wic, w_gate_up[i], bsz, seq)
        xf = _outproj(xf, a_out, b_out, w_out[i], ln_mix_g[i][None], ln_mix_b[i][None])
        xf = _moe(xf, w_router, b_router, w_exp_gate[i], w_exp_up[i], w_exp_down[i],
                  ln_ffn_g[i][None], ln_ffn_b[i][None])
        xf = _ple(xf, p[i].reshape(t, PLE_DIM), w_ple_gate[i], w_ple[i], ln_ple_g[i][None], ln_ple_b[i][None])
    return xf.reshape(bsz, seq, d)
```

```python
import functools
import math

import numpy as np
import jax
import jax.numpy as jnp
from jax import lax
from jax.experimental import pallas as pl
from jax.experimental.pallas import tpu as pltpu

D_MODEL = 1024
DEPTH = 2
CHUNK = 64
A_HEAD_DIM = 64
A_WIDTH = D_MODEL // 2
A_HEADS = A_WIDTH // A_HEAD_DIM
IDX_HEADS = 8
IDX_DIM = 64
TOPK_MAX = 256
N_BUCKETS = 32
MAX_DISTANCE = 128
B_HEAD_DIM = 64
B_WIDTH = D_MODEL - A_WIDTH
B_HEADS = B_WIDTH // B_HEAD_DIM
DECAY_LORA = 64
ICLR_LORA = 64
GATE_LORA = 128
A_SIZES = (A_WIDTH, A_WIDTH, A_WIDTH, IDX_HEADS * IDX_DIM, IDX_DIM, IDX_HEADS)
B_SIZES = (B_WIDTH, B_WIDTH, B_WIDTH, DECAY_LORA, ICLR_LORA, GATE_LORA)
A_COLS = sum(A_SIZES)
B_COLS = sum(B_SIZES)
N_EXPERTS = 16
N_GROUPS = 4
EXPERTS_PER_GROUP = N_EXPERTS // N_GROUPS
D_EXPERT = 256
PLE_DIM = 256
ALPHA = (2 * DEPTH) ** 0.25
LN_EPS = 1e-5
GN_EPS = 64e-5
NEG = -1e30

LANES = 128
VMEM_LIMIT = 56 * 1024 * 1024
F32 = jnp.float32
BF16 = jnp.bfloat16
HI = lax.Precision.HIGHEST
NT_DIMS = (((1,), (1,)), ((), ()))


def _cparams(sem):
    return pltpu.CompilerParams(dimension_semantics=sem, vmem_limit_bytes=VMEM_LIMIT)


def _ln(x, g, b):
    mu = jnp.mean(x, axis=-1, keepdims=True)
    xc = x - mu
    var = jnp.mean(xc * xc, axis=-1, keepdims=True)
    return xc * lax.rsqrt(var + LN_EPS) * g + b


def _dot(a, b):
    return jnp.dot(a, b, preferred_element_type=F32)


def _dot_hi(a, b):
    return jnp.dot(a, b, preferred_element_type=F32, precision=HI)


def _dot_nt(a, b):
    return lax.dot_general(a, b, NT_DIMS, preferred_element_type=F32)


def _dot_nt_hi(a, b):
    return lax.dot_general(a, b, NT_DIMS, preferred_element_type=F32, precision=HI)


def _split2(x):
    hi = x.astype(BF16)
    return hi, (x - hi.astype(F32)).astype(BF16)


def _split3(x):
    hi = x.astype(BF16)
    rest = x - hi.astype(F32)
    mid = rest.astype(BF16)
    return hi, mid, (rest - mid.astype(F32)).astype(BF16)


IN_NCHUNK = 512


WI_ROWS = 16


def _inproj_kernel(x_ref, g_ref, b_ref, wr_ref, wf_ref, *out_refs, apply_ln):
    x = x_ref[...]
    if apply_ln:
        x = _ln(x, g_ref[...], b_ref[...])
        out_refs[7][...] = x
    xb = x.astype(BF16)
    k_ref, kk_ref, hb_ref, qt_ref, qit_ref, vt_ref, wit_ref = out_refs[:7]
    aw = A_WIDTH
    k_ref[...] = _dot(xb, wr_ref[:, :aw]).astype(BF16)
    kk_ref[...] = _dot(xb, wr_ref[:, aw:aw + LANES]).astype(BF16)
    nb = hb_ref.shape[1]
    for c0 in range(0, nb, IN_NCHUNK):
        c1 = min(c0 + IN_NCHUNK, nb)
        hb_ref[:, c0:c1] = _dot(xb, wr_ref[:, aw + LANES + c0:aw + LANES + c1])
    qt_ref[...] = _dot_nt(wf_ref[:aw], xb).astype(BF16)
    qit_ref[...] = _dot_nt(wf_ref[aw:2 * aw], xb).astype(BF16)
    vt = _dot_nt(wf_ref[2 * aw:3 * aw], xb).astype(BF16)
    for n in range(vt_ref.shape[0]):
        vt_ref[n] = vt[:, n * TQ:(n + 1) * TQ]
    wit_ref[...] = _dot_nt(wf_ref[3 * aw:], xb)


def _inproj(x, g, b, w_rm, w_fm, apply_ln, tm=512):
    t, d = x.shape
    aw = A_WIDTH
    nb = w_rm.shape[1] - aw - LANES
    row = lambda i: (i, 0)
    col = lambda i: (0, i)
    const = lambda i: (0, 0)
    out_shape = [jax.ShapeDtypeStruct((t, aw), BF16), jax.ShapeDtypeStruct((t, LANES), BF16),
                 jax.ShapeDtypeStruct((t, nb), F32), jax.ShapeDtypeStruct((aw, t), BF16),
                 jax.ShapeDtypeStruct((aw, t), BF16), jax.ShapeDtypeStruct((t // TQ, aw, TQ), BF16),
                 jax.ShapeDtypeStruct((WI_ROWS, t), F32)]
    out_specs = [pl.BlockSpec((tm, aw), row), pl.BlockSpec((tm, LANES), row), pl.BlockSpec((tm, nb), row),
                 pl.BlockSpec((aw, tm), col), pl.BlockSpec((aw, tm), col),
                 pl.BlockSpec((tm // TQ, aw, TQ), lambda i: (i, 0, 0)), pl.BlockSpec((WI_ROWS, tm), col)]
    if apply_ln:
        out_shape.append(jax.ShapeDtypeStruct((t, d), F32))
        out_specs.append(pl.BlockSpec((tm, d), row))
    return pl.pallas_call(
        functools.partial(_inproj_kernel, apply_ln=apply_ln),
        grid=(t // tm,),
        in_specs=[pl.BlockSpec((tm, d), row), pl.BlockSpec((1, d), const), pl.BlockSpec((1, d), const),
                  pl.BlockSpec(w_rm.shape, const), pl.BlockSpec(w_fm.shape, const)],
        out_specs=out_specs, out_shape=out_shape,
        compiler_params=_cparams(("arbitrary",)), name="inproj_ln" if apply_ln else "inproj",
    )(x, g, b, w_rm, w_fm)


def _pack_w_in(w):
    d = w.shape[0]
    aw = A_WIDTH
    cuts = np.cumsum(A_SIZES)
    wq, wk, wv, wqi = w[:, :aw], w[:, aw:2 * aw], w[:, 2 * aw:3 * aw], w[:, cuts[2]:cuts[3]]
    wki = w[:, cuts[3]:cuts[4]]
    wwi = w[:, cuts[4]:cuts[5]]
    w_rm = jnp.concatenate([wk, wki, wki, w[:, A_COLS:]], axis=1).astype(BF16)
    pad = jnp.zeros((d, WI_ROWS - IDX_HEADS), w.dtype)
    w_fm = jnp.concatenate([wq, wqi, wv, wwi, pad], axis=1).T.astype(BF16)
    return w_rm, w_fm


TQ = 256
ROW_GROUP = 32
BIGF = 3.0e38


def _t5_bucket_np(rel):
    nb = N_BUCKETS // 2
    max_exact = nb // 2
    ret = np.where(rel > 0, nb, 0)
    n = np.abs(rel)
    nf = np.maximum(n, 1).astype(np.float64)
    large = max_exact + np.floor(np.log(nf / max_exact) / math.log(MAX_DISTANCE / max_exact)
                                 * (nb - max_exact) + 1e-9).astype(np.int64)
    large = np.minimum(large, nb - 1)
    return ret + np.where(n < max_exact, n, large)


def _near_bias(rel_bias):
    qpos = np.arange(TQ)[None, :]
    kpos = np.arange(TQ)[:, None]
    bk = np.stack([_t5_bucket_np(kpos - qpos), _t5_bucket_np(kpos - TQ - qpos)])
    far = N_BUCKETS // 2 - 1
    tab = rel_bias - rel_bias[far][None, :]
    onehot = jnp.asarray(bk[..., None] == np.arange(N_BUCKETS)).astype(F32)
    return jnp.einsum('nkqb,bh->nhkq', onehot, tab.astype(F32), precision=HI)


BISECT_STEPS = 12
SUBLANES = 8


def _fold8(x, op):
    return op(x.reshape(x.shape[0] // SUBLANES, SUBLANES, x.shape[1]), axis=0)


def _dsa_kernel(qt_ref, qit_ref, wit_ref, k_ref, vt_ref, kk_ref, bias_ref, o_ref,
                s_ref, m_ref, l_ref, acc_ref, p_ref, *, ksel):
    i = pl.program_id(1)
    nt = i + 1
    tq = qt_ref.shape[1]
    kself = float(ksel)
    idx_scale = (IDX_HEADS ** -0.5) * (IDX_DIM ** -0.5)
    att_scale = A_HEAD_DIM ** -0.5

    sub = lax.broadcasted_iota(jnp.int32, (LANES, tq), 0)
    lower = sub < (LANES // 2)
    zero_b = jnp.zeros((LANES, tq), BF16)

    def head_parts(ref, scale):
        parts = []
        for h in range(A_HEADS):
            j, s = divmod(h, 2)
            blk = ref[j * LANES:(j + 1) * LANES, :]
            if scale != 1.0:
                blk = (blk.astype(F32) * scale).astype(BF16)
            parts.append(jnp.where(lower if s == 0 else jnp.logical_not(lower), blk, zero_b))
        return parts

    def key_rows(kt):
        return pl.ds(pl.multiple_of(kt * tq, tq), tq)

    qi_parts = head_parts(qit_ref, 1.0)
    wi = wit_ref[...] * idx_scale
    wrows = [wi[h:h + 1, :] for h in range(IDX_HEADS)]
    krow = lax.broadcasted_iota(jnp.int32, (tq, tq), 0)
    qcol = lax.broadcasted_iota(jnp.int32, (tq, tq), 1)
    adm_diag = (krow // CHUNK) <= (qcol // CHUNK)

    def score_tile(kt):
        kk = kk_ref[key_rows(kt), :]
        acc = jnp.zeros((tq, tq), F32)
        for h in range(IDX_HEADS):
            acc = acc + wrows[h] * jnp.maximum(_dot(kk, qi_parts[h]), 0.0)
        return acc

    def p1_body(kt, c):
        s_ref[kt] = score_tile(kt)
        return c

    lax.fori_loop(0, i, p1_body, 0)
    s_ref[i] = jnp.where(adm_diag, score_tile(i), NEG)

    def reduce_tiles(fn, inits, ops):
        def body(kt, carry):
            return fn(s_ref[kt], kt, carry)

        carry = lax.fori_loop(0, nt, body, tuple(jnp.full((SUBLANES, tq), v, F32) for v in inits))
        return [op(c, axis=0, keepdims=True) for op, c in zip(ops, carry)]

    def minmax(s, kt, carry):
        mn, mx = carry
        return (jnp.minimum(mn, _fold8(jnp.where(s > 0.5 * NEG, s, BIGF), jnp.min)),
                jnp.maximum(mx, _fold8(s, jnp.max)))

    rowmin, rowmax = reduce_tiles(minmax, (BIGF, -BIGF), (jnp.min, jnp.max))
    qpos = i * tq + lax.broadcasted_iota(jnp.int32, (1, tq), 1)
    n_adm = (qpos // CHUNK + 1) * CHUNK
    trivial = n_adm <= ksel
    lo0 = jnp.where(trivial, 0.1 * NEG, rowmin)
    hi0 = jnp.where(trivial, 0.1 * NEG, rowmax)

    def count_ge(mid):
        def cnt(s, kt, carry):
            return (carry[0] + _fold8(jnp.where(s >= mid, 1.0, 0.0), jnp.sum),)

        return reduce_tiles(cnt, (0.0,), (jnp.sum,))[0]

    def a_body(_, st):
        lo, hi, clo = st
        mid = 0.5 * (lo + hi)
        c = count_ge(mid)
        ok = c >= kself
        return jnp.where(ok, mid, lo), jnp.where(ok, hi, mid), jnp.where(ok, c, clo)

    lo1, hi1, clo1 = lax.fori_loop(0, BISECT_STEPS, a_body, (lo0, hi0, n_adm.astype(F32)))
    hi1 = jnp.where(jnp.logical_or(clo1 == kself, trivial), lo1, hi1)

    def b_cond(st):
        lo, hi = st
        return jnp.max(jnp.where(lo < hi, 1.0, 0.0)) > 0.0

    def b_body(st):
        lo, hi = st
        mid = 0.5 * (lo + hi)
        mid = jnp.where(mid > lo, mid, hi)

        def cnt(s, kt, carry):
            c, vge, vlt = carry
            ge = s >= mid
            return (c + _fold8(jnp.where(ge, 1.0, 0.0), jnp.sum),
                    jnp.minimum(vge, _fold8(jnp.where(ge, s, BIGF), jnp.min)),
                    jnp.maximum(vlt, _fold8(jnp.where(ge, -BIGF, s), jnp.max)))

        c, vge, vlt = reduce_tiles(cnt, (0.0, BIGF, -BIGF), (jnp.sum, jnp.min, jnp.max))
        ok = c >= kself
        return jnp.where(ok, vge, lo), jnp.where(ok, hi, vlt)

    thr, _ = lax.while_loop(b_cond, b_body, (lo1, hi1))

    def count_ties(s, kt, carry):
        return (carry[0] + _fold8(jnp.where(s >= thr, 1.0, 0.0), jnp.sum),
                carry[1] + _fold8(jnp.where(s > thr, 1.0, 0.0), jnp.sum))

    c_ge, c_gt = reduce_tiles(count_ties, (0.0, 0.0), (jnp.sum, jnp.sum))
    need = kself - c_gt
    tied = c_ge > kself
    s_total = float(s_ref.shape[0] * tq)
    jlo0 = jnp.where(tied, 0.0, s_total - 1.0)
    jhi0 = jnp.where(tied, (nt * tq).astype(F32), s_total)
    kidx = krow.astype(F32)

    def tie_cond(st):
        jlo, jhi = st
        return jnp.max(jhi - jlo) > 1.0

    def tie_body(st):
        jlo, jhi = st
        mid = jnp.floor(0.5 * (jlo + jhi))

        def cnt(s, kt, carry):
            idx = kidx + (kt * tq).astype(F32)
            return (carry[0] + _fold8(jnp.where(jnp.logical_and(s == thr, idx < mid), 1.0, 0.0), jnp.sum),)

        (c,) = reduce_tiles(cnt, (0.0,), (jnp.sum,))
        ok = c >= need
        done = (jhi - jlo) <= 1.0
        return (jnp.where(jnp.logical_or(ok, done), jlo, mid),
                jnp.where(jnp.logical_and(ok, jnp.logical_not(done)), mid, jhi))

    _, jcut = lax.while_loop(tie_cond, tie_body, (jlo0, jhi0))

    q_parts = head_parts(qt_ref, att_scale)
    m_ref[...] = jnp.full(m_ref.shape, NEG, F32)
    l_ref[...] = jnp.zeros(l_ref.shape, F32)
    acc_ref[...] = jnp.zeros(acc_ref.shape, F32)

    def logits(kt, near):
        mb = s_ref[kt]
        for j in range(A_HEADS // 2):
            kp = k_ref[key_rows(kt), j * LANES:(j + 1) * LANES]
            for s_half in range(2):
                h = 2 * j + s_half
                lg = _dot(kp, q_parts[h]) + mb
                if near is not None:
                    lg = lg + bias_ref[near, h]
                yield h, lg

    def sweep_max(kt, near):
        s = s_ref[kt]
        idx = kidx + (kt * tq).astype(F32)
        sel = jnp.logical_or(s > thr, jnp.logical_and(s == thr, idx < jcut))
        s_ref[kt] = jnp.where(sel, 0.0, NEG)
        for h, lg in logits(kt, near):
            m_ref[h] = jnp.maximum(m_ref[h], _fold8(lg, jnp.max))

    def sweep_sum(kt, near):
        for h, lg in logits(kt, near):
            p = jnp.exp(lg - m_ref[h][:1])
            l_ref[h] += _fold8(p, jnp.sum)
            p_ref[h] = p.astype(BF16)
        for h in range(A_HEADS):
            acc_ref[h] += _dot(vt_ref[kt, h * A_HEAD_DIM:(h + 1) * A_HEAD_DIM, :], p_ref[h])

    def all_tiles(sweep):
        def body(kt, c):
            sweep(kt, None)
            return c

        lax.fori_loop(0, jnp.maximum(i - 1, 0), body, 0)

        @pl.when(i >= 1)
        def _():
            sweep(i - 1, 1)

        sweep(i, 0)

    all_tiles(sweep_max)
    for h in range(A_HEADS):
        m_ref[h] = jnp.broadcast_to(jnp.max(m_ref[h], axis=0, keepdims=True), (SUBLANES, tq))
    all_tiles(sweep_sum)

    for j in range(A_HEADS // 2):
        outs = [acc_ref[h] / jnp.sum(l_ref[h], axis=0, keepdims=True) for h in (2 * j, 2 * j + 1)]
        o_ref[:, j * LANES:(j + 1) * LANES] = jnp.concatenate(outs, axis=0).T.astype(o_ref.dtype)


def _dsa(qt, qit, wit, k, vt, kiki, near_bias, bsz, seq):
    nq = seq // TQ
    ksel = min(TOPK_MAX, seq // 4)
    qcol = lambda b, i: (0, b * nq + i)
    return pl.pallas_call(
        functools.partial(_dsa_kernel, ksel=ksel),
        grid=(bsz, nq),
        in_specs=[
            pl.BlockSpec((A_WIDTH, TQ), qcol),
            pl.BlockSpec((A_WIDTH, TQ), qcol),
            pl.BlockSpec((WI_ROWS, TQ), qcol),
            pl.BlockSpec((seq, A_WIDTH), lambda b, i: (b, 0)),
            pl.BlockSpec((nq, A_WIDTH, TQ), lambda b, i: (b, 0, 0)),
            pl.BlockSpec((seq, LANES), lambda b, i: (b, 0)),
            pl.BlockSpec((2, A_HEADS, TQ, TQ), lambda b, i: (0, 0, 0, 0)),
        ],
        out_specs=pl.BlockSpec((TQ, A_WIDTH), lambda b, i: (b * nq + i, 0)),
        out_shape=jax.ShapeDtypeStruct((bsz * seq, A_WIDTH), BF16),
        scratch_shapes=[pltpu.VMEM((nq, TQ, TQ), F32), pltpu.VMEM((A_HEADS, SUBLANES, TQ), F32),
                        pltpu.VMEM((A_HEADS, SUBLANES, TQ), F32), pltpu.VMEM((A_HEADS, A_HEAD_DIM, TQ), F32),
                        pltpu.VMEM((A_HEADS, TQ, TQ), BF16)],
        compiler_params=_cparams(("arbitrary", "arbitrary")), name="dsa_attention",
    )(qt, qit, wit, k, vt, kiki, near_bias)


RC = 64
RTS = 256


def _rwkv_kernel(r_ref, k_ref, v_ref, wa_ref, gd_ref, rp_ref, kp_ref, vp_ref, wap_ref, gdp_ref,
                 par_ref, muw_ref, mug_ref, wdec_ref, wic_ref, wg_ref, o_ref, z_ref):
    t = pl.program_id(2)
    ts = r_ref.shape[0]

    @pl.when(t == 0)
    def _():
        z_ref[...] = jnp.zeros(z_ref.shape, F32)

    row1 = lax.broadcasted_iota(jnp.int32, (ts, LANES), 0)
    first = jnp.where(t == 0, 0.0, 1.0)

    def shifted(ref, pref, mu):
        x = ref[...]
        prev = pltpu.roll(x, 1, 0)
        prev = jnp.where(row1 == 0, pref[7:8, :] * first, prev)
        return x + (prev - x) * mu

    par = par_ref[...]
    w0, a0, k_k, k_a, r_k, gn_g, gn_b = (par[n:n + 1] for n in range(7))
    r = shifted(r_ref, rp_ref, par[8:9])
    k = shifted(k_ref, kp_ref, par[9:10])
    v = shifted(v_ref, vp_ref, par[10:11])
    wa = shifted(wa_ref, wap_ref, muw_ref[...])
    gd = shifted(gd_ref, gdp_ref, mug_ref[...])

    lane = lax.broadcasted_iota(jnp.int32, (LANES, LANES), 1)
    rowl = lax.broadcasted_iota(jnp.int32, (LANES, LANES), 0)
    same_head = (lane // B_HEAD_DIM) == (rowl // B_HEAD_DIM)
    gones = jnp.where(same_head, 1.0, 0.0)
    gones16 = gones.astype(BF16)

    def head_sum(x):
        hi, lo = _split2(x)
        return _dot(hi, gones16) + _dot(lo, gones16)

    th_hi, th_lo = _split2(jnp.tanh(wa))
    wd_hi, wd_lo = _split2(wdec_ref[...])
    z = w0 + (_dot(th_hi, wd_hi) + _dot(th_hi, wd_lo) + _dot(th_lo, wd_hi))
    w_log = -(jnp.maximum(-z, 0.0) + jnp.log1p(jnp.exp(-jnp.abs(z)))) - 0.5
    e = jnp.exp(w_log)
    a = jax.nn.sigmoid(a0 + _dot(wa.astype(BF16), wic_ref[...].astype(BF16)))
    g = _dot(jax.nn.sigmoid(gd).astype(BF16), wg_ref[...].astype(BF16))
    kkr = k * k_k
    kk = kkr / jnp.maximum(jnp.sqrt(head_sum(kkr * kkr)), 1e-12)
    k2 = k * (1.0 + (a - 1.0) * k_a)
    bonus = head_sum(r * k2 * r_k) * v
    kka = kk * a

    hl = lax.broadcasted_iota(jnp.int32, (RC, LANES), 1)
    head0 = hl < B_HEAD_DIM
    strict = jnp.logical_and(same_head, rowl > lane)
    incl = jnp.logical_and(same_head, rowl >= lane)
    eye = jnp.where(lane == rowl, 1.0, 0.0)

    def stack_heads(x):
        return jnp.concatenate([jnp.where(head0, x, 0.0), jnp.where(head0, 0.0, x)], axis=0)

    def fold_heads(x):
        return x[:RC] + x[RC:]

    def twice(x):
        return jnp.concatenate([x, x], axis=0)

    rt = lax.broadcasted_iota(jnp.int32, (ts, ts), 0)
    ct = lax.broadcasted_iota(jnp.int32, (ts, ts), 1)
    same_chunk = (rt // RC) == (ct // RC)
    tri16 = jnp.where(jnp.logical_and(same_chunk, rt >= ct), 1.0, 0.0).astype(BF16)
    e_hi, e_mid, e_lo = _split3(e)
    cum = _dot(tri16, e_hi) + _dot(tri16, e_mid) + _dot(tri16, e_lo)
    chunks = range(ts // RC)
    sls = [slice(c * RC, (c + 1) * RC) for c in chunks]
    tot = [cum[(c + 1) * RC - 1:(c + 1) * RC] for c in chunks]
    gro_in = jnp.exp(cum)
    a_hat = -kk * jnp.exp(e - cum)
    r_hat = r * jnp.exp(-cum)
    b_til = (kka * gro_in).astype(BF16)
    k_til = (k2 * gro_in).astype(BF16)
    rem = [jnp.exp(cum[sls[c]] - tot[c]) for c in chunks]
    v16 = v.astype(BF16)

    a_s = [stack_heads(a_hat[sl]).astype(BF16) for sl in sls]
    r_s = [stack_heads(r_hat[sl]).astype(BF16) for sl in sls]
    v_s = [stack_heads(v[sl]).astype(BF16) for sl in sls]
    ar = [jnp.concatenate([a_s[c], r_s[c]], axis=0) for c in chunks]
    bk = [jnp.concatenate([twice(b_til[sl]), twice(k_til[sl])], axis=0) for sl in sls]
    m_all = [_dot_nt(ar[c], bk[c]) for c in chunks]
    l_ab = [jnp.where(strict, m_all[c][:LANES, :LANES], 0.0) for c in chunks]
    l_ak = [jnp.where(strict, m_all[c][:LANES, LANES:], 0.0).astype(BF16) for c in chunks]
    m_rb = [jnp.where(incl, m_all[c][LANES:, :LANES], 0.0).astype(BF16) for c in chunks]
    m_rk = [jnp.where(incl, m_all[c][LANES:, LANES:], 0.0).astype(BF16) for c in chunks]
    inv = [eye + l_ab[c] for c in chunks]
    pw = [l_ab[c].astype(BF16) for c in chunks]
    for _ in range(5):
        pw = [_dot(pw[c], pw[c]).astype(BF16) for c in chunks]
        inv = [inv[c] + _dot(inv[c].astype(BF16), pw[c]) for c in chunks]
    lv = [_dot(l_ak[c], v_s[c]).astype(BF16) for c in chunks]
    x_av = [_dot(inv[c].astype(BF16), jnp.concatenate([a_s[c], lv[c]], axis=1)) for c in chunks]
    y_av = [_dot(m_rb[c], x_av[c].astype(BF16)) for c in chunks]
    y_kv = [_dot(m_rk[c], v_s[c]) for c in chunks]
    r_p = [r_hat[sls[c]] + fold_heads(y_av[c][:, :LANES]) for c in chunks]
    y0 = [fold_heads(y_av[c][:, LANES:] + y_kv[c]) for c in chunks]
    gh = [_dot((kka[sls[c]] * rem[c]).T.astype(BF16), fold_heads(x_av[c]).astype(BF16)) for c in chunks]
    kv = [_dot((k2[sls[c]] * rem[c]).T.astype(BF16), v16[sls[c]]) for c in chunks]
    g_mat = [eye * jnp.exp(-tot[c]) + gones * gh[c][:, :LANES] for c in chunks]
    h_mat = [gones * (gh[c][:, LANES:] + kv[c]) for c in chunks]
    rg = [jnp.concatenate([r_p[c], g_mat[c]], axis=0).astype(BF16) for c in chunks]

    ys = []
    zc = z_ref[...]
    for c in chunks:
        yz = _dot(rg[c], zc.astype(BF16))
        ys.append(yz[:RC] + y0[c])
        zc = yz[RC:] + h_mat[c]
    z_ref[...] = zc

    y = jnp.concatenate(ys, axis=0)
    mean = head_sum(y) * (1.0 / B_HEAD_DIM)
    yc = y - mean
    var = head_sum(yc * yc) * (1.0 / B_HEAD_DIM)
    yn = yc * lax.rsqrt(var + GN_EPS) * gn_g + gn_b
    o_ref[...] = ((yn + bonus) * g).astype(o_ref.dtype)


def _rwkv(hb, par, mu_wa, mu_gd, wdec, wic, wg, bsz, seq):
    nt = seq // RTS
    npair = B_WIDTH // LANES
    cb = B_WIDTH // LANES

    def main(colblk):
        return pl.BlockSpec((RTS, LANES), lambda b, p, t: (b * nt + t, colblk(p)))

    def prev(colblk):
        return pl.BlockSpec((8, LANES), lambda b, p, t: (jnp.maximum((b * nt + t) * (RTS // 8) - 1, 0), colblk(p)))

    cols = [lambda p: p, lambda p: cb + p, lambda p: 2 * cb + p, lambda p: 3 * cb, lambda p: 3 * cb + 1]
    in_specs = [main(c) for c in cols] + [prev(c) for c in cols] + [
        pl.BlockSpec((16, LANES), lambda b, p, t: (0, p)),
        pl.BlockSpec((1, LANES), lambda b, p, t: (0, 0)),
        pl.BlockSpec((1, LANES), lambda b, p, t: (0, 0)),
        pl.BlockSpec((LANES, LANES), lambda b, p, t: (0, p)),
        pl.BlockSpec((LANES, LANES), lambda b, p, t: (0, p)),
        pl.BlockSpec((LANES, LANES), lambda b, p, t: (0, p)),
    ]
    return pl.pallas_call(
        _rwkv_kernel,
        grid=(bsz, npair, nt),
        in_specs=in_specs,
        out_specs=pl.BlockSpec((RTS, LANES), lambda b, p, t: (b * nt + t, p)),
        out_shape=jax.ShapeDtypeStruct((bsz * seq, B_WIDTH), BF16),
        scratch_shapes=[pltpu.VMEM((LANES, LANES), F32)],
        compiler_params=_cparams(("arbitrary", "arbitrary", "arbitrary")), name="rwkv7",
    )(*([hb] * 10), par, mu_wa, mu_gd, wdec, wic, wg)


RNP = 4


def _rwkv_kernel(r_ref, k_ref, v_ref, wa_ref, gd_ref, rp_ref, kp_ref, vp_ref, wap_ref, gdp_ref,
                 par_ref, muw_ref, mug_ref, wdec_ref, wic_ref, wg_ref, o_ref, z_ref):
    t = pl.program_id(2)
    ts = r_ref.shape[0]

    @pl.when(t == 0)
    def _():
        z_ref[...] = jnp.zeros(z_ref.shape, F32)

    first = jnp.where(t == 0, 0.0, 1.0)

    def shifted(ref, pref, mu):
        x = ref[...]
        row = lax.broadcasted_iota(jnp.int32, x.shape, 0)
        prev = jnp.where(row == 0, pref[7:8, :] * first, pltpu.roll(x, 1, 0))
        return x + (prev - x) * mu

    par = par_ref[...]
    r_all = shifted(r_ref, rp_ref, par[8:9])
    k_all = shifted(k_ref, kp_ref, par[9:10])
    v_all = shifted(v_ref, vp_ref, par[10:11])
    wa = shifted(wa_ref, wap_ref, muw_ref[...])
    gd = shifted(gd_ref, gdp_ref, mug_ref[...])

    lane = lax.broadcasted_iota(jnp.int32, (LANES, LANES), 1)
    rowl = lax.broadcasted_iota(jnp.int32, (LANES, LANES), 0)
    same_head = (lane // B_HEAD_DIM) == (rowl // B_HEAD_DIM)
    gones = jnp.where(same_head, 1.0, 0.0)
    gones16 = gones.astype(BF16)
    strict = jnp.logical_and(same_head, rowl > lane)
    incl = jnp.logical_and(same_head, rowl >= lane)
    eye = jnp.where(lane == rowl, 1.0, 0.0)
    head0 = lax.broadcasted_iota(jnp.int32, (RC, LANES), 1) < B_HEAD_DIM

    def head_sum(x):
        hi, lo = _split2(x)
        return _dot(hi, gones16) + _dot(lo, gones16)

    def stack_heads(x):
        return jnp.concatenate([jnp.where(head0, x, 0.0), jnp.where(head0, 0.0, x)], axis=0)

    def fold_heads(x):
        return x[:RC] + x[RC:]

    def twice(x):
        return jnp.concatenate([x, x], axis=0)

    rt = lax.broadcasted_iota(jnp.int32, (ts, ts), 0)
    ct = lax.broadcasted_iota(jnp.int32, (ts, ts), 1)
    tri16 = jnp.where(jnp.logical_and((rt // RC) == (ct // RC), rt >= ct), 1.0, 0.0).astype(BF16)
    th_hi, th_lo = _split2(jnp.tanh(wa))
    wa16 = wa.astype(BF16)
    sg16 = jax.nn.sigmoid(gd).astype(BF16)
    nchunk = ts // RC
    npair = r_ref.shape[1] // LANES

    pairs = []
    for q in range(npair):
        ls = slice(q * LANES, (q + 1) * LANES)
        w0, a0, k_k, k_a, r_k, gn_g, gn_b = (par[n:n + 1, ls] for n in range(7))
        r, k, v = r_all[:, ls], k_all[:, ls], v_all[:, ls]
        wd_hi, wd_lo = _split2(wdec_ref[:, ls])
        z = w0 + (_dot(th_hi, wd_hi) + _dot(th_hi, wd_lo) + _dot(th_lo, wd_hi))
        w_log = -(jnp.maximum(-z, 0.0) + jnp.log1p(jnp.exp(-jnp.abs(z)))) - 0.5
        e = jnp.exp(w_log)
        a = jax.nn.sigmoid(a0 + _dot(wa16, wic_ref[:, ls].astype(BF16)))
        g = _dot(sg16, wg_ref[:, ls].astype(BF16))
        kkr = k * k_k
        kk = kkr / jnp.maximum(jnp.sqrt(head_sum(kkr * kkr)), 1e-12)
        k2 = k * (1.0 + (a - 1.0) * k_a)
        bonus = head_sum(r * k2 * r_k) * v
        kka = kk * a
        e_hi, e_mid, e_lo = _split3(e)
        cum = _dot(tri16, e_hi) + _dot(tri16, e_mid) + _dot(tri16, e_lo)
        pairs.append(dict(r_hat=r * jnp.exp(-cum), a_hat=-kk * jnp.exp(e - cum), cum=cum, kka=kka, k2=k2, v=v,
                          b_til=(kka * jnp.exp(cum)).astype(BF16), k_til=(k2 * jnp.exp(cum)).astype(BF16),
                          v16=v.astype(BF16), bonus=bonus, g=g, gn_g=gn_g, gn_b=gn_b))

    items = [(q, c) for q in range(npair) for c in range(nchunk)]
    n = range(len(items))
    sl = [slice(c * RC, (c + 1) * RC) for _, c in items]
    pq = [pairs[q] for q, _ in items]
    tot = [pq[i]["cum"][sl[i].stop - 1:sl[i].stop] for i in n]
    rem = [jnp.exp(pq[i]["cum"][sl[i]] - tot[i]) for i in n]
    a_s = [stack_heads(pq[i]["a_hat"][sl[i]]).astype(BF16) for i in n]
    r_s = [stack_heads(pq[i]["r_hat"][sl[i]]).astype(BF16) for i in n]
    v_s = [stack_heads(pq[i]["v"][sl[i]]).astype(BF16) for i in n]
    ar = [jnp.concatenate([a_s[i], r_s[i]], axis=0) for i in n]
    bk = [jnp.concatenate([twice(pq[i]["b_til"][sl[i]]), twice(pq[i]["k_til"][sl[i]])], axis=0) for i in n]
    m_all = [_dot_nt(ar[i], bk[i]) for i in n]
    l_ab = [jnp.where(strict, m_all[i][:LANES, :LANES], 0.0) for i in n]
    l_ak = [jnp.where(strict, m_all[i][:LANES, LANES:], 0.0).astype(BF16) for i in n]
    m_rb = [jnp.where(incl, m_all[i][LANES:, :LANES], 0.0).astype(BF16) for i in n]
    m_rk = [jnp.where(incl, m_all[i][LANES:, LANES:], 0.0).astype(BF16) for i in n]
    inv = [eye + l_ab[i] for i in n]
    pw = [l_ab[i].astype(BF16) for i in n]
    for _ in range(5):
        pw = [_dot(pw[i], pw[i]).astype(BF16) for i in n]
        inv = [inv[i] + _dot(inv[i].astype(BF16), pw[i]) for i in n]
    lv = [_dot(l_ak[i], v_s[i]).astype(BF16) for i in n]
    x_av = [_dot(inv[i].astype(BF16), jnp.concatenate([a_s[i], lv[i]], axis=1)) for i in n]
    y_av = [_dot(m_rb[i], x_av[i].astype(BF16)) for i in n]
    y_kv = [_dot(m_rk[i], v_s[i]) for i in n]
    r_p = [pq[i]["r_hat"][sl[i]] + fold_heads(y_av[i][:, :LANES]) for i in n]
    y0 = [fold_heads(y_av[i][:, LANES:] + y_kv[i]) for i in n]
    gh = [_dot((pq[i]["kka"][sl[i]] * rem[i]).T.astype(BF16), fold_heads(x_av[i]).astype(BF16)) for i in n]
    kv = [_dot((pq[i]["k2"][sl[i]] * rem[i]).T.astype(BF16), pq[i]["v16"][sl[i]]) for i in n]
    g_mat = [eye * jnp.exp(-tot[i]) + gones * gh[i][:, :LANES] for i in n]
    h_mat = [gones * (gh[i][:, LANES:] + kv[i]) for i in n]
    rg = [jnp.concatenate([r_p[i], g_mat[i]], axis=0).astype(BF16) for i in n]

    zc = [z_ref[q] for q in range(npair)]
    ys = [[] for _ in range(npair)]
    for c in range(nchunk):
        for q in range(npair):
            i = q * nchunk + c
            yz = _dot(rg[i], zc[q].astype(BF16))
            ys[q].append(yz[:RC] + y0[i])
            zc[q] = yz[RC:] + h_mat[i]
    for q in range(npair):
        z_ref[q] = zc[q]
        p = pairs[q]
        y = jnp.concatenate(ys[q], axis=0)
        mean = head_sum(y) * (1.0 / B_HEAD_DIM)
        yc = y - mean
        var = head_sum(yc * yc) * (1.0 / B_HEAD_DIM)
        yn = yc * lax.rsqrt(var + GN_EPS) * p["gn_g"] + p["gn_b"]
        o_ref[:, q * LANES:(q + 1) * LANES] = ((yn + p["bonus"]) * p["g"]).astype(o_ref.dtype)


def _rwkv(hb, par, mu_wa, mu_gd, wdec, wic, wg, bsz, seq):
    nt = seq // RTS
    width = RNP * LANES
    ngroup = B_WIDTH // width
    cb = B_WIDTH // width
    small = (3 * B_WIDTH) // LANES

    def main(shape_w, colblk):
        return pl.BlockSpec((RTS, shape_w), lambda b, p, t: (b * nt + t, colblk(p)))

    def prev(shape_w, colblk):
        return pl.BlockSpec((8, shape_w), lambda b, p, t: (jnp.maximum((b * nt + t) * (RTS // 8) - 1, 0), colblk(p)))

    wide = [lambda p: p, lambda p: cb + p, lambda p: 2 * cb + p]
    narrow = [lambda p: small, lambda p: small + 1]
    in_specs = ([main(width, c) for c in wide] + [main(LANES, c) for c in narrow]
                + [prev(width, c) for c in wide] + [prev(LANES, c) for c in narrow] + [
        pl.BlockSpec((16, width), lambda b, p, t: (0, p)),
        pl.BlockSpec((1, LANES), lambda b, p, t: (0, 0)),
        pl.BlockSpec((1, LANES), lambda b, p, t: (0, 0)),
        pl.BlockSpec((LANES, width), lambda b, p, t: (0, p)),
        pl.BlockSpec((LANES, width), lambda b, p, t: (0, p)),
        pl.BlockSpec((LANES, width), lambda b, p, t: (0, p)),
    ])
    return pl.pallas_call(
        _rwkv_kernel,
        grid=(bsz, ngroup, nt),
        in_specs=in_specs,
        out_specs=pl.BlockSpec((RTS, width), lambda b, p, t: (b * nt + t, p)),
        out_shape=jax.ShapeDtypeStruct((bsz * seq, B_WIDTH), BF16),
        scratch_shapes=[pltpu.VMEM((RNP, LANES, LANES), F32)],
        compiler_params=_cparams(("arbitrary", "arbitrary", "arbitrary")), name="rwkv7",
    )(*([hb] * 10), par, mu_wa, mu_gd, wdec, wic, wg)


def _pack_rwkv_params(mu, w0, w_decay, a0, w_iclr, k_k, k_a, r_k, gn_g, gn_b):
    bw = B_WIDTH
    zero = jnp.zeros((bw,), F32)
    rows = [w0, a0, k_k, k_a, r_k.reshape(bw), gn_g, gn_b, zero,
            mu[:bw], mu[bw:2 * bw], mu[2 * bw:3 * bw], zero, zero, zero, zero, zero]
    par = jnp.stack(rows)
    mu_wa = mu[3 * bw:3 * bw + LANES][None]
    mu_gd = mu[3 * bw + LANES:][None]
    zpad = jnp.zeros((DECAY_LORA, bw), F32)
    wdec = jnp.concatenate([w_decay, zpad], axis=0)
    wic = jnp.concatenate([zpad, w_iclr], axis=0)
    return par, mu_wa, mu_gd, wdec, wic


def _outproj_kernel(x_ref, a_ref, b_ref, wa_ref, wb_ref, g_ref, be_ref, o_ref):
    mix = _dot(a_ref[...], wa_ref[...]) + _dot(b_ref[...], wb_ref[...])
    o_ref[...] = _ln(ALPHA * x_ref[...] + mix, g_ref[...], be_ref[...])


def _outproj(x, a_out, b_out, w_out, g, b, tm=512):
    t, d = x.shape
    wa = w_out[:A_WIDTH].astype(BF16)
    wb = w_out[A_WIDTH:].astype(BF16)
    row = lambda i: (i, 0)
    const = lambda i: (0, 0)
    return pl.pallas_call(
        _outproj_kernel, grid=(t // tm,),
        in_specs=[pl.BlockSpec((tm, d), row), pl.BlockSpec((tm, A_WIDTH), row), pl.BlockSpec((tm, B_WIDTH), row),
                  pl.BlockSpec((A_WIDTH, d), const), pl.BlockSpec((B_WIDTH, d), const),
                  pl.BlockSpec((1, d), const), pl.BlockSpec((1, d), const)],
        out_specs=pl.BlockSpec((tm, d), row), out_shape=jax.ShapeDtypeStruct((t, d), F32),
        compiler_params=_cparams(("arbitrary",)), name="outproj_ln",
    )(x, a_out, b_out, wa, wb, g, b)


def _xor_partner(x, lane, bit):
    up = pltpu.roll(x, bit, 1)
    down = pltpu.roll(x, LANES - bit, 1)
    return jnp.where((lane & bit) != 0, up, down)


def _moe_gates(xf, wr, br):
    tm = xf.shape[0]
    lane = lax.broadcasted_iota(jnp.int32, (tm, LANES), 1)
    valid = lane < N_EXPERTS
    scores = jax.nn.sigmoid(_dot_hi(xf, wr))
    sel = jnp.where(valid, scores + br, NEG)
    p1 = _xor_partner(sel, lane, 1)
    hi1, lo1 = jnp.maximum(sel, p1), jnp.minimum(sel, p1)
    hi2, lo2 = _xor_partner(hi1, lane, 2), _xor_partner(lo1, lane, 2)
    gscore = jnp.maximum(hi1, hi2) + jnp.maximum(jnp.minimum(hi1, hi2), jnp.maximum(lo1, lo2))
    gscore = jnp.where(valid, gscore, NEG)
    lanef = lane.astype(F32)
    gbest = jnp.max(gscore, axis=1, keepdims=True)
    first = jnp.min(jnp.where(gscore == gbest, lanef, float(LANES)), axis=1, keepdims=True)
    in_group = jnp.floor(lanef * (1.0 / EXPERTS_PER_GROUP)) == jnp.floor(first * (1.0 / EXPERTS_PER_GROUP))
    masked = jnp.where(in_group, sel, NEG)
    m1 = jnp.max(masked, axis=1, keepdims=True)
    i1 = jnp.min(jnp.where(masked == m1, lanef, float(LANES)), axis=1, keepdims=True)
    pick1 = lanef == i1
    masked2 = jnp.where(pick1, NEG, masked)
    m2 = jnp.max(masked2, axis=1, keepdims=True)
    i2 = jnp.min(jnp.where(masked2 == m2, lanef, float(LANES)), axis=1, keepdims=True)
    pick2 = lanef == i2
    s1 = jnp.sum(jnp.where(pick1, scores, 0.0), axis=1, keepdims=True)
    s2 = jnp.sum(jnp.where(pick2, scores, 0.0), axis=1, keepdims=True)
    tot = s1 + s2
    return jnp.where(pick1, s1 / tot, 0.0) + jnp.where(pick2, s2 / tot, 0.0)


def _moe_kernel(x_ref, wr_ref, br_ref, wgu_ref, wd_ref, g_ref, b_ref, o_ref, gate_ref, acc_ref, xb_ref):
    e = pl.program_id(1)

    @pl.when(e == 0)
    def _():
        xf = x_ref[...]
        gate_ref[...] = _moe_gates(xf, wr_ref[...], br_ref[...])
        xb_ref[...] = xf.astype(BF16)
        acc_ref[...] = jnp.zeros(acc_ref.shape, F32)

    gates = gate_ref[...]
    lane = lax.broadcasted_iota(jnp.int32, gates.shape, 1)
    gcol = jnp.sum(jnp.where(lane == e, gates, 0.0), axis=1, keepdims=True)
    hu = _dot(xb_ref[...], wgu_ref[0])
    hg, up = hu[:, :D_EXPERT], hu[:, D_EXPERT:]
    hid = (hg * jax.nn.sigmoid(hg)) * up * gcol
    acc_ref[...] += _dot(hid.astype(BF16), wd_ref[0])

    @pl.when(e == pl.num_programs(1) - 1)
    def _():
        o_ref[...] = _ln(ALPHA * x_ref[...] + acc_ref[...], g_ref[...], b_ref[...])


def _moe(x, w_router, b_router, w_gate, w_up, w_down, g, b, tm=1024):
    t, d = x.shape
    ne = w_gate.shape[0]
    wgu = jnp.concatenate([w_gate, w_up], axis=2).astype(BF16)
    wd = w_down.astype(BF16)
    wr = jnp.pad(w_router, ((0, 0), (0, LANES - ne)))
    br = jnp.pad(b_router, (0, LANES - ne))[None]
    row = lambda i, e: (i, 0)
    const = lambda i, e: (0, 0)
    return pl.pallas_call(
        _moe_kernel, grid=(t // tm, ne),
        in_specs=[pl.BlockSpec((tm, d), row), pl.BlockSpec((d, LANES), const), pl.BlockSpec((1, LANES), const),
                  pl.BlockSpec((1, d, 2 * D_EXPERT), lambda i, e: (e, 0, 0)),
                  pl.BlockSpec((1, D_EXPERT, d), lambda i, e: (e, 0, 0)),
                  pl.BlockSpec((1, d), const), pl.BlockSpec((1, d), const)],
        out_specs=pl.BlockSpec((tm, d), row), out_shape=jax.ShapeDtypeStruct((t, d), F32),
        scratch_shapes=[pltpu.VMEM((tm, LANES), F32), pltpu.VMEM((tm, d), F32), pltpu.VMEM((tm, d), BF16)],
        compiler_params=_cparams(("arbitrary", "arbitrary")), name="moe_ln",
    )(x, wr, br, wgu, wd, g, b)


def _ple_kernel(x_ref, p_ref, wg_ref, wp_ref, g_ref, b_ref, o_ref):
    x = x_ref[...]
    gate = jax.nn.sigmoid(_dot(x.astype(BF16), wg_ref[...]))
    pe = _dot(p_ref[...].astype(BF16), wp_ref[...])
    o_ref[...] = _ln(ALPHA * x + gate * pe, g_ref[...], b_ref[...])


def _ple(x, p, w_gate, w_ple, g, b, tm=512):
    t, d = x.shape
    row = lambda i: (i, 0)
    const = lambda i: (0, 0)
    return pl.pallas_call(
        _ple_kernel, grid=(t // tm,),
        in_specs=[pl.BlockSpec((tm, d), row), pl.BlockSpec((tm, PLE_DIM), row),
                  pl.BlockSpec((d, d), const), pl.BlockSpec((PLE_DIM, d), const),
                  pl.BlockSpec((1, d), const), pl.BlockSpec((1, d), const)],
        out_specs=pl.BlockSpec((tm, d), row), out_shape=jax.ShapeDtypeStruct((t, d), F32),
        compiler_params=_cparams(("arbitrary",)), name="ple_ln",
    )(x, p, w_gate.astype(BF16), w_ple.astype(BF16), g, b)


def kernel(x, p, ln_in_g, ln_in_b, w_in, w_out, mu_shift, w0, w_decay, a0, w_iclr, w_gate_up, k_k, k_a, r_k,
           gn_g, gn_b, rel_bias, w_router, b_router, w_exp_gate, w_exp_up, w_exp_down, w_ple, w_ple_gate,
           ln_mix_g, ln_mix_b, ln_ffn_g, ln_ffn_b, ln_ple_g, ln_ple_b):
    bsz, seq, d = x.shape
    t = bsz * seq
    depth = w_in.shape[0]
    xf = x.reshape(t, d)
    near_bias = _near_bias(rel_bias)
    for i in range(depth):
        w_rm, w_fm = _pack_w_in(w_in[i])
        outs = _inproj(xf, ln_in_g[None], ln_in_b[None], w_rm, w_fm, apply_ln=(i == 0))
        k_rm, kiki, hb, qt, qit, vt, wit = outs[:7]
        if i == 0:
            xf = outs[7]
        a_out = _dsa(qt, qit, wit, k_rm, vt, kiki, near_bias, bsz, seq)
        par, mu_wa, mu_gd, wdec, wic = _pack_rwkv_params(mu_shift[i], w0[i], w_decay[i], a0[i], w_iclr[i],
                                                         k_k[i], k_a[i], r_k[i], gn_g[i], gn_b[i])
        b_out = _rwkv(hb, par, mu_wa, mu_gd, wdec, wic, w_gate_up[i], bsz, seq)
        xf = _outproj(xf, a_out, b_out, w_out[i], ln_mix_g[i][None], ln_mix_b[i][None])
        xf = _moe(xf, w_router, b_router, w_exp_gate[i], w_exp_up[i], w_exp_down[i],
                  ln_ffn_g[i][None], ln_ffn_b[i][None])
        xf = _ple(xf, p[i].reshape(t, PLE_DIM), w_ple_gate[i], w_ple[i], ln_ple_g[i][None], ln_ple_b[i][None])
    return xf.reshape(bsz, seq, d)
```

```python
import functools
import math

import numpy as np
import jax
import jax.numpy as jnp
from jax import lax
from jax.experimental import pallas as pl
from jax.experimental.pallas import tpu as pltpu

D_MODEL = 1024
DEPTH = 2
CHUNK = 64
A_HEAD_DIM = 64
A_WIDTH = D_MODEL // 2
A_HEADS = A_WIDTH // A_HEAD_DIM
IDX_HEADS = 8
IDX_DIM = 64
TOPK_MAX = 256
N_BUCKETS = 32
MAX_DISTANCE = 128
B_HEAD_DIM = 64
B_WIDTH = D_MODEL - A_WIDTH
B_HEADS = B_WIDTH // B_HEAD_DIM
DECAY_LORA = 64
ICLR_LORA = 64
GATE_LORA = 128
A_SIZES = (A_WIDTH, A_WIDTH, A_WIDTH, IDX_HEADS * IDX_DIM, IDX_DIM, IDX_HEADS)
B_SIZES = (B_WIDTH, B_WIDTH, B_WIDTH, DECAY_LORA, ICLR_LORA, GATE_LORA)
A_COLS = sum(A_SIZES)
B_COLS = sum(B_SIZES)
N_EXPERTS = 16
N_GROUPS = 4
EXPERTS_PER_GROUP = N_EXPERTS // N_GROUPS
D_EXPERT = 256
PLE_DIM = 256
ALPHA = (2 * DEPTH) ** 0.25
LN_EPS = 1e-5
GN_EPS = 64e-5
NEG = -1e30

LANES = 128
VMEM_LIMIT = 56 * 1024 * 1024
F32 = jnp.float32
BF16 = jnp.bfloat16
HI = lax.Precision.HIGHEST
NT_DIMS = (((1,), (1,)), ((), ()))


def _cparams(sem):
    return pltpu.CompilerParams(dimension_semantics=sem, vmem_limit_bytes=VMEM_LIMIT)


def _ln(x, g, b):
    mu = jnp.mean(x, axis=-1, keepdims=True)
    xc = x - mu
    var = jnp.mean(xc * xc, axis=-1, keepdims=True)
    return xc * lax.rsqrt(var + LN_EPS) * g + b


def _dot(a, b):
    return jnp.dot(a, b, preferred_element_type=F32)


def _dot_hi(a, b):
    return jnp.dot(a, b, preferred_element_type=F32, precision=HI)


def _dot_nt(a, b):
    return lax.dot_general(a, b, NT_DIMS, preferred_element_type=F32)


def _dot_nt_hi(a, b):
    return lax.dot_general(a, b, NT_DIMS, preferred_element_type=F32, precision=HI)


def _split2(x):
    hi = x.astype(BF16)
    return hi, (x - hi.astype(F32)).astype(BF16)


def _split3(x):
    hi = x.astype(BF16)
    rest = x - hi.astype(F32)
    mid = rest.astype(BF16)
    return hi, mid, (rest - mid.astype(F32)).astype(BF16)


IN_NCHUNK = 512


WI_ROWS = 16
LOG2E = math.log2(math.e)
Q_SCALE = A_HEAD_DIM ** -0.5 * LOG2E


def _inproj_kernel(x_ref, g_ref, b_ref, wr_ref, wf_ref, *out_refs, apply_ln):
    x = x_ref[...]
    if apply_ln:
        x = _ln(x, g_ref[...], b_ref[...])
        out_refs[7][...] = x
    xb = x.astype(BF16)
    k_ref, kk_ref, hb_ref, qt_ref, qit_ref, vt_ref, wit_ref = out_refs[:7]
    aw = A_WIDTH
    k_ref[...] = _dot(xb, wr_ref[:, :aw]).astype(BF16)
    kk_ref[...] = _dot(xb, wr_ref[:, aw:aw + LANES]).astype(BF16)
    nb = hb_ref.shape[1]
    for c0 in range(0, nb, IN_NCHUNK):
        c1 = min(c0 + IN_NCHUNK, nb)
        hb_ref[:, c0:c1] = _dot(xb, wr_ref[:, aw + LANES + c0:aw + LANES + c1])
    qt_ref[...] = (_dot_nt(wf_ref[:aw], xb) * Q_SCALE).astype(BF16)
    qit_ref[...] = _dot_nt(wf_ref[aw:2 * aw], xb).astype(BF16)
    vt = _dot_nt(wf_ref[2 * aw:3 * aw], xb).astype(BF16)
    for n in range(vt_ref.shape[0]):
        vt_ref[n] = vt[:, n * TQ:(n + 1) * TQ]
    wit_ref[...] = _dot_nt(wf_ref[3 * aw:], xb)


def _inproj(x, g, b, w_rm, w_fm, apply_ln, tm=512):
    t, d = x.shape
    aw = A_WIDTH
    nb = w_rm.shape[1] - aw - LANES
    row = lambda i: (i, 0)
    col = lambda i: (0, i)
    const = lambda i: (0, 0)
    out_shape = [jax.ShapeDtypeStruct((t, aw), BF16), jax.ShapeDtypeStruct((t, LANES), BF16),
                 jax.ShapeDtypeStruct((t, nb), F32), jax.ShapeDtypeStruct((aw, t), BF16),
                 jax.ShapeDtypeStruct((aw, t), BF16), jax.ShapeDtypeStruct((t // TQ, aw, TQ), BF16),
                 jax.ShapeDtypeStruct((WI_ROWS, t), F32)]
    out_specs = [pl.BlockSpec((tm, aw), row), pl.BlockSpec((tm, LANES), row), pl.BlockSpec((tm, nb), row),
                 pl.BlockSpec((aw, tm), col), pl.BlockSpec((aw, tm), col),
                 pl.BlockSpec((tm // TQ, aw, TQ), lambda i: (i, 0, 0)), pl.BlockSpec((WI_ROWS, tm), col)]
    if apply_ln:
        out_shape.append(jax.ShapeDtypeStruct((t, d), F32))
        out_specs.append(pl.BlockSpec((tm, d), row))
    return pl.pallas_call(
        functools.partial(_inproj_kernel, apply_ln=apply_ln),
        grid=(t // tm,),
        in_specs=[pl.BlockSpec((tm, d), row), pl.BlockSpec((1, d), const), pl.BlockSpec((1, d), const),
                  pl.BlockSpec(w_rm.shape, const), pl.BlockSpec(w_fm.shape, const)],
        out_specs=out_specs, out_shape=out_shape,
        compiler_params=_cparams(("arbitrary",)), name="inproj_ln" if apply_ln else "inproj",
    )(x, g, b, w_rm, w_fm)


def _pack_w_in(w):
    d = w.shape[0]
    aw = A_WIDTH
    cuts = np.cumsum(A_SIZES)
    wq, wk, wv, wqi = w[:, :aw], w[:, aw:2 * aw], w[:, 2 * aw:3 * aw], w[:, cuts[2]:cuts[3]]
    wki = w[:, cuts[3]:cuts[4]]
    wwi = w[:, cuts[4]:cuts[5]]
    w_rm = jnp.concatenate([wk, wki, wki, w[:, A_COLS:]], axis=1).astype(BF16)
    pad = jnp.zeros((d, WI_ROWS - IDX_HEADS), w.dtype)
    w_fm = jnp.concatenate([wq, wqi, wv, wwi, pad], axis=1).T.astype(BF16)
    return w_rm, w_fm


TQ = 256
ROW_GROUP = 32
BIGF = 3.0e38


def _t5_bucket_np(rel):
    nb = N_BUCKETS // 2
    max_exact = nb // 2
    ret = np.where(rel > 0, nb, 0)
    n = np.abs(rel)
    nf = np.maximum(n, 1).astype(np.float64)
    large = max_exact + np.floor(np.log(nf / max_exact) / math.log(MAX_DISTANCE / max_exact)
                                 * (nb - max_exact) + 1e-9).astype(np.int64)
    large = np.minimum(large, nb - 1)
    return ret + np.where(n < max_exact, n, large)


def _near_bias(rel_bias):
    qpos = np.arange(TQ)[None, :]
    kpos = np.arange(TQ)[:, None]
    bk = np.stack([_t5_bucket_np(kpos - qpos), _t5_bucket_np(kpos - TQ - qpos)])
    far = N_BUCKETS // 2 - 1
    tab = (rel_bias - rel_bias[far][None, :]) * LOG2E
    onehot = jnp.asarray(bk[..., None] == np.arange(N_BUCKETS)).astype(F32)
    return jnp.einsum('nkqb,bh->nhkq', onehot, tab.astype(F32), precision=HI)


BISECT_STEPS = 12
SUBLANES = 8


def _fold8(x, op):
    return op(x.reshape(x.shape[0] // SUBLANES, SUBLANES, x.shape[1]), axis=0)


def _dsa_kernel(qt_ref, qit_ref, wit_ref, k_ref, vt_ref, kk_ref, bias_ref, o_ref,
                s_ref, m_ref, l_ref, acc_ref, p_ref, *, ksel):
    i = pl.program_id(1)
    nt = i + 1
    tq = qt_ref.shape[1]
    kself = float(ksel)
    idx_scale = (IDX_HEADS ** -0.5) * (IDX_DIM ** -0.5)

    sub = lax.broadcasted_iota(jnp.int32, (LANES, tq), 0)
    lower = sub < (LANES // 2)
    zero_b = jnp.zeros((LANES, tq), BF16)

    def head_parts(ref):
        parts = []
        for h in range(A_HEADS):
            j, s = divmod(h, 2)
            blk = ref[j * LANES:(j + 1) * LANES, :]
            parts.append(jnp.where(lower if s == 0 else jnp.logical_not(lower), blk, zero_b))
        return parts

    def key_rows(kt):
        return pl.ds(pl.multiple_of(kt * tq, tq), tq)

    qi_parts = head_parts(qit_ref)
    wi = wit_ref[...] * idx_scale
    wrows = [wi[h:h + 1, :] for h in range(IDX_HEADS)]
    krow = lax.broadcasted_iota(jnp.int32, (tq, tq), 0)
    qcol = lax.broadcasted_iota(jnp.int32, (tq, tq), 1)
    adm_diag = (krow // CHUNK) <= (qcol // CHUNK)

    def score_tile(kt):
        kk = kk_ref[key_rows(kt), :]
        acc = jnp.zeros((tq, tq), F32)
        for h in range(IDX_HEADS):
            acc = acc + wrows[h] * jnp.maximum(_dot(kk, qi_parts[h]), 0.0)
        return acc

    def p1_body(kt, carry):
        sc = score_tile(kt)
        s_ref[kt] = sc
        return jnp.minimum(carry[0], _fold8(sc, jnp.min)), jnp.maximum(carry[1], _fold8(sc, jnp.max))

    mn8, mx8 = lax.fori_loop(0, i, p1_body, (jnp.full((SUBLANES, tq), BIGF, F32), jnp.full((SUBLANES, tq), -BIGF, F32)))
    sc = score_tile(i)
    s_ref[i] = jnp.where(adm_diag, sc, NEG)
    mn8 = jnp.minimum(mn8, _fold8(jnp.where(adm_diag, sc, BIGF), jnp.min))
    mx8 = jnp.maximum(mx8, _fold8(jnp.where(adm_diag, sc, -BIGF), jnp.max))
    rowmin = jnp.min(mn8, axis=0, keepdims=True)
    rowmax = jnp.max(mx8, axis=0, keepdims=True)

    def reduce_tiles(fn, inits, ops):
        def body(kt, carry):
            return fn(s_ref[kt], kt, carry)

        carry = lax.fori_loop(0, nt, body, tuple(jnp.full((SUBLANES, tq), v, F32) for v in inits))
        return [op(c, axis=0, keepdims=True) for op, c in zip(ops, carry)]

    qpos = i * tq + lax.broadcasted_iota(jnp.int32, (1, tq), 1)
    n_adm = (qpos // CHUNK + 1) * CHUNK
    trivial = n_adm <= ksel
    lo0 = jnp.where(trivial, 0.1 * NEG, rowmin)
    hi0 = jnp.where(trivial, 0.1 * NEG, rowmax)

    def count_ge(mid):
        def cnt(s, kt, carry):
            return (carry[0] + _fold8(jnp.where(s >= mid, 1.0, 0.0), jnp.sum),)

        return reduce_tiles(cnt, (0.0,), (jnp.sum,))[0]

    def a_body(_, st):
        lo, hi, clo = st
        mid = 0.5 * (lo + hi)
        c = count_ge(mid)
        ok = c >= kself
        return jnp.where(ok, mid, lo), jnp.where(ok, hi, mid), jnp.where(ok, c, clo)

    lo1, hi1, clo1 = lax.fori_loop(0, BISECT_STEPS, a_body, (lo0, hi0, n_adm.astype(F32)))
    hi1 = jnp.where(jnp.logical_or(clo1 == kself, trivial), lo1, hi1)

    def b_cond(st):
        lo, hi = st
        return jnp.max(jnp.where(lo < hi, 1.0, 0.0)) > 0.0

    def b_body(st):
        lo, hi = st
        mid = 0.5 * (lo + hi)
        mid = jnp.where(mid > lo, mid, hi)

        def cnt(s, kt, carry):
            c, vge, vlt = carry
            ge = s >= mid
            return (c + _fold8(jnp.where(ge, 1.0, 0.0), jnp.sum),
                    jnp.minimum(vge, _fold8(jnp.where(ge, s, BIGF), jnp.min)),
                    jnp.maximum(vlt, _fold8(jnp.where(ge, -BIGF, s), jnp.max)))

        c, vge, vlt = reduce_tiles(cnt, (0.0, BIGF, -BIGF), (jnp.sum, jnp.min, jnp.max))
        ok = c >= kself
        return jnp.where(ok, vge, lo), jnp.where(ok, hi, vlt)

    thr, _ = lax.while_loop(b_cond, b_body, (lo1, hi1))

    def count_ties(s, kt, carry):
        return (carry[0] + _fold8(jnp.where(s >= thr, 1.0, 0.0), jnp.sum),
                carry[1] + _fold8(jnp.where(s > thr, 1.0, 0.0), jnp.sum))

    c_ge, c_gt = reduce_tiles(count_ties, (0.0, 0.0), (jnp.sum, jnp.sum))
    need = kself - c_gt
    tied = c_ge > kself
    s_total = float(s_ref.shape[0] * tq)
    jlo0 = jnp.where(tied, 0.0, s_total - 1.0)
    jhi0 = jnp.where(tied, (nt * tq).astype(F32), s_total)
    kidx = krow.astype(F32)

    def tie_cond(st):
        jlo, jhi = st
        return jnp.max(jhi - jlo) > 1.0

    def tie_body(st):
        jlo, jhi = st
        mid = jnp.floor(0.5 * (jlo + jhi))

        def cnt(s, kt, carry):
            idx = kidx + (kt * tq).astype(F32)
            return (carry[0] + _fold8(jnp.where(jnp.logical_and(s == thr, idx < mid), 1.0, 0.0), jnp.sum),)

        (c,) = reduce_tiles(cnt, (0.0,), (jnp.sum,))
        ok = c >= need
        done = (jhi - jlo) <= 1.0
        return (jnp.where(jnp.logical_or(ok, done), jlo, mid),
                jnp.where(jnp.logical_and(ok, jnp.logical_not(done)), mid, jhi))

    _, jcut = lax.while_loop(tie_cond, tie_body, (jlo0, jhi0))

    q_parts = head_parts(qt_ref)
    m_ref[...] = jnp.full(m_ref.shape, NEG, F32)
    l_ref[...] = jnp.zeros(l_ref.shape, F32)
    acc_ref[...] = jnp.zeros(acc_ref.shape, F32)

    def logits(kt, near):
        mb = s_ref[kt]
        for j in range(A_HEADS // 2):
            kp = k_ref[key_rows(kt), j * LANES:(j + 1) * LANES]
            for s_half in range(2):
                h = 2 * j + s_half
                lg = _dot(kp, q_parts[h]) + mb
                if near is not None:
                    lg = lg + bias_ref[near, h]
                yield h, lg

    def sweep_max(kt, near):
        s = s_ref[kt]
        idx = kidx + (kt * tq).astype(F32)
        sel = jnp.logical_or(s > thr, jnp.logical_and(s == thr, idx < jcut))
        s_ref[kt] = jnp.where(sel, 0.0, NEG)
        for h, lg in logits(kt, near):
            m_ref[h] = jnp.maximum(m_ref[h], _fold8(lg, jnp.max))

    def sweep_sum(kt, near):
        for h, lg in logits(kt, near):
            p = jnp.exp2(lg - m_ref[h][:1])
            l_ref[h] += _fold8(p, jnp.sum)
            p_ref[h] = p.astype(BF16)
        for h in range(A_HEADS):
            acc_ref[h] += _dot(vt_ref[kt, h * A_HEAD_DIM:(h + 1) * A_HEAD_DIM, :], p_ref[h])

    def all_tiles(sweep):
        def body(kt, c):
            sweep(kt, None)
            return c

        lax.fori_loop(0, jnp.maximum(i - 1, 0), body, 0)

        @pl.when(i >= 1)
        def _():
            sweep(i - 1, 1)

        sweep(i, 0)

    all_tiles(sweep_max)
    for h in range(A_HEADS):
        m_ref[h] = jnp.broadcast_to(jnp.max(m_ref[h], axis=0, keepdims=True), (SUBLANES, tq))
    all_tiles(sweep_sum)

    for j in range(A_HEADS // 2):
        outs = [acc_ref[h] / jnp.sum(l_ref[h], axis=0, keepdims=True) for h in (2 * j, 2 * j + 1)]
        o_ref[:, j * LANES:(j + 1) * LANES] = jnp.concatenate(outs, axis=0).T.astype(o_ref.dtype)


def _dsa(qt, qit, wit, k, vt, kiki, near_bias, bsz, seq):
    nq = seq // TQ
    ksel = min(TOPK_MAX, seq // 4)
    qcol = lambda b, i: (0, b * nq + i)
    return pl.pallas_call(
        functools.partial(_dsa_kernel, ksel=ksel),
        grid=(bsz, nq),
        in_specs=[
            pl.BlockSpec((A_WIDTH, TQ), qcol),
            pl.BlockSpec((A_WIDTH, TQ), qcol),
            pl.BlockSpec((WI_ROWS, TQ), qcol),
            pl.BlockSpec((seq, A_WIDTH), lambda b, i: (b, 0)),
            pl.BlockSpec((nq, A_WIDTH, TQ), lambda b, i: (b, 0, 0)),
            pl.BlockSpec((seq, LANES), lambda b, i: (b, 0)),
            pl.BlockSpec((2, A_HEADS, TQ, TQ), lambda b, i: (0, 0, 0, 0)),
        ],
        out_specs=pl.BlockSpec((TQ, A_WIDTH), lambda b, i: (b * nq + i, 0)),
        out_shape=jax.ShapeDtypeStruct((bsz * seq, A_WIDTH), BF16),
        scratch_shapes=[pltpu.VMEM((nq, TQ, TQ), F32), pltpu.VMEM((A_HEADS, SUBLANES, TQ), F32),
                        pltpu.VMEM((A_HEADS, SUBLANES, TQ), F32), pltpu.VMEM((A_HEADS, A_HEAD_DIM, TQ), F32),
                        pltpu.VMEM((A_HEADS, TQ, TQ), BF16)],
        compiler_params=_cparams(("arbitrary", "arbitrary")), name="dsa_attention",
    )(qt, qit, wit, k, vt, kiki, near_bias)


RC = 64
RTS = 256


RNP = 4


def _rwkv_kernel(r_ref, k_ref, v_ref, wa_ref, gd_ref, rp_ref, kp_ref, vp_ref, wap_ref, gdp_ref,
                 par_ref, muw_ref, mug_ref, wdec_ref, wic_ref, wg_ref, o_ref, z_ref):
    t = pl.program_id(2)
    ts = r_ref.shape[0]

    @pl.when(t == 0)
    def _():
        z_ref[...] = jnp.zeros(z_ref.shape, F32)

    first = jnp.where(t == 0, 0.0, 1.0)

    def shifted(ref, pref, mu):
        x = ref[...]
        row = lax.broadcasted_iota(jnp.int32, x.shape, 0)
        prev = jnp.where(row == 0, pref[7:8, :] * first, pltpu.roll(x, 1, 0))
        return x + (prev - x) * mu

    par = par_ref[...]
    r_all = shifted(r_ref, rp_ref, par[8:9])
    k_all = shifted(k_ref, kp_ref, par[9:10])
    v_all = shifted(v_ref, vp_ref, par[10:11])
    wa = shifted(wa_ref, wap_ref, muw_ref[...])
    gd = shifted(gd_ref, gdp_ref, mug_ref[...])

    lane = lax.broadcasted_iota(jnp.int32, (LANES, LANES), 1)
    rowl = lax.broadcasted_iota(jnp.int32, (LANES, LANES), 0)
    same_head = (lane // B_HEAD_DIM) == (rowl // B_HEAD_DIM)
    gones = jnp.where(same_head, 1.0, 0.0)
    gones16 = gones.astype(BF16)
    strict = jnp.logical_and(same_head, rowl > lane)
    incl = jnp.logical_and(same_head, rowl >= lane)
    eye = jnp.where(lane == rowl, 1.0, 0.0)
    head0 = lax.broadcasted_iota(jnp.int32, (RC, LANES), 1) < B_HEAD_DIM

    def head_sum(x):
        hi, lo = _split2(x)
        return _dot(hi, gones16) + _dot(lo, gones16)

    def stack_heads(x):
        return jnp.concatenate([jnp.where(head0, x, 0.0), jnp.where(head0, 0.0, x)], axis=0)

    def fold_heads(x):
        return x[:RC] + x[RC:]

    def twice(x):
        return jnp.concatenate([x, x], axis=0)

    rt = lax.broadcasted_iota(jnp.int32, (ts, ts), 0)
    ct = lax.broadcasted_iota(jnp.int32, (ts, ts), 1)
    tri16 = jnp.where(jnp.logical_and((rt // RC) == (ct // RC), rt >= ct), 1.0, 0.0).astype(BF16)
    th_hi, th_lo = _split2(jnp.tanh(wa))
    wa16 = wa.astype(BF16)
    sg16 = jax.nn.sigmoid(gd).astype(BF16)
    nchunk = ts // RC
    npair = r_ref.shape[1] // LANES

    pairs = []
    for q in range(npair):
        ls = slice(q * LANES, (q + 1) * LANES)
        w0, a0, k_k, k_a, r_k, gn_g, gn_b = (par[n:n + 1, ls] for n in range(7))
        r, k, v = r_all[:, ls], k_all[:, ls], v_all[:, ls]
        wd_hi, wd_lo = _split2(wdec_ref[:, ls])
        z = w0 + (_dot(th_hi, wd_hi) + _dot(th_hi, wd_lo) + _dot(th_lo, wd_hi))
        w_log = -(jnp.maximum(-z, 0.0) + jnp.log1p(jnp.exp(-jnp.abs(z)))) - 0.5
        e = jnp.exp(w_log)
        a = jax.nn.sigmoid(a0 + _dot(wa16, wic_ref[:, ls].astype(BF16)))
        g = _dot(sg16, wg_ref[:, ls].astype(BF16))
        kkr = k * k_k
        kk = kkr / jnp.maximum(jnp.sqrt(head_sum(kkr * kkr)), 1e-12)
        k2 = k * (1.0 + (a - 1.0) * k_a)
        bonus = head_sum(r * k2 * r_k) * v
        kka = kk * a
        e_hi, e_mid, e_lo = _split3(e)
        cum = _dot(tri16, e_hi) + _dot(tri16, e_mid) + _dot(tri16, e_lo)
        pairs.append(dict(r_hat=r * jnp.exp(-cum), a_hat=-kk * jnp.exp(e - cum), cum=cum, kka=kka, k2=k2, v=v,
                          b_til=(kka * jnp.exp(cum)).astype(BF16), k_til=(k2 * jnp.exp(cum)).astype(BF16),
                          v16=v.astype(BF16), bonus=bonus, g=g, gn_g=gn_g, gn_b=gn_b))

    items = [(q, c) for q in range(npair) for c in range(nchunk)]
    n = range(len(items))
    sl = [slice(c * RC, (c + 1) * RC) for _, c in items]
    pq = [pairs[q] for q, _ in items]
    tot = [pq[i]["cum"][sl[i].stop - 1:sl[i].stop] for i in n]
    rem = [jnp.exp(pq[i]["cum"][sl[i]] - tot[i]) for i in n]
    a_s = [stack_heads(pq[i]["a_hat"][sl[i]]).astype(BF16) for i in n]
    r_s = [stack_heads(pq[i]["r_hat"][sl[i]]).astype(BF16) for i in n]
    v_s = [stack_heads(pq[i]["v"][sl[i]]).astype(BF16) for i in n]
    ar = [jnp.concatenate([a_s[i], r_s[i]], axis=0) for i in n]
    bk = [jnp.concatenate([twice(pq[i]["b_til"][sl[i]]), twice(pq[i]["k_til"][sl[i]])], axis=0) for i in n]
    m_all = [_dot_nt(ar[i], bk[i]) for i in n]
    l_ab = [jnp.where(strict, m_all[i][:LANES, :LANES], 0.0) for i in n]
    l_ak = [jnp.where(strict, m_all[i][:LANES, LANES:], 0.0).astype(BF16) for i in n]
    m_rb = [jnp.where(incl, m_all[i][LANES:, :LANES], 0.0).astype(BF16) for i in n]
    m_rk = [jnp.where(incl, m_all[i][LANES:, LANES:], 0.0).astype(BF16) for i in n]
    inv = [eye + l_ab[i] for i in n]
    pw = [l_ab[i].astype(BF16) for i in n]
    for _ in range(5):
        pw = [_dot(pw[i], pw[i]).astype(BF16) for i in n]
        inv = [inv[i] + _dot(inv[i].astype(BF16), pw[i]) for i in n]
    lv = [_dot(l_ak[i], v_s[i]).astype(BF16) for i in n]
    x_av = [_dot(inv[i].astype(BF16), jnp.concatenate([a_s[i], lv[i]], axis=1)) for i in n]
    y_av = [_dot(m_rb[i], x_av[i].astype(BF16)) for i in n]
    y_kv = [_dot(m_rk[i], v_s[i]) for i in n]
    r_p = [pq[i]["r_hat"][sl[i]] + fold_heads(y_av[i][:, :LANES]) for i in n]
    y0 = [fold_heads(y_av[i][:, LANES:] + y_kv[i]) for i in n]
    gh = [_dot((pq[i]["kka"][sl[i]] * rem[i]).T.astype(BF16), fold_heads(x_av[i]).astype(BF16)) for i in n]
    kv = [_dot((pq[i]["k2"][sl[i]] * rem[i]).T.astype(BF16), pq[i]["v16"][sl[i]]) for i in n]
    g_mat = [eye * jnp.exp(-tot[i]) + gones * gh[i][:, :LANES] for i in n]
    h_mat = [gones * (gh[i][:, LANES:] + kv[i]) for i in n]
    rg = [jnp.concatenate([r_p[i], g_mat[i]], axis=0).astype(BF16) for i in n]

    zc = [z_ref[q] for q in range(npair)]
    ys = [[] for _ in range(npair)]
    for c in range(nchunk):
        for q in range(npair):
            i = q * nchunk + c
            yz = _dot(rg[i], zc[q].astype(BF16))
            ys[q].append(yz[:RC] + y0[i])
            zc[q] = yz[RC:] + h_mat[i]
    for q in range(npair):
        z_ref[q] = zc[q]
        p = pairs[q]
        y = jnp.concatenate(ys[q], axis=0)
        mean = head_sum(y) * (1.0 / B_HEAD_DIM)
        yc = y - mean
        var = head_sum(yc * yc) * (1.0 / B_HEAD_DIM)
        yn = yc * lax.rsqrt(var + GN_EPS) * p["gn_g"] + p["gn_b"]
        o_ref[:, q * LANES:(q + 1) * LANES] = ((yn + p["bonus"]) * p["g"]).astype(o_ref.dtype)


def _rwkv(hb, par, mu_wa, mu_gd, wdec, wic, wg, bsz, seq):
    nt = seq // RTS
    width = RNP * LANES
    ngroup = B_WIDTH // width
    cb = B_WIDTH // width
    small = (3 * B_WIDTH) // LANES

    def main(shape_w, colblk):
        return pl.BlockSpec((RTS, shape_w), lambda b, p, t: (b * nt + t, colblk(p)))

    def prev(shape_w, colblk):
        return pl.BlockSpec((8, shape_w), lambda b, p, t: (jnp.maximum((b * nt + t) * (RTS // 8) - 1, 0), colblk(p)))

    wide = [lambda p: p, lambda p: cb + p, lambda p: 2 * cb + p]
    narrow = [lambda p: small, lambda p: small + 1]
    in_specs = ([main(width, c) for c in wide] + [main(LANES, c) for c in narrow]
                + [prev(width, c) for c in wide] + [prev(LANES, c) for c in narrow] + [
        pl.BlockSpec((16, width), lambda b, p, t: (0, p)),
        pl.BlockSpec((1, LANES), lambda b, p, t: (0, 0)),
        pl.BlockSpec((1, LANES), lambda b, p, t: (0, 0)),
        pl.BlockSpec((LANES, width), lambda b, p, t: (0, p)),
        pl.BlockSpec((LANES, width), lambda b, p, t: (0, p)),
        pl.BlockSpec((LANES, width), lambda b, p, t: (0, p)),
    ])
    return pl.pallas_call(
        _rwkv_kernel,
        grid=(bsz, ngroup, nt),
        in_specs=in_specs,
        out_specs=pl.BlockSpec((RTS, width), lambda b, p, t: (b * nt + t, p)),
        out_shape=jax.ShapeDtypeStruct((bsz * seq, B_WIDTH), BF16),
        scratch_shapes=[pltpu.VMEM((RNP, LANES, LANES), F32)],
        compiler_params=_cparams(("arbitrary", "arbitrary", "arbitrary")), name="rwkv7",
    )(*([hb] * 10), par, mu_wa, mu_gd, wdec, wic, wg)


def _pack_rwkv_params(mu, w0, w_decay, a0, w_iclr, k_k, k_a, r_k, gn_g, gn_b):
    bw = B_WIDTH
    zero = jnp.zeros((bw,), F32)
    rows = [w0, a0, k_k, k_a, r_k.reshape(bw), gn_g, gn_b, zero,
            mu[:bw], mu[bw:2 * bw], mu[2 * bw:3 * bw], zero, zero, zero, zero, zero]
    par = jnp.stack(rows)
    mu_wa = mu[3 * bw:3 * bw + LANES][None]
    mu_gd = mu[3 * bw + LANES:][None]
    zpad = jnp.zeros((DECAY_LORA, bw), F32)
    wdec = jnp.concatenate([w_decay, zpad], axis=0)
    wic = jnp.concatenate([zpad, w_iclr], axis=0)
    return par, mu_wa, mu_gd, wdec, wic


def _outproj_kernel(x_ref, a_ref, b_ref, wa_ref, wb_ref, g_ref, be_ref, o_ref):
    mix = _dot(a_ref[...], wa_ref[...]) + _dot(b_ref[...], wb_ref[...])
    o_ref[...] = _ln(ALPHA * x_ref[...] + mix, g_ref[...], be_ref[...])


def _outproj(x, a_out, b_out, w_out, g, b, tm=512):
    t, d = x.shape
    wa = w_out[:A_WIDTH].astype(BF16)
    wb = w_out[A_WIDTH:].astype(BF16)
    row = lambda i: (i, 0)
    const = lambda i: (0, 0)
    return pl.pallas_call(
        _outproj_kernel, grid=(t // tm,),
        in_specs=[pl.BlockSpec((tm, d), row), pl.BlockSpec((tm, A_WIDTH), row), pl.BlockSpec((tm, B_WIDTH), row),
                  pl.BlockSpec((A_WIDTH, d), const), pl.BlockSpec((B_WIDTH, d), const),
                  pl.BlockSpec((1, d), const), pl.BlockSpec((1, d), const)],
        out_specs=pl.BlockSpec((tm, d), row), out_shape=jax.ShapeDtypeStruct((t, d), F32),
        compiler_params=_cparams(("arbitrary",)), name="outproj_ln",
    )(x, a_out, b_out, wa, wb, g, b)


def _xor_partner(x, lane, bit):
    up = pltpu.roll(x, bit, 1)
    down = pltpu.roll(x, LANES - bit, 1)
    return jnp.where((lane & bit) != 0, up, down)


def _moe_gates(xf, wr, br):
    tm = xf.shape[0]
    lane = lax.broadcasted_iota(jnp.int32, (tm, LANES), 1)
    valid = lane < N_EXPERTS
    x_hi, x_lo = _split2(xf)
    w_hi, w_lo = _split2(wr)
    scores = jax.nn.sigmoid(_dot(x_hi, w_hi) + _dot(x_hi, w_lo) + _dot(x_lo, w_hi))
    sel = jnp.where(valid, scores + br, NEG)
    p1 = _xor_partner(sel, lane, 1)
    hi1, lo1 = jnp.maximum(sel, p1), jnp.minimum(sel, p1)
    hi2, lo2 = _xor_partner(hi1, lane, 2), _xor_partner(lo1, lane, 2)
    gscore = jnp.maximum(hi1, hi2) + jnp.maximum(jnp.minimum(hi1, hi2), jnp.maximum(lo1, lo2))
    gscore = jnp.where(valid, gscore, NEG)
    lanef = lane.astype(F32)
    gbest = jnp.max(gscore, axis=1, keepdims=True)
    first = jnp.min(jnp.where(gscore == gbest, lanef, float(LANES)), axis=1, keepdims=True)
    in_group = jnp.floor(lanef * (1.0 / EXPERTS_PER_GROUP)) == jnp.floor(first * (1.0 / EXPERTS_PER_GROUP))
    masked = jnp.where(in_group, sel, NEG)
    m1 = jnp.max(masked, axis=1, keepdims=True)
    i1 = jnp.min(jnp.where(masked == m1, lanef, float(LANES)), axis=1, keepdims=True)
    pick1 = lanef == i1
    masked2 = jnp.where(pick1, NEG, masked)
    m2 = jnp.max(masked2, axis=1, keepdims=True)
    i2 = jnp.min(jnp.where(masked2 == m2, lanef, float(LANES)), axis=1, keepdims=True)
    pick2 = lanef == i2
    s1 = jnp.sum(jnp.where(pick1, scores, 0.0), axis=1, keepdims=True)
    s2 = jnp.sum(jnp.where(pick2, scores, 0.0), axis=1, keepdims=True)
    tot = s1 + s2
    return jnp.where(pick1, s1 / tot, 0.0) + jnp.where(pick2, s2 / tot, 0.0)


def _moe_kernel(x_ref, wr_ref, br_ref, wgu_ref, wd_ref, g_ref, b_ref, o_ref, gate_ref, acc_ref, xb_ref):
    e = pl.program_id(1)

    @pl.when(e == 0)
    def _():
        xf = x_ref[...]
        gate_ref[...] = _moe_gates(xf, wr_ref[...], br_ref[...])
        xb_ref[...] = xf.astype(BF16)
        acc_ref[...] = jnp.zeros(acc_ref.shape, F32)

    gates = gate_ref[...]
    lane = lax.broadcasted_iota(jnp.int32, gates.shape, 1)
    gcol = jnp.sum(jnp.where(lane == e, gates, 0.0), axis=1, keepdims=True)
    hu = _dot(xb_ref[...], wgu_ref[0])
    hg, up = hu[:, :D_EXPERT], hu[:, D_EXPERT:]
    hid = (hg * jax.nn.sigmoid(hg)) * up * gcol
    acc_ref[...] += _dot(hid.astype(BF16), wd_ref[0])

    @pl.when(e == pl.num_programs(1) - 1)
    def _():
        o_ref[...] = _ln(ALPHA * x_ref[...] + acc_ref[...], g_ref[...], b_ref[...])


def _moe(x, w_router, b_router, w_gate, w_up, w_down, g, b, tm=1024):
    t, d = x.shape
    ne = w_gate.shape[0]
    wgu = jnp.concatenate([w_gate, w_up], axis=2).astype(BF16)
    wd = w_down.astype(BF16)
    wr = jnp.pad(w_router, ((0, 0), (0, LANES - ne)))
    br = jnp.pad(b_router, (0, LANES - ne))[None]
    row = lambda i, e: (i, 0)
    const = lambda i, e: (0, 0)
    return pl.pallas_call(
        _moe_kernel, grid=(t // tm, ne),
        in_specs=[pl.BlockSpec((tm, d), row), pl.BlockSpec((d, LANES), const), pl.BlockSpec((1, LANES), const),
                  pl.BlockSpec((1, d, 2 * D_EXPERT), lambda i, e: (e, 0, 0)),
                  pl.BlockSpec((1, D_EXPERT, d), lambda i, e: (e, 0, 0)),
                  pl.BlockSpec((1, d), const), pl.BlockSpec((1, d), const)],
        out_specs=pl.BlockSpec((tm, d), row), out_shape=jax.ShapeDtypeStruct((t, d), F32),
        scratch_shapes=[pltpu.VMEM((tm, LANES), F32), pltpu.VMEM((tm, d), F32), pltpu.VMEM((tm, d), BF16)],
        compiler_params=_cparams(("arbitrary", "arbitrary")), name="moe_ln",
    )(x, wr, br, wgu, wd, g, b)


def _ple_kernel(x_ref, p_ref, wg_ref, wp_ref, g_ref, b_ref, o_ref):
    x = x_ref[...]
    gate = jax.nn.sigmoid(_dot(x.astype(BF16), wg_ref[...]))
    pe = _dot(p_ref[...].astype(BF16), wp_ref[...])
    o_ref[...] = _ln(ALPHA * x + gate * pe, g_ref[...], b_ref[...])


def _ple(x, p, w_gate, w_ple, g, b, tm=512):
    t, d = x.shape
    row = lambda i: (i, 0)
    const = lambda i: (0, 0)
    return pl.pallas_call(
        _ple_kernel, grid=(t // tm,),
        in_specs=[pl.BlockSpec((tm, d), row), pl.BlockSpec((tm, PLE_DIM), row),
                  pl.BlockSpec((d, d), const), pl.BlockSpec((PLE_DIM, d), const),
                  pl.BlockSpec((1, d), const), pl.BlockSpec((1, d), const)],
        out_specs=pl.BlockSpec((tm, d), row), out_shape=jax.ShapeDtypeStruct((t, d), F32),
        compiler_params=_cparams(("arbitrary",)), name="ple_ln",
    )(x, p, w_gate.astype(BF16), w_ple.astype(BF16), g, b)


def kernel(x, p, ln_in_g, ln_in_b, w_in, w_out, mu_shift, w0, w_decay, a0, w_iclr, w_gate_up, k_k, k_a, r_k,
           gn_g, gn_b, rel_bias, w_router, b_router, w_exp_gate, w_exp_up, w_exp_down, w_ple, w_ple_gate,
           ln_mix_g, ln_mix_b, ln_ffn_g, ln_ffn_b, ln_ple_g, ln_ple_b):
    bsz, seq, d = x.shape
    t = bsz * seq
    depth = w_in.shape[0]
    xf = x.reshape(t, d)
    near_bias = _near_bias(rel_bias)
    for i in range(depth):
        w_rm, w_fm = _pack_w_in(w_in[i])
        outs = _inproj(xf, ln_in_g[None], ln_in_b[None], w_rm, w_fm, apply_ln=(i == 0))
        k_rm, kiki, hb, qt, qit, vt, wit = outs[:7]
        if i == 0:
            xf = outs[7]
        a_out = _dsa(qt, qit, wit, k_rm, vt, kiki, near_bias, bsz, seq)
        par, mu_wa, mu_gd, wdec, wic = _pack_rwkv_params(mu_shift[i], w0[i], w_decay[i], a0[i], w_iclr[i],
                                                         k_k[i], k_a[i], r_k[i], gn_g[i], gn_b[i])
        b_out = _rwkv(hb, par, mu_wa, mu_gd, wdec, wic, w_gate_up[i], bsz, seq)
        xf = _outproj(xf, a_out, b_out, w_out[i], ln_mix_g[i][None], ln_mix_b[i][None])
        xf = _moe(xf, w_router, b_router, w_exp_gate[i], w_exp_up[i], w_exp_down[i],
                  ln_ffn_g[i][None], ln_ffn_b[i][None])
        xf = _ple(xf, p[i].reshape(t, PLE_DIM), w_ple_gate[i], w_ple[i], ln_ple_g[i][None], ln_ple_b[i][None])
    return xf.reshape(bsz, seq, d)
```

```python
import functools
import math

import numpy as np
import jax
import jax.numpy as jnp
from jax import lax
from jax.experimental import pallas as pl
from jax.experimental.pallas import tpu as pltpu

D_MODEL = 1024
DEPTH = 2
CHUNK = 64
A_HEAD_DIM = 64
A_WIDTH = D_MODEL // 2
A_HEADS = A_WIDTH // A_HEAD_DIM
IDX_HEADS = 8
IDX_DIM = 64
TOPK_MAX = 256
N_BUCKETS = 32
MAX_DISTANCE = 128
B_HEAD_DIM = 64
B_WIDTH = D_MODEL - A_WIDTH
B_HEADS = B_WIDTH // B_HEAD_DIM
DECAY_LORA = 64
ICLR_LORA = 64
GATE_LORA = 128
A_SIZES = (A_WIDTH, A_WIDTH, A_WIDTH, IDX_HEADS * IDX_DIM, IDX_DIM, IDX_HEADS)
B_SIZES = (B_WIDTH, B_WIDTH, B_WIDTH, DECAY_LORA, ICLR_LORA, GATE_LORA)
A_COLS = sum(A_SIZES)
B_COLS = sum(B_SIZES)
N_EXPERTS = 16
N_GROUPS = 4
EXPERTS_PER_GROUP = N_EXPERTS // N_GROUPS
D_EXPERT = 256
PLE_DIM = 256
ALPHA = (2 * DEPTH) ** 0.25
LN_EPS = 1e-5
GN_EPS = 64e-5
NEG = -1e30

LANES = 128
VMEM_LIMIT = 56 * 1024 * 1024
F32 = jnp.float32
BF16 = jnp.bfloat16
HI = lax.Precision.HIGHEST
NT_DIMS = (((1,), (1,)), ((), ()))


def _cparams(sem):
    return pltpu.CompilerParams(dimension_semantics=sem, vmem_limit_bytes=VMEM_LIMIT)


def _ln(x, g, b):
    mu = jnp.mean(x, axis=-1, keepdims=True)
    xc = x - mu
    var = jnp.mean(xc * xc, axis=-1, keepdims=True)
    return xc * lax.rsqrt(var + LN_EPS) * g + b


def _dot(a, b):
    return jnp.dot(a, b, preferred_element_type=F32)


def _dot_hi(a, b):
    return jnp.dot(a, b, preferred_element_type=F32, precision=HI)


def _dot_nt(a, b):
    return lax.dot_general(a, b, NT_DIMS, preferred_element_type=F32)


def _dot_nt_hi(a, b):
    return lax.dot_general(a, b, NT_DIMS, preferred_element_type=F32, precision=HI)


def _split2(x):
    hi = x.astype(BF16)
    return hi, (x - hi.astype(F32)).astype(BF16)


def _split3(x):
    hi = x.astype(BF16)
    rest = x - hi.astype(F32)
    mid = rest.astype(BF16)
    return hi, mid, (rest - mid.astype(F32)).astype(BF16)


IN_NCHUNK = 512


WI_ROWS = 16
LOG2E = math.log2(math.e)
Q_SCALE = A_HEAD_DIM ** -0.5 * LOG2E


def _inproj_kernel(x_ref, g_ref, b_ref, wr_ref, wf_ref, *out_refs, apply_ln):
    x = x_ref[...]
    if apply_ln:
        x = _ln(x, g_ref[...], b_ref[...])
        out_refs[7][...] = x
    xb = x.astype(BF16)
    k_ref, kk_ref, hb_ref, qt_ref, qit_ref, vt_ref, wit_ref = out_refs[:7]
    aw = A_WIDTH
    k_ref[...] = _dot(xb, wr_ref[:, :aw]).astype(BF16)
    kk_ref[...] = _dot(xb, wr_ref[:, aw:aw + LANES]).astype(BF16)
    nb = hb_ref.shape[1]
    for c0 in range(0, nb, IN_NCHUNK):
        c1 = min(c0 + IN_NCHUNK, nb)
        hb_ref[:, c0:c1] = _dot(xb, wr_ref[:, aw + LANES + c0:aw + LANES + c1])
    qt_ref[...] = (_dot_nt(wf_ref[:aw], xb) * Q_SCALE).astype(BF16)
    qit_ref[...] = _dot_nt(wf_ref[aw:2 * aw], xb).astype(BF16)
    vt = _dot_nt(wf_ref[2 * aw:3 * aw], xb).astype(BF16)
    for n in range(vt_ref.shape[0]):
        vt_ref[n] = vt[:, n * TQ:(n + 1) * TQ]
    wit_ref[...] = _dot_nt(wf_ref[3 * aw:], xb)


def _inproj(x, g, b, w_rm, w_fm, apply_ln, tm=512):
    t, d = x.shape
    aw = A_WIDTH
    nb = w_rm.shape[1] - aw - LANES
    row = lambda i: (i, 0)
    col = lambda i: (0, i)
    const = lambda i: (0, 0)
    out_shape = [jax.ShapeDtypeStruct((t, aw), BF16), jax.ShapeDtypeStruct((t, LANES), BF16),
                 jax.ShapeDtypeStruct((t, nb), F32), jax.ShapeDtypeStruct((aw, t), BF16),
                 jax.ShapeDtypeStruct((aw, t), BF16), jax.ShapeDtypeStruct((t // TQ, aw, TQ), BF16),
                 jax.ShapeDtypeStruct((WI_ROWS, t), F32)]
    out_specs = [pl.BlockSpec((tm, aw), row), pl.BlockSpec((tm, LANES), row), pl.BlockSpec((tm, nb), row),
                 pl.BlockSpec((aw, tm), col), pl.BlockSpec((aw, tm), col),
                 pl.BlockSpec((tm // TQ, aw, TQ), lambda i: (i, 0, 0)), pl.BlockSpec((WI_ROWS, tm), col)]
    if apply_ln:
        out_shape.append(jax.ShapeDtypeStruct((t, d), F32))
        out_specs.append(pl.BlockSpec((tm, d), row))
    return pl.pallas_call(
        functools.partial(_inproj_kernel, apply_ln=apply_ln),
        grid=(t // tm,),
        in_specs=[pl.BlockSpec((tm, d), row), pl.BlockSpec((1, d), const), pl.BlockSpec((1, d), const),
                  pl.BlockSpec(w_rm.shape, const), pl.BlockSpec(w_fm.shape, const)],
        out_specs=out_specs, out_shape=out_shape,
        compiler_params=_cparams(("arbitrary",)), name="inproj_ln" if apply_ln else "inproj",
    )(x, g, b, w_rm, w_fm)


def _pack_w_in(w):
    d = w.shape[0]
    aw = A_WIDTH
    cuts = np.cumsum(A_SIZES)
    wq, wk, wv, wqi = w[:, :aw], w[:, aw:2 * aw], w[:, 2 * aw:3 * aw], w[:, cuts[2]:cuts[3]]
    wki = w[:, cuts[3]:cuts[4]]
    wwi = w[:, cuts[4]:cuts[5]]
    w_rm = jnp.concatenate([wk, wki, wki, w[:, A_COLS:]], axis=1).astype(BF16)
    pad = jnp.zeros((d, WI_ROWS - IDX_HEADS), w.dtype)
    w_fm = jnp.concatenate([wq, wqi, wv, wwi, pad], axis=1).T.astype(BF16)
    return w_rm, w_fm


TQ = 256
ROW_GROUP = 32
BIGF = 3.0e38


def _t5_bucket_np(rel):
    nb = N_BUCKETS // 2
    max_exact = nb // 2
    ret = np.where(rel > 0, nb, 0)
    n = np.abs(rel)
    nf = np.maximum(n, 1).astype(np.float64)
    large = max_exact + np.floor(np.log(nf / max_exact) / math.log(MAX_DISTANCE / max_exact)
                                 * (nb - max_exact) + 1e-9).astype(np.int64)
    large = np.minimum(large, nb - 1)
    return ret + np.where(n < max_exact, n, large)


def _near_bias(rel_bias):
    qpos = np.arange(TQ)[None, :]
    kpos = np.arange(TQ)[:, None]
    bk = np.stack([_t5_bucket_np(kpos - qpos), _t5_bucket_np(kpos - TQ - qpos)])
    far = N_BUCKETS // 2 - 1
    tab = (rel_bias - rel_bias[far][None, :]) * LOG2E
    onehot = jnp.asarray(bk[..., None] == np.arange(N_BUCKETS)).astype(F32)
    return jnp.einsum('nkqb,bh->nhkq', onehot, tab.astype(F32), precision=HI)


BISECT_STEPS = 12
SUBLANES = 8


def _paired_loop(n, body, carry):
    def pair(j, c):
        return body(2 * j + 1, 1, body(2 * j, 0, c))

    carry = lax.fori_loop(0, lax.shift_right_logical(n, 1), pair, carry)
    return lax.cond((n & 1) == 1, lambda c: body(n - 1, 0, c), lambda c: c, carry)


def _fold8(x, op):
    return op(x.reshape(x.shape[0] // SUBLANES, SUBLANES, x.shape[1]), axis=0)


def _dsa_kernel(qt_ref, qit_ref, wit_ref, k_ref, vt_ref, kk_ref, bias_ref, o_ref,
                s_ref, m_ref, l_ref, acc_ref, p_ref, *, ksel):
    i = pl.program_id(1)
    nt = i + 1
    tq = qt_ref.shape[1]
    kself = float(ksel)
    idx_scale = (IDX_HEADS ** -0.5) * (IDX_DIM ** -0.5)

    sub = lax.broadcasted_iota(jnp.int32, (LANES, tq), 0)
    lower = sub < (LANES // 2)
    zero_b = jnp.zeros((LANES, tq), BF16)

    def head_parts(ref):
        parts = []
        for h in range(A_HEADS):
            j, s = divmod(h, 2)
            blk = ref[j * LANES:(j + 1) * LANES, :]
            parts.append(jnp.where(lower if s == 0 else jnp.logical_not(lower), blk, zero_b))
        return parts

    def key_rows(kt):
        return pl.ds(pl.multiple_of(kt * tq, tq), tq)

    qi_parts = head_parts(qit_ref)
    wi = wit_ref[...] * idx_scale
    wrows = [wi[h:h + 1, :] for h in range(IDX_HEADS)]
    krow = lax.broadcasted_iota(jnp.int32, (tq, tq), 0)
    qcol = lax.broadcasted_iota(jnp.int32, (tq, tq), 1)
    adm_diag = (krow // CHUNK) <= (qcol // CHUNK)

    def score_tile(kt):
        kk = kk_ref[key_rows(kt), :]
        acc = jnp.zeros((tq, tq), F32)
        for h in range(IDX_HEADS):
            acc = acc + wrows[h] * jnp.maximum(_dot(kk, qi_parts[h]), 0.0)
        return acc

    def p1_body(kt, _, carry):
        sc = score_tile(kt)
        s_ref[kt] = sc
        return jnp.minimum(carry[0], _fold8(sc, jnp.min)), jnp.maximum(carry[1], _fold8(sc, jnp.max))

    mn8, mx8 = _paired_loop(i, p1_body, (jnp.full((SUBLANES, tq), BIGF, F32), jnp.full((SUBLANES, tq), -BIGF, F32)))
    sc = score_tile(i)
    s_ref[i] = jnp.where(adm_diag, sc, NEG)
    mn8 = jnp.minimum(mn8, _fold8(jnp.where(adm_diag, sc, BIGF), jnp.min))
    mx8 = jnp.maximum(mx8, _fold8(jnp.where(adm_diag, sc, -BIGF), jnp.max))
    rowmin = jnp.min(mn8, axis=0, keepdims=True)
    rowmax = jnp.max(mx8, axis=0, keepdims=True)

    def reduce_tiles(fn, inits, ops):
        def body(kt, _, carry):
            return fn(s_ref[kt], kt, carry)

        carry = _paired_loop(nt, body, tuple(jnp.full((SUBLANES, tq), v, F32) for v in inits))
        return [op(c, axis=0, keepdims=True) for op, c in zip(ops, carry)]

    qpos = i * tq + lax.broadcasted_iota(jnp.int32, (1, tq), 1)
    n_adm = (qpos // CHUNK + 1) * CHUNK
    trivial = n_adm <= ksel
    lo0 = jnp.where(trivial, 0.1 * NEG, rowmin)
    hi0 = jnp.where(trivial, 0.1 * NEG, rowmax)

    def count_ge(mid):
        def cnt(s, kt, carry):
            return (carry[0] + _fold8(jnp.where(s >= mid, 1.0, 0.0), jnp.sum),)

        return reduce_tiles(cnt, (0.0,), (jnp.sum,))[0]

    def a_body(_, st):
        lo, hi, clo = st
        mid = 0.5 * (lo + hi)
        c = count_ge(mid)
        ok = c >= kself
        return jnp.where(ok, mid, lo), jnp.where(ok, hi, mid), jnp.where(ok, c, clo)

    lo1, hi1, clo1 = lax.fori_loop(0, BISECT_STEPS, a_body, (lo0, hi0, n_adm.astype(F32)))
    hi1 = jnp.where(jnp.logical_or(clo1 == kself, trivial), lo1, hi1)

    def b_cond(st):
        lo, hi = st
        return jnp.max(jnp.where(lo < hi, 1.0, 0.0)) > 0.0

    def b_body(st):
        lo, hi = st
        mid = 0.5 * (lo + hi)
        mid = jnp.where(mid > lo, mid, hi)

        def cnt(s, kt, carry):
            c, vge, vlt = carry
            ge = s >= mid
            return (c + _fold8(jnp.where(ge, 1.0, 0.0), jnp.sum),
                    jnp.minimum(vge, _fold8(jnp.where(ge, s, BIGF), jnp.min)),
                    jnp.maximum(vlt, _fold8(jnp.where(ge, -BIGF, s), jnp.max)))

        c, vge, vlt = reduce_tiles(cnt, (0.0, BIGF, -BIGF), (jnp.sum, jnp.min, jnp.max))
        ok = c >= kself
        return jnp.where(ok, vge, lo), jnp.where(ok, hi, vlt)

    thr, _ = lax.while_loop(b_cond, b_body, (lo1, hi1))

    def count_ties(s, kt, carry):
        return (carry[0] + _fold8(jnp.where(s >= thr, 1.0, 0.0), jnp.sum),
                carry[1] + _fold8(jnp.where(s > thr, 1.0, 0.0), jnp.sum))

    c_ge, c_gt = reduce_tiles(count_ties, (0.0, 0.0), (jnp.sum, jnp.sum))
    need = kself - c_gt
    tied = c_ge > kself
    s_total = float(s_ref.shape[0] * tq)
    jlo0 = jnp.where(tied, 0.0, s_total - 1.0)
    jhi0 = jnp.where(tied, (nt * tq).astype(F32), s_total)
    kidx = krow.astype(F32)

    def tie_cond(st):
        jlo, jhi = st
        return jnp.max(jhi - jlo) > 1.0

    def tie_body(st):
        jlo, jhi = st
        mid = jnp.floor(0.5 * (jlo + jhi))

        def cnt(s, kt, carry):
            idx = kidx + (kt * tq).astype(F32)
            return (carry[0] + _fold8(jnp.where(jnp.logical_and(s == thr, idx < mid), 1.0, 0.0), jnp.sum),)

        (c,) = reduce_tiles(cnt, (0.0,), (jnp.sum,))
        ok = c >= need
        done = (jhi - jlo) <= 1.0
        return (jnp.where(jnp.logical_or(ok, done), jlo, mid),
                jnp.where(jnp.logical_and(ok, jnp.logical_not(done)), mid, jhi))

    _, jcut = lax.while_loop(tie_cond, tie_body, (jlo0, jhi0))

    q_parts = head_parts(qt_ref)
    m_ref[...] = jnp.full(m_ref.shape, NEG, F32)
    l_ref[...] = jnp.zeros(l_ref.shape, F32)
    acc_ref[...] = jnp.zeros(acc_ref.shape, F32)

    def logits(kt, near):
        mb = s_ref[kt]
        for j in range(A_HEADS // 2):
            kp = k_ref[key_rows(kt), j * LANES:(j + 1) * LANES]
            for s_half in range(2):
                h = 2 * j + s_half
                lg = _dot(kp, q_parts[h]) + mb
                if near is not None:
                    lg = lg + bias_ref[near, h]
                yield h, lg

    def sweep_max(kt, near, slot):
        s = s_ref[kt]
        idx = kidx + (kt * tq).astype(F32)
        sel = jnp.logical_or(s > thr, jnp.logical_and(s == thr, idx < jcut))
        s_ref[kt] = jnp.where(sel, 0.0, NEG)
        for h, lg in logits(kt, near):
            m_ref[h] = jnp.maximum(m_ref[h], _fold8(lg, jnp.max))

    def sweep_sum(kt, near, slot):
        for h, lg in logits(kt, near):
            p = jnp.exp2(lg - m_ref[h][:1])
            l_ref[h] += _fold8(p, jnp.sum)
            p_ref[slot, h] = p.astype(BF16)
        for h in range(A_HEADS):
            acc_ref[h] += _dot(vt_ref[kt, h * A_HEAD_DIM:(h + 1) * A_HEAD_DIM, :], p_ref[slot, h])

    def all_tiles(sweep):
        def body(kt, slot, c):
            sweep(kt, None, slot)
            return c

        _paired_loop(jnp.maximum(i - 1, 0), body, 0)

        @pl.when(i >= 1)
        def _():
            sweep(i - 1, 1, 1)

        sweep(i, 0, 0)

    all_tiles(sweep_max)
    for h in range(A_HEADS):
        m_ref[h] = jnp.broadcast_to(jnp.max(m_ref[h], axis=0, keepdims=True), (SUBLANES, tq))
    all_tiles(sweep_sum)

    for j in range(A_HEADS // 2):
        outs = [acc_ref[h] / jnp.sum(l_ref[h], axis=0, keepdims=True) for h in (2 * j, 2 * j + 1)]
        o_ref[:, j * LANES:(j + 1) * LANES] = jnp.concatenate(outs, axis=0).T.astype(o_ref.dtype)


def _dsa(qt, qit, wit, k, vt, kiki, near_bias, bsz, seq):
    nq = seq // TQ
    ksel = min(TOPK_MAX, seq // 4)
    qcol = lambda b, i: (0, b * nq + i)
    return pl.pallas_call(
        functools.partial(_dsa_kernel, ksel=ksel),
        grid=(bsz, nq),
        in_specs=[
            pl.BlockSpec((A_WIDTH, TQ), qcol),
            pl.BlockSpec((A_WIDTH, TQ), qcol),
            pl.BlockSpec((WI_ROWS, TQ), qcol),
            pl.BlockSpec((seq, A_WIDTH), lambda b, i: (b, 0)),
            pl.BlockSpec((nq, A_WIDTH, TQ), lambda b, i: (b, 0, 0)),
            pl.BlockSpec((seq, LANES), lambda b, i: (b, 0)),
            pl.BlockSpec((2, A_HEADS, TQ, TQ), lambda b, i: (0, 0, 0, 0)),
        ],
        out_specs=pl.BlockSpec((TQ, A_WIDTH), lambda b, i: (b * nq + i, 0)),
        out_shape=jax.ShapeDtypeStruct((bsz * seq, A_WIDTH), BF16),
        scratch_shapes=[pltpu.VMEM((nq, TQ, TQ), F32), pltpu.VMEM((A_HEADS, SUBLANES, TQ), F32),
                        pltpu.VMEM((A_HEADS, SUBLANES, TQ), F32), pltpu.VMEM((A_HEADS, A_HEAD_DIM, TQ), F32),
                        pltpu.VMEM((2, A_HEADS, TQ, TQ), BF16)],
        compiler_params=_cparams(("arbitrary", "arbitrary")), name="dsa_attention",
    )(qt, qit, wit, k, vt, kiki, near_bias)


RC = 64
RTS = 256


RNP = 4


def _rwkv_kernel(r_ref, k_ref, v_ref, wa_ref, gd_ref, rp_ref, kp_ref, vp_ref, wap_ref, gdp_ref,
                 par_ref, muw_ref, mug_ref, wdec_ref, wic_ref, wg_ref, o_ref, z_ref):
    t = pl.program_id(2)
    ts = r_ref.shape[0]

    @pl.when(t == 0)
    def _():
        z_ref[...] = jnp.zeros(z_ref.shape, F32)

    first = jnp.where(t == 0, 0.0, 1.0)

    def shifted(ref, pref, mu):
        x = ref[...]
        row = lax.broadcasted_iota(jnp.int32, x.shape, 0)
        prev = jnp.where(row == 0, pref[7:8, :] * first, pltpu.roll(x, 1, 0))
        return x + (prev - x) * mu

    par = par_ref[...]
    r_all = shifted(r_ref, rp_ref, par[8:9])
    k_all = shifted(k_ref, kp_ref, par[9:10])
    v_all = shifted(v_ref, vp_ref, par[10:11])
    wa = shifted(wa_ref, wap_ref, muw_ref[...])
    gd = shifted(gd_ref, gdp_ref, mug_ref[...])

    lane = lax.broadcasted_iota(jnp.int32, (LANES, LANES), 1)
    rowl = lax.broadcasted_iota(jnp.int32, (LANES, LANES), 0)
    same_head = (lane // B_HEAD_DIM) == (rowl // B_HEAD_DIM)
    gones = jnp.where(same_head, 1.0, 0.0)
    gones16 = gones.astype(BF16)
    strict = jnp.logical_and(same_head, rowl > lane)
    incl = jnp.logical_and(same_head, rowl >= lane)
    eye = jnp.where(lane == rowl, 1.0, 0.0)
    head0 = lax.broadcasted_iota(jnp.int32, (RC, LANES), 1) < B_HEAD_DIM

    def head_sum(x):
        hi, lo = _split2(x)
        return _dot(hi, gones16) + _dot(lo, gones16)

    def stack_heads(x):
        return jnp.concatenate([jnp.where(head0, x, 0.0), jnp.where(head0, 0.0, x)], axis=0)

    def fold_heads(x):
        return x[:RC] + x[RC:]

    def twice(x):
        return jnp.concatenate([x, x], axis=0)

    rt = lax.broadcasted_iota(jnp.int32, (ts, ts), 0)
    ct = lax.broadcasted_iota(jnp.int32, (ts, ts), 1)
    tri16 = jnp.where(jnp.logical_and((rt // RC) == (ct // RC), rt >= ct), 1.0, 0.0).astype(BF16)
    th_hi, th_lo = _split2(jnp.tanh(wa))
    wa16 = wa.astype(BF16)
    sg16 = jax.nn.sigmoid(gd).astype(BF16)
    nchunk = ts // RC
    npair = r_ref.shape[1] // LANES

    pairs = []
    for q in range(npair):
        ls = slice(q * LANES, (q + 1) * LANES)
        w0, a0, k_k, k_a, r_k, gn_g, gn_b = (par[n:n + 1, ls] for n in range(7))
        r, k, v = r_all[:, ls], k_all[:, ls], v_all[:, ls]
        wd_hi, wd_lo = _split2(wdec_ref[:, ls])
        z = w0 + (_dot(th_hi, wd_hi) + _dot(th_hi, wd_lo) + _dot(th_lo, wd_hi))
        w_log = -(jnp.maximum(-z, 0.0) + jnp.log1p(jnp.exp(-jnp.abs(z)))) - 0.5
        e = jnp.exp(w_log)
        a = jax.nn.sigmoid(a0 + _dot(wa16, wic_ref[:, ls].astype(BF16)))
        g = _dot(sg16, wg_ref[:, ls].astype(BF16))
        kkr = k * k_k
        kk = kkr / jnp.maximum(jnp.sqrt(head_sum(kkr * kkr)), 1e-12)
        k2 = k * (1.0 + (a - 1.0) * k_a)
        bonus = head_sum(r * k2 * r_k) * v
        kka = kk * a
        e_hi, e_mid, e_lo = _split3(e)
        cum = _dot(tri16, e_hi) + _dot(tri16, e_mid) + _dot(tri16, e_lo)
        pairs.append(dict(r_hat=r * jnp.exp(-cum), a_hat=-kk * jnp.exp(e - cum), cum=cum, kka=kka, k2=k2, v=v,
                          b_til=(kka * jnp.exp(cum)).astype(BF16), k_til=(k2 * jnp.exp(cum)).astype(BF16),
                          v16=v.astype(BF16), bonus=bonus, g=g, gn_g=gn_g, gn_b=gn_b))

    items = [(q, c) for q in range(npair) for c in range(nchunk)]
    n = range(len(items))
    sl = [slice(c * RC, (c + 1) * RC) for _, c in items]
    pq = [pairs[q] for q, _ in items]
    tot = [pq[i]["cum"][sl[i].stop - 1:sl[i].stop] for i in n]
    rem = [jnp.exp(pq[i]["cum"][sl[i]] - tot[i]) for i in n]
    a_s = [stack_heads(pq[i]["a_hat"][sl[i]]).astype(BF16) for i in n]
    r_s = [stack_heads(pq[i]["r_hat"][sl[i]]).astype(BF16) for i in n]
    v_s = [stack_heads(pq[i]["v"][sl[i]]).astype(BF16) for i in n]
    ar = [jnp.concatenate([a_s[i], r_s[i]], axis=0) for i in n]
    bk = [jnp.concatenate([twice(pq[i]["b_til"][sl[i]]), twice(pq[i]["k_til"][sl[i]])], axis=0) for i in n]
    m_all = [_dot_nt(ar[i], bk[i]) for i in n]
    l_ab = [jnp.where(strict, m_all[i][:LANES, :LANES], 0.0) for i in n]
    l_ak = [jnp.where(strict, m_all[i][:LANES, LANES:], 0.0).astype(BF16) for i in n]
    m_rb = [jnp.where(incl, m_all[i][LANES:, :LANES], 0.0).astype(BF16) for i in n]
    m_rk = [jnp.where(incl, m_all[i][LANES:, LANES:], 0.0).astype(BF16) for i in n]
    inv = [eye + l_ab[i] for i in n]
    pw = [l_ab[i].astype(BF16) for i in n]
    for _ in range(5):
        pw = [_dot(pw[i], pw[i]).astype(BF16) for i in n]
        inv = [inv[i] + _dot(inv[i].astype(BF16), pw[i]) for i in n]
    lv = [_dot(l_ak[i], v_s[i]).astype(BF16) for i in n]
    x_av = [_dot(inv[i].astype(BF16), jnp.concatenate([a_s[i], lv[i]], axis=1)) for i in n]
    y_av = [_dot(m_rb[i], x_av[i].astype(BF16)) for i in n]
    y_kv = [_dot(m_rk[i], v_s[i]) for i in n]
    r_p = [pq[i]["r_hat"][sl[i]] + fold_heads(y_av[i][:, :LANES]) for i in n]
    y0 = [fold_heads(y_av[i][:, LANES:] + y_kv[i]) for i in n]
    gh = [_dot((pq[i]["kka"][sl[i]] * rem[i]).T.astype(BF16), fold_heads(x_av[i]).astype(BF16)) for i in n]
    kv = [_dot((pq[i]["k2"][sl[i]] * rem[i]).T.astype(BF16), pq[i]["v16"][sl[i]]) for i in n]
    g_mat = [eye * jnp.exp(-tot[i]) + gones * gh[i][:, :LANES] for i in n]
    h_mat = [gones * (gh[i][:, LANES:] + kv[i]) for i in n]
    rg = [jnp.concatenate([r_p[i], g_mat[i]], axis=0).astype(BF16) for i in n]

    zc = [z_ref[q] for q in range(npair)]
    ys = [[] for _ in range(npair)]
    for c in range(nchunk):
        for q in range(npair):
            i = q * nchunk + c
            yz = _dot(rg[i], zc[q].astype(BF16))
            ys[q].append(yz[:RC] + y0[i])
            zc[q] = yz[RC:] + h_mat[i]
    for q in range(npair):
        z_ref[q] = zc[q]
        p = pairs[q]
        y = jnp.concatenate(ys[q], axis=0)
        mean = head_sum(y) * (1.0 / B_HEAD_DIM)
        yc = y - mean
        var = head_sum(yc * yc) * (1.0 / B_HEAD_DIM)
        yn = yc * lax.rsqrt(var + GN_EPS) * p["gn_g"] + p["gn_b"]
        o_ref[:, q * LANES:(q + 1) * LANES] = ((yn + p["bonus"]) * p["g"]).astype(o_ref.dtype)


def _rwkv(hb, par, mu_wa, mu_gd, wdec, wic, wg, bsz, seq):
    nt = seq // RTS
    width = RNP * LANES
    ngroup = B_WIDTH // width
    cb = B_WIDTH // width
    small = (3 * B_WIDTH) // LANES

    def main(shape_w, colblk):
        return pl.BlockSpec((RTS, shape_w), lambda b, p, t: (b * nt + t, colblk(p)))

    def prev(shape_w, colblk):
        return pl.BlockSpec((8, shape_w), lambda b, p, t: (jnp.maximum((b * nt + t) * (RTS // 8) - 1, 0), colblk(p)))

    wide = [lambda p: p, lambda p: cb + p, lambda p: 2 * cb + p]
    narrow = [lambda p: small, lambda p: small + 1]
    in_specs = ([main(width, c) for c in wide] + [main(LANES, c) for c in narrow]
                + [prev(width, c) for c in wide] + [prev(LANES, c) for c in narrow] + [
        pl.BlockSpec((16, width), lambda b, p, t: (0, p)),
        pl.BlockSpec((1, LANES), lambda b, p, t: (0, 0)),
        pl.BlockSpec((1, LANES), lambda b, p, t: (0, 0)),
        pl.BlockSpec((LANES, width), lambda b, p, t: (0, p)),
        pl.BlockSpec((LANES, width), lambda b, p, t: (0, p)),
        pl.BlockSpec((LANES, width), lambda b, p, t: (0, p)),
    ])
    return pl.pallas_call(
        _rwkv_kernel,
        grid=(bsz, ngroup, nt),
        in_specs=in_specs,
        out_specs=pl.BlockSpec((RTS, width), lambda b, p, t: (b * nt + t, p)),
        out_shape=jax.ShapeDtypeStruct((bsz * seq, B_WIDTH), BF16),
        scratch_shapes=[pltpu.VMEM((RNP, LANES, LANES), F32)],
        compiler_params=_cparams(("arbitrary", "arbitrary", "arbitrary")), name="rwkv7",
    )(*([hb] * 10), par, mu_wa, mu_gd, wdec, wic, wg)


def _pack_rwkv_params(mu, w0, w_decay, a0, w_iclr, k_k, k_a, r_k, gn_g, gn_b):
    bw = B_WIDTH
    zero = jnp.zeros((bw,), F32)
    rows = [w0, a0, k_k, k_a, r_k.reshape(bw), gn_g, gn_b, zero,
            mu[:bw], mu[bw:2 * bw], mu[2 * bw:3 * bw], zero, zero, zero, zero, zero]
    par = jnp.stack(rows)
    mu_wa = mu[3 * bw:3 * bw + LANES][None]
    mu_gd = mu[3 * bw + LANES:][None]
    zpad = jnp.zeros((DECAY_LORA, bw), F32)
    wdec = jnp.concatenate([w_decay, zpad], axis=0)
    wic = jnp.concatenate([zpad, w_iclr], axis=0)
    return par, mu_wa, mu_gd, wdec, wic


def _outproj_kernel(x_ref, a_ref, b_ref, wa_ref, wb_ref, g_ref, be_ref, o_ref):
    mix = _dot(a_ref[...], wa_ref[...]) + _dot(b_ref[...], wb_ref[...])
    o_ref[...] = _ln(ALPHA * x_ref[...] + mix, g_ref[...], be_ref[...])


def _outproj(x, a_out, b_out, w_out, g, b, tm=512):
    t, d = x.shape
    wa = w_out[:A_WIDTH].astype(BF16)
    wb = w_out[A_WIDTH:].astype(BF16)
    row = lambda i: (i, 0)
    const = lambda i: (0, 0)
    return pl.pallas_call(
        _outproj_kernel, grid=(t // tm,),
        in_specs=[pl.BlockSpec((tm, d), row), pl.BlockSpec((tm, A_WIDTH), row), pl.BlockSpec((tm, B_WIDTH), row),
                  pl.BlockSpec((A_WIDTH, d), const), pl.BlockSpec((B_WIDTH, d), const),
                  pl.BlockSpec((1, d), const), pl.BlockSpec((1, d), const)],
        out_specs=pl.BlockSpec((tm, d), row), out_shape=jax.ShapeDtypeStruct((t, d), F32),
        compiler_params=_cparams(("arbitrary",)), name="outproj_ln",
    )(x, a_out, b_out, wa, wb, g, b)


def _xor_partner(x, lane, bit):
    up = pltpu.roll(x, bit, 1)
    down = pltpu.roll(x, LANES - bit, 1)
    return jnp.where((lane & bit) != 0, up, down)


def _moe_gates(xf, wr, br):
    tm = xf.shape[0]
    lane = lax.broadcasted_iota(jnp.int32, (tm, LANES), 1)
    valid = lane < N_EXPERTS
    x_hi, x_lo = _split2(xf)
    w_hi, w_lo = _split2(wr)
    scores = jax.nn.sigmoid(_dot(x_hi, w_hi) + _dot(x_hi, w_lo) + _dot(x_lo, w_hi))
    sel = jnp.where(valid, scores + br, NEG)
    p1 = _xor_partner(sel, lane, 1)
    hi1, lo1 = jnp.maximum(sel, p1), jnp.minimum(sel, p1)
    hi2, lo2 = _xor_partner(hi1, lane, 2), _xor_partner(lo1, lane, 2)
    gscore = jnp.maximum(hi1, hi2) + jnp.maximum(jnp.minimum(hi1, hi2), jnp.maximum(lo1, lo2))
    gscore = jnp.where(valid, gscore, NEG)
    lanef = lane.astype(F32)
    gbest = jnp.max(gscore, axis=1, keepdims=True)
    first = jnp.min(jnp.where(gscore == gbest, lanef, float(LANES)), axis=1, keepdims=True)
    in_group = jnp.floor(lanef * (1.0 / EXPERTS_PER_GROUP)) == jnp.floor(first * (1.0 / EXPERTS_PER_GROUP))
    masked = jnp.where(in_group, sel, NEG)
    m1 = jnp.max(masked, axis=1, keepdims=True)
    i1 = jnp.min(jnp.where(masked == m1, lanef, float(LANES)), axis=1, keepdims=True)
    pick1 = lanef == i1
    masked2 = jnp.where(pick1, NEG, masked)
    m2 = jnp.max(masked2, axis=1, keepdims=True)
    i2 = jnp.min(jnp.where(masked2 == m2, lanef, float(LANES)), axis=1, keepdims=True)
    pick2 = lanef == i2
    s1 = jnp.sum(jnp.where(pick1, scores, 0.0), axis=1, keepdims=True)
    s2 = jnp.sum(jnp.where(pick2, scores, 0.0), axis=1, keepdims=True)
    tot = s1 + s2
    return jnp.where(pick1, s1 / tot, 0.0) + jnp.where(pick2, s2 / tot, 0.0)


def _moe_kernel(x_ref, wr_ref, br_ref, wgu_ref, wd_ref, g_ref, b_ref, o_ref, gate_ref, acc_ref, xb_ref):
    e = pl.program_id(1)

    @pl.when(e == 0)
    def _():
        xf = x_ref[...]
        gate_ref[...] = _moe_gates(xf, wr_ref[...], br_ref[...])
        xb_ref[...] = xf.astype(BF16)
        acc_ref[...] = jnp.zeros(acc_ref.shape, F32)

    gates = gate_ref[...]
    lane = lax.broadcasted_iota(jnp.int32, gates.shape, 1)
    gcol = jnp.sum(jnp.where(lane == e, gates, 0.0), axis=1, keepdims=True)
    hu = _dot(xb_ref[...], wgu_ref[0])
    hg, up = hu[:, :D_EXPERT], hu[:, D_EXPERT:]
    hid = (hg * jax.nn.sigmoid(hg)) * up * gcol
    acc_ref[...] += _dot(hid.astype(BF16), wd_ref[0])

    @pl.when(e == pl.num_programs(1) - 1)
    def _():
        o_ref[...] = _ln(ALPHA * x_ref[...] + acc_ref[...], g_ref[...], b_ref[...])


def _moe(x, w_router, b_router, w_gate, w_up, w_down, g, b, tm=1024):
    t, d = x.shape
    ne = w_gate.shape[0]
    wgu = jnp.concatenate([w_gate, w_up], axis=2).astype(BF16)
    wd = w_down.astype(BF16)
    wr = jnp.pad(w_router, ((0, 0), (0, LANES - ne)))
    br = jnp.pad(b_router, (0, LANES - ne))[None]
    row = lambda i, e: (i, 0)
    const = lambda i, e: (0, 0)
    return pl.pallas_call(
        _moe_kernel, grid=(t // tm, ne),
        in_specs=[pl.BlockSpec((tm, d), row), pl.BlockSpec((d, LANES), const), pl.BlockSpec((1, LANES), const),
                  pl.BlockSpec((1, d, 2 * D_EXPERT), lambda i, e: (e, 0, 0)),
                  pl.BlockSpec((1, D_EXPERT, d), lambda i, e: (e, 0, 0)),
                  pl.BlockSpec((1, d), const), pl.BlockSpec((1, d), const)],
        out_specs=pl.BlockSpec((tm, d), row), out_shape=jax.ShapeDtypeStruct((t, d), F32),
        scratch_shapes=[pltpu.VMEM((tm, LANES), F32), pltpu.VMEM((tm, d), F32), pltpu.VMEM((tm, d), BF16)],
        compiler_params=_cparams(("arbitrary", "arbitrary")), name="moe_ln",
    )(x, wr, br, wgu, wd, g, b)


def _ple_kernel(x_ref, p_ref, wg_ref, wp_ref, g_ref, b_ref, o_ref):
    x = x_ref[...]
    gate = jax.nn.sigmoid(_dot(x.astype(BF16), wg_ref[...]))
    pe = _dot(p_ref[...].astype(BF16), wp_ref[...])
    o_ref[...] = _ln(ALPHA * x + gate * pe, g_ref[...], b_ref[...])


def _ple(x, p, w_gate, w_ple, g, b, tm=512):
    t, d = x.shape
    row = lambda i: (i, 0)
    const = lambda i: (0, 0)
    return pl.pallas_call(
        _ple_kernel, grid=(t // tm,),
        in_specs=[pl.BlockSpec((tm, d), row), pl.BlockSpec((tm, PLE_DIM), row),
                  pl.BlockSpec((d, d), const), pl.BlockSpec((PLE_DIM, d), const),
                  pl.BlockSpec((1, d), const), pl.BlockSpec((1, d), const)],
        out_specs=pl.BlockSpec((tm, d), row), out_shape=jax.ShapeDtypeStruct((t, d), F32),
        compiler_params=_cparams(("arbitrary",)), name="ple_ln",
    )(x, p, w_gate.astype(BF16), w_ple.astype(BF16), g, b)


def kernel(x, p, ln_in_g, ln_in_b, w_in, w_out, mu_shift, w0, w_decay, a0, w_iclr, w_gate_up, k_k, k_a, r_k,
           gn_g, gn_b, rel_bias, w_router, b_router, w_exp_gate, w_exp_up, w_exp_down, w_ple, w_ple_gate,
           ln_mix_g, ln_mix_b, ln_ffn_g, ln_ffn_b, ln_ple_g, ln_ple_b):
    bsz, seq, d = x.shape
    t = bsz * seq
    depth = w_in.shape[0]
    xf = x.reshape(t, d)
    near_bias = _near_bias(rel_bias)
    for i in range(depth):
        w_rm, w_fm = _pack_w_in(w_in[i])
        outs = _inproj(xf, ln_in_g[None], ln_in_b[None], w_rm, w_fm, apply_ln=(i == 0))
        k_rm, kiki, hb, qt, qit, vt, wit = outs[:7]
        if i == 0:
            xf = outs[7]
        a_out = _dsa(qt, qit, wit, k_rm, vt, kiki, near_bias, bsz, seq)
        par, mu_wa, mu_gd, wdec, wic = _pack_rwkv_params(mu_shift[i], w0[i], w_decay[i], a0[i], w_iclr[i],
                                                         k_k[i], k_a[i], r_k[i], gn_g[i], gn_b[i])
        b_out = _rwkv(hb, par, mu_wa, mu_gd, wdec, wic, w_gate_up[i], bsz, seq)
        xf = _outproj(xf, a_out, b_out, w_out[i], ln_mix_g[i][None], ln_mix_b[i][None])
        xf = _moe(xf, w_router, b_router, w_exp_gate[i], w_exp_up[i], w_exp_down[i],
                  ln_ffn_g[i][None], ln_ffn_b[i][None])
        xf = _ple(xf, p[i].reshape(t, PLE_DIM), w_ple_gate[i], w_ple[i], ln_ple_g[i][None], ln_ple_b[i][None])
    return xf.reshape(bsz, seq, d)
```

```python
import functools
import math

import numpy as np
import jax
import jax.numpy as jnp
from jax import lax
from jax.experimental import pallas as pl
from jax.experimental.pallas import tpu as pltpu

D_MODEL = 1024
DEPTH = 2
CHUNK = 64
A_HEAD_DIM = 64
A_WIDTH = D_MODEL // 2
A_HEADS = A_WIDTH // A_HEAD_DIM
IDX_HEADS = 8
IDX_DIM = 64
TOPK_MAX = 256
N_BUCKETS = 32
MAX_DISTANCE = 128
B_HEAD_DIM = 64
B_WIDTH = D_MODEL - A_WIDTH
B_HEADS = B_WIDTH // B_HEAD_DIM
DECAY_LORA = 64
ICLR_LORA = 64
GATE_LORA = 128
A_SIZES = (A_WIDTH, A_WIDTH, A_WIDTH, IDX_HEADS * IDX_DIM, IDX_DIM, IDX_HEADS)
B_SIZES = (B_WIDTH, B_WIDTH, B_WIDTH, DECAY_LORA, ICLR_LORA, GATE_LORA)
A_COLS = sum(A_SIZES)
B_COLS = sum(B_SIZES)
N_EXPERTS = 16
N_GROUPS = 4
EXPERTS_PER_GROUP = N_EXPERTS // N_GROUPS
D_EXPERT = 256
PLE_DIM = 256
ALPHA = (2 * DEPTH) ** 0.25
LN_EPS = 1e-5
GN_EPS = 64e-5
NEG = -1e30

LANES = 128
VMEM_LIMIT = 56 * 1024 * 1024
F32 = jnp.float32
BF16 = jnp.bfloat16
HI = lax.Precision.HIGHEST
NT_DIMS = (((1,), (1,)), ((), ()))


def _cparams(sem):
    return pltpu.CompilerParams(dimension_semantics=sem, vmem_limit_bytes=VMEM_LIMIT)


def _ln(x, g, b):
    mu = jnp.mean(x, axis=-1, keepdims=True)
    xc = x - mu
    var = jnp.mean(xc * xc, axis=-1, keepdims=True)
    return xc * lax.rsqrt(var + LN_EPS) * g + b


def _dot(a, b):
    return jnp.dot(a, b, preferred_element_type=F32)


def _dot_hi(a, b):
    return jnp.dot(a, b, preferred_element_type=F32, precision=HI)


def _dot_nt(a, b):
    return lax.dot_general(a, b, NT_DIMS, preferred_element_type=F32)


def _dot_nt_hi(a, b):
    return lax.dot_general(a, b, NT_DIMS, preferred_element_type=F32, precision=HI)


def _split2(x):
    hi = x.astype(BF16)
    return hi, (x - hi.astype(F32)).astype(BF16)


def _split3(x):
    hi = x.astype(BF16)
    rest = x - hi.astype(F32)
    mid = rest.astype(BF16)
    return hi, mid, (rest - mid.astype(F32)).astype(BF16)


IN_NCHUNK = 512


WI_ROWS = 16
LOG2E = math.log2(math.e)
Q_SCALE = A_HEAD_DIM ** -0.5 * LOG2E


def _inproj_kernel(x_ref, g_ref, b_ref, wr_ref, wf_ref, *out_refs, apply_ln):
    x = x_ref[...]
    if apply_ln:
        x = _ln(x, g_ref[...], b_ref[...])
        out_refs[7][...] = x
    xb = x.astype(BF16)
    k_ref, kk_ref, hb_ref, qt_ref, qit_ref, vt_ref, wit_ref = out_refs[:7]
    aw = A_WIDTH
    k_ref[...] = _dot(xb, wr_ref[:, :aw]).astype(BF16)
    kk_ref[...] = _dot(xb, wr_ref[:, aw:aw + LANES]).astype(BF16)
    nb = hb_ref.shape[1]
    for c0 in range(0, nb, IN_NCHUNK):
        c1 = min(c0 + IN_NCHUNK, nb)
        hb_ref[:, c0:c1] = _dot(xb, wr_ref[:, aw + LANES + c0:aw + LANES + c1])
    qt_ref[...] = (_dot_nt(wf_ref[:aw], xb) * Q_SCALE).astype(BF16)
    qit_ref[...] = _dot_nt(wf_ref[aw:2 * aw], xb).astype(BF16)
    vt = _dot_nt(wf_ref[2 * aw:3 * aw], xb).astype(BF16)
    for n in range(vt_ref.shape[0]):
        vt_ref[n] = vt[:, n * TQ:(n + 1) * TQ]
    wit_ref[...] = _dot_nt(wf_ref[3 * aw:], xb)


def _inproj(x, g, b, w_rm, w_fm, apply_ln, tm=512):
    t, d = x.shape
    aw = A_WIDTH
    nb = w_rm.shape[1] - aw - LANES
    row = lambda i: (i, 0)
    col = lambda i: (0, i)
    const = lambda i: (0, 0)
    out_shape = [jax.ShapeDtypeStruct((t, aw), BF16), jax.ShapeDtypeStruct((t, LANES), BF16),
                 jax.ShapeDtypeStruct((t, nb), F32), jax.ShapeDtypeStruct((aw, t), BF16),
                 jax.ShapeDtypeStruct((aw, t), BF16), jax.ShapeDtypeStruct((t // TQ, aw, TQ), BF16),
                 jax.ShapeDtypeStruct((WI_ROWS, t), F32)]
    out_specs = [pl.BlockSpec((tm, aw), row), pl.BlockSpec((tm, LANES), row), pl.BlockSpec((tm, nb), row),
                 pl.BlockSpec((aw, tm), col), pl.BlockSpec((aw, tm), col),
                 pl.BlockSpec((tm // TQ, aw, TQ), lambda i: (i, 0, 0)), pl.BlockSpec((WI_ROWS, tm), col)]
    if apply_ln:
        out_shape.append(jax.ShapeDtypeStruct((t, d), F32))
        out_specs.append(pl.BlockSpec((tm, d), row))
    return pl.pallas_call(
        functools.partial(_inproj_kernel, apply_ln=apply_ln),
        grid=(t // tm,),
        in_specs=[pl.BlockSpec((tm, d), row), pl.BlockSpec((1, d), const), pl.BlockSpec((1, d), const),
                  pl.BlockSpec(w_rm.shape, const), pl.BlockSpec(w_fm.shape, const)],
        out_specs=out_specs, out_shape=out_shape,
        compiler_params=_cparams(("arbitrary",)), name="inproj_ln" if apply_ln else "inproj",
    )(x, g, b, w_rm, w_fm)


def _pack_w_in(w):
    d = w.shape[0]
    aw = A_WIDTH
    cuts = np.cumsum(A_SIZES)
    wq, wk, wv, wqi = w[:, :aw], w[:, aw:2 * aw], w[:, 2 * aw:3 * aw], w[:, cuts[2]:cuts[3]]
    wki = w[:, cuts[3]:cuts[4]]
    wwi = w[:, cuts[4]:cuts[5]]
    w_rm = jnp.concatenate([wk, wki, wki, w[:, A_COLS:]], axis=1).astype(BF16)
    pad = jnp.zeros((d, WI_ROWS - IDX_HEADS), w.dtype)
    w_fm = jnp.concatenate([wq, wqi, wv, wwi, pad], axis=1).T.astype(BF16)
    return w_rm, w_fm


TQ = 256
ROW_GROUP = 32
BIGF = 3.0e38


def _t5_bucket_np(rel):
    nb = N_BUCKETS // 2
    max_exact = nb // 2
    ret = np.where(rel > 0, nb, 0)
    n = np.abs(rel)
    nf = np.maximum(n, 1).astype(np.float64)
    large = max_exact + np.floor(np.log(nf / max_exact) / math.log(MAX_DISTANCE / max_exact)
                                 * (nb - max_exact) + 1e-9).astype(np.int64)
    large = np.minimum(large, nb - 1)
    return ret + np.where(n < max_exact, n, large)


def _near_bias(rel_bias):
    qpos = np.arange(TQ)[None, :]
    kpos = np.arange(TQ)[:, None]
    bk = np.stack([_t5_bucket_np(kpos - qpos), _t5_bucket_np(kpos - TQ - qpos)])
    far = N_BUCKETS // 2 - 1
    tab = (rel_bias - rel_bias[far][None, :]) * LOG2E
    onehot = jnp.asarray(bk[..., None] == np.arange(N_BUCKETS)).astype(F32)
    return jnp.einsum('nkqb,bh->nhkq', onehot, tab.astype(F32), precision=HI)


BISECT_STEPS = 12
SUBLANES = 8
EXP2_RANGE = 100.0


def _paired_loop(n, body, carry):
    def pair(j, c):
        return body(2 * j + 1, 1, body(2 * j, 0, c))

    carry = lax.fori_loop(0, lax.shift_right_logical(n, 1), pair, carry)
    return lax.cond((n & 1) == 1, lambda c: body(n - 1, 0, c), lambda c: c, carry)


def _fold8(x, op):
    return op(x.reshape(x.shape[0] // SUBLANES, SUBLANES, x.shape[1]), axis=0)


def _dsa_kernel(qt_ref, qit_ref, wit_ref, k_ref, vt_ref, kk_ref, bias_ref, o_ref,
                s_ref, m_ref, l_ref, acc_ref, p_ref, g_ref, *, ksel):
    i = pl.program_id(1)
    nt = i + 1
    tq = qt_ref.shape[1]
    kself = float(ksel)
    idx_scale = (IDX_HEADS ** -0.5) * (IDX_DIM ** -0.5)

    sub = lax.broadcasted_iota(jnp.int32, (LANES, tq), 0)
    lower = sub < (LANES // 2)
    zero_b = jnp.zeros((LANES, tq), BF16)

    def head_parts(ref):
        parts = []
        for h in range(A_HEADS):
            j, s = divmod(h, 2)
            blk = ref[j * LANES:(j + 1) * LANES, :]
            parts.append(jnp.where(lower if s == 0 else jnp.logical_not(lower), blk, zero_b))
        return parts

    def key_rows(kt):
        return pl.ds(pl.multiple_of(kt * tq, tq), tq)

    qi_parts = head_parts(qit_ref)
    wi = wit_ref[...] * idx_scale
    wrows = [wi[h:h + 1, :] for h in range(IDX_HEADS)]
    krow = lax.broadcasted_iota(jnp.int32, (tq, tq), 0)
    qcol = lax.broadcasted_iota(jnp.int32, (tq, tq), 1)
    adm_diag = (krow // CHUNK) <= (qcol // CHUNK)

    def score_tile(kt):
        kk = kk_ref[key_rows(kt), :]
        acc = jnp.zeros((tq, tq), F32)
        for h in range(IDX_HEADS):
            acc = acc + wrows[h] * jnp.maximum(_dot(kk, qi_parts[h]), 0.0)
        return acc

    def p1_body(kt, _, carry):
        sc = score_tile(kt)
        s_ref[kt] = sc
        return jnp.minimum(carry[0], _fold8(sc, jnp.min)), jnp.maximum(carry[1], _fold8(sc, jnp.max))

    mn8, mx8 = _paired_loop(i, p1_body, (jnp.full((SUBLANES, tq), BIGF, F32), jnp.full((SUBLANES, tq), -BIGF, F32)))
    sc = score_tile(i)
    s_ref[i] = jnp.where(adm_diag, sc, NEG)
    mn8 = jnp.minimum(mn8, _fold8(jnp.where(adm_diag, sc, BIGF), jnp.min))
    mx8 = jnp.maximum(mx8, _fold8(jnp.where(adm_diag, sc, -BIGF), jnp.max))
    rowmin = jnp.min(mn8, axis=0, keepdims=True)
    rowmax = jnp.max(mx8, axis=0, keepdims=True)

    def reduce_tiles(fn, inits, ops):
        def body(kt, _, carry):
            return fn(s_ref[kt], kt, carry)

        carry = _paired_loop(nt, body, tuple(jnp.full((SUBLANES, tq), v, F32) for v in inits))
        return [op(c, axis=0, keepdims=True) for op, c in zip(ops, carry)]

    qpos = i * tq + lax.broadcasted_iota(jnp.int32, (1, tq), 1)
    n_adm = (qpos // CHUNK + 1) * CHUNK
    trivial = n_adm <= ksel
    lo0 = jnp.where(trivial, 0.1 * NEG, rowmin)
    hi0 = jnp.where(trivial, 0.1 * NEG, rowmax)

    def count_ge(mid):
        def cnt(s, kt, carry):
            return (carry[0] + _fold8(jnp.where(s >= mid, 1.0, 0.0), jnp.sum),)

        return reduce_tiles(cnt, (0.0,), (jnp.sum,))[0]

    def a_body(_, st):
        lo, hi, clo = st
        mid = 0.5 * (lo + hi)
        c = count_ge(mid)
        ok = c >= kself
        return jnp.where(ok, mid, lo), jnp.where(ok, hi, mid), jnp.where(ok, c, clo)

    lo1, hi1, clo1 = lax.fori_loop(0, BISECT_STEPS, a_body, (lo0, hi0, n_adm.astype(F32)))
    hi1 = jnp.where(jnp.logical_or(clo1 == kself, trivial), lo1, hi1)

    def b_cond(st):
        lo, hi = st
        return jnp.max(jnp.where(lo < hi, 1.0, 0.0)) > 0.0

    def b_body(st):
        lo, hi = st
        mid = 0.5 * (lo + hi)
        mid = jnp.where(mid > lo, mid, hi)

        def cnt(s, kt, carry):
            c, vge, vlt = carry
            ge = s >= mid
            return (c + _fold8(jnp.where(ge, 1.0, 0.0), jnp.sum),
                    jnp.minimum(vge, _fold8(jnp.where(ge, s, BIGF), jnp.min)),
                    jnp.maximum(vlt, _fold8(jnp.where(ge, -BIGF, s), jnp.max)))

        c, vge, vlt = reduce_tiles(cnt, (0.0, BIGF, -BIGF), (jnp.sum, jnp.min, jnp.max))
        ok = c >= kself
        return jnp.where(ok, vge, lo), jnp.where(ok, hi, vlt)

    thr, _ = lax.while_loop(b_cond, b_body, (lo1, hi1))

    def count_ties(s, kt, carry):
        return (carry[0] + _fold8(jnp.where(s >= thr, 1.0, 0.0), jnp.sum),
                carry[1] + _fold8(jnp.where(s > thr, 1.0, 0.0), jnp.sum))

    c_ge, c_gt = reduce_tiles(count_ties, (0.0, 0.0), (jnp.sum, jnp.sum))
    need = kself - c_gt
    tied = c_ge > kself
    s_total = float(s_ref.shape[0] * tq)
    jlo0 = jnp.where(tied, 0.0, s_total - 1.0)
    jhi0 = jnp.where(tied, (nt * tq).astype(F32), s_total)
    kidx = krow.astype(F32)

    def tie_cond(st):
        jlo, jhi = st
        return jnp.max(jhi - jlo) > 1.0

    def tie_body(st):
        jlo, jhi = st
        mid = jnp.floor(0.5 * (jlo + jhi))

        def cnt(s, kt, carry):
            idx = kidx + (kt * tq).astype(F32)
            return (carry[0] + _fold8(jnp.where(jnp.logical_and(s == thr, idx < mid), 1.0, 0.0), jnp.sum),)

        (c,) = reduce_tiles(cnt, (0.0,), (jnp.sum,))
        ok = c >= need
        done = (jhi - jlo) <= 1.0
        return (jnp.where(jnp.logical_or(ok, done), jlo, mid),
                jnp.where(jnp.logical_and(ok, jnp.logical_not(done)), mid, jhi))

    _, jcut = lax.while_loop(tie_cond, tie_body, (jlo0, jhi0))

    q_parts = head_parts(qt_ref)

    def reset(m_init):
        l_ref[...] = jnp.zeros(l_ref.shape, F32)
        acc_ref[...] = jnp.zeros(acc_ref.shape, F32)
        if m_init is not None:
            m_ref[...] = jnp.full(m_ref.shape, m_init, F32)

    def write_mask(kt):
        s = s_ref[kt]
        idx = kidx + (kt * tq).astype(F32)
        sel = jnp.logical_or(s > thr, jnp.logical_and(s == thr, idx < jcut))
        s_ref[kt] = jnp.where(sel, 0.0, NEG)

    def logits(kt, near):
        mb = s_ref[kt]
        for j in range(A_HEADS // 2):
            kp = k_ref[key_rows(kt), j * LANES:(j + 1) * LANES]
            for s_half in range(2):
                h = 2 * j + s_half
                lg = _dot(kp, q_parts[h]) + mb
                if near is not None:
                    lg = lg + bias_ref[near, h]
                yield h, lg

    def accumulate(kt, slot, shifted):
        for h, d in shifted:
            p = jnp.exp2(d)
            l_ref[h] += _fold8(p, jnp.sum)
            p_ref[slot, h] = p.astype(BF16)
        for h in range(A_HEADS):
            acc_ref[h] += _dot(vt_ref[kt, h * A_HEAD_DIM:(h + 1) * A_HEAD_DIM, :], p_ref[slot, h])

    def sweep_single(kt, near, slot):
        write_mask(kt)

        def shifted():
            for h, lg in logits(kt, near):
                d = lg - m_ref[h][:1]
                g_ref[h] = jnp.maximum(g_ref[h], _fold8(d, jnp.max))
                yield h, d

        accumulate(kt, slot, shifted())

    def sweep_max(kt, near, slot):
        for h, lg in logits(kt, near):
            m_ref[h] = jnp.maximum(m_ref[h], _fold8(lg, jnp.max))

    def sweep_sum(kt, near, slot):
        accumulate(kt, slot, ((h, lg - m_ref[h][:1]) for h, lg in logits(kt, near)))

    def all_tiles(sweep):
        def body(kt, slot, c):
            sweep(kt, None, slot)
            return c

        _paired_loop(jnp.maximum(i - 1, 0), body, 0)

        @pl.when(i >= 1)
        def _():
            sweep(i - 1, 1, 1)

        sweep(i, 0, 0)

    def settle_rows(ref):
        for h in range(A_HEADS):
            ref[h] = jnp.broadcast_to(jnp.max(ref[h], axis=0, keepdims=True), (SUBLANES, tq))

    reset(None)
    g_ref[...] = jnp.full(g_ref.shape, NEG, F32)
    for j in range(A_HEADS // 2):
        kp = k_ref[key_rows(i), j * LANES:(j + 1) * LANES]
        for h in (2 * j, 2 * j + 1):
            m_ref[h] = _fold8(_dot(kp, q_parts[h]) + bias_ref[0, h], jnp.max)
    settle_rows(m_ref)
    all_tiles(sweep_single)
    out_of_range = [jnp.max(jnp.where(jnp.abs(jnp.max(g_ref[h], axis=0, keepdims=True)) > EXP2_RANGE, 1.0, 0.0))
                    for h in range(A_HEADS)]

    @pl.when(functools.reduce(jnp.maximum, out_of_range) > 0.0)
    def _():
        reset(NEG)
        all_tiles(sweep_max)
        settle_rows(m_ref)
        all_tiles(sweep_sum)

    for j in range(A_HEADS // 2):
        outs = [acc_ref[h] / jnp.sum(l_ref[h], axis=0, keepdims=True) for h in (2 * j, 2 * j + 1)]
        o_ref[:, j * LANES:(j + 1) * LANES] = jnp.concatenate(outs, axis=0).T.astype(o_ref.dtype)


def _dsa(qt, qit, wit, k, vt, kiki, near_bias, bsz, seq):
    nq = seq // TQ
    ksel = min(TOPK_MAX, seq // 4)
    qcol = lambda b, i: (0, b * nq + i)
    return pl.pallas_call(
        functools.partial(_dsa_kernel, ksel=ksel),
        grid=(bsz, nq),
        in_specs=[
            pl.BlockSpec((A_WIDTH, TQ), qcol),
            pl.BlockSpec((A_WIDTH, TQ), qcol),
            pl.BlockSpec((WI_ROWS, TQ), qcol),
            pl.BlockSpec((seq, A_WIDTH), lambda b, i: (b, 0)),
            pl.BlockSpec((nq, A_WIDTH, TQ), lambda b, i: (b, 0, 0)),
            pl.BlockSpec((seq, LANES), lambda b, i: (b, 0)),
            pl.BlockSpec((2, A_HEADS, TQ, TQ), lambda b, i: (0, 0, 0, 0)),
        ],
        out_specs=pl.BlockSpec((TQ, A_WIDTH), lambda b, i: (b * nq + i, 0)),
        out_shape=jax.ShapeDtypeStruct((bsz * seq, A_WIDTH), BF16),
        scratch_shapes=[pltpu.VMEM((nq, TQ, TQ), F32), pltpu.VMEM((A_HEADS, SUBLANES, TQ), F32),
                        pltpu.VMEM((A_HEADS, SUBLANES, TQ), F32), pltpu.VMEM((A_HEADS, A_HEAD_DIM, TQ), F32),
                        pltpu.VMEM((2, A_HEADS, TQ, TQ), BF16), pltpu.VMEM((A_HEADS, SUBLANES, TQ), F32)],
        compiler_params=_cparams(("arbitrary", "arbitrary")), name="dsa_attention",
    )(qt, qit, wit, k, vt, kiki, near_bias)


RC = 64
RTS = 256


RNP = 4


def _rwkv_kernel(r_ref, k_ref, v_ref, wa_ref, gd_ref, rp_ref, kp_ref, vp_ref, wap_ref, gdp_ref,
                 par_ref, muw_ref, mug_ref, wdec_ref, wic_ref, wg_ref, o_ref, z_ref):
    t = pl.program_id(2)
    ts = r_ref.shape[0]

    @pl.when(t == 0)
    def _():
        z_ref[...] = jnp.zeros(z_ref.shape, F32)

    first = jnp.where(t == 0, 0.0, 1.0)

    def shifted(ref, pref, mu):
        x = ref[...]
        row = lax.broadcasted_iota(jnp.int32, x.shape, 0)
        prev = jnp.where(row == 0, pref[7:8, :] * first, pltpu.roll(x, 1, 0))
        return x + (prev - x) * mu

    par = par_ref[...]
    r_all = shifted(r_ref, rp_ref, par[8:9])
    k_all = shifted(k_ref, kp_ref, par[9:10])
    v_all = shifted(v_ref, vp_ref, par[10:11])
    wa = shifted(wa_ref, wap_ref, muw_ref[...])
    gd = shifted(gd_ref, gdp_ref, mug_ref[...])

    lane = lax.broadcasted_iota(jnp.int32, (LANES, LANES), 1)
    rowl = lax.broadcasted_iota(jnp.int32, (LANES, LANES), 0)
    same_head = (lane // B_HEAD_DIM) == (rowl // B_HEAD_DIM)
    gones = jnp.where(same_head, 1.0, 0.0)
    gones16 = gones.astype(BF16)
    strict = jnp.logical_and(same_head, rowl > lane)
    incl = jnp.logical_and(same_head, rowl >= lane)
    eye = jnp.where(lane == rowl, 1.0, 0.0)
    head0 = lax.broadcasted_iota(jnp.int32, (RC, LANES), 1) < B_HEAD_DIM

    def head_sum(x):
        hi, lo = _split2(x)
        return _dot(hi, gones16) + _dot(lo, gones16)

    def stack_heads(x):
        return jnp.concatenate([jnp.where(head0, x, 0.0), jnp.where(head0, 0.0, x)], axis=0)

    def fold_heads(x):
        return x[:RC] + x[RC:]

    def twice(x):
        return jnp.concatenate([x, x], axis=0)

    rt = lax.broadcasted_iota(jnp.int32, (ts, ts), 0)
    ct = lax.broadcasted_iota(jnp.int32, (ts, ts), 1)
    tri16 = jnp.where(jnp.logical_and((rt // RC) == (ct // RC), rt >= ct), 1.0, 0.0).astype(BF16)
    th_hi, th_lo = _split2(jnp.tanh(wa))
    wa16 = wa.astype(BF16)
    sg16 = jax.nn.sigmoid(gd).astype(BF16)
    nchunk = ts // RC
    npair = r_ref.shape[1] // LANES

    pairs = []
    for q in range(npair):
        ls = slice(q * LANES, (q + 1) * LANES)
        w0, a0, k_k, k_a, r_k, gn_g, gn_b = (par[n:n + 1, ls] for n in range(7))
        r, k, v = r_all[:, ls], k_all[:, ls], v_all[:, ls]
        wd_hi, wd_lo = _split2(wdec_ref[:, ls])
        z = w0 + (_dot(th_hi, wd_hi) + _dot(th_hi, wd_lo) + _dot(th_lo, wd_hi))
        w_log = -(jnp.maximum(-z, 0.0) + jnp.log1p(jnp.exp(-jnp.abs(z)))) - 0.5
        e = jnp.exp(w_log)
        a = jax.nn.sigmoid(a0 + _dot(wa16, wic_ref[:, ls].astype(BF16)))
        g = _dot(sg16, wg_ref[:, ls].astype(BF16))
        kkr = k * k_k
        kk = kkr / jnp.maximum(jnp.sqrt(head_sum(kkr * kkr)), 1e-12)
        k2 = k * (1.0 + (a - 1.0) * k_a)
        bonus = head_sum(r * k2 * r_k) * v
        kka = kk * a
        e_hi, e_mid, e_lo = _split3(e)
        cum = _dot(tri16, e_hi) + _dot(tri16, e_mid) + _dot(tri16, e_lo)
        pairs.append(dict(r_hat=r * jnp.exp(-cum), a_hat=-kk * jnp.exp(e - cum), cum=cum, kka=kka, k2=k2, v=v,
                          b_til=(kka * jnp.exp(cum)).astype(BF16), k_til=(k2 * jnp.exp(cum)).astype(BF16),
                          v16=v.astype(BF16), bonus=bonus, g=g, gn_g=gn_g, gn_b=gn_b))

    items = [(q, c) for q in range(npair) for c in range(nchunk)]
    n = range(len(items))
    sl = [slice(c * RC, (c + 1) * RC) for _, c in items]
    pq = [pairs[q] for q, _ in items]
    tot = [pq[i]["cum"][sl[i].stop - 1:sl[i].stop] for i in n]
    rem = [jnp.exp(pq[i]["cum"][sl[i]] - tot[i]) for i in n]
    a_s = [stack_heads(pq[i]["a_hat"][sl[i]]).astype(BF16) for i in n]
    r_s = [stack_heads(pq[i]["r_hat"][sl[i]]).astype(BF16) for i in n]
    v_s = [stack_heads(pq[i]["v"][sl[i]]).astype(BF16) for i in n]
    ar = [jnp.concatenate([a_s[i], r_s[i]], axis=0) for i in n]
    bk = [jnp.concatenate([twice(pq[i]["b_til"][sl[i]]), twice(pq[i]["k_til"][sl[i]])], axis=0) for i in n]
    m_all = [_dot_nt(ar[i], bk[i]) for i in n]
    l_ab = [jnp.where(strict, m_all[i][:LANES, :LANES], 0.0) for i in n]
    l_ak = [jnp.where(strict, m_all[i][:LANES, LANES:], 0.0).astype(BF16) for i in n]
    m_rb = [jnp.where(incl, m_all[i][LANES:, :LANES], 0.0).astype(BF16) for i in n]
    m_rk = [jnp.where(incl, m_all[i][LANES:, LANES:], 0.0).astype(BF16) for i in n]
    inv = [eye + l_ab[i] for i in n]
    pw = [l_ab[i].astype(BF16) for i in n]
    for _ in range(5):
        pw = [_dot(pw[i], pw[i]).astype(BF16) for i in n]
        inv = [inv[i] + _dot(inv[i].astype(BF16), pw[i]) for i in n]
    lv = [_dot(l_ak[i], v_s[i]).astype(BF16) for i in n]
    x_av = [_dot(inv[i].astype(BF16), jnp.concatenate([a_s[i], lv[i]], axis=1)) for i in n]
    y_av = [_dot(m_rb[i], x_av[i].astype(BF16)) for i in n]
    y_kv = [_dot(m_rk[i], v_s[i]) for i in n]
    r_p = [pq[i]["r_hat"][sl[i]] + fold_heads(y_av[i][:, :LANES]) for i in n]
    y0 = [fold_heads(y_av[i][:, LANES:] + y_kv[i]) for i in n]
    gh = [_dot((pq[i]["kka"][sl[i]] * rem[i]).T.astype(BF16), fold_heads(x_av[i]).astype(BF16)) for i in n]
    kv = [_dot((pq[i]["k2"][sl[i]] * rem[i]).T.astype(BF16), pq[i]["v16"][sl[i]]) for i in n]
    g_mat = [eye * jnp.exp(-tot[i]) + gones * gh[i][:, :LANES] for i in n]
    h_mat = [gones * (gh[i][:, LANES:] + kv[i]) for i in n]
    rg = [jnp.concatenate([r_p[i], g_mat[i]], axis=0).astype(BF16) for i in n]

    zc = [z_ref[q] for q in range(npair)]
    ys = [[] for _ in range(npair)]
    for c in range(nchunk):
        for q in range(npair):
            i = q * nchunk + c
            yz = _dot(rg[i], zc[q].astype(BF16))
            ys[q].append(yz[:RC] + y0[i])
            zc[q] = yz[RC:] + h_mat[i]
    for q in range(npair):
        z_ref[q] = zc[q]
        p = pairs[q]
        y = jnp.concatenate(ys[q], axis=0)
        mean = head_sum(y) * (1.0 / B_HEAD_DIM)
        yc = y - mean
        var = head_sum(yc * yc) * (1.0 / B_HEAD_DIM)
        yn = yc * lax.rsqrt(var + GN_EPS) * p["gn_g"] + p["gn_b"]
        o_ref[:, q * LANES:(q + 1) * LANES] = ((yn + p["bonus"]) * p["g"]).astype(o_ref.dtype)


def _rwkv(hb, par, mu_wa, mu_gd, wdec, wic, wg, bsz, seq):
    nt = seq // RTS
    width = RNP * LANES
    ngroup = B_WIDTH // width
    cb = B_WIDTH // width
    small = (3 * B_WIDTH) // LANES

    def main(shape_w, colblk):
        return pl.BlockSpec((RTS, shape_w), lambda b, p, t: (b * nt + t, colblk(p)))

    def prev(shape_w, colblk):
        return pl.BlockSpec((8, shape_w), lambda b, p, t: (jnp.maximum((b * nt + t) * (RTS // 8) - 1, 0), colblk(p)))

    wide = [lambda p: p, lambda p: cb + p, lambda p: 2 * cb + p]
    narrow = [lambda p: small, lambda p: small + 1]
    in_specs = ([main(width, c) for c in wide] + [main(LANES, c) for c in narrow]
                + [prev(width, c) for c in wide] + [prev(LANES, c) for c in narrow] + [
        pl.BlockSpec((16, width), lambda b, p, t: (0, p)),
        pl.BlockSpec((1, LANES), lambda b, p, t: (0, 0)),
        pl.BlockSpec((1, LANES), lambda b, p, t: (0, 0)),
        pl.BlockSpec((LANES, width), lambda b, p, t: (0, p)),
        pl.BlockSpec((LANES, width), lambda b, p, t: (0, p)),
        pl.BlockSpec((LANES, width), lambda b, p, t: (0, p)),
    ])
    return pl.pallas_call(
        _rwkv_kernel,
        grid=(bsz, ngroup, nt),
        in_specs=in_specs,
        out_specs=pl.BlockSpec((RTS, width), lambda b, p, t: (b * nt + t, p)),
        out_shape=jax.ShapeDtypeStruct((bsz * seq, B_WIDTH), BF16),
        scratch_shapes=[pltpu.VMEM((RNP, LANES, LANES), F32)],
        compiler_params=_cparams(("arbitrary", "arbitrary", "arbitrary")), name="rwkv7",
    )(*([hb] * 10), par, mu_wa, mu_gd, wdec, wic, wg)


def _pack_rwkv_params(mu, w0, w_decay, a0, w_iclr, k_k, k_a, r_k, gn_g, gn_b):
    bw = B_WIDTH
    zero = jnp.zeros((bw,), F32)
    rows = [w0, a0, k_k, k_a, r_k.reshape(bw), gn_g, gn_b, zero,
            mu[:bw], mu[bw:2 * bw], mu[2 * bw:3 * bw], zero, zero, zero, zero, zero]
    par = jnp.stack(rows)
    mu_wa = mu[3 * bw:3 * bw + LANES][None]
    mu_gd = mu[3 * bw + LANES:][None]
    zpad = jnp.zeros((DECAY_LORA, bw), F32)
    wdec = jnp.concatenate([w_decay, zpad], axis=0)
    wic = jnp.concatenate([zpad, w_iclr], axis=0)
    return par, mu_wa, mu_gd, wdec, wic


def _outproj_kernel(x_ref, a_ref, b_ref, wa_ref, wb_ref, g_ref, be_ref, o_ref):
    mix = _dot(a_ref[...], wa_ref[...]) + _dot(b_ref[...], wb_ref[...])
    o_ref[...] = _ln(ALPHA * x_ref[...] + mix, g_ref[...], be_ref[...])


def _outproj(x, a_out, b_out, w_out, g, b, tm=512):
    t, d = x.shape
    wa = w_out[:A_WIDTH].astype(BF16)
    wb = w_out[A_WIDTH:].astype(BF16)
    row = lambda i: (i, 0)
    const = lambda i: (0, 0)
    return pl.pallas_call(
        _outproj_kernel, grid=(t // tm,),
        in_specs=[pl.BlockSpec((tm, d), row), pl.BlockSpec((tm, A_WIDTH), row), pl.BlockSpec((tm, B_WIDTH), row),
                  pl.BlockSpec((A_WIDTH, d), const), pl.BlockSpec((B_WIDTH, d), const),
                  pl.BlockSpec((1, d), const), pl.BlockSpec((1, d), const)],
        out_specs=pl.BlockSpec((tm, d), row), out_shape=jax.ShapeDtypeStruct((t, d), F32),
        compiler_params=_cparams(("arbitrary",)), name="outproj_ln",
    )(x, a_out, b_out, wa, wb, g, b)


def _xor_partner(x, lane, bit):
    up = pltpu.roll(x, bit, 1)
    down = pltpu.roll(x, LANES - bit, 1)
    return jnp.where((lane & bit) != 0, up, down)


def _moe_gates(xf, wr, br):
    tm = xf.shape[0]
    lane = lax.broadcasted_iota(jnp.int32, (tm, LANES), 1)
    valid = lane < N_EXPERTS
    x_hi, x_lo = _split2(xf)
    w_hi, w_lo = _split2(wr)
    scores = jax.nn.sigmoid(_dot(x_hi, w_hi) + _dot(x_hi, w_lo) + _dot(x_lo, w_hi))
    sel = jnp.where(valid, scores + br, NEG)
    p1 = _xor_partner(sel, lane, 1)
    hi1, lo1 = jnp.maximum(sel, p1), jnp.minimum(sel, p1)
    hi2, lo2 = _xor_partner(hi1, lane, 2), _xor_partner(lo1, lane, 2)
    gscore = jnp.maximum(hi1, hi2) + jnp.maximum(jnp.minimum(hi1, hi2), jnp.maximum(lo1, lo2))
    gscore = jnp.where(valid, gscore, NEG)
    lanef = lane.astype(F32)
    gbest = jnp.max(gscore, axis=1, keepdims=True)
    first = jnp.min(jnp.where(gscore == gbest, lanef, float(LANES)), axis=1, keepdims=True)
    in_group = jnp.floor(lanef * (1.0 / EXPERTS_PER_GROUP)) == jnp.floor(first * (1.0 / EXPERTS_PER_GROUP))
    masked = jnp.where(in_group, sel, NEG)
    m1 = jnp.max(masked, axis=1, keepdims=True)
    i1 = jnp.min(jnp.where(masked == m1, lanef, float(LANES)), axis=1, keepdims=True)
    pick1 = lanef == i1
    masked2 = jnp.where(pick1, NEG, masked)
    m2 = jnp.max(masked2, axis=1, keepdims=True)
    i2 = jnp.min(jnp.where(masked2 == m2, lanef, float(LANES)), axis=1, keepdims=True)
    pick2 = lanef == i2
    s1 = jnp.sum(jnp.where(pick1, scores, 0.0), axis=1, keepdims=True)
    s2 = jnp.sum(jnp.where(pick2, scores, 0.0), axis=1, keepdims=True)
    tot = s1 + s2
    return jnp.where(pick1, s1 / tot, 0.0) + jnp.where(pick2, s2 / tot, 0.0)


def _moe_kernel(x_ref, wr_ref, br_ref, wgu_ref, wd_ref, g_ref, b_ref, o_ref, gate_ref, acc_ref, xb_ref):
    e = pl.program_id(1)

    @pl.when(e == 0)
    def _():
        xf = x_ref[...]
        gate_ref[...] = _moe_gates(xf, wr_ref[...], br_ref[...])
        xb_ref[...] = xf.astype(BF16)
        acc_ref[...] = jnp.zeros(acc_ref.shape, F32)

    gates = gate_ref[...]
    lane = lax.broadcasted_iota(jnp.int32, gates.shape, 1)
    gcol = jnp.sum(jnp.where(lane == e, gates, 0.0), axis=1, keepdims=True)
    hu = _dot(xb_ref[...], wgu_ref[0])
    hg, up = hu[:, :D_EXPERT], hu[:, D_EXPERT:]
    hid = (hg * jax.nn.sigmoid(hg)) * up * gcol
    acc_ref[...] += _dot(hid.astype(BF16), wd_ref[0])

    @pl.when(e == pl.num_programs(1) - 1)
    def _():
        o_ref[...] = _ln(ALPHA * x_ref[...] + acc_ref[...], g_ref[...], b_ref[...])


def _moe(x, w_router, b_router, w_gate, w_up, w_down, g, b, tm=1024):
    t, d = x.shape
    ne = w_gate.shape[0]
    wgu = jnp.concatenate([w_gate, w_up], axis=2).astype(BF16)
    wd = w_down.astype(BF16)
    wr = jnp.pad(w_router, ((0, 0), (0, LANES - ne)))
    br = jnp.pad(b_router, (0, LANES - ne))[None]
    row = lambda i, e: (i, 0)
    const = lambda i, e: (0, 0)
    return pl.pallas_call(
        _moe_kernel, grid=(t // tm, ne),
        in_specs=[pl.BlockSpec((tm, d), row), pl.BlockSpec((d, LANES), const), pl.BlockSpec((1, LANES), const),
                  pl.BlockSpec((1, d, 2 * D_EXPERT), lambda i, e: (e, 0, 0)),
                  pl.BlockSpec((1, D_EXPERT, d), lambda i, e: (e, 0, 0)),
                  pl.BlockSpec((1, d), const), pl.BlockSpec((1, d), const)],
        out_specs=pl.BlockSpec((tm, d), row), out_shape=jax.ShapeDtypeStruct((t, d), F32),
        scratch_shapes=[pltpu.VMEM((tm, LANES), F32), pltpu.VMEM((tm, d), F32), pltpu.VMEM((tm, d), BF16)],
        compiler_params=_cparams(("arbitrary", "arbitrary")), name="moe_ln",
    )(x, wr, br, wgu, wd, g, b)


def _ple_kernel(x_ref, p_ref, wg_ref, wp_ref, g_ref, b_ref, o_ref):
    x = x_ref[...]
    gate = jax.nn.sigmoid(_dot(x.astype(BF16), wg_ref[...]))
    pe = _dot(p_ref[...].astype(BF16), wp_ref[...])
    o_ref[...] = _ln(ALPHA * x + gate * pe, g_ref[...], b_ref[...])


def _ple(x, p, w_gate, w_ple, g, b, tm=512):
    t, d = x.shape
    row = lambda i: (i, 0)
    const = lambda i: (0, 0)
    return pl.pallas_call(
        _ple_kernel, grid=(t // tm,),
        in_specs=[pl.BlockSpec((tm, d), row), pl.BlockSpec((tm, PLE_DIM), row),
                  pl.BlockSpec((d, d), const), pl.BlockSpec((PLE_DIM, d), const),
                  pl.BlockSpec((1, d), const), pl.BlockSpec((1, d), const)],
        out_specs=pl.BlockSpec((tm, d), row), out_shape=jax.ShapeDtypeStruct((t, d), F32),
        compiler_params=_cparams(("arbitrary",)), name="ple_ln",
    )(x, p, w_gate.astype(BF16), w_ple.astype(BF16), g, b)


def kernel(x, p, ln_in_g, ln_in_b, w_in, w_out, mu_shift, w0, w_decay, a0, w_iclr, w_gate_up, k_k, k_a, r_k,
           gn_g, gn_b, rel_bias, w_router, b_router, w_exp_gate, w_exp_up, w_exp_down, w_ple, w_ple_gate,
           ln_mix_g, ln_mix_b, ln_ffn_g, ln_ffn_b, ln_ple_g, ln_ple_b):
    bsz, seq, d = x.shape
    t = bsz * seq
    depth = w_in.shape[0]
    xf = x.reshape(t, d)
    near_bias = _near_bias(rel_bias)
    for i in range(depth):
        w_rm, w_fm = _pack_w_in(w_in[i])
        outs = _inproj(xf, ln_in_g[None], ln_in_b[None], w_rm, w_fm, apply_ln=(i == 0))
        k_rm, kiki, hb, qt, qit, vt, wit = outs[:7]
        if i == 0:
            xf = outs[7]
        a_out = _dsa(qt, qit, wit, k_rm, vt, kiki, near_bias, bsz, seq)
        par, mu_wa, mu_gd, wdec, wic = _pack_rwkv_params(mu_shift[i], w0[i], w_decay[i], a0[i], w_iclr[i],
                                                         k_k[i], k_a[i], r_k[i], gn_g[i], gn_b[i])
        b_out = _rwkv(hb, par, mu_wa, mu_gd, wdec, wic, w_gate_up[i], bsz, seq)
        xf = _outproj(xf, a_out, b_out, w_out[i], ln_mix_g[i][None], ln_mix_b[i][None])
        xf = _moe(xf, w_router, b_router, w_exp_gate[i], w_exp_up[i], w_exp_down[i],
                  ln_ffn_g[i][None], ln_ffn_b[i][None])
        xf = _ple(xf, p[i].reshape(t, PLE_DIM), w_ple_gate[i], w_ple[i], ln_ple_g[i][None], ln_ple_b[i][None])
    return xf.reshape(bsz, seq, d)
```

```python
import functools
import math

import numpy as np
import jax
import jax.numpy as jnp
from jax import lax
from jax.experimental import pallas as pl
from jax.experimental.pallas import tpu as pltpu

D_MODEL = 1024
DEPTH = 2
CHUNK = 64
A_HEAD_DIM = 64
A_WIDTH = D_MODEL // 2
A_HEADS = A_WIDTH // A_HEAD_DIM
IDX_HEADS = 8
IDX_DIM = 64
TOPK_MAX = 256
N_BUCKETS = 32
MAX_DISTANCE = 128
B_HEAD_DIM = 64
B_WIDTH = D_MODEL - A_WIDTH
B_HEADS = B_WIDTH // B_HEAD_DIM
DECAY_LORA = 64
ICLR_LORA = 64
GATE_LORA = 128
A_SIZES = (A_WIDTH, A_WIDTH, A_WIDTH, IDX_HEADS * IDX_DIM, IDX_DIM, IDX_HEADS)
B_SIZES = (B_WIDTH, B_WIDTH, B_WIDTH, DECAY_LORA, ICLR_LORA, GATE_LORA)
A_COLS = sum(A_SIZES)
B_COLS = sum(B_SIZES)
N_EXPERTS = 16
N_GROUPS = 4
EXPERTS_PER_GROUP = N_EXPERTS // N_GROUPS
D_EXPERT = 256
PLE_DIM = 256
ALPHA = (2 * DEPTH) ** 0.25
LN_EPS = 1e-5
GN_EPS = 64e-5
NEG = -1e30

LANES = 128
VMEM_LIMIT = 56 * 1024 * 1024
F32 = jnp.float32
BF16 = jnp.bfloat16
HI = lax.Precision.HIGHEST
NT_DIMS = (((1,), (1,)), ((), ()))


def _cparams(sem):
    return pltpu.CompilerParams(dimension_semantics=sem, vmem_limit_bytes=VMEM_LIMIT)


def _ln(x, g, b):
    mu = jnp.mean(x, axis=-1, keepdims=True)
    xc = x - mu
    var = jnp.mean(xc * xc, axis=-1, keepdims=True)
    return xc * lax.rsqrt(var + LN_EPS) * g + b


def _dot(a, b):
    return jnp.dot(a, b, preferred_element_type=F32)


def _dot_hi(a, b):
    return jnp.dot(a, b, preferred_element_type=F32, precision=HI)


def _dot_nt(a, b):
    return lax.dot_general(a, b, NT_DIMS, preferred_element_type=F32)


def _dot_nt_hi(a, b):
    return lax.dot_general(a, b, NT_DIMS, preferred_element_type=F32, precision=HI)


def _split2(x):
    hi = x.astype(BF16)
    return hi, (x - hi.astype(F32)).astype(BF16)


def _split3(x):
    hi = x.astype(BF16)
    rest = x - hi.astype(F32)
    mid = rest.astype(BF16)
    return hi, mid, (rest - mid.astype(F32)).astype(BF16)


IN_NCHUNK = 512


WI_ROWS = 16
LOG2E = math.log2(math.e)
Q_SCALE = A_HEAD_DIM ** -0.5 * LOG2E


def _inproj_kernel(x_ref, g_ref, b_ref, wr_ref, wf_ref, *out_refs, apply_ln):
    x = x_ref[...]
    if apply_ln:
        x = _ln(x, g_ref[...], b_ref[...])
        out_refs[7][...] = x
    xb = x.astype(BF16)
    k_ref, kk_ref, hb_ref, qt_ref, qit_ref, vt_ref, wit_ref = out_refs[:7]
    aw = A_WIDTH
    k_ref[...] = _dot(xb, wr_ref[:, :aw]).astype(BF16)
    kk_ref[...] = _dot(xb, wr_ref[:, aw:aw + LANES]).astype(BF16)
    nb = hb_ref.shape[1]
    for c0 in range(0, nb, IN_NCHUNK):
        c1 = min(c0 + IN_NCHUNK, nb)
        hb_ref[:, c0:c1] = _dot(xb, wr_ref[:, aw + LANES + c0:aw + LANES + c1])
    qt_ref[...] = (_dot_nt(wf_ref[:aw], xb) * Q_SCALE).astype(BF16)
    qit_ref[...] = _dot_nt(wf_ref[aw:2 * aw], xb).astype(BF16)
    vt = _dot_nt(wf_ref[2 * aw:3 * aw], xb).astype(BF16)
    for n in range(vt_ref.shape[0]):
        vt_ref[n] = vt[:, n * TQ:(n + 1) * TQ]
    wit_ref[...] = _dot_nt(wf_ref[3 * aw:], xb)


def _inproj(x, g, b, w_rm, w_fm, apply_ln, tm=512):
    t, d = x.shape
    aw = A_WIDTH
    nb = w_rm.shape[1] - aw - LANES
    row = lambda i: (i, 0)
    col = lambda i: (0, i)
    const = lambda i: (0, 0)
    out_shape = [jax.ShapeDtypeStruct((t, aw), BF16), jax.ShapeDtypeStruct((t, LANES), BF16),
                 jax.ShapeDtypeStruct((t, nb), F32), jax.ShapeDtypeStruct((aw, t), BF16),
                 jax.ShapeDtypeStruct((aw, t), BF16), jax.ShapeDtypeStruct((t // TQ, aw, TQ), BF16),
                 jax.ShapeDtypeStruct((WI_ROWS, t), F32)]
    out_specs = [pl.BlockSpec((tm, aw), row), pl.BlockSpec((tm, LANES), row), pl.BlockSpec((tm, nb), row),
                 pl.BlockSpec((aw, tm), col), pl.BlockSpec((aw, tm), col),
                 pl.BlockSpec((tm // TQ, aw, TQ), lambda i: (i, 0, 0)), pl.BlockSpec((WI_ROWS, tm), col)]
    if apply_ln:
        out_shape.append(jax.ShapeDtypeStruct((t, d), F32))
        out_specs.append(pl.BlockSpec((tm, d), row))
    return pl.pallas_call(
        functools.partial(_inproj_kernel, apply_ln=apply_ln),
        grid=(t // tm,),
        in_specs=[pl.BlockSpec((tm, d), row), pl.BlockSpec((1, d), const), pl.BlockSpec((1, d), const),
                  pl.BlockSpec(w_rm.shape, const), pl.BlockSpec(w_fm.shape, const)],
        out_specs=out_specs, out_shape=out_shape,
        compiler_params=_cparams(("arbitrary",)), name="inproj_ln" if apply_ln else "inproj",
    )(x, g, b, w_rm, w_fm)


def _pack_w_in(w):
    d = w.shape[0]
    aw = A_WIDTH
    cuts = np.cumsum(A_SIZES)
    wq, wk, wv, wqi = w[:, :aw], w[:, aw:2 * aw], w[:, 2 * aw:3 * aw], w[:, cuts[2]:cuts[3]]
    wki = w[:, cuts[3]:cuts[4]]
    wwi = w[:, cuts[4]:cuts[5]]
    w_rm = jnp.concatenate([wk, wki, wki, w[:, A_COLS:]], axis=1).astype(BF16)
    pad = jnp.zeros((d, WI_ROWS - IDX_HEADS), w.dtype)
    w_fm = jnp.concatenate([wq, wqi, wv, wwi, pad], axis=1).T.astype(BF16)
    return w_rm, w_fm


TQ = 256
ROW_GROUP = 32
BIGF = 3.0e38


def _t5_bucket_np(rel):
    nb = N_BUCKETS // 2
    max_exact = nb // 2
    ret = np.where(rel > 0, nb, 0)
    n = np.abs(rel)
    nf = np.maximum(n, 1).astype(np.float64)
    large = max_exact + np.floor(np.log(nf / max_exact) / math.log(MAX_DISTANCE / max_exact)
                                 * (nb - max_exact) + 1e-9).astype(np.int64)
    large = np.minimum(large, nb - 1)
    return ret + np.where(n < max_exact, n, large)


def _near_bias(rel_bias):
    qpos = np.arange(TQ)[None, :]
    kpos = np.arange(TQ)[:, None]
    bk = np.stack([_t5_bucket_np(kpos - qpos), _t5_bucket_np(kpos - TQ - qpos)])
    far = N_BUCKETS // 2 - 1
    tab = (rel_bias - rel_bias[far][None, :]) * LOG2E
    onehot = jnp.asarray(bk[..., None] == np.arange(N_BUCKETS)).astype(F32)
    return jnp.einsum('nkqb,bh->nhkq', onehot, tab.astype(F32), precision=HI)


BISECT_STEPS = 12
SUBLANES = 8
EXP2_RANGE = 100.0


def _paired_loop(n, body, carry):
    def pair(j, c):
        return body(2 * j + 1, 1, body(2 * j, 0, c))

    carry = lax.fori_loop(0, lax.shift_right_logical(n, 1), pair, carry)
    return lax.cond((n & 1) == 1, lambda c: body(n - 1, 0, c), lambda c: c, carry)


def _fold8(x, op):
    return op(x.reshape(x.shape[0] // SUBLANES, SUBLANES, x.shape[1]), axis=0)


def _dsa_kernel(qt_ref, qit_ref, wit_ref, k_ref, vt_ref, kk_ref, bias_ref, o_ref,
                s_ref, m_ref, l_ref, acc_ref, p_ref, g_ref, *, ksel):
    i = pl.program_id(1)
    nt = i + 1
    tq = qt_ref.shape[1]
    kself = float(ksel)
    idx_scale = (IDX_HEADS ** -0.5) * (IDX_DIM ** -0.5)

    sub = lax.broadcasted_iota(jnp.int32, (LANES, tq), 0)
    lower = sub < (LANES // 2)
    zero_b = jnp.zeros((LANES, tq), BF16)

    def head_parts(ref):
        parts = []
        for h in range(A_HEADS):
            j, s = divmod(h, 2)
            blk = ref[j * LANES:(j + 1) * LANES, :]
            parts.append(jnp.where(lower if s == 0 else jnp.logical_not(lower), blk, zero_b))
        return parts

    def key_rows(kt):
        return pl.ds(pl.multiple_of(kt * tq, tq), tq)

    qi_parts = head_parts(qit_ref)
    wi = wit_ref[...] * idx_scale
    wrows = [wi[h:h + 1, :] for h in range(IDX_HEADS)]
    krow = lax.broadcasted_iota(jnp.int32, (tq, tq), 0)
    qcol = lax.broadcasted_iota(jnp.int32, (tq, tq), 1)
    adm_diag = (krow // CHUNK) <= (qcol // CHUNK)

    def score_tile(kt):
        kk = kk_ref[key_rows(kt), :]
        acc = jnp.zeros((tq, tq), F32)
        for h in range(IDX_HEADS):
            acc = acc + wrows[h] * jnp.maximum(_dot(kk, qi_parts[h]), 0.0)
        return acc

    def p1_body(kt, _, carry):
        sc = score_tile(kt)
        s_ref[kt] = sc
        return jnp.minimum(carry[0], _fold8(sc, jnp.min)), jnp.maximum(carry[1], _fold8(sc, jnp.max))

    mn8, mx8 = _paired_loop(i, p1_body, (jnp.full((SUBLANES, tq), BIGF, F32), jnp.full((SUBLANES, tq), -BIGF, F32)))
    sc = score_tile(i)
    s_ref[i] = jnp.where(adm_diag, sc, NEG)
    mn8 = jnp.minimum(mn8, _fold8(jnp.where(adm_diag, sc, BIGF), jnp.min))
    mx8 = jnp.maximum(mx8, _fold8(jnp.where(adm_diag, sc, -BIGF), jnp.max))
    rowmin = jnp.min(mn8, axis=0, keepdims=True)
    rowmax = jnp.max(mx8, axis=0, keepdims=True)

    def reduce_tiles(fn, inits, ops):
        def body(kt, _, carry):
            return fn(s_ref[kt], kt, carry)

        carry = _paired_loop(nt, body, tuple(jnp.full((SUBLANES, tq), v, F32) for v in inits))
        return [op(c, axis=0, keepdims=True) for op, c in zip(ops, carry)]

    qpos = i * tq + lax.broadcasted_iota(jnp.int32, (1, tq), 1)
    n_adm = (qpos // CHUNK + 1) * CHUNK
    trivial = n_adm <= ksel
    lo0 = jnp.where(trivial, 0.1 * NEG, rowmin)
    hi0 = jnp.where(trivial, 0.1 * NEG, rowmax)

    def count_ge(mid):
        def cnt(s, kt, carry):
            return (carry[0] + _fold8(jnp.where(s >= mid, 1.0, 0.0), jnp.sum),)

        return reduce_tiles(cnt, (0.0,), (jnp.sum,))[0]

    def a_body(_, st):
        lo, hi, clo = st
        mid = 0.5 * (lo + hi)
        c = count_ge(mid)
        ok = c >= kself
        return jnp.where(ok, mid, lo), jnp.where(ok, hi, mid), jnp.where(ok, c, clo)

    lo1, hi1, clo1 = lax.fori_loop(0, BISECT_STEPS, a_body, (lo0, hi0, n_adm.astype(F32)))
    hi1 = jnp.where(jnp.logical_or(clo1 == kself, trivial), lo1, hi1)

    def b_cond(st):
        lo, hi = st
        return jnp.max(jnp.where(lo < hi, 1.0, 0.0)) > 0.0

    def b_body(st):
        lo, hi = st
        mid = 0.5 * (lo + hi)
        mid = jnp.where(mid > lo, mid, hi)

        def cnt(s, kt, carry):
            c, vge, vlt = carry
            ge = s >= mid
            return (c + _fold8(jnp.where(ge, 1.0, 0.0), jnp.sum),
                    jnp.minimum(vge, _fold8(jnp.where(ge, s, BIGF), jnp.min)),
                    jnp.maximum(vlt, _fold8(jnp.where(ge, -BIGF, s), jnp.max)))

        c, vge, vlt = reduce_tiles(cnt, (0.0, BIGF, -BIGF), (jnp.sum, jnp.min, jnp.max))
        ok = c >= kself
        return jnp.where(ok, vge, lo), jnp.where(ok, hi, vlt)

    thr, _ = lax.while_loop(b_cond, b_body, (lo1, hi1))

    def count_ties(s, kt, carry):
        return (carry[0] + _fold8(jnp.where(s >= thr, 1.0, 0.0), jnp.sum),
                carry[1] + _fold8(jnp.where(s > thr, 1.0, 0.0), jnp.sum))

    c_ge, c_gt = reduce_tiles(count_ties, (0.0, 0.0), (jnp.sum, jnp.sum))
    need = kself - c_gt
    tied = c_ge > kself
    s_total = float(s_ref.shape[0] * tq)
    jlo0 = jnp.where(tied, 0.0, s_total - 1.0)
    jhi0 = jnp.where(tied, (nt * tq).astype(F32), s_total)
    kidx = krow.astype(F32)

    def tie_cond(st):
        jlo, jhi = st
        return jnp.max(jhi - jlo) > 1.0

    def tie_body(st):
        jlo, jhi = st
        mid = jnp.floor(0.5 * (jlo + jhi))

        def cnt(s, kt, carry):
            idx = kidx + (kt * tq).astype(F32)
            return (carry[0] + _fold8(jnp.where(jnp.logical_and(s == thr, idx < mid), 1.0, 0.0), jnp.sum),)

        (c,) = reduce_tiles(cnt, (0.0,), (jnp.sum,))
        ok = c >= need
        done = (jhi - jlo) <= 1.0
        return (jnp.where(jnp.logical_or(ok, done), jlo, mid),
                jnp.where(jnp.logical_and(ok, jnp.logical_not(done)), mid, jhi))

    _, jcut = lax.while_loop(tie_cond, tie_body, (jlo0, jhi0))

    q_parts = head_parts(qt_ref)

    def reset(m_init):
        l_ref[...] = jnp.zeros(l_ref.shape, F32)
        acc_ref[...] = jnp.zeros(acc_ref.shape, F32)
        if m_init is not None:
            m_ref[...] = jnp.full(m_ref.shape, m_init, F32)

    def write_mask(kt):
        s = s_ref[kt]
        idx = kidx + (kt * tq).astype(F32)
        sel = jnp.logical_or(s > thr, jnp.logical_and(s == thr, idx < jcut))
        s_ref[kt] = jnp.where(sel, 0.0, NEG)

    def logits(kt, near):
        mb = s_ref[kt]
        for j in range(A_HEADS // 2):
            kp = k_ref[key_rows(kt), j * LANES:(j + 1) * LANES]
            for s_half in range(2):
                h = 2 * j + s_half
                lg = _dot(kp, q_parts[h]) + mb
                if near is not None:
                    lg = lg + bias_ref[near, h]
                yield h, lg

    def accumulate(kt, slot, shifted):
        for h, d in shifted:
            p = jnp.exp2(d)
            l_ref[h] += _fold8(p, jnp.sum)
            p_ref[slot, h] = p.astype(BF16)
        for h in range(A_HEADS):
            acc_ref[h] += _dot(vt_ref[kt, h * A_HEAD_DIM:(h + 1) * A_HEAD_DIM, :], p_ref[slot, h])

    def sweep_single(kt, near, slot):
        write_mask(kt)

        def shifted():
            for h, lg in logits(kt, near):
                d = lg - m_ref[h][:1]
                g_ref[h] = jnp.maximum(g_ref[h], _fold8(d, jnp.max))
                yield h, d

        accumulate(kt, slot, shifted())

    def sweep_max(kt, near, slot):
        for h, lg in logits(kt, near):
            m_ref[h] = jnp.maximum(m_ref[h], _fold8(lg, jnp.max))

    def sweep_sum(kt, near, slot):
        accumulate(kt, slot, ((h, lg - m_ref[h][:1]) for h, lg in logits(kt, near)))

    def all_tiles(sweep):
        def body(kt, slot, c):
            sweep(kt, None, slot)
            return c

        _paired_loop(jnp.maximum(i - 1, 0), body, 0)

        @pl.when(i >= 1)
        def _():
            sweep(i - 1, 1, 1)

        sweep(i, 0, 0)

    def settle_rows(ref):
        for h in range(A_HEADS):
            ref[h] = jnp.broadcast_to(jnp.max(ref[h], axis=0, keepdims=True), (SUBLANES, tq))

    reset(None)
    g_ref[...] = jnp.full(g_ref.shape, NEG, F32)
    for j in range(A_HEADS // 2):
        kp = k_ref[key_rows(i), j * LANES:(j + 1) * LANES]
        for h in (2 * j, 2 * j + 1):
            m_ref[h] = _fold8(_dot(kp, q_parts[h]) + bias_ref[0, h], jnp.max)
    settle_rows(m_ref)
    all_tiles(sweep_single)
    out_of_range = [jnp.max(jnp.where(jnp.abs(jnp.max(g_ref[h], axis=0, keepdims=True)) > EXP2_RANGE, 1.0, 0.0))
                    for h in range(A_HEADS)]

    @pl.when(functools.reduce(jnp.maximum, out_of_range) > 0.0)
    def _():
        reset(NEG)
        all_tiles(sweep_max)
        settle_rows(m_ref)
        all_tiles(sweep_sum)

    for j in range(A_HEADS // 2):
        outs = [acc_ref[h] / jnp.sum(l_ref[h], axis=0, keepdims=True) for h in (2 * j, 2 * j + 1)]
        o_ref[:, j * LANES:(j + 1) * LANES] = jnp.concatenate(outs, axis=0).T.astype(o_ref.dtype)


def _dsa(qt, qit, wit, k, vt, kiki, near_bias, bsz, seq):
    nq = seq // TQ
    ksel = min(TOPK_MAX, seq // 4)
    qcol = lambda b, i: (0, b * nq + i)
    return pl.pallas_call(
        functools.partial(_dsa_kernel, ksel=ksel),
        grid=(bsz, nq),
        in_specs=[
            pl.BlockSpec((A_WIDTH, TQ), qcol),
            pl.BlockSpec((A_WIDTH, TQ), qcol),
            pl.BlockSpec((WI_ROWS, TQ), qcol),
            pl.BlockSpec((seq, A_WIDTH), lambda b, i: (b, 0)),
            pl.BlockSpec((nq, A_WIDTH, TQ), lambda b, i: (b, 0, 0)),
            pl.BlockSpec((seq, LANES), lambda b, i: (b, 0)),
            pl.BlockSpec((2, A_HEADS, TQ, TQ), lambda b, i: (0, 0, 0, 0)),
        ],
        out_specs=pl.BlockSpec((TQ, A_WIDTH), lambda b, i: (b * nq + i, 0)),
        out_shape=jax.ShapeDtypeStruct((bsz * seq, A_WIDTH), BF16),
        scratch_shapes=[pltpu.VMEM((nq, TQ, TQ), F32), pltpu.VMEM((A_HEADS, SUBLANES, TQ), F32),
                        pltpu.VMEM((A_HEADS, SUBLANES, TQ), F32), pltpu.VMEM((A_HEADS, A_HEAD_DIM, TQ), F32),
                        pltpu.VMEM((2, A_HEADS, TQ, TQ), BF16), pltpu.VMEM((A_HEADS, SUBLANES, TQ), F32)],
        compiler_params=_cparams(("arbitrary", "arbitrary")), name="dsa_attention",
    )(qt, qit, wit, k, vt, kiki, near_bias)


RC = 64
RTS = 256


RNP = 4


def _rwkv_kernel(r_ref, k_ref, v_ref, wa_ref, gd_ref, rp_ref, kp_ref, vp_ref, wap_ref, gdp_ref,
                 par_ref, muw_ref, mug_ref, wdec_ref, wic_ref, wg_ref, o_ref, z_ref):
    t = pl.program_id(2)
    ts = r_ref.shape[0]

    @pl.when(t == 0)
    def _():
        z_ref[...] = jnp.zeros(z_ref.shape, F32)

    first = jnp.where(t == 0, 0.0, 1.0)

    def shifted(ref, pref, mu):
        x = ref[...]
        row = lax.broadcasted_iota(jnp.int32, x.shape, 0)
        prev = jnp.where(row == 0, pref[7:8, :] * first, pltpu.roll(x, 1, 0))
        return x + (prev - x) * mu

    par = par_ref[...]
    r_all = shifted(r_ref, rp_ref, par[8:9])
    k_all = shifted(k_ref, kp_ref, par[9:10])
    v_all = shifted(v_ref, vp_ref, par[10:11])
    wa = shifted(wa_ref, wap_ref, muw_ref[...])
    gd = shifted(gd_ref, gdp_ref, mug_ref[...])

    lane = lax.broadcasted_iota(jnp.int32, (LANES, LANES), 1)
    rowl = lax.broadcasted_iota(jnp.int32, (LANES, LANES), 0)
    same_head = (lane // B_HEAD_DIM) == (rowl // B_HEAD_DIM)
    gones = jnp.where(same_head, 1.0, 0.0)
    gones16 = gones.astype(BF16)
    strict = jnp.logical_and(same_head, rowl > lane)
    incl = jnp.logical_and(same_head, rowl >= lane)
    eye = jnp.where(lane == rowl, 1.0, 0.0)
    head0 = lax.broadcasted_iota(jnp.int32, (RC, LANES), 1) < B_HEAD_DIM

    def head_sum(x):
        hi, lo = _split2(x)
        return _dot(hi, gones16) + _dot(lo, gones16)

    def stack_heads(x):
        return jnp.concatenate([jnp.where(head0, x, 0.0), jnp.where(head0, 0.0, x)], axis=0)

    def fold_heads(x):
        return x[:RC] + x[RC:]

    def twice(x):
        return jnp.concatenate([x, x], axis=0)

    rt = lax.broadcasted_iota(jnp.int32, (ts, ts), 0)
    ct = lax.broadcasted_iota(jnp.int32, (ts, ts), 1)
    tri16 = jnp.where(jnp.logical_and((rt // RC) == (ct // RC), rt >= ct), 1.0, 0.0).astype(BF16)
    th_hi, th_lo = _split2(jnp.tanh(wa))
    wa16 = wa.astype(BF16)
    sg16 = jax.nn.sigmoid(gd).astype(BF16)
    nchunk = ts // RC
    npair = r_ref.shape[1] // LANES

    pairs = []
    for q in range(npair):
        ls = slice(q * LANES, (q + 1) * LANES)
        w0, a0, k_k, k_a, r_k, gn_g, gn_b = (par[n:n + 1, ls] for n in range(7))
        r, k, v = r_all[:, ls], k_all[:, ls], v_all[:, ls]
        wd_hi, wd_lo = _split2(wdec_ref[:, ls])
        z = w0 + (_dot(th_hi, wd_hi) + _dot(th_hi, wd_lo) + _dot(th_lo, wd_hi))
        w_log = -(jnp.maximum(-z, 0.0) + jnp.log1p(jnp.exp(-jnp.abs(z)))) - 0.5
        e = jnp.exp(w_log)
        a = jax.nn.sigmoid(a0 + _dot(wa16, wic_ref[:, ls].astype(BF16)))
        g = _dot(sg16, wg_ref[:, ls].astype(BF16))
        kkr = k * k_k
        kk = kkr / jnp.maximum(jnp.sqrt(head_sum(kkr * kkr)), 1e-12)
        k2 = k * (1.0 + (a - 1.0) * k_a)
        bonus = head_sum(r * k2 * r_k) * v
        kka = kk * a
        e_hi, e_mid, e_lo = _split3(e)
        cum = _dot(tri16, e_hi) + _dot(tri16, e_mid) + _dot(tri16, e_lo)
        pairs.append(dict(r_hat=r * jnp.exp(-cum), a_hat=-kk * jnp.exp(e - cum), cum=cum, kka=kka, k2=k2, v=v,
                          b_til=(kka * jnp.exp(cum)).astype(BF16), k_til=(k2 * jnp.exp(cum)).astype(BF16),
                          v16=v.astype(BF16), bonus=bonus, g=g, gn_g=gn_g, gn_b=gn_b))

    items = [(q, c) for q in range(npair) for c in range(nchunk)]
    n = range(len(items))
    sl = [slice(c * RC, (c + 1) * RC) for _, c in items]
    pq = [pairs[q] for q, _ in items]
    tot = [pq[i]["cum"][sl[i].stop - 1:sl[i].stop] for i in n]
    rem = [jnp.exp(pq[i]["cum"][sl[i]] - tot[i]) for i in n]
    a_s = [stack_heads(pq[i]["a_hat"][sl[i]]).astype(BF16) for i in n]
    r_s = [stack_heads(pq[i]["r_hat"][sl[i]]).astype(BF16) for i in n]
    v_s = [stack_heads(pq[i]["v"][sl[i]]).astype(BF16) for i in n]
    ar = [jnp.concatenate([a_s[i], r_s[i]], axis=0) for i in n]
    bk = [jnp.concatenate([twice(pq[i]["b_til"][sl[i]]), twice(pq[i]["k_til"][sl[i]])], axis=0) for i in n]
    m_all = [_dot_nt(ar[i], bk[i]) for i in n]
    l_ab = [jnp.where(strict, m_all[i][:LANES, :LANES], 0.0) for i in n]
    l_ak = [jnp.where(strict, m_all[i][:LANES, LANES:], 0.0).astype(BF16) for i in n]
    m_rb = [jnp.where(incl, m_all[i][LANES:, :LANES], 0.0).astype(BF16) for i in n]
    m_rk = [jnp.where(incl, m_all[i][LANES:, LANES:], 0.0).astype(BF16) for i in n]
    inv = [eye + l_ab[i] for i in n]
    pw = [l_ab[i].astype(BF16) for i in n]
    for _ in range(5):
        pw = [_dot(pw[i], pw[i]).astype(BF16) for i in n]
        inv = [inv[i] + _dot(inv[i].astype(BF16), pw[i]) for i in n]
    lv = [_dot(l_ak[i], v_s[i]).astype(BF16) for i in n]
    x_av = [_dot(inv[i].astype(BF16), jnp.concatenate([a_s[i], lv[i]], axis=1)) for i in n]
    y_av = [_dot(m_rb[i], x_av[i].astype(BF16)) for i in n]
    y_kv = [_dot(m_rk[i], v_s[i]) for i in n]
    r_p = [pq[i]["r_hat"][sl[i]] + fold_heads(y_av[i][:, :LANES]) for i in n]
    y0 = [fold_heads(y_av[i][:, LANES:] + y_kv[i]) for i in n]
    gh = [_dot((pq[i]["kka"][sl[i]] * rem[i]).T.astype(BF16), fold_heads(x_av[i]).astype(BF16)) for i in n]
    kv = [_dot((pq[i]["k2"][sl[i]] * rem[i]).T.astype(BF16), pq[i]["v16"][sl[i]]) for i in n]
    g_mat = [eye * jnp.exp(-tot[i]) + gones * gh[i][:, :LANES] for i in n]
    h_mat = [gones * (gh[i][:, LANES:] + kv[i]) for i in n]
    rg = [jnp.concatenate([r_p[i], g_mat[i]], axis=0).astype(BF16) for i in n]

    zc = [z_ref[q] for q in range(npair)]
    ys = [[] for _ in range(npair)]
    for c in range(nchunk):
        for q in range(npair):
            i = q * nchunk + c
            yz = _dot(rg[i], zc[q].astype(BF16))
            ys[q].append(yz[:RC] + y0[i])
            zc[q] = yz[RC:] + h_mat[i]
    for q in range(npair):
        z_ref[q] = zc[q]
        p = pairs[q]
        y = jnp.concatenate(ys[q], axis=0)
        mean = head_sum(y) * (1.0 / B_HEAD_DIM)
        yc = y - mean
        var = head_sum(yc * yc) * (1.0 / B_HEAD_DIM)
        yn = yc * lax.rsqrt(var + GN_EPS) * p["gn_g"] + p["gn_b"]
        o_ref[:, q * LANES:(q + 1) * LANES] = ((yn + p["bonus"]) * p["g"]).astype(o_ref.dtype)


def _rwkv(hb, par, mu_wa, mu_gd, wdec, wic, wg, bsz, seq):
    nt = seq // RTS
    width = RNP * LANES
    ngroup = B_WIDTH // width
    cb = B_WIDTH // width
    small = (3 * B_WIDTH) // LANES

    def main(shape_w, colblk):
        return pl.BlockSpec((RTS, shape_w), lambda b, p, t: (b * nt + t, colblk(p)))

    def prev(shape_w, colblk):
        return pl.BlockSpec((8, shape_w), lambda b, p, t: (jnp.maximum((b * nt + t) * (RTS // 8) - 1, 0), colblk(p)))

    wide = [lambda p: p, lambda p: cb + p, lambda p: 2 * cb + p]
    narrow = [lambda p: small, lambda p: small + 1]
    in_specs = ([main(width, c) for c in wide] + [main(LANES, c) for c in narrow]
                + [prev(width, c) for c in wide] + [prev(LANES, c) for c in narrow] + [
        pl.BlockSpec((16, width), lambda b, p, t: (0, p)),
        pl.BlockSpec((1, LANES), lambda b, p, t: (0, 0)),
        pl.BlockSpec((1, LANES), lambda b, p, t: (0, 0)),
        pl.BlockSpec((LANES, width), lambda b, p, t: (0, p)),
        pl.BlockSpec((LANES, width), lambda b, p, t: (0, p)),
        pl.BlockSpec((LANES, width), lambda b, p, t: (0, p)),
    ])
    return pl.pallas_call(
        _rwkv_kernel,
        grid=(bsz, ngroup, nt),
        in_specs=in_specs,
        out_specs=pl.BlockSpec((RTS, width), lambda b, p, t: (b * nt + t, p)),
        out_shape=jax.ShapeDtypeStruct((bsz * seq, B_WIDTH), BF16),
        scratch_shapes=[pltpu.VMEM((RNP, LANES, LANES), F32)],
        compiler_params=_cparams(("arbitrary", "arbitrary", "arbitrary")), name="rwkv7",
    )(*([hb] * 10), par, mu_wa, mu_gd, wdec, wic, wg)


def _pack_rwkv_params(mu, w0, w_decay, a0, w_iclr, k_k, k_a, r_k, gn_g, gn_b):
    bw = B_WIDTH
    zero = jnp.zeros((bw,), F32)
    rows = [w0, a0, k_k, k_a, r_k.reshape(bw), gn_g, gn_b, zero,
            mu[:bw], mu[bw:2 * bw], mu[2 * bw:3 * bw], zero, zero, zero, zero, zero]
    par = jnp.stack(rows)
    mu_wa = mu[3 * bw:3 * bw + LANES][None]
    mu_gd = mu[3 * bw + LANES:][None]
    zpad = jnp.zeros((DECAY_LORA, bw), F32)
    wdec = jnp.concatenate([w_decay, zpad], axis=0)
    wic = jnp.concatenate([zpad, w_iclr], axis=0)
    return par, mu_wa, mu_gd, wdec, wic


def _outproj_kernel(x_ref, a_ref, b_ref, wa_ref, wb_ref, g_ref, be_ref, o_ref):
    mix = _dot(a_ref[...], wa_ref[...]) + _dot(b_ref[...], wb_ref[...])
    o_ref[...] = _ln(ALPHA * x_ref[...] + mix, g_ref[...], be_ref[...])


def _outproj(x, a_out, b_out, w_out, g, b, tm=512):
    t, d = x.shape
    wa = w_out[:A_WIDTH].astype(BF16)
    wb = w_out[A_WIDTH:].astype(BF16)
    row = lambda i: (i, 0)
    const = lambda i: (0, 0)
    return pl.pallas_call(
        _outproj_kernel, grid=(t // tm,),
        in_specs=[pl.BlockSpec((tm, d), row), pl.BlockSpec((tm, A_WIDTH), row), pl.BlockSpec((tm, B_WIDTH), row),
                  pl.BlockSpec((A_WIDTH, d), const), pl.BlockSpec((B_WIDTH, d), const),
                  pl.BlockSpec((1, d), const), pl.BlockSpec((1, d), const)],
        out_specs=pl.BlockSpec((tm, d), row), out_shape=jax.ShapeDtypeStruct((t, d), F32),
        compiler_params=_cparams(("arbitrary",)), name="outproj_ln",
    )(x, a_out, b_out, wa, wb, g, b)


def _xor_partner_rows(x, row, bit):
    up = pltpu.roll(x, bit, 0)
    down = pltpu.roll(x, x.shape[0] - bit, 0)
    return jnp.where((row & bit) != 0, up, down)


def _moe_gates_t(x_hi, x_lo, wrt, br):
    w_hi, w_lo = _split2(wrt)
    logits = _dot_nt(w_hi, x_hi) + _dot_nt(w_lo, x_hi) + _dot_nt(w_hi, x_lo)
    row = lax.broadcasted_iota(jnp.int32, logits.shape, 0)
    rowf = row.astype(F32)
    big = float(N_EXPERTS)
    scores = jax.nn.sigmoid(logits)
    sel = scores + br
    p1 = _xor_partner_rows(sel, row, 1)
    hi1, lo1 = jnp.maximum(sel, p1), jnp.minimum(sel, p1)
    hi2, lo2 = _xor_partner_rows(hi1, row, 2), _xor_partner_rows(lo1, row, 2)
    gscore = jnp.maximum(hi1, hi2) + jnp.maximum(jnp.minimum(hi1, hi2), jnp.maximum(lo1, lo2))
    gbest = jnp.max(gscore, axis=0, keepdims=True)
    first = jnp.min(jnp.where(gscore == gbest, rowf, big), axis=0, keepdims=True)
    in_group = jnp.floor(rowf * (1.0 / EXPERTS_PER_GROUP)) == jnp.floor(first * (1.0 / EXPERTS_PER_GROUP))
    masked = jnp.where(in_group, sel, NEG)
    m1 = jnp.max(masked, axis=0, keepdims=True)
    i1 = jnp.min(jnp.where(masked == m1, rowf, big), axis=0, keepdims=True)
    pick1 = rowf == i1
    masked2 = jnp.where(pick1, NEG, masked)
    m2 = jnp.max(masked2, axis=0, keepdims=True)
    i2 = jnp.min(jnp.where(masked2 == m2, rowf, big), axis=0, keepdims=True)
    pick2 = rowf == i2
    s1 = jnp.sum(jnp.where(pick1, scores, 0.0), axis=0, keepdims=True)
    s2 = jnp.sum(jnp.where(pick2, scores, 0.0), axis=0, keepdims=True)
    tot = s1 + s2
    return jnp.where(pick1, s1 / tot, 0.0) + jnp.where(pick2, s2 / tot, 0.0)


def _moe_ple_kernel(x_ref, p_ref, wrt_ref, br_ref, wgu_ref, wd_ref, g_ref, b_ref, wpg_ref, wpp_ref, g2_ref, b2_ref,
                    o_ref, gate_ref, acc_ref, xb_ref):
    e = pl.program_id(1)

    @pl.when(e == 0)
    def _():
        x_hi, x_lo = _split2(x_ref[...])
        gates_t = _moe_gates_t(x_hi, x_lo, wrt_ref[...], br_ref[:, :1])
        pad = jnp.zeros((LANES - gates_t.shape[0], gates_t.shape[1]), F32)
        gate_ref[...] = jnp.concatenate([gates_t, pad], axis=0).T
        xb_ref[...] = x_hi
        acc_ref[...] = jnp.zeros(acc_ref.shape, F32)

    gates = gate_ref[...]
    lane = lax.broadcasted_iota(jnp.int32, gates.shape, 1)
    gcol = jnp.sum(jnp.where(lane == e, gates, 0.0), axis=1, keepdims=True)
    hu = _dot(xb_ref[...], wgu_ref[0])
    hg, up = hu[:, :D_EXPERT], hu[:, D_EXPERT:]
    hid = (hg * jax.nn.sigmoid(hg)) * up * gcol
    acc_ref[...] += _dot(hid.astype(BF16), wd_ref[0])

    @pl.when(e == pl.num_programs(1) - 1)
    def _():
        x2 = _ln(ALPHA * x_ref[...] + acc_ref[...], g_ref[...], b_ref[...])
        gate = jax.nn.sigmoid(_dot(x2.astype(BF16), wpg_ref[...]))
        pe = _dot(p_ref[...].astype(BF16), wpp_ref[...])
        o_ref[...] = _ln(ALPHA * x2 + gate * pe, g2_ref[...], b2_ref[...])


def _moe_ple(x, p, w_router, b_router, w_gate, w_up, w_down, g, b, w_ple_gate, w_ple, g2, b2, tm=1024):
    t, d = x.shape
    ne = w_gate.shape[0]
    wgu = jnp.concatenate([w_gate, w_up], axis=2).astype(BF16)
    wd = w_down.astype(BF16)
    br = jnp.broadcast_to(b_router[:, None], (ne, LANES))
    row = lambda i, e: (i, 0)
    const = lambda i, e: (0, 0)
    return pl.pallas_call(
        _moe_ple_kernel, grid=(t // tm, ne),
        in_specs=[pl.BlockSpec((tm, d), row), pl.BlockSpec((tm, PLE_DIM), row),
                  pl.BlockSpec((ne, d), const), pl.BlockSpec((ne, LANES), const),
                  pl.BlockSpec((1, d, 2 * D_EXPERT), lambda i, e: (e, 0, 0)),
                  pl.BlockSpec((1, D_EXPERT, d), lambda i, e: (e, 0, 0)),
                  pl.BlockSpec((1, d), const), pl.BlockSpec((1, d), const),
                  pl.BlockSpec((d, d), const), pl.BlockSpec((PLE_DIM, d), const),
                  pl.BlockSpec((1, d), const), pl.BlockSpec((1, d), const)],
        out_specs=pl.BlockSpec((tm, d), row), out_shape=jax.ShapeDtypeStruct((t, d), F32),
        scratch_shapes=[pltpu.VMEM((tm, LANES), F32), pltpu.VMEM((tm, d), F32), pltpu.VMEM((tm, d), BF16)],
        compiler_params=_cparams(("arbitrary", "arbitrary")), name="moe_ple_ln",
    )(x, p, w_router.T, br, wgu, wd, g, b, w_ple_gate.astype(BF16), w_ple.astype(BF16), g2, b2)


def kernel(x, p, ln_in_g, ln_in_b, w_in, w_out, mu_shift, w0, w_decay, a0, w_iclr, w_gate_up, k_k, k_a, r_k,
           gn_g, gn_b, rel_bias, w_router, b_router, w_exp_gate, w_exp_up, w_exp_down, w_ple, w_ple_gate,
           ln_mix_g, ln_mix_b, ln_ffn_g, ln_ffn_b, ln_ple_g, ln_ple_b):
    bsz, seq, d = x.shape
    t = bsz * seq
    depth = w_in.shape[0]
    xf = x.reshape(t, d)
    near_bias = _near_bias(rel_bias)
    for i in range(depth):
        w_rm, w_fm = _pack_w_in(w_in[i])
        outs = _inproj(xf, ln_in_g[None], ln_in_b[None], w_rm, w_fm, apply_ln=(i == 0))
        k_rm, kiki, hb, qt, qit, vt, wit = outs[:7]
        if i == 0:
            xf = outs[7]
        a_out = _dsa(qt, qit, wit, k_rm, vt, kiki, near_bias, bsz, seq)
        par, mu_wa, mu_gd, wdec, wic = _pack_rwkv_params(mu_shift[i], w0[i], w_decay[i], a0[i], w_iclr[i],
                                                         k_k[i], k_a[i], r_k[i], gn_g[i], gn_b[i])
        b_out = _rwkv(hb, par, mu_wa, mu_gd, wdec, wic, w_gate_up[i], bsz, seq)
        xf = _outproj(xf, a_out, b_out, w_out[i], ln_mix_g[i][None], ln_mix_b[i][None])
        xf = _moe_ple(xf, p[i].reshape(t, PLE_DIM), w_router, b_router, w_exp_gate[i], w_exp_up[i], w_exp_down[i],
                      ln_ffn_g[i][None], ln_ffn_b[i][None], w_ple_gate[i], w_ple[i],
                      ln_ple_g[i][None], ln_ple_b[i][None])
    return xf.reshape(bsz, seq, d)
```

```python
import functools
import math

import numpy as np
import jax
import jax.numpy as jnp
from jax import lax
from jax.experimental import pallas as pl
from jax.experimental.pallas import tpu as pltpu

D_MODEL = 1024
DEPTH = 2
CHUNK = 64
A_HEAD_DIM = 64
A_WIDTH = D_MODEL // 2
A_HEADS = A_WIDTH // A_HEAD_DIM
IDX_HEADS = 8
IDX_DIM = 64
TOPK_MAX = 256
N_BUCKETS = 32
MAX_DISTANCE = 128
B_HEAD_DIM = 64
B_WIDTH = D_MODEL - A_WIDTH
B_HEADS = B_WIDTH // B_HEAD_DIM
DECAY_LORA = 64
ICLR_LORA = 64
GATE_LORA = 128
A_SIZES = (A_WIDTH, A_WIDTH, A_WIDTH, IDX_HEADS * IDX_DIM, IDX_DIM, IDX_HEADS)
B_SIZES = (B_WIDTH, B_WIDTH, B_WIDTH, DECAY_LORA, ICLR_LORA, GATE_LORA)
A_COLS = sum(A_SIZES)
B_COLS = sum(B_SIZES)
N_EXPERTS = 16
N_GROUPS = 4
EXPERTS_PER_GROUP = N_EXPERTS // N_GROUPS
D_EXPERT = 256
PLE_DIM = 256
ALPHA = (2 * DEPTH) ** 0.25
LN_EPS = 1e-5
GN_EPS = 64e-5
NEG = -1e30

LANES = 128
VMEM_LIMIT = 56 * 1024 * 1024
F32 = jnp.float32
BF16 = jnp.bfloat16
HI = lax.Precision.HIGHEST
NT_DIMS = (((1,), (1,)), ((), ()))


def _cparams(sem):
    return pltpu.CompilerParams(dimension_semantics=sem, vmem_limit_bytes=VMEM_LIMIT)


def _ln(x, g, b):
    mu = jnp.mean(x, axis=-1, keepdims=True)
    xc = x - mu
    var = jnp.mean(xc * xc, axis=-1, keepdims=True)
    return xc * lax.rsqrt(var + LN_EPS) * g + b


def _dot(a, b):
    return jnp.dot(a, b, preferred_element_type=F32)


def _dot_hi(a, b):
    return jnp.dot(a, b, preferred_element_type=F32, precision=HI)


def _dot_nt(a, b):
    return lax.dot_general(a, b, NT_DIMS, preferred_element_type=F32)


def _dot_nt_hi(a, b):
    return lax.dot_general(a, b, NT_DIMS, preferred_element_type=F32, precision=HI)


def _split2(x):
    hi = x.astype(BF16)
    return hi, (x - hi.astype(F32)).astype(BF16)


def _split3(x):
    hi = x.astype(BF16)
    rest = x - hi.astype(F32)
    mid = rest.astype(BF16)
    return hi, mid, (rest - mid.astype(F32)).astype(BF16)


IN_NCHUNK = 512


WI_ROWS = 16
LOG2E = math.log2(math.e)
Q_SCALE = A_HEAD_DIM ** -0.5 * LOG2E


def _inproj_kernel(x_ref, g_ref, b_ref, wr_ref, wf_ref, *out_refs, apply_ln):
    x = x_ref[...]
    if apply_ln:
        x = _ln(x, g_ref[...], b_ref[...])
        out_refs[7][...] = x
    xb = x.astype(BF16)
    k_ref, kk_ref, hb_ref, qt_ref, qit_ref, vt_ref, wit_ref = out_refs[:7]
    aw = A_WIDTH
    k_ref[...] = _dot(xb, wr_ref[:, :aw]).astype(BF16)
    kk_ref[...] = _dot(xb, wr_ref[:, aw:aw + LANES]).astype(BF16)
    nb = hb_ref.shape[1]
    for c0 in range(0, nb, IN_NCHUNK):
        c1 = min(c0 + IN_NCHUNK, nb)
        hb_ref[:, c0:c1] = _dot(xb, wr_ref[:, aw + LANES + c0:aw + LANES + c1])
    qt_ref[...] = (_dot_nt(wf_ref[:aw], xb) * Q_SCALE).astype(BF16)
    qit_ref[...] = _dot_nt(wf_ref[aw:2 * aw], xb).astype(BF16)
    vt = _dot_nt(wf_ref[2 * aw:3 * aw], xb).astype(BF16)
    for n in range(vt_ref.shape[0]):
        vt_ref[n] = vt[:, n * TQ:(n + 1) * TQ]
    wit_ref[...] = _dot_nt(wf_ref[3 * aw:], xb)


def _inproj(x, g, b, w_rm, w_fm, apply_ln, tm=512):
    t, d = x.shape
    aw = A_WIDTH
    nb = w_rm.shape[1] - aw - LANES
    row = lambda i: (i, 0)
    col = lambda i: (0, i)
    const = lambda i: (0, 0)
    out_shape = [jax.ShapeDtypeStruct((t, aw), BF16), jax.ShapeDtypeStruct((t, LANES), BF16),
                 jax.ShapeDtypeStruct((t, nb), F32), jax.ShapeDtypeStruct((aw, t), BF16),
                 jax.ShapeDtypeStruct((aw, t), BF16), jax.ShapeDtypeStruct((t // TQ, aw, TQ), BF16),
                 jax.ShapeDtypeStruct((WI_ROWS, t), F32)]
    out_specs = [pl.BlockSpec((tm, aw), row), pl.BlockSpec((tm, LANES), row), pl.BlockSpec((tm, nb), row),
                 pl.BlockSpec((aw, tm), col), pl.BlockSpec((aw, tm), col),
                 pl.BlockSpec((tm // TQ, aw, TQ), lambda i: (i, 0, 0)), pl.BlockSpec((WI_ROWS, tm), col)]
    if apply_ln:
        out_shape.append(jax.ShapeDtypeStruct((t, d), F32))
        out_specs.append(pl.BlockSpec((tm, d), row))
    return pl.pallas_call(
        functools.partial(_inproj_kernel, apply_ln=apply_ln),
        grid=(t // tm,),
        in_specs=[pl.BlockSpec((tm, d), row), pl.BlockSpec((1, d), const), pl.BlockSpec((1, d), const),
                  pl.BlockSpec(w_rm.shape, const), pl.BlockSpec(w_fm.shape, const)],
        out_specs=out_specs, out_shape=out_shape,
        compiler_params=_cparams(("arbitrary",)), name="inproj_ln" if apply_ln else "inproj",
    )(x, g, b, w_rm, w_fm)


def _pack_w_in(w):
    d = w.shape[0]
    aw = A_WIDTH
    cuts = np.cumsum(A_SIZES)
    wq, wk, wv, wqi = w[:, :aw], w[:, aw:2 * aw], w[:, 2 * aw:3 * aw], w[:, cuts[2]:cuts[3]]
    wki = w[:, cuts[3]:cuts[4]]
    wwi = w[:, cuts[4]:cuts[5]]
    w_rm = jnp.concatenate([wk, wki, wki, w[:, A_COLS:]], axis=1).astype(BF16)
    pad = jnp.zeros((d, WI_ROWS - IDX_HEADS), w.dtype)
    w_fm = jnp.concatenate([wq, wqi, wv, wwi, pad], axis=1).T.astype(BF16)
    return w_rm, w_fm


TQ = 256
ROW_GROUP = 32
BIGF = 3.0e38


def _t5_bucket_np(rel):
    nb = N_BUCKETS // 2
    max_exact = nb // 2
    ret = np.where(rel > 0, nb, 0)
    n = np.abs(rel)
    nf = np.maximum(n, 1).astype(np.float64)
    large = max_exact + np.floor(np.log(nf / max_exact) / math.log(MAX_DISTANCE / max_exact)
                                 * (nb - max_exact) + 1e-9).astype(np.int64)
    large = np.minimum(large, nb - 1)
    return ret + np.where(n < max_exact, n, large)


def _near_bias(rel_bias):
    qpos = np.arange(TQ)[None, :]
    kpos = np.arange(TQ)[:, None]
    bk = np.stack([_t5_bucket_np(kpos - qpos), _t5_bucket_np(kpos - TQ - qpos)])
    far = N_BUCKETS // 2 - 1
    tab = (rel_bias - rel_bias[far][None, :]) * LOG2E
    onehot = jnp.asarray(bk[..., None] == np.arange(N_BUCKETS)).astype(F32)
    return jnp.einsum('nkqb,bh->nhkq', onehot, tab.astype(F32), precision=HI)


BISECT_STEPS = 12
SUBLANES = 8
ROWSUM_MIN = 2.0 ** -60
ROWSUM_MAX = 2.0 ** 100
SNAP_STEPS = 3


def _paired_loop(n, body, carry):
    def pair(j, c):
        return body(2 * j + 1, 1, body(2 * j, 0, c))

    carry = lax.fori_loop(0, lax.shift_right_logical(n, 1), pair, carry)
    return lax.cond((n & 1) == 1, lambda c: body(n - 1, 0, c), lambda c: c, carry)


def _fold8(x, op):
    return op(x.reshape(x.shape[0] // SUBLANES, SUBLANES, x.shape[1]), axis=0)


def _dsa_kernel(qt_ref, qit_ref, wit_ref, k_ref, vt_ref, kk_ref, bias_ref, o_ref,
                s_ref, m_ref, l_ref, acc_ref, p_ref, *, ksel):
    i = pl.program_id(1)
    nt = i + 1
    tq = qt_ref.shape[1]
    kself = float(ksel)
    idx_scale = (IDX_HEADS ** -0.5) * (IDX_DIM ** -0.5)

    sub = lax.broadcasted_iota(jnp.int32, (LANES, tq), 0)
    lower = sub < (LANES // 2)
    zero_b = jnp.zeros((LANES, tq), BF16)

    def head_parts(ref):
        parts = []
        for h in range(A_HEADS):
            j, s = divmod(h, 2)
            blk = ref[j * LANES:(j + 1) * LANES, :]
            parts.append(jnp.where(lower if s == 0 else jnp.logical_not(lower), blk, zero_b))
        return parts

    def key_rows(kt):
        return pl.ds(pl.multiple_of(kt * tq, tq), tq)

    qi_parts = head_parts(qit_ref)
    wi = wit_ref[...] * idx_scale
    wrows = [wi[h:h + 1, :] for h in range(IDX_HEADS)]
    krow = lax.broadcasted_iota(jnp.int32, (tq, tq), 0)
    qcol = lax.broadcasted_iota(jnp.int32, (tq, tq), 1)
    adm_diag = (krow // CHUNK) <= (qcol // CHUNK)

    def score_tile(kt):
        kk = kk_ref[key_rows(kt), :]
        acc = jnp.zeros((tq, tq), F32)
        for h in range(IDX_HEADS):
            acc = acc + wrows[h] * jnp.maximum(_dot(kk, qi_parts[h]), 0.0)
        return acc

    def p1_body(kt, _, carry):
        sc = score_tile(kt)
        s_ref[kt] = sc
        return jnp.minimum(carry[0], _fold8(sc, jnp.min)), jnp.maximum(carry[1], _fold8(sc, jnp.max))

    mn8, mx8 = _paired_loop(i, p1_body, (jnp.full((SUBLANES, tq), BIGF, F32), jnp.full((SUBLANES, tq), -BIGF, F32)))
    sc = score_tile(i)
    s_ref[i] = jnp.where(adm_diag, sc, NEG)
    mn8 = jnp.minimum(mn8, _fold8(jnp.where(adm_diag, sc, BIGF), jnp.min))
    mx8 = jnp.maximum(mx8, _fold8(jnp.where(adm_diag, sc, -BIGF), jnp.max))
    rowmin = jnp.min(mn8, axis=0, keepdims=True)
    rowmax = jnp.max(mx8, axis=0, keepdims=True)

    def reduce_tiles(fn, inits, ops):
        def body(kt, _, carry):
            return fn(s_ref[kt], kt, carry)

        carry = _paired_loop(nt, body, tuple(jnp.full((SUBLANES, tq), v, F32) for v in inits))
        return [op(c, axis=0, keepdims=True) for op, c in zip(ops, carry)]

    qpos = i * tq + lax.broadcasted_iota(jnp.int32, (1, tq), 1)
    n_adm = (qpos // CHUNK + 1) * CHUNK
    trivial = n_adm <= ksel
    lo0 = jnp.where(trivial, 0.1 * NEG, rowmin)
    hi0 = jnp.where(trivial, 0.1 * NEG, rowmax)

    def count_ge(mid):
        def cnt(s, kt, carry):
            return (carry[0] + _fold8(jnp.where(s >= mid, 1.0, 0.0), jnp.sum),)

        return reduce_tiles(cnt, (0.0,), (jnp.sum,))[0]

    def a_body(_, st):
        lo, hi, clo = st
        mid = 0.5 * (lo + hi)
        c = count_ge(mid)
        ok = c >= kself
        return jnp.where(ok, mid, lo), jnp.where(ok, hi, mid), jnp.where(ok, c, clo)

    lo1, hi1, clo1 = lax.fori_loop(0, BISECT_STEPS, a_body, (lo0, hi0, n_adm.astype(F32)))
    hi1 = jnp.where(jnp.logical_or(clo1 == kself, trivial), lo1, hi1)

    def b_cond(st):
        lo, hi = st
        return jnp.max(jnp.where(lo < hi, 1.0, 0.0)) > 0.0

    def b_body(st):
        lo, hi = st
        mid = 0.5 * (lo + hi)
        mid = jnp.where(mid > lo, mid, hi)

        def cnt(s, kt, carry):
            c, vge, vlt = carry
            ge = s >= mid
            return (c + _fold8(jnp.where(ge, 1.0, 0.0), jnp.sum),
                    jnp.minimum(vge, _fold8(jnp.where(ge, s, BIGF), jnp.min)),
                    jnp.maximum(vlt, _fold8(jnp.where(ge, -BIGF, s), jnp.max)))

        c, vge, vlt = reduce_tiles(cnt, (0.0, BIGF, -BIGF), (jnp.sum, jnp.min, jnp.max))
        ok = c >= kself
        move_lo = jnp.logical_and(ok, jnp.logical_not(trivial))
        move_hi = jnp.logical_and(jnp.logical_not(ok), jnp.logical_not(trivial))
        return jnp.where(move_lo, vge, lo), jnp.where(move_hi, vlt, hi)

    snapped = lax.fori_loop(0, SNAP_STEPS, lambda _, st: b_body(st), (lo1, hi1))
    thr, _ = lax.while_loop(b_cond, b_body, snapped)

    def count_ties(s, kt, carry):
        return (carry[0] + _fold8(jnp.where(s >= thr, 1.0, 0.0), jnp.sum),
                carry[1] + _fold8(jnp.where(s > thr, 1.0, 0.0), jnp.sum))

    c_ge, c_gt = reduce_tiles(count_ties, (0.0, 0.0), (jnp.sum, jnp.sum))
    need = kself - c_gt
    tied = c_ge > kself
    s_total = float(s_ref.shape[0] * tq)
    jlo0 = jnp.where(tied, 0.0, s_total - 1.0)
    jhi0 = jnp.where(tied, (nt * tq).astype(F32), s_total)
    kidx = krow.astype(F32)

    def tie_cond(st):
        jlo, jhi = st
        return jnp.max(jhi - jlo) > 1.0

    def tie_body(st):
        jlo, jhi = st
        mid = jnp.floor(0.5 * (jlo + jhi))

        def cnt(s, kt, carry):
            idx = kidx + (kt * tq).astype(F32)
            return (carry[0] + _fold8(jnp.where(jnp.logical_and(s == thr, idx < mid), 1.0, 0.0), jnp.sum),)

        (c,) = reduce_tiles(cnt, (0.0,), (jnp.sum,))
        ok = c >= need
        done = (jhi - jlo) <= 1.0
        return (jnp.where(jnp.logical_or(ok, done), jlo, mid),
                jnp.where(jnp.logical_and(ok, jnp.logical_not(done)), mid, jhi))

    _, jcut = lax.while_loop(tie_cond, tie_body, (jlo0, jhi0))

    q_parts = head_parts(qt_ref)

    def reset(m_init):
        l_ref[...] = jnp.zeros(l_ref.shape, F32)
        acc_ref[...] = jnp.zeros(acc_ref.shape, F32)
        if m_init is not None:
            m_ref[...] = jnp.full(m_ref.shape, m_init, F32)

    def write_mask(kt):
        s = s_ref[kt]
        idx = kidx + (kt * tq).astype(F32)
        sel = jnp.logical_or(s > thr, jnp.logical_and(s == thr, idx < jcut))
        s_ref[kt] = jnp.where(sel, 0.0, NEG)

    def logits(kt, near):
        mb = s_ref[kt]
        for j in range(A_HEADS // 2):
            kp = k_ref[key_rows(kt), j * LANES:(j + 1) * LANES]
            for s_half in range(2):
                h = 2 * j + s_half
                lg = _dot(kp, q_parts[h]) + mb
                if near is not None:
                    lg = lg + bias_ref[near, h]
                yield h, lg

    def accumulate(kt, slot, shifted):
        for h, d in shifted:
            p = jnp.exp2(d)
            l_ref[h] += _fold8(p, jnp.sum)
            p_ref[slot, h] = p.astype(BF16)
        for h in range(A_HEADS):
            acc_ref[h] += _dot(vt_ref[kt, h * A_HEAD_DIM:(h + 1) * A_HEAD_DIM, :], p_ref[slot, h])

    def sweep_single(kt, near, slot):
        write_mask(kt)
        accumulate(kt, slot, ((h, lg - m_ref[h][:1]) for h, lg in logits(kt, near)))

    def sweep_max(kt, near, slot):
        for h, lg in logits(kt, near):
            m_ref[h] = jnp.maximum(m_ref[h], _fold8(lg, jnp.max))

    def sweep_sum(kt, near, slot):
        accumulate(kt, slot, ((h, lg - m_ref[h][:1]) for h, lg in logits(kt, near)))

    def all_tiles(sweep):
        def body(kt, slot, c):
            sweep(kt, None, slot)
            return c

        _paired_loop(jnp.maximum(i - 1, 0), body, 0)

        @pl.when(i >= 1)
        def _():
            sweep(i - 1, 1, 1)

        sweep(i, 0, 0)

    def settle_rows(ref):
        for h in range(A_HEADS):
            ref[h] = jnp.broadcast_to(jnp.max(ref[h], axis=0, keepdims=True), (SUBLANES, tq))

    reset(None)
    for j in range(A_HEADS // 2):
        kp = k_ref[key_rows(i), j * LANES:(j + 1) * LANES]
        for h in (2 * j, 2 * j + 1):
            m_ref[h] = _fold8(_dot(kp, q_parts[h]) + bias_ref[0, h], jnp.max)
    settle_rows(m_ref)
    all_tiles(sweep_single)

    def out_of_range(h):
        tot = jnp.sum(l_ref[h], axis=0, keepdims=True)
        ok = jnp.logical_and(tot >= ROWSUM_MIN, tot <= ROWSUM_MAX)
        return jnp.max(jnp.where(ok, 0.0, 1.0))

    @pl.when(functools.reduce(jnp.maximum, [out_of_range(h) for h in range(A_HEADS)]) > 0.0)
    def _():
        reset(NEG)
        all_tiles(sweep_max)
        settle_rows(m_ref)
        all_tiles(sweep_sum)

    for j in range(A_HEADS // 2):
        outs = [acc_ref[h] / jnp.sum(l_ref[h], axis=0, keepdims=True) for h in (2 * j, 2 * j + 1)]
        o_ref[:, j * LANES:(j + 1) * LANES] = jnp.concatenate(outs, axis=0).T.astype(o_ref.dtype)


def _dsa(qt, qit, wit, k, vt, kiki, near_bias, bsz, seq):
    nq = seq // TQ
    ksel = min(TOPK_MAX, seq // 4)
    qcol = lambda b, i: (0, b * nq + i)
    return pl.pallas_call(
        functools.partial(_dsa_kernel, ksel=ksel),
        grid=(bsz, nq),
        in_specs=[
            pl.BlockSpec((A_WIDTH, TQ), qcol),
            pl.BlockSpec((A_WIDTH, TQ), qcol),
            pl.BlockSpec((WI_ROWS, TQ), qcol),
            pl.BlockSpec((seq, A_WIDTH), lambda b, i: (b, 0)),
            pl.BlockSpec((nq, A_WIDTH, TQ), lambda b, i: (b, 0, 0)),
            pl.BlockSpec((seq, LANES), lambda b, i: (b, 0)),
            pl.BlockSpec((2, A_HEADS, TQ, TQ), lambda b, i: (0, 0, 0, 0)),
        ],
        out_specs=pl.BlockSpec((TQ, A_WIDTH), lambda b, i: (b * nq + i, 0)),
        out_shape=jax.ShapeDtypeStruct((bsz * seq, A_WIDTH), BF16),
        scratch_shapes=[pltpu.VMEM((nq, TQ, TQ), F32), pltpu.VMEM((A_HEADS, SUBLANES, TQ), F32),
                        pltpu.VMEM((A_HEADS, SUBLANES, TQ), F32), pltpu.VMEM((A_HEADS, A_HEAD_DIM, TQ), F32),
                        pltpu.VMEM((2, A_HEADS, TQ, TQ), BF16)],
        compiler_params=_cparams(("arbitrary", "arbitrary")), name="dsa_attention",
    )(qt, qit, wit, k, vt, kiki, near_bias)


RC = 64
RTS = 256


RNP = 4


def _rwkv_kernel(r_ref, k_ref, v_ref, wa_ref, gd_ref, rp_ref, kp_ref, vp_ref, wap_ref, gdp_ref,
                 par_ref, muw_ref, mug_ref, wdec_ref, wic_ref, wg_ref, o_ref, z_ref):
    t = pl.program_id(2)
    ts = r_ref.shape[0]

    @pl.when(t == 0)
    def _():
        z_ref[...] = jnp.zeros(z_ref.shape, F32)

    first = jnp.where(t == 0, 0.0, 1.0)

    def shifted(ref, pref, mu):
        x = ref[...]
        row = lax.broadcasted_iota(jnp.int32, x.shape, 0)
        prev = jnp.where(row == 0, pref[7:8, :] * first, pltpu.roll(x, 1, 0))
        return x + (prev - x) * mu

    par = par_ref[...]
    r_all = shifted(r_ref, rp_ref, par[8:9])
    k_all = shifted(k_ref, kp_ref, par[9:10])
    v_all = shifted(v_ref, vp_ref, par[10:11])
    wa = shifted(wa_ref, wap_ref, muw_ref[...])
    gd = shifted(gd_ref, gdp_ref, mug_ref[...])

    lane = lax.broadcasted_iota(jnp.int32, (LANES, LANES), 1)
    rowl = lax.broadcasted_iota(jnp.int32, (LANES, LANES), 0)
    same_head = (lane // B_HEAD_DIM) == (rowl // B_HEAD_DIM)
    gones = jnp.where(same_head, 1.0, 0.0)
    gones16 = gones.astype(BF16)
    strict = jnp.logical_and(same_head, rowl > lane)
    incl = jnp.logical_and(same_head, rowl >= lane)
    eye = jnp.where(lane == rowl, 1.0, 0.0)
    head0 = lax.broadcasted_iota(jnp.int32, (RC, LANES), 1) < B_HEAD_DIM

    def head_sum(x):
        hi, lo = _split2(x)
        return _dot(hi, gones16) + _dot(lo, gones16)

    def stack_heads(x):
        return jnp.concatenate([jnp.where(head0, x, 0.0), jnp.where(head0, 0.0, x)], axis=0)

    def fold_heads(x):
        return x[:RC] + x[RC:]

    def twice(x):
        return jnp.concatenate([x, x], axis=0)

    rt = lax.broadcasted_iota(jnp.int32, (ts, ts), 0)
    ct = lax.broadcasted_iota(jnp.int32, (ts, ts), 1)
    tri16 = jnp.where(jnp.logical_and((rt // RC) == (ct // RC), rt >= ct), 1.0, 0.0).astype(BF16)
    th_hi, th_lo = _split2(jnp.tanh(wa))
    wa16 = wa.astype(BF16)
    sg16 = jax.nn.sigmoid(gd).astype(BF16)
    nchunk = ts // RC
    npair = r_ref.shape[1] // LANES

    pairs = []
    for q in range(npair):
        ls = slice(q * LANES, (q + 1) * LANES)
        w0, a0, k_k, k_a, r_k, gn_g, gn_b = (par[n:n + 1, ls] for n in range(7))
        r, k, v = r_all[:, ls], k_all[:, ls], v_all[:, ls]
        wd_hi, wd_lo = _split2(wdec_ref[:, ls])
        z = w0 + (_dot(th_hi, wd_hi) + _dot(th_hi, wd_lo) + _dot(th_lo, wd_hi))
        w_log = -(jnp.maximum(-z, 0.0) + jnp.log1p(jnp.exp(-jnp.abs(z)))) - 0.5
        e = jnp.exp(w_log)
        a = jax.nn.sigmoid(a0 + _dot(wa16, wic_ref[:, ls].astype(BF16)))
        g = _dot(sg16, wg_ref[:, ls].astype(BF16))
        kkr = k * k_k
        kk = kkr / jnp.maximum(jnp.sqrt(head_sum(kkr * kkr)), 1e-12)
        k2 = k * (1.0 + (a - 1.0) * k_a)
        bonus = head_sum(r * k2 * r_k) * v
        kka = kk * a
        e_hi, e_mid, e_lo = _split3(e)
        cum = _dot(tri16, e_hi) + _dot(tri16, e_mid) + _dot(tri16, e_lo)
        pairs.append(dict(r_hat=r * jnp.exp(-cum), a_hat=-kk * jnp.exp(e - cum), cum=cum, kka=kka, k2=k2, v=v,
                          b_til=(kka * jnp.exp(cum)).astype(BF16), k_til=(k2 * jnp.exp(cum)).astype(BF16),
                          v16=v.astype(BF16), bonus=bonus, g=g, gn_g=gn_g, gn_b=gn_b))

    items = [(q, c) for q in range(npair) for c in range(nchunk)]
    n = range(len(items))
    sl = [slice(c * RC, (c + 1) * RC) for _, c in items]
    pq = [pairs[q] for q, _ in items]
    tot = [pq[i]["cum"][sl[i].stop - 1:sl[i].stop] for i in n]
    rem = [jnp.exp(pq[i]["cum"][sl[i]] - tot[i]) for i in n]
    a_s = [stack_heads(pq[i]["a_hat"][sl[i]]).astype(BF16) for i in n]
    r_s = [stack_heads(pq[i]["r_hat"][sl[i]]).astype(BF16) for i in n]
    v_s = [stack_heads(pq[i]["v"][sl[i]]).astype(BF16) for i in n]
    ar = [jnp.concatenate([a_s[i], r_s[i]], axis=0) for i in n]
    bk = [jnp.concatenate([twice(pq[i]["b_til"][sl[i]]), twice(pq[i]["k_til"][sl[i]])], axis=0) for i in n]
    m_all = [_dot_nt(ar[i], bk[i]) for i in n]
    l_ab = [jnp.where(strict, m_all[i][:LANES, :LANES], 0.0) for i in n]
    l_ak = [jnp.where(strict, m_all[i][:LANES, LANES:], 0.0).astype(BF16) for i in n]
    m_rb = [jnp.where(incl, m_all[i][LANES:, :LANES], 0.0).astype(BF16) for i in n]
    m_rk = [jnp.where(incl, m_all[i][LANES:, LANES:], 0.0).astype(BF16) for i in n]
    inv = [eye + l_ab[i] for i in n]
    pw = [l_ab[i].astype(BF16) for i in n]
    for _ in range(5):
        pw = [_dot(pw[i], pw[i]).astype(BF16) for i in n]
        inv = [inv[i] + _dot(inv[i].astype(BF16), pw[i]) for i in n]
    lv = [_dot(l_ak[i], v_s[i]).astype(BF16) for i in n]
    x_av = [_dot(inv[i].astype(BF16), jnp.concatenate([a_s[i], lv[i]], axis=1)) for i in n]
    y_av = [_dot(m_rb[i], x_av[i].astype(BF16)) for i in n]
    y_kv = [_dot(m_rk[i], v_s[i]) for i in n]
    r_p = [pq[i]["r_hat"][sl[i]] + fold_heads(y_av[i][:, :LANES]) for i in n]
    y0 = [fold_heads(y_av[i][:, LANES:] + y_kv[i]) for i in n]
    gh = [_dot((pq[i]["kka"][sl[i]] * rem[i]).T.astype(BF16), fold_heads(x_av[i]).astype(BF16)) for i in n]
    kv = [_dot((pq[i]["k2"][sl[i]] * rem[i]).T.astype(BF16), pq[i]["v16"][sl[i]]) for i in n]
    g_mat = [eye * jnp.exp(-tot[i]) + gones * gh[i][:, :LANES] for i in n]
    h_mat = [gones * (gh[i][:, LANES:] + kv[i]) for i in n]
    rg = [jnp.concatenate([r_p[i], g_mat[i]], axis=0).astype(BF16) for i in n]

    zc = [z_ref[q] for q in range(npair)]
    ys = [[] for _ in range(npair)]
    for c in range(nchunk):
        for q in range(npair):
            i = q * nchunk + c
            yz = _dot(rg[i], zc[q].astype(BF16))
            ys[q].append(yz[:RC] + y0[i])
            zc[q] = yz[RC:] + h_mat[i]
    for q in range(npair):
        z_ref[q] = zc[q]
        p = pairs[q]
        y = jnp.concatenate(ys[q], axis=0)
        mean = head_sum(y) * (1.0 / B_HEAD_DIM)
        yc = y - mean
        var = head_sum(yc * yc) * (1.0 / B_HEAD_DIM)
        yn = yc * lax.rsqrt(var + GN_EPS) * p["gn_g"] + p["gn_b"]
        o_ref[:, q * LANES:(q + 1) * LANES] = ((yn + p["bonus"]) * p["g"]).astype(o_ref.dtype)


def _rwkv(hb, par, mu_wa, mu_gd, wdec, wic, wg, bsz, seq):
    nt = seq // RTS
    width = RNP * LANES
    ngroup = B_WIDTH // width
    cb = B_WIDTH // width
    small = (3 * B_WIDTH) // LANES

    def main(shape_w, colblk):
        return pl.BlockSpec((RTS, shape_w), lambda b, p, t: (b * nt + t, colblk(p)))

    def prev(shape_w, colblk):
        return pl.BlockSpec((8, shape_w), lambda b, p, t: (jnp.maximum((b * nt + t) * (RTS // 8) - 1, 0), colblk(p)))

    wide = [lambda p: p, lambda p: cb + p, lambda p: 2 * cb + p]
    narrow = [lambda p: small, lambda p: small + 1]
    in_specs = ([main(width, c) for c in wide] + [main(LANES, c) for c in narrow]
                + [prev(width, c) for c in wide] + [prev(LANES, c) for c in narrow] + [
        pl.BlockSpec((16, width), lambda b, p, t: (0, p)),
        pl.BlockSpec((1, LANES), lambda b, p, t: (0, 0)),
        pl.BlockSpec((1, LANES), lambda b, p, t: (0, 0)),
        pl.BlockSpec((LANES, width), lambda b, p, t: (0, p)),
        pl.BlockSpec((LANES, width), lambda b, p, t: (0, p)),
        pl.BlockSpec((LANES, width), lambda b, p, t: (0, p)),
    ])
    return pl.pallas_call(
        _rwkv_kernel,
        grid=(bsz, ngroup, nt),
        in_specs=in_specs,
        out_specs=pl.BlockSpec((RTS, width), lambda b, p, t: (b * nt + t, p)),
        out_shape=jax.ShapeDtypeStruct((bsz * seq, B_WIDTH), BF16),
        scratch_shapes=[pltpu.VMEM((RNP, LANES, LANES), F32)],
        compiler_params=_cparams(("arbitrary", "arbitrary", "arbitrary")), name="rwkv7",
    )(*([hb] * 10), par, mu_wa, mu_gd, wdec, wic, wg)


def _pack_rwkv_params(mu, w0, w_decay, a0, w_iclr, k_k, k_a, r_k, gn_g, gn_b):
    bw = B_WIDTH
    zero = jnp.zeros((bw,), F32)
    rows = [w0, a0, k_k, k_a, r_k.reshape(bw), gn_g, gn_b, zero,
            mu[:bw], mu[bw:2 * bw], mu[2 * bw:3 * bw], zero, zero, zero, zero, zero]
    par = jnp.stack(rows)
    mu_wa = mu[3 * bw:3 * bw + LANES][None]
    mu_gd = mu[3 * bw + LANES:][None]
    zpad = jnp.zeros((DECAY_LORA, bw), F32)
    wdec = jnp.concatenate([w_decay, zpad], axis=0)
    wic = jnp.concatenate([zpad, w_iclr], axis=0)
    return par, mu_wa, mu_gd, wdec, wic


def _xor_partner_rows(x, row, bit):
    up = pltpu.roll(x, bit, 0)
    down = pltpu.roll(x, x.shape[0] - bit, 0)
    return jnp.where((row & bit) != 0, up, down)


def _moe_gates_t(x_hi, x_lo, wrt, br):
    w_hi, w_lo = _split2(wrt)
    logits = _dot_nt(w_hi, x_hi) + _dot_nt(w_lo, x_hi) + _dot_nt(w_hi, x_lo)
    row = lax.broadcasted_iota(jnp.int32, logits.shape, 0)
    rowf = row.astype(F32)
    big = float(N_EXPERTS)
    scores = jax.nn.sigmoid(logits)
    sel = scores + br
    p1 = _xor_partner_rows(sel, row, 1)
    hi1, lo1 = jnp.maximum(sel, p1), jnp.minimum(sel, p1)
    hi2, lo2 = _xor_partner_rows(hi1, row, 2), _xor_partner_rows(lo1, row, 2)
    gscore = jnp.maximum(hi1, hi2) + jnp.maximum(jnp.minimum(hi1, hi2), jnp.maximum(lo1, lo2))
    gbest = jnp.max(gscore, axis=0, keepdims=True)
    first = jnp.min(jnp.where(gscore == gbest, rowf, big), axis=0, keepdims=True)
    in_group = jnp.floor(rowf * (1.0 / EXPERTS_PER_GROUP)) == jnp.floor(first * (1.0 / EXPERTS_PER_GROUP))
    masked = jnp.where(in_group, sel, NEG)
    m1 = jnp.max(masked, axis=0, keepdims=True)
    i1 = jnp.min(jnp.where(masked == m1, rowf, big), axis=0, keepdims=True)
    pick1 = rowf == i1
    masked2 = jnp.where(pick1, NEG, masked)
    m2 = jnp.max(masked2, axis=0, keepdims=True)
    i2 = jnp.min(jnp.where(masked2 == m2, rowf, big), axis=0, keepdims=True)
    pick2 = rowf == i2
    s1 = jnp.sum(jnp.where(pick1, scores, 0.0), axis=0, keepdims=True)
    s2 = jnp.sum(jnp.where(pick2, scores, 0.0), axis=0, keepdims=True)
    tot = s1 + s2
    return jnp.where(pick1, s1 / tot, 0.0) + jnp.where(pick2, s2 / tot, 0.0)


def _mix_moe_ple_kernel(x_ref, a_ref, bo_ref, woa_ref, wob_ref, gm_ref, bm_ref, p_ref, wrt_ref, br_ref, wgu_ref, wd_ref,
                        g_ref, b_ref, wpg_ref, wpp_ref, g2_ref, b2_ref, o_ref, gate_ref, acc_ref, xb_ref, x1_ref):
    e = pl.program_id(1)

    @pl.when(e == 0)
    def _():
        mix = _dot(a_ref[...], woa_ref[...]) + _dot(bo_ref[...], wob_ref[...])
        x1 = _ln(ALPHA * x_ref[...] + mix, gm_ref[...], bm_ref[...])
        x1_ref[...] = x1
        x_hi, x_lo = _split2(x1)
        gates_t = _moe_gates_t(x_hi, x_lo, wrt_ref[...], br_ref[:, :1])
        pad = jnp.zeros((LANES - gates_t.shape[0], gates_t.shape[1]), F32)
        gate_ref[...] = jnp.concatenate([gates_t, pad], axis=0).T
        xb_ref[...] = x_hi
        acc_ref[...] = jnp.zeros(acc_ref.shape, F32)

    gates = gate_ref[...]
    lane = lax.broadcasted_iota(jnp.int32, gates.shape, 1)
    gcol = jnp.sum(jnp.where(lane == e, gates, 0.0), axis=1, keepdims=True)
    hu = _dot(xb_ref[...], wgu_ref[0])
    hg, up = hu[:, :D_EXPERT], hu[:, D_EXPERT:]
    hid = (hg * jax.nn.sigmoid(hg)) * up * gcol
    acc_ref[...] += _dot(hid.astype(BF16), wd_ref[0])

    @pl.when(e == pl.num_programs(1) - 1)
    def _():
        x2 = _ln(ALPHA * x1_ref[...] + acc_ref[...], g_ref[...], b_ref[...])
        gate = jax.nn.sigmoid(_dot(x2.astype(BF16), wpg_ref[...]))
        pe = _dot(p_ref[...].astype(BF16), wpp_ref[...])
        o_ref[...] = _ln(ALPHA * x2 + gate * pe, g2_ref[...], b2_ref[...])


def _mix_moe_ple(x, a_out, b_out, w_out, gm, bm, p, w_router, b_router, w_gate, w_up, w_down, g, b,
                 w_ple_gate, w_ple, g2, b2, tm=1024):
    t, d = x.shape
    ne = w_gate.shape[0]
    woa = w_out[:A_WIDTH].astype(BF16)
    wob = w_out[A_WIDTH:].astype(BF16)
    wgu = jnp.concatenate([w_gate, w_up], axis=2).astype(BF16)
    wd = w_down.astype(BF16)
    br = jnp.broadcast_to(b_router[:, None], (ne, LANES))
    row = lambda i, e: (i, 0)
    const = lambda i, e: (0, 0)
    vec = pl.BlockSpec((1, d), const)
    return pl.pallas_call(
        _mix_moe_ple_kernel, grid=(t // tm, ne),
        in_specs=[pl.BlockSpec((tm, d), row), pl.BlockSpec((tm, A_WIDTH), row), pl.BlockSpec((tm, B_WIDTH), row),
                  pl.BlockSpec((A_WIDTH, d), const), pl.BlockSpec((B_WIDTH, d), const), vec, vec,
                  pl.BlockSpec((tm, PLE_DIM), row),
                  pl.BlockSpec((ne, d), const), pl.BlockSpec((ne, LANES), const),
                  pl.BlockSpec((1, d, 2 * D_EXPERT), lambda i, e: (e, 0, 0)),
                  pl.BlockSpec((1, D_EXPERT, d), lambda i, e: (e, 0, 0)),
                  vec, vec, pl.BlockSpec((d, d), const), pl.BlockSpec((PLE_DIM, d), const), vec, vec],
        out_specs=pl.BlockSpec((tm, d), row), out_shape=jax.ShapeDtypeStruct((t, d), F32),
        scratch_shapes=[pltpu.VMEM((tm, LANES), F32), pltpu.VMEM((tm, d), F32), pltpu.VMEM((tm, d), BF16),
                        pltpu.VMEM((tm, d), F32)],
        compiler_params=_cparams(("arbitrary", "arbitrary")), name="mix_moe_ple_ln",
    )(x, a_out, b_out, woa, wob, gm, bm, p, w_router.T, br, wgu, wd, g, b,
      w_ple_gate.astype(BF16), w_ple.astype(BF16), g2, b2)


def kernel(x, p, ln_in_g, ln_in_b, w_in, w_out, mu_shift, w0, w_decay, a0, w_iclr, w_gate_up, k_k, k_a, r_k,
           gn_g, gn_b, rel_bias, w_router, b_router, w_exp_gate, w_exp_up, w_exp_down, w_ple, w_ple_gate,
           ln_mix_g, ln_mix_b, ln_ffn_g, ln_ffn_b, ln_ple_g, ln_ple_b):
    bsz, seq, d = x.shape
    t = bsz * seq
    depth = w_in.shape[0]
    xf = x.reshape(t, d)
    near_bias = _near_bias(rel_bias)
    for i in range(depth):
        w_rm, w_fm = _pack_w_in(w_in[i])
        outs = _inproj(xf, ln_in_g[None], ln_in_b[None], w_rm, w_fm, apply_ln=(i == 0))
        k_rm, kiki, hb, qt, qit, vt, wit = outs[:7]
        if i == 0:
            xf = outs[7]
        a_out = _dsa(qt, qit, wit, k_rm, vt, kiki, near_bias, bsz, seq)
        par, mu_wa, mu_gd, wdec, wic = _pack_rwkv_params(mu_shift[i], w0[i], w_decay[i], a0[i], w_iclr[i],
                                                         k_k[i], k_a[i], r_k[i], gn_g[i], gn_b[i])
        b_out = _rwkv(hb, par, mu_wa, mu_gd, wdec, wic, w_gate_up[i], bsz, seq)
        xf = _mix_moe_ple(xf, a_out, b_out, w_out[i], ln_mix_g[i][None], ln_mix_b[i][None],
                          p[i].reshape(t, PLE_DIM), w_router, b_router, w_exp_gate[i], w_exp_up[i], w_exp_down[i],
                          ln_ffn_g[i][None], ln_ffn_b[i][None], w_ple_gate[i], w_ple[i],
                          ln_ple_g[i][None], ln_ple_b[i][None])
    return xf.reshape(bsz, seq, d)
```

```python
import functools
import math

import numpy as np
import jax
import jax.numpy as jnp
from jax import lax
from jax.experimental import pallas as pl
from jax.experimental.pallas import tpu as pltpu

D_MODEL = 1024
DEPTH = 2
CHUNK = 64
A_HEAD_DIM = 64
A_WIDTH = D_MODEL // 2
A_HEADS = A_WIDTH // A_HEAD_DIM
IDX_HEADS = 8
IDX_DIM = 64
TOPK_MAX = 256
N_BUCKETS = 32
MAX_DISTANCE = 128
B_HEAD_DIM = 64
B_WIDTH = D_MODEL - A_WIDTH
B_HEADS = B_WIDTH // B_HEAD_DIM
DECAY_LORA = 64
ICLR_LORA = 64
GATE_LORA = 128
A_SIZES = (A_WIDTH, A_WIDTH, A_WIDTH, IDX_HEADS * IDX_DIM, IDX_DIM, IDX_HEADS)
B_SIZES = (B_WIDTH, B_WIDTH, B_WIDTH, DECAY_LORA, ICLR_LORA, GATE_LORA)
A_COLS = sum(A_SIZES)
B_COLS = sum(B_SIZES)
N_EXPERTS = 16
N_GROUPS = 4
EXPERTS_PER_GROUP = N_EXPERTS // N_GROUPS
D_EXPERT = 256
PLE_DIM = 256
ALPHA = (2 * DEPTH) ** 0.25
LN_EPS = 1e-5
GN_EPS = 64e-5
NEG = -1e30

LANES = 128
VMEM_LIMIT = 56 * 1024 * 1024
F32 = jnp.float32
BF16 = jnp.bfloat16
HI = lax.Precision.HIGHEST
NT_DIMS = (((1,), (1,)), ((), ()))


def _cparams(sem):
    return pltpu.CompilerParams(dimension_semantics=sem, vmem_limit_bytes=VMEM_LIMIT)


def _ln(x, g, b):
    mu = jnp.mean(x, axis=-1, keepdims=True)
    xc = x - mu
    var = jnp.mean(xc * xc, axis=-1, keepdims=True)
    return xc * lax.rsqrt(var + LN_EPS) * g + b


def _dot(a, b):
    return jnp.dot(a, b, preferred_element_type=F32)


def _dot_hi(a, b):
    return jnp.dot(a, b, preferred_element_type=F32, precision=HI)


def _dot_nt(a, b):
    return lax.dot_general(a, b, NT_DIMS, preferred_element_type=F32)


def _dot_nt_hi(a, b):
    return lax.dot_general(a, b, NT_DIMS, preferred_element_type=F32, precision=HI)


def _split2(x):
    hi = x.astype(BF16)
    return hi, (x - hi.astype(F32)).astype(BF16)


def _split3(x):
    hi = x.astype(BF16)
    rest = x - hi.astype(F32)
    mid = rest.astype(BF16)
    return hi, mid, (rest - mid.astype(F32)).astype(BF16)


IN_NCHUNK = 512


WI_ROWS = 16
LOG2E = math.log2(math.e)
Q_SCALE = A_HEAD_DIM ** -0.5 * LOG2E


def _inproj_kernel(x_ref, g_ref, b_ref, wr_ref, wf_ref, *out_refs, apply_ln):
    x = x_ref[...]
    if apply_ln:
        x = _ln(x, g_ref[...], b_ref[...])
        out_refs[7][...] = x
    xb = x.astype(BF16)
    k_ref, kk_ref, hb_ref, qt_ref, qit_ref, vt_ref, wit_ref = out_refs[:7]
    aw = A_WIDTH
    k_ref[...] = _dot(xb, wr_ref[:, :aw]).astype(BF16)
    kk_ref[...] = _dot(xb, wr_ref[:, aw:aw + LANES]).astype(BF16)
    nb = hb_ref.shape[1]
    for c0 in range(0, nb, IN_NCHUNK):
        c1 = min(c0 + IN_NCHUNK, nb)
        hb_ref[:, c0:c1] = _dot(xb, wr_ref[:, aw + LANES + c0:aw + LANES + c1])
    qt_ref[...] = (_dot_nt(wf_ref[:aw], xb) * Q_SCALE).astype(BF16)
    qit_ref[...] = _dot_nt(wf_ref[aw:2 * aw], xb).astype(BF16)
    vt = _dot_nt(wf_ref[2 * aw:3 * aw], xb).astype(BF16)
    for n in range(vt_ref.shape[0]):
        vt_ref[n] = vt[:, n * TQ:(n + 1) * TQ]
    wit_ref[...] = _dot_nt(wf_ref[3 * aw:], xb)


def _inproj(x, g, b, w_rm, w_fm, apply_ln, tm=512):
    t, d = x.shape
    aw = A_WIDTH
    nb = w_rm.shape[1] - aw - LANES
    row = lambda i: (i, 0)
    col = lambda i: (0, i)
    const = lambda i: (0, 0)
    out_shape = [jax.ShapeDtypeStruct((t, aw), BF16), jax.ShapeDtypeStruct((t, LANES), BF16),
                 jax.ShapeDtypeStruct((t, nb), F32), jax.ShapeDtypeStruct((aw, t), BF16),
                 jax.ShapeDtypeStruct((aw, t), BF16), jax.ShapeDtypeStruct((t // TQ, aw, TQ), BF16),
                 jax.ShapeDtypeStruct((WI_ROWS, t), F32)]
    out_specs = [pl.BlockSpec((tm, aw), row), pl.BlockSpec((tm, LANES), row), pl.BlockSpec((tm, nb), row),
                 pl.BlockSpec((aw, tm), col), pl.BlockSpec((aw, tm), col),
                 pl.BlockSpec((tm // TQ, aw, TQ), lambda i: (i, 0, 0)), pl.BlockSpec((WI_ROWS, tm), col)]
    if apply_ln:
        out_shape.append(jax.ShapeDtypeStruct((t, d), F32))
        out_specs.append(pl.BlockSpec((tm, d), row))
    return pl.pallas_call(
        functools.partial(_inproj_kernel, apply_ln=apply_ln),
        grid=(t // tm,),
        in_specs=[pl.BlockSpec((tm, d), row), pl.BlockSpec((1, d), const), pl.BlockSpec((1, d), const),
                  pl.BlockSpec(w_rm.shape, const), pl.BlockSpec(w_fm.shape, const)],
        out_specs=out_specs, out_shape=out_shape,
        compiler_params=_cparams(("arbitrary",)), name="inproj_ln" if apply_ln else "inproj",
    )(x, g, b, w_rm, w_fm)


def _pack_w_in(w):
    d = w.shape[0]
    aw = A_WIDTH
    cuts = np.cumsum(A_SIZES)
    wq, wk, wv, wqi = w[:, :aw], w[:, aw:2 * aw], w[:, 2 * aw:3 * aw], w[:, cuts[2]:cuts[3]]
    wki = w[:, cuts[3]:cuts[4]]
    wwi = w[:, cuts[4]:cuts[5]]
    w_rm = jnp.concatenate([wk, wki, wki, w[:, A_COLS:]], axis=1).astype(BF16)
    pad = jnp.zeros((d, WI_ROWS - IDX_HEADS), w.dtype)
    w_fm = jnp.concatenate([wq, wqi, wv, wwi, pad], axis=1).T.astype(BF16)
    return w_rm, w_fm


TQ = 256
ROW_GROUP = 32
BIGF = 3.0e38


def _t5_bucket_np(rel):
    nb = N_BUCKETS // 2
    max_exact = nb // 2
    ret = np.where(rel > 0, nb, 0)
    n = np.abs(rel)
    nf = np.maximum(n, 1).astype(np.float64)
    large = max_exact + np.floor(np.log(nf / max_exact) / math.log(MAX_DISTANCE / max_exact)
                                 * (nb - max_exact) + 1e-9).astype(np.int64)
    large = np.minimum(large, nb - 1)
    return ret + np.where(n < max_exact, n, large)


def _near_bias(rel_bias):
    qpos = np.arange(TQ)[None, :]
    kpos = np.arange(TQ)[:, None]
    bk = np.stack([_t5_bucket_np(kpos - qpos), _t5_bucket_np(kpos - TQ - qpos)])
    far = N_BUCKETS // 2 - 1
    tab = (rel_bias - rel_bias[far][None, :]) * LOG2E
    onehot = jnp.asarray(bk[..., None] == np.arange(N_BUCKETS)).astype(F32)
    return jnp.einsum('nkqb,bh->nhkq', onehot, tab.astype(F32), precision=HI)


BISECT_STEPS = 12
SUBLANES = 8
ROWSUM_MIN = 2.0 ** -60
ROWSUM_MAX = 2.0 ** 100
SNAP_STEPS = 3


def _paired_loop(n, body, carry):
    def pair(j, c):
        return body(2 * j + 1, 1, body(2 * j, 0, c))

    carry = lax.fori_loop(0, lax.shift_right_logical(n, 1), pair, carry)
    return lax.cond((n & 1) == 1, lambda c: body(n - 1, 0, c), lambda c: c, carry)


def _fold8(x, op):
    return op(x.reshape(x.shape[0] // SUBLANES, SUBLANES, x.shape[1]), axis=0)


def _dsa_kernel(qt_ref, qit_ref, wit_ref, k_ref, vt_ref, kk_ref, bias_ref, o_ref,
                s_ref, m_ref, l_ref, acc_ref, p_ref, *, ksel):
    i = pl.program_id(1)
    nt = i + 1
    tq = qt_ref.shape[1]
    kself = float(ksel)
    idx_scale = (IDX_HEADS ** -0.5) * (IDX_DIM ** -0.5)

    sub = lax.broadcasted_iota(jnp.int32, (LANES, tq), 0)
    lower = sub < (LANES // 2)
    zero_b = jnp.zeros((LANES, tq), BF16)

    def head_parts(ref):
        parts = []
        for h in range(A_HEADS):
            j, s = divmod(h, 2)
            blk = ref[j * LANES:(j + 1) * LANES, :]
            parts.append(jnp.where(lower if s == 0 else jnp.logical_not(lower), blk, zero_b))
        return parts

    def key_rows(kt):
        return pl.ds(pl.multiple_of(kt * tq, tq), tq)

    qi_parts = head_parts(qit_ref)
    wi = wit_ref[...] * idx_scale
    wrows = [wi[h:h + 1, :] for h in range(IDX_HEADS)]
    krow = lax.broadcasted_iota(jnp.int32, (tq, tq), 0)
    qcol = lax.broadcasted_iota(jnp.int32, (tq, tq), 1)
    adm_diag = (krow // CHUNK) <= (qcol // CHUNK)

    def score_tile(kt):
        kk = kk_ref[key_rows(kt), :]
        acc = jnp.zeros((tq, tq), F32)
        for h in range(IDX_HEADS):
            acc = acc + wrows[h] * jnp.maximum(_dot(kk, qi_parts[h]), 0.0)
        return acc

    def p1_body(kt, _, carry):
        sc = score_tile(kt)
        s_ref[kt] = sc
        return jnp.minimum(carry[0], _fold8(sc, jnp.min)), jnp.maximum(carry[1], _fold8(sc, jnp.max))

    mn8, mx8 = _paired_loop(i, p1_body, (jnp.full((SUBLANES, tq), BIGF, F32), jnp.full((SUBLANES, tq), -BIGF, F32)))
    sc = score_tile(i)
    s_ref[i] = jnp.where(adm_diag, sc, NEG)
    mn8 = jnp.minimum(mn8, _fold8(jnp.where(adm_diag, sc, BIGF), jnp.min))
    mx8 = jnp.maximum(mx8, _fold8(jnp.where(adm_diag, sc, -BIGF), jnp.max))
    rowmin = jnp.min(mn8, axis=0, keepdims=True)
    rowmax = jnp.max(mx8, axis=0, keepdims=True)

    def reduce_tiles(fn, inits, ops):
        def body(kt, _, carry):
            return fn(s_ref[kt], kt, carry)

        carry = _paired_loop(nt, body, tuple(jnp.full((SUBLANES, tq), v, F32) for v in inits))
        return [op(c, axis=0, keepdims=True) for op, c in zip(ops, carry)]

    qpos = i * tq + lax.broadcasted_iota(jnp.int32, (1, tq), 1)
    n_adm = (qpos // CHUNK + 1) * CHUNK
    trivial = n_adm <= ksel
    lo0 = jnp.where(trivial, 0.1 * NEG, rowmin)
    hi0 = jnp.where(trivial, 0.1 * NEG, rowmax)

    def count_ge(mid):
        def cnt(s, kt, carry):
            return (carry[0] + _fold8(jnp.where(s >= mid, 1.0, 0.0), jnp.sum),)

        return reduce_tiles(cnt, (0.0,), (jnp.sum,))[0]

    def a_body(_, st):
        lo, hi, clo = st
        mid = 0.5 * (lo + hi)
        c = count_ge(mid)
        ok = c >= kself
        return jnp.where(ok, mid, lo), jnp.where(ok, hi, mid), jnp.where(ok, c, clo)

    lo1, hi1, clo1 = lax.fori_loop(0, BISECT_STEPS, a_body, (lo0, hi0, n_adm.astype(F32)))
    hi1 = jnp.where(jnp.logical_or(clo1 == kself, trivial), lo1, hi1)

    def b_cond(st):
        lo, hi = st
        return jnp.max(jnp.where(lo < hi, 1.0, 0.0)) > 0.0

    def b_body(st):
        lo, hi = st
        mid = 0.5 * (lo + hi)
        mid = jnp.where(mid > lo, mid, hi)

        def cnt(s, kt, carry):
            c, vge, vlt = carry
            ge = s >= mid
            return (c + _fold8(jnp.where(ge, 1.0, 0.0), jnp.sum),
                    jnp.minimum(vge, _fold8(jnp.where(ge, s, BIGF), jnp.min)),
                    jnp.maximum(vlt, _fold8(jnp.where(ge, -BIGF, s), jnp.max)))

        c, vge, vlt = reduce_tiles(cnt, (0.0, BIGF, -BIGF), (jnp.sum, jnp.min, jnp.max))
        ok = c >= kself
        move_lo = jnp.logical_and(ok, jnp.logical_not(trivial))
        move_hi = jnp.logical_and(jnp.logical_not(ok), jnp.logical_not(trivial))
        return jnp.where(move_lo, vge, lo), jnp.where(move_hi, vlt, hi)

    snapped = lax.fori_loop(0, SNAP_STEPS, lambda _, st: b_body(st), (lo1, hi1))
    thr, _ = lax.while_loop(b_cond, b_body, snapped)

    def count_ties(s, kt, carry):
        return (carry[0] + _fold8(jnp.where(s >= thr, 1.0, 0.0), jnp.sum),
                carry[1] + _fold8(jnp.where(s > thr, 1.0, 0.0), jnp.sum))

    c_ge, c_gt = reduce_tiles(count_ties, (0.0, 0.0), (jnp.sum, jnp.sum))
    need = kself - c_gt
    tied = c_ge > kself
    s_total = float(s_ref.shape[0] * tq)
    jlo0 = jnp.where(tied, 0.0, s_total - 1.0)
    jhi0 = jnp.where(tied, (nt * tq).astype(F32), s_total)
    kidx = krow.astype(F32)

    def tie_cond(st):
        jlo, jhi = st
        return jnp.max(jhi - jlo) > 1.0

    def tie_body(st):
        jlo, jhi = st
        mid = jnp.floor(0.5 * (jlo + jhi))

        def cnt(s, kt, carry):
            idx = kidx + (kt * tq).astype(F32)
            return (carry[0] + _fold8(jnp.where(jnp.logical_and(s == thr, idx < mid), 1.0, 0.0), jnp.sum),)

        (c,) = reduce_tiles(cnt, (0.0,), (jnp.sum,))
        ok = c >= need
        done = (jhi - jlo) <= 1.0
        return (jnp.where(jnp.logical_or(ok, done), jlo, mid),
                jnp.where(jnp.logical_and(ok, jnp.logical_not(done)), mid, jhi))

    _, jcut = lax.while_loop(tie_cond, tie_body, (jlo0, jhi0))

    q_parts = head_parts(qt_ref)

    def reset(m_init):
        l_ref[...] = jnp.zeros(l_ref.shape, F32)
        acc_ref[...] = jnp.zeros(acc_ref.shape, F32)
        if m_init is not None:
            m_ref[...] = jnp.full(m_ref.shape, m_init, F32)

    def logits(kt, near, mb):
        for j in range(A_HEADS // 2):
            kp = k_ref[key_rows(kt), j * LANES:(j + 1) * LANES]
            for s_half in range(2):
                h = 2 * j + s_half
                lg = _dot(kp, q_parts[h]) + mb
                if near is not None:
                    lg = lg + bias_ref[near, h]
                yield h, lg

    def accumulate(kt, slot, shifted):
        for h, d in shifted:
            p = jnp.exp2(d)
            l_ref[h] += _fold8(p, jnp.sum)
            p_ref[slot, h] = p.astype(BF16)
        for h in range(A_HEADS):
            acc_ref[h] += _dot(vt_ref[kt, h * A_HEAD_DIM:(h + 1) * A_HEAD_DIM, :], p_ref[slot, h])

    def sweep_single(kt, near, slot, neg_shift):
        s = s_ref[kt]
        idx = kidx + (kt * tq).astype(F32)
        sel = jnp.logical_or(s > thr, jnp.logical_and(s == thr, idx < jcut))
        s_ref[kt] = jnp.where(sel, 0.0, NEG)
        accumulate(kt, slot, logits(kt, near, jnp.where(sel, neg_shift, NEG)))

    def sweep_max(kt, near, slot):
        for h, lg in logits(kt, near, s_ref[kt]):
            m_ref[h] = jnp.maximum(m_ref[h], _fold8(lg, jnp.max))

    def sweep_sum(kt, near, slot):
        accumulate(kt, slot, ((h, lg - m_ref[h][:1]) for h, lg in logits(kt, near, s_ref[kt])))

    def all_tiles(sweep):
        def body(kt, slot, c):
            sweep(kt, None, slot)
            return c

        _paired_loop(jnp.maximum(i - 1, 0), body, 0)

        @pl.when(i >= 1)
        def _():
            sweep(i - 1, 1, 1)

        sweep(i, 0, 0)

    def settle_rows(ref):
        for h in range(A_HEADS):
            ref[h] = jnp.broadcast_to(jnp.max(ref[h], axis=0, keepdims=True), (SUBLANES, tq))

    reset(None)
    diag_max = jnp.full((SUBLANES, tq), NEG, F32)
    for j in range(A_HEADS // 2):
        kp = k_ref[key_rows(i), j * LANES:(j + 1) * LANES]
        for h in (2 * j, 2 * j + 1):
            diag_max = jnp.maximum(diag_max, _fold8(_dot(kp, q_parts[h]) + bias_ref[0, h], jnp.max))
    neg_shift = -jnp.max(diag_max, axis=0, keepdims=True)
    all_tiles(functools.partial(sweep_single, neg_shift=neg_shift))

    def out_of_range(h):
        tot = jnp.sum(l_ref[h], axis=0, keepdims=True)
        ok = jnp.logical_and(tot >= ROWSUM_MIN, tot <= ROWSUM_MAX)
        return jnp.max(jnp.where(ok, 0.0, 1.0))

    @pl.when(functools.reduce(jnp.maximum, [out_of_range(h) for h in range(A_HEADS)]) > 0.0)
    def _():
        reset(NEG)
        all_tiles(sweep_max)
        settle_rows(m_ref)
        all_tiles(sweep_sum)

    for j in range(A_HEADS // 2):
        outs = [acc_ref[h] / jnp.sum(l_ref[h], axis=0, keepdims=True) for h in (2 * j, 2 * j + 1)]
        o_ref[:, j * LANES:(j + 1) * LANES] = jnp.concatenate(outs, axis=0).T.astype(o_ref.dtype)


def _dsa(qt, qit, wit, k, vt, kiki, near_bias, bsz, seq):
    nq = seq // TQ
    ksel = min(TOPK_MAX, seq // 4)
    qcol = lambda b, i: (0, b * nq + i)
    return pl.pallas_call(
        functools.partial(_dsa_kernel, ksel=ksel),
        grid=(bsz, nq),
        in_specs=[
            pl.BlockSpec((A_WIDTH, TQ), qcol),
            pl.BlockSpec((A_WIDTH, TQ), qcol),
            pl.BlockSpec((WI_ROWS, TQ), qcol),
            pl.BlockSpec((seq, A_WIDTH), lambda b, i: (b, 0)),
            pl.BlockSpec((nq, A_WIDTH, TQ), lambda b, i: (b, 0, 0)),
            pl.BlockSpec((seq, LANES), lambda b, i: (b, 0)),
            pl.BlockSpec((2, A_HEADS, TQ, TQ), lambda b, i: (0, 0, 0, 0)),
        ],
        out_specs=pl.BlockSpec((TQ, A_WIDTH), lambda b, i: (b * nq + i, 0)),
        out_shape=jax.ShapeDtypeStruct((bsz * seq, A_WIDTH), BF16),
        scratch_shapes=[pltpu.VMEM((nq, TQ, TQ), F32), pltpu.VMEM((A_HEADS, SUBLANES, TQ), F32),
                        pltpu.VMEM((A_HEADS, SUBLANES, TQ), F32), pltpu.VMEM((A_HEADS, A_HEAD_DIM, TQ), F32),
                        pltpu.VMEM((2, A_HEADS, TQ, TQ), BF16)],
        compiler_params=_cparams(("arbitrary", "arbitrary")), name="dsa_attention",
    )(qt, qit, wit, k, vt, kiki, near_bias)


RC = 64
RTS = 256


RNP = 4


def _rwkv_kernel(r_ref, k_ref, v_ref, wa_ref, gd_ref, rp_ref, kp_ref, vp_ref, wap_ref, gdp_ref,
                 par_ref, muw_ref, mug_ref, wdec_ref, wic_ref, wg_ref, o_ref, z_ref):
    t = pl.program_id(2)
    ts = r_ref.shape[0]

    @pl.when(t == 0)
    def _():
        z_ref[...] = jnp.zeros(z_ref.shape, F32)

    first = jnp.where(t == 0, 0.0, 1.0)

    def shifted(ref, pref, mu):
        x = ref[...]
        row = lax.broadcasted_iota(jnp.int32, x.shape, 0)
        prev = jnp.where(row == 0, pref[7:8, :] * first, pltpu.roll(x, 1, 0))
        return x + (prev - x) * mu

    par = par_ref[...]
    r_all = shifted(r_ref, rp_ref, par[8:9])
    k_all = shifted(k_ref, kp_ref, par[9:10])
    v_all = shifted(v_ref, vp_ref, par[10:11])
    wa = shifted(wa_ref, wap_ref, muw_ref[...])
    gd = shifted(gd_ref, gdp_ref, mug_ref[...])

    lane = lax.broadcasted_iota(jnp.int32, (LANES, LANES), 1)
    rowl = lax.broadcasted_iota(jnp.int32, (LANES, LANES), 0)
    same_head = (lane // B_HEAD_DIM) == (rowl // B_HEAD_DIM)
    gones = jnp.where(same_head, 1.0, 0.0)
    gones16 = gones.astype(BF16)
    strict = jnp.logical_and(same_head, rowl > lane)
    incl = jnp.logical_and(same_head, rowl >= lane)
    eye = jnp.where(lane == rowl, 1.0, 0.0)
    head0 = lax.broadcasted_iota(jnp.int32, (RC, LANES), 1) < B_HEAD_DIM

    def head_sum(x):
        hi, lo = _split2(x)
        return _dot(hi, gones16) + _dot(lo, gones16)

    def stack_heads(x):
        return jnp.concatenate([jnp.where(head0, x, 0.0), jnp.where(head0, 0.0, x)], axis=0)

    def fold_heads(x):
        return x[:RC] + x[RC:]

    def twice(x):
        return jnp.concatenate([x, x], axis=0)

    rt = lax.broadcasted_iota(jnp.int32, (ts, ts), 0)
    ct = lax.broadcasted_iota(jnp.int32, (ts, ts), 1)
    tri16 = jnp.where(jnp.logical_and((rt // RC) == (ct // RC), rt >= ct), 1.0, 0.0).astype(BF16)
    th_hi, th_lo = _split2(jnp.tanh(wa))
    wa16 = wa.astype(BF16)
    sg16 = jax.nn.sigmoid(gd).astype(BF16)
    nchunk = ts // RC
    npair = r_ref.shape[1] // LANES

    pairs = []
    for q in range(npair):
        ls = slice(q * LANES, (q + 1) * LANES)
        w0, a0, k_k, k_a, r_k, gn_g, gn_b = (par[n:n + 1, ls] for n in range(7))
        r, k, v = r_all[:, ls], k_all[:, ls], v_all[:, ls]
        wd_hi, wd_lo = _split2(wdec_ref[:, ls])
        z = w0 + (_dot(th_hi, wd_hi) + _dot(th_hi, wd_lo) + _dot(th_lo, wd_hi))
        w_log = -(jnp.maximum(-z, 0.0) + jnp.log1p(jnp.exp(-jnp.abs(z)))) - 0.5
        e = jnp.exp(w_log)
        a = jax.nn.sigmoid(a0 + _dot(wa16, wic_ref[:, ls].astype(BF16)))
        g = _dot(sg16, wg_ref[:, ls].astype(BF16))
        kkr = k * k_k
        kk = kkr / jnp.maximum(jnp.sqrt(head_sum(kkr * kkr)), 1e-12)
        k2 = k * (1.0 + (a - 1.0) * k_a)
        bonus = head_sum(r * k2 * r_k) * v
        kka = kk * a
        e_hi, e_mid, e_lo = _split3(e)
        cum = _dot(tri16, e_hi) + _dot(tri16, e_mid) + _dot(tri16, e_lo)
        pairs.append(dict(r_hat=r * jnp.exp(-cum), a_hat=-kk * jnp.exp(e - cum), cum=cum, kka=kka, k2=k2, v=v,
                          b_til=(kka * jnp.exp(cum)).astype(BF16), k_til=(k2 * jnp.exp(cum)).astype(BF16),
                          v16=v.astype(BF16), bonus=bonus, g=g, gn_g=gn_g, gn_b=gn_b))

    items = [(q, c) for q in range(npair) for c in range(nchunk)]
    n = range(len(items))
    sl = [slice(c * RC, (c + 1) * RC) for _, c in items]
    pq = [pairs[q] for q, _ in items]
    tot = [pq[i]["cum"][sl[i].stop - 1:sl[i].stop] for i in n]
    rem = [jnp.exp(pq[i]["cum"][sl[i]] - tot[i]) for i in n]
    a_s = [stack_heads(pq[i]["a_hat"][sl[i]]).astype(BF16) for i in n]
    r_s = [stack_heads(pq[i]["r_hat"][sl[i]]).astype(BF16) for i in n]
    v_s = [stack_heads(pq[i]["v"][sl[i]]).astype(BF16) for i in n]
    ar = [jnp.concatenate([a_s[i], r_s[i]], axis=0) for i in n]
    bk = [jnp.concatenate([twice(pq[i]["b_til"][sl[i]]), twice(pq[i]["k_til"][sl[i]])], axis=0) for i in n]
    m_all = [_dot_nt(ar[i], bk[i]) for i in n]
    l_ab = [jnp.where(strict, m_all[i][:LANES, :LANES], 0.0) for i in n]
    l_ak = [jnp.where(strict, m_all[i][:LANES, LANES:], 0.0).astype(BF16) for i in n]
    m_rb = [jnp.where(incl, m_all[i][LANES:, :LANES], 0.0).astype(BF16) for i in n]
    m_rk = [jnp.where(incl, m_all[i][LANES:, LANES:], 0.0).astype(BF16) for i in n]
    inv = [eye + l_ab[i] for i in n]
    pw = [l_ab[i].astype(BF16) for i in n]
    for _ in range(5):
        pw = [_dot(pw[i], pw[i]).astype(BF16) for i in n]
        inv = [inv[i] + _dot(inv[i].astype(BF16), pw[i]) for i in n]
    lv = [_dot(l_ak[i], v_s[i]).astype(BF16) for i in n]
    x_av = [_dot(inv[i].astype(BF16), jnp.concatenate([a_s[i], lv[i]], axis=1)) for i in n]
    y_av = [_dot(m_rb[i], x_av[i].astype(BF16)) for i in n]
    y_kv = [_dot(m_rk[i], v_s[i]) for i in n]
    r_p = [pq[i]["r_hat"][sl[i]] + fold_heads(y_av[i][:, :LANES]) for i in n]
    y0 = [fold_heads(y_av[i][:, LANES:] + y_kv[i]) for i in n]
    gh = [_dot((pq[i]["kka"][sl[i]] * rem[i]).T.astype(BF16), fold_heads(x_av[i]).astype(BF16)) for i in n]
    kv = [_dot((pq[i]["k2"][sl[i]] * rem[i]).T.astype(BF16), pq[i]["v16"][sl[i]]) for i in n]
    g_mat = [eye * jnp.exp(-tot[i]) + gones * gh[i][:, :LANES] for i in n]
    h_mat = [gones * (gh[i][:, LANES:] + kv[i]) for i in n]
    rg = [jnp.concatenate([r_p[i], g_mat[i]], axis=0).astype(BF16) for i in n]

    zc = [z_ref[q] for q in range(npair)]
    ys = [[] for _ in range(npair)]
    for c in range(nchunk):
        for q in range(npair):
            i = q * nchunk + c
            yz = _dot(rg[i], zc[q].astype(BF16))
            ys[q].append(yz[:RC] + y0[i])
            zc[q] = yz[RC:] + h_mat[i]
    for q in range(npair):
        z_ref[q] = zc[q]
        p = pairs[q]
        y = jnp.concatenate(ys[q], axis=0)
        mean = head_sum(y) * (1.0 / B_HEAD_DIM)
        yc = y - mean
        var = head_sum(yc * yc) * (1.0 / B_HEAD_DIM)
        yn = yc * lax.rsqrt(var + GN_EPS) * p["gn_g"] + p["gn_b"]
        o_ref[:, q * LANES:(q + 1) * LANES] = ((yn + p["bonus"]) * p["g"]).astype(o_ref.dtype)


def _rwkv(hb, par, mu_wa, mu_gd, wdec, wic, wg, bsz, seq):
    nt = seq // RTS
    width = RNP * LANES
    ngroup = B_WIDTH // width
    cb = B_WIDTH // width
    small = (3 * B_WIDTH) // LANES

    def main(shape_w, colblk):
        return pl.BlockSpec((RTS, shape_w), lambda b, p, t: (b * nt + t, colblk(p)))

    def prev(shape_w, colblk):
        return pl.BlockSpec((8, shape_w), lambda b, p, t: (jnp.maximum((b * nt + t) * (RTS // 8) - 1, 0), colblk(p)))

    wide = [lambda p: p, lambda p: cb + p, lambda p: 2 * cb + p]
    narrow = [lambda p: small, lambda p: small + 1]
    in_specs = ([main(width, c) for c in wide] + [main(LANES, c) for c in narrow]
                + [prev(width, c) for c in wide] + [prev(LANES, c) for c in narrow] + [
        pl.BlockSpec((16, width), lambda b, p, t: (0, p)),
        pl.BlockSpec((1, LANES), lambda b, p, t: (0, 0)),
        pl.BlockSpec((1, LANES), lambda b, p, t: (0, 0)),
        pl.BlockSpec((LANES, width), lambda b, p, t: (0, p)),
        pl.BlockSpec((LANES, width), lambda b, p, t: (0, p)),
        pl.BlockSpec((LANES, width), lambda b, p, t: (0, p)),
    ])
    return pl.pallas_call(
        _rwkv_kernel,
        grid=(bsz, ngroup, nt),
        in_specs=in_specs,
        out_specs=pl.BlockSpec((RTS, width), lambda b, p, t: (b * nt + t, p)),
        out_shape=jax.ShapeDtypeStruct((bsz * seq, B_WIDTH), BF16),
        scratch_shapes=[pltpu.VMEM((RNP, LANES, LANES), F32)],
        compiler_params=_cparams(("arbitrary", "arbitrary", "arbitrary")), name="rwkv7",
    )(*([hb] * 10), par, mu_wa, mu_gd, wdec, wic, wg)


def _pack_rwkv_params(mu, w0, w_decay, a0, w_iclr, k_k, k_a, r_k, gn_g, gn_b):
    bw = B_WIDTH
    zero = jnp.zeros((bw,), F32)
    rows = [w0, a0, k_k, k_a, r_k.reshape(bw), gn_g, gn_b, zero,
            mu[:bw], mu[bw:2 * bw], mu[2 * bw:3 * bw], zero, zero, zero, zero, zero]
    par = jnp.stack(rows)
    mu_wa = mu[3 * bw:3 * bw + LANES][None]
    mu_gd = mu[3 * bw + LANES:][None]
    zpad = jnp.zeros((DECAY_LORA, bw), F32)
    wdec = jnp.concatenate([w_decay, zpad], axis=0)
    wic = jnp.concatenate([zpad, w_iclr], axis=0)
    return par, mu_wa, mu_gd, wdec, wic


def _xor_partner_rows(x, row, bit):
    up = pltpu.roll(x, bit, 0)
    down = pltpu.roll(x, x.shape[0] - bit, 0)
    return jnp.where((row & bit) != 0, up, down)


def _moe_gates_t(x_hi, x_lo, wrt, br):
    w_hi, w_lo = _split2(wrt)
    logits = _dot_nt(w_hi, x_hi) + _dot_nt(w_lo, x_hi) + _dot_nt(w_hi, x_lo)
    row = lax.broadcasted_iota(jnp.int32, logits.shape, 0)
    rowf = row.astype(F32)
    big = float(N_EXPERTS)
    scores = jax.nn.sigmoid(logits)
    sel = scores + br
    p1 = _xor_partner_rows(sel, row, 1)
    hi1, lo1 = jnp.maximum(sel, p1), jnp.minimum(sel, p1)
    hi2, lo2 = _xor_partner_rows(hi1, row, 2), _xor_partner_rows(lo1, row, 2)
    gscore = jnp.maximum(hi1, hi2) + jnp.maximum(jnp.minimum(hi1, hi2), jnp.maximum(lo1, lo2))
    gbest = jnp.max(gscore, axis=0, keepdims=True)
    first = jnp.min(jnp.where(gscore == gbest, rowf, big), axis=0, keepdims=True)
    in_group = jnp.floor(rowf * (1.0 / EXPERTS_PER_GROUP)) == jnp.floor(first * (1.0 / EXPERTS_PER_GROUP))
    masked = jnp.where(in_group, sel, NEG)
    m1 = jnp.max(masked, axis=0, keepdims=True)
    i1 = jnp.min(jnp.where(masked == m1, rowf, big), axis=0, keepdims=True)
    pick1 = rowf == i1
    masked2 = jnp.where(pick1, NEG, masked)
    m2 = jnp.max(masked2, axis=0, keepdims=True)
    i2 = jnp.min(jnp.where(masked2 == m2, rowf, big), axis=0, keepdims=True)
    pick2 = rowf == i2
    s1 = jnp.sum(jnp.where(pick1, scores, 0.0), axis=0, keepdims=True)
    s2 = jnp.sum(jnp.where(pick2, scores, 0.0), axis=0, keepdims=True)
    tot = s1 + s2
    return jnp.where(pick1, s1 / tot, 0.0) + jnp.where(pick2, s2 / tot, 0.0)


def _mix_moe_ple_kernel(x_ref, a_ref, bo_ref, woa_ref, wob_ref, gm_ref, bm_ref, p_ref, wrt_ref, br_ref, wgu_ref, wd_ref,
                        g_ref, b_ref, wpg_ref, wpp_ref, g2_ref, b2_ref, o_ref, gate_ref, acc_ref, xb_ref, x1_ref):
    e = pl.program_id(1)

    @pl.when(e == 0)
    def _():
        mix = _dot(a_ref[...], woa_ref[...]) + _dot(bo_ref[...], wob_ref[...])
        x1 = _ln(ALPHA * x_ref[...] + mix, gm_ref[...], bm_ref[...])
        x1_ref[...] = x1
        x_hi, x_lo = _split2(x1)
        gates_t = _moe_gates_t(x_hi, x_lo, wrt_ref[...], br_ref[:, :1])
        pad = jnp.zeros((LANES - gates_t.shape[0], gates_t.shape[1]), F32)
        gate_ref[...] = jnp.concatenate([gates_t, pad], axis=0).T
        xb_ref[...] = x_hi
        acc_ref[...] = jnp.zeros(acc_ref.shape, F32)

    gates = gate_ref[...]
    lane = lax.broadcasted_iota(jnp.int32, gates.shape, 1)
    gcol = jnp.sum(jnp.where(lane == e, gates, 0.0), axis=1, keepdims=True)
    hu = _dot(xb_ref[...], wgu_ref[0])
    hg, up = hu[:, :D_EXPERT], hu[:, D_EXPERT:]
    hid = (hg * jax.nn.sigmoid(hg)) * up * gcol
    acc_ref[...] += _dot(hid.astype(BF16), wd_ref[0])

    @pl.when(e == pl.num_programs(1) - 1)
    def _():
        x2 = _ln(ALPHA * x1_ref[...] + acc_ref[...], g_ref[...], b_ref[...])
        gate = jax.nn.sigmoid(_dot(x2.astype(BF16), wpg_ref[...]))
        pe = _dot(p_ref[...].astype(BF16), wpp_ref[...])
        o_ref[...] = _ln(ALPHA * x2 + gate * pe, g2_ref[...], b2_ref[...])


def _mix_moe_ple(x, a_out, b_out, w_out, gm, bm, p, w_router, b_router, w_gate, w_up, w_down, g, b,
                 w_ple_gate, w_ple, g2, b2, tm=1024):
    t, d = x.shape
    ne = w_gate.shape[0]
    woa = w_out[:A_WIDTH].astype(BF16)
    wob = w_out[A_WIDTH:].astype(BF16)
    wgu = jnp.concatenate([w_gate, w_up], axis=2).astype(BF16)
    wd = w_down.astype(BF16)
    br = jnp.broadcast_to(b_router[:, None], (ne, LANES))
    row = lambda i, e: (i, 0)
    const = lambda i, e: (0, 0)
    vec = pl.BlockSpec((1, d), const)
    return pl.pallas_call(
        _mix_moe_ple_kernel, grid=(t // tm, ne),
        in_specs=[pl.BlockSpec((tm, d), row), pl.BlockSpec((tm, A_WIDTH), row), pl.BlockSpec((tm, B_WIDTH), row),
                  pl.BlockSpec((A_WIDTH, d), const), pl.BlockSpec((B_WIDTH, d), const), vec, vec,
                  pl.BlockSpec((tm, PLE_DIM), row),
                  pl.BlockSpec((ne, d), const), pl.BlockSpec((ne, LANES), const),
                  pl.BlockSpec((1, d, 2 * D_EXPERT), lambda i, e: (e, 0, 0)),
                  pl.BlockSpec((1, D_EXPERT, d), lambda i, e: (e, 0, 0)),
                  vec, vec, pl.BlockSpec((d, d), const), pl.BlockSpec((PLE_DIM, d), const), vec, vec],
        out_specs=pl.BlockSpec((tm, d), row), out_shape=jax.ShapeDtypeStruct((t, d), F32),
        scratch_shapes=[pltpu.VMEM((tm, LANES), F32), pltpu.VMEM((tm, d), F32), pltpu.VMEM((tm, d), BF16),
                        pltpu.VMEM((tm, d), F32)],
        compiler_params=_cparams(("arbitrary", "arbitrary")), name="mix_moe_ple_ln",
    )(x, a_out, b_out, woa, wob, gm, bm, p, w_router.T, br, wgu, wd, g, b,
      w_ple_gate.astype(BF16), w_ple.astype(BF16), g2, b2)


def kernel(x, p, ln_in_g, ln_in_b, w_in, w_out, mu_shift, w0, w_decay, a0, w_iclr, w_gate_up, k_k, k_a, r_k,
           gn_g, gn_b, rel_bias, w_router, b_router, w_exp_gate, w_exp_up, w_exp_down, w_ple, w_ple_gate,
           ln_mix_g, ln_mix_b, ln_ffn_g, ln_ffn_b, ln_ple_g, ln_ple_b):
    bsz, seq, d = x.shape
    t = bsz * seq
    depth = w_in.shape[0]
    xf = x.reshape(t, d)
    near_bias = _near_bias(rel_bias)
    for i in range(depth):
        w_rm, w_fm = _pack_w_in(w_in[i])
        outs = _inproj(xf, ln_in_g[None], ln_in_b[None], w_rm, w_fm, apply_ln=(i == 0))
        k_rm, kiki, hb, qt, qit, vt, wit = outs[:7]
        if i == 0:
            xf = outs[7]
        a_out = _dsa(qt, qit, wit, k_rm, vt, kiki, near_bias, bsz, seq)
        par, mu_wa, mu_gd, wdec, wic = _pack_rwkv_params(mu_shift[i], w0[i], w_decay[i], a0[i], w_iclr[i],
                                                         k_k[i], k_a[i], r_k[i], gn_g[i], gn_b[i])
        b_out = _rwkv(hb, par, mu_wa, mu_gd, wdec, wic, w_gate_up[i], bsz, seq)
        xf = _mix_moe_ple(xf, a_out, b_out, w_out[i], ln_mix_g[i][None], ln_mix_b[i][None],
                          p[i].reshape(t, PLE_DIM), w_router, b_router, w_exp_gate[i], w_exp_up[i], w_exp_down[i],
                          ln_ffn_g[i][None], ln_ffn_b[i][None], w_ple_gate[i], w_ple[i],
                          ln_ple_g[i][None], ln_ple_b[i][None])
    return xf.reshape(bsz, seq, d)
```

```python
import functools
import math

import numpy as np
import jax
import jax.numpy as jnp
from jax import lax
from jax.experimental import pallas as pl
from jax.experimental.pallas import tpu as pltpu

D_MODEL = 1024
DEPTH = 2
CHUNK = 64
A_HEAD_DIM = 64
A_WIDTH = D_MODEL // 2
A_HEADS = A_WIDTH // A_HEAD_DIM
IDX_HEADS = 8
IDX_DIM = 64
TOPK_MAX = 256
N_BUCKETS = 32
MAX_DISTANCE = 128
B_HEAD_DIM = 64
B_WIDTH = D_MODEL - A_WIDTH
B_HEADS = B_WIDTH // B_HEAD_DIM
DECAY_LORA = 64
ICLR_LORA = 64
GATE_LORA = 128
A_SIZES = (A_WIDTH, A_WIDTH, A_WIDTH, IDX_HEADS * IDX_DIM, IDX_DIM, IDX_HEADS)
B_SIZES = (B_WIDTH, B_WIDTH, B_WIDTH, DECAY_LORA, ICLR_LORA, GATE_LORA)
A_COLS = sum(A_SIZES)
B_COLS = sum(B_SIZES)
N_EXPERTS = 16
N_GROUPS = 4
EXPERTS_PER_GROUP = N_EXPERTS // N_GROUPS
D_EXPERT = 256
PLE_DIM = 256
ALPHA = (2 * DEPTH) ** 0.25
LN_EPS = 1e-5
GN_EPS = 64e-5
NEG = -1e30

LANES = 128
VMEM_LIMIT = 56 * 1024 * 1024
F32 = jnp.float32
BF16 = jnp.bfloat16
HI = lax.Precision.HIGHEST
NT_DIMS = (((1,), (1,)), ((), ()))


def _cparams(sem):
    return pltpu.CompilerParams(dimension_semantics=sem, vmem_limit_bytes=VMEM_LIMIT)


def _ln(x, g, b):
    mu = jnp.mean(x, axis=-1, keepdims=True)
    xc = x - mu
    var = jnp.mean(xc * xc, axis=-1, keepdims=True)
    return xc * lax.rsqrt(var + LN_EPS) * g + b


def _dot(a, b):
    return jnp.dot(a, b, preferred_element_type=F32)


def _dot_hi(a, b):
    return jnp.dot(a, b, preferred_element_type=F32, precision=HI)


def _dot_nt(a, b):
    return lax.dot_general(a, b, NT_DIMS, preferred_element_type=F32)


def _dot_nt_hi(a, b):
    return lax.dot_general(a, b, NT_DIMS, preferred_element_type=F32, precision=HI)


def _split2(x):
    hi = x.astype(BF16)
    return hi, (x - hi.astype(F32)).astype(BF16)


def _split3(x):
    hi = x.astype(BF16)
    rest = x - hi.astype(F32)
    mid = rest.astype(BF16)
    return hi, mid, (rest - mid.astype(F32)).astype(BF16)


IN_NCHUNK = 512


WI_ROWS = 16
LOG2E = math.log2(math.e)
Q_SCALE = A_HEAD_DIM ** -0.5 * LOG2E


def _inproj_kernel(x_ref, g_ref, b_ref, wr_ref, wf_ref, *out_refs, apply_ln):
    x = x_ref[...]
    if apply_ln:
        x = _ln(x, g_ref[...], b_ref[...])
        out_refs[7][...] = x
    xb = x.astype(BF16)
    k_ref, kk_ref, hb_ref, qt_ref, qit_ref, vt_ref, wit_ref = out_refs[:7]
    aw = A_WIDTH
    k_ref[...] = _dot(xb, wr_ref[:, :aw]).astype(BF16)
    kk_ref[...] = _dot(xb, wr_ref[:, aw:aw + LANES]).astype(BF16)
    nb = hb_ref.shape[1]
    for c0 in range(0, nb, IN_NCHUNK):
        c1 = min(c0 + IN_NCHUNK, nb)
        hb_ref[:, c0:c1] = _dot(xb, wr_ref[:, aw + LANES + c0:aw + LANES + c1])
    qt_ref[...] = (_dot_nt(wf_ref[:aw], xb) * Q_SCALE).astype(BF16)
    qit_ref[...] = _dot_nt(wf_ref[aw:2 * aw], xb).astype(BF16)
    vt = _dot_nt(wf_ref[2 * aw:3 * aw], xb).astype(BF16)
    for n in range(vt_ref.shape[0]):
        vt_ref[n] = vt[:, n * TQ:(n + 1) * TQ]
    wit_ref[...] = _dot_nt(wf_ref[3 * aw:], xb)


def _inproj(x, g, b, w_rm, w_fm, apply_ln, tm=512):
    t, d = x.shape
    aw = A_WIDTH
    nb = w_rm.shape[1] - aw - LANES
    row = lambda i: (i, 0)
    col = lambda i: (0, i)
    const = lambda i: (0, 0)
    out_shape = [jax.ShapeDtypeStruct((t, aw), BF16), jax.ShapeDtypeStruct((t, LANES), BF16),
                 jax.ShapeDtypeStruct((t, nb), F32), jax.ShapeDtypeStruct((aw, t), BF16),
                 jax.ShapeDtypeStruct((aw, t), BF16), jax.ShapeDtypeStruct((t // TQ, aw, TQ), BF16),
                 jax.ShapeDtypeStruct((WI_ROWS, t), F32)]
    out_specs = [pl.BlockSpec((tm, aw), row), pl.BlockSpec((tm, LANES), row), pl.BlockSpec((tm, nb), row),
                 pl.BlockSpec((aw, tm), col), pl.BlockSpec((aw, tm), col),
                 pl.BlockSpec((tm // TQ, aw, TQ), lambda i: (i, 0, 0)), pl.BlockSpec((WI_ROWS, tm), col)]
    if apply_ln:
        out_shape.append(jax.ShapeDtypeStruct((t, d), F32))
        out_specs.append(pl.BlockSpec((tm, d), row))
    return pl.pallas_call(
        functools.partial(_inproj_kernel, apply_ln=apply_ln),
        grid=(t // tm,),
        in_specs=[pl.BlockSpec((tm, d), row), pl.BlockSpec((1, d), const), pl.BlockSpec((1, d), const),
                  pl.BlockSpec(w_rm.shape, const), pl.BlockSpec(w_fm.shape, const)],
        out_specs=out_specs, out_shape=out_shape,
        compiler_params=_cparams(("arbitrary",)), name="inproj_ln" if apply_ln else "inproj",
    )(x, g, b, w_rm, w_fm)


def _pack_w_in(w):
    d = w.shape[0]
    aw = A_WIDTH
    cuts = np.cumsum(A_SIZES)
    wq, wk, wv, wqi = w[:, :aw], w[:, aw:2 * aw], w[:, 2 * aw:3 * aw], w[:, cuts[2]:cuts[3]]
    wki = w[:, cuts[3]:cuts[4]]
    wwi = w[:, cuts[4]:cuts[5]]
    w_rm = jnp.concatenate([wk, wki, wki, w[:, A_COLS:]], axis=1).astype(BF16)
    pad = jnp.zeros((d, WI_ROWS - IDX_HEADS), w.dtype)
    w_fm = jnp.concatenate([wq, wqi, wv, wwi, pad], axis=1).T.astype(BF16)
    return w_rm, w_fm


TQ = 256
ROW_GROUP = 32
BIGF = 3.0e38


def _t5_bucket_np(rel):
    nb = N_BUCKETS // 2
    max_exact = nb // 2
    ret = np.where(rel > 0, nb, 0)
    n = np.abs(rel)
    nf = np.maximum(n, 1).astype(np.float64)
    large = max_exact + np.floor(np.log(nf / max_exact) / math.log(MAX_DISTANCE / max_exact)
                                 * (nb - max_exact) + 1e-9).astype(np.int64)
    large = np.minimum(large, nb - 1)
    return ret + np.where(n < max_exact, n, large)


def _near_bias(rel_bias):
    qpos = np.arange(TQ)[None, :]
    kpos = np.arange(TQ)[:, None]
    bk = np.stack([_t5_bucket_np(kpos - qpos), _t5_bucket_np(kpos - TQ - qpos)])
    far = N_BUCKETS // 2 - 1
    tab = (rel_bias - rel_bias[far][None, :]) * LOG2E
    onehot = jnp.asarray(bk[..., None] == np.arange(N_BUCKETS)).astype(F32)
    return jnp.einsum('nkqb,bh->nhkq', onehot, tab.astype(F32), precision=HI)


BISECT_STEPS = 12
SUBLANES = 8
ROWSUM_MIN = 2.0 ** -60
ROWSUM_MAX = 2.0 ** 100
SNAP_STEPS = 3


def _paired_loop(n, body, carry):
    def pair(j, c):
        return body(2 * j + 1, 1, body(2 * j, 0, c))

    carry = lax.fori_loop(0, lax.shift_right_logical(n, 1), pair, carry)
    return lax.cond((n & 1) == 1, lambda c: body(n - 1, 0, c), lambda c: c, carry)


def _fold8(x, op):
    return op(x.reshape(x.shape[0] // SUBLANES, SUBLANES, x.shape[1]), axis=0)


def _dsa_kernel(qt_ref, qit_ref, wit_ref, k_ref, vt_ref, kk_ref, bias_ref, o_ref,
                s_ref, m_ref, l_ref, acc_ref, p_ref, *, ksel):
    i = pl.program_id(1)
    nt = i + 1
    tq = qt_ref.shape[1]
    kself = float(ksel)
    idx_scale = (IDX_HEADS ** -0.5) * (IDX_DIM ** -0.5)

    sub = lax.broadcasted_iota(jnp.int32, (LANES, tq), 0)
    lower = sub < (LANES // 2)
    zero_b = jnp.zeros((LANES, tq), BF16)

    def head_parts(ref):
        parts = []
        for h in range(A_HEADS):
            j, s = divmod(h, 2)
            blk = ref[j * LANES:(j + 1) * LANES, :]
            parts.append(jnp.where(lower if s == 0 else jnp.logical_not(lower), blk, zero_b))
        return parts

    def key_rows(kt):
        return pl.ds(pl.multiple_of(kt * tq, tq), tq)

    qi_parts = head_parts(qit_ref)
    wi = wit_ref[...] * idx_scale
    wrows = [wi[h:h + 1, :] for h in range(IDX_HEADS)]
    krow = lax.broadcasted_iota(jnp.int32, (tq, tq), 0)
    qcol = lax.broadcasted_iota(jnp.int32, (tq, tq), 1)
    adm_diag = (krow // CHUNK) <= (qcol // CHUNK)

    def score_tile(kt):
        kk = kk_ref[key_rows(kt), :]
        acc = jnp.zeros((tq, tq), F32)
        for h in range(IDX_HEADS):
            acc = acc + wrows[h] * jnp.maximum(_dot(kk, qi_parts[h]), 0.0)
        return acc

    def p1_body(kt, _, carry):
        sc = score_tile(kt)
        s_ref[kt] = sc
        return jnp.minimum(carry[0], _fold8(sc, jnp.min)), jnp.maximum(carry[1], _fold8(sc, jnp.max))

    mn8, mx8 = _paired_loop(i, p1_body, (jnp.full((SUBLANES, tq), BIGF, F32), jnp.full((SUBLANES, tq), -BIGF, F32)))
    sc = score_tile(i)
    s_ref[i] = jnp.where(adm_diag, sc, NEG)
    mn8 = jnp.minimum(mn8, _fold8(jnp.where(adm_diag, sc, BIGF), jnp.min))
    mx8 = jnp.maximum(mx8, _fold8(jnp.where(adm_diag, sc, -BIGF), jnp.max))
    rowmin = jnp.min(mn8, axis=0, keepdims=True)
    rowmax = jnp.max(mx8, axis=0, keepdims=True)

    def reduce_tiles(fn, inits, ops):
        def body(kt, _, carry):
            return fn(s_ref[kt], kt, carry)

        carry = _paired_loop(nt, body, tuple(jnp.full((SUBLANES, tq), v, F32) for v in inits))
        return [op(c, axis=0, keepdims=True) for op, c in zip(ops, carry)]

    qpos = i * tq + lax.broadcasted_iota(jnp.int32, (1, tq), 1)
    n_adm = (qpos // CHUNK + 1) * CHUNK
    trivial = n_adm <= ksel
    lo0 = jnp.where(trivial, 0.1 * NEG, rowmin)
    hi0 = jnp.where(trivial, 0.1 * NEG, rowmax)

    def count_ge(mid):
        def cnt(s, kt, carry):
            return (carry[0] + _fold8(jnp.where(s >= mid, 1.0, 0.0), jnp.sum),)

        return reduce_tiles(cnt, (0.0,), (jnp.sum,))[0]

    def a_body(_, st):
        lo, hi, clo = st
        mid = 0.5 * (lo + hi)
        c = count_ge(mid)
        ok = c >= kself
        return jnp.where(ok, mid, lo), jnp.where(ok, hi, mid), jnp.where(ok, c, clo)

    lo1, hi1, clo1 = lax.fori_loop(0, BISECT_STEPS, a_body, (lo0, hi0, n_adm.astype(F32)))
    hi1 = jnp.where(jnp.logical_or(clo1 == kself, trivial), lo1, hi1)

    def b_cond(st):
        lo, hi = st
        return jnp.max(jnp.where(lo < hi, 1.0, 0.0)) > 0.0

    def b_body(st):
        lo, hi = st
        mid = 0.5 * (lo + hi)
        mid = jnp.where(mid > lo, mid, hi)

        def cnt(s, kt, carry):
            c, vge, vlt = carry
            ge = s >= mid
            return (c + _fold8(jnp.where(ge, 1.0, 0.0), jnp.sum),
                    jnp.minimum(vge, _fold8(jnp.where(ge, s, BIGF), jnp.min)),
                    jnp.maximum(vlt, _fold8(jnp.where(ge, -BIGF, s), jnp.max)))

        c, vge, vlt = reduce_tiles(cnt, (0.0, BIGF, -BIGF), (jnp.sum, jnp.min, jnp.max))
        ok = c >= kself
        move_lo = jnp.logical_and(ok, jnp.logical_not(trivial))
        move_hi = jnp.logical_and(jnp.logical_not(ok), jnp.logical_not(trivial))
        return jnp.where(move_lo, vge, lo), jnp.where(move_hi, vlt, hi)

    snapped = lax.fori_loop(0, SNAP_STEPS, lambda _, st: b_body(st), (lo1, hi1))
    thr, _ = lax.while_loop(b_cond, b_body, snapped)

    def count_ties(s, kt, carry):
        return (carry[0] + _fold8(jnp.where(s >= thr, 1.0, 0.0), jnp.sum),
                carry[1] + _fold8(jnp.where(s > thr, 1.0, 0.0), jnp.sum))

    c_ge, c_gt = reduce_tiles(count_ties, (0.0, 0.0), (jnp.sum, jnp.sum))
    need = kself - c_gt
    tied = c_ge > kself
    s_total = float(s_ref.shape[0] * tq)
    jlo0 = jnp.where(tied, 0.0, s_total - 1.0)
    jhi0 = jnp.where(tied, (nt * tq).astype(F32), s_total)
    kidx = krow.astype(F32)

    def tie_cond(st):
        jlo, jhi = st
        return jnp.max(jhi - jlo) > 1.0

    def tie_body(st):
        jlo, jhi = st
        mid = jnp.floor(0.5 * (jlo + jhi))

        def cnt(s, kt, carry):
            idx = kidx + (kt * tq).astype(F32)
            return (carry[0] + _fold8(jnp.where(jnp.logical_and(s == thr, idx < mid), 1.0, 0.0), jnp.sum),)

        (c,) = reduce_tiles(cnt, (0.0,), (jnp.sum,))
        ok = c >= need
        done = (jhi - jlo) <= 1.0
        return (jnp.where(jnp.logical_or(ok, done), jlo, mid),
                jnp.where(jnp.logical_and(ok, jnp.logical_not(done)), mid, jhi))

    _, jcut = lax.while_loop(tie_cond, tie_body, (jlo0, jhi0))

    q_parts = head_parts(qt_ref)

    def reset(m_init):
        l_ref[...] = jnp.zeros(l_ref.shape, F32)
        acc_ref[...] = jnp.zeros(acc_ref.shape, F32)
        if m_init is not None:
            m_ref[...] = jnp.full(m_ref.shape, m_init, F32)

    def logits(kt, near, mb):
        for j in range(A_HEADS // 2):
            kp = k_ref[key_rows(kt), j * LANES:(j + 1) * LANES]
            for s_half in range(2):
                h = 2 * j + s_half
                lg = _dot(kp, q_parts[h]) + mb
                if near is not None:
                    lg = lg + bias_ref[near, h]
                yield h, lg

    def accumulate(kt, slot, shifted):
        for h, d in shifted:
            p = jnp.exp2(d)
            l_ref[h] += _fold8(p, jnp.sum)
            p_ref[slot, h] = p.astype(BF16)
        for h in range(A_HEADS):
            acc_ref[h] += _dot(vt_ref[kt, h * A_HEAD_DIM:(h + 1) * A_HEAD_DIM, :], p_ref[slot, h])

    def sweep_single(kt, near, slot, neg_shift):
        s = s_ref[kt]
        idx = kidx + (kt * tq).astype(F32)
        sel = jnp.logical_or(s > thr, jnp.logical_and(s == thr, idx < jcut))
        s_ref[kt] = jnp.where(sel, 0.0, NEG)
        accumulate(kt, slot, logits(kt, near, jnp.where(sel, neg_shift, NEG)))

    def sweep_max(kt, near, slot):
        for h, lg in logits(kt, near, s_ref[kt]):
            m_ref[h] = jnp.maximum(m_ref[h], _fold8(lg, jnp.max))

    def sweep_sum(kt, near, slot):
        accumulate(kt, slot, ((h, lg - m_ref[h][:1]) for h, lg in logits(kt, near, s_ref[kt])))

    def all_tiles(sweep):
        def body(kt, slot, c):
            sweep(kt, None, slot)
            return c

        _paired_loop(jnp.maximum(i - 1, 0), body, 0)

        @pl.when(i >= 1)
        def _():
            sweep(i - 1, 1, 1)

        sweep(i, 0, 0)

    def settle_rows(ref):
        for h in range(A_HEADS):
            ref[h] = jnp.broadcast_to(jnp.max(ref[h], axis=0, keepdims=True), (SUBLANES, tq))

    reset(None)
    diag_max = jnp.full((SUBLANES, tq), NEG, F32)
    for j in range(A_HEADS // 2):
        kp = k_ref[key_rows(i), j * LANES:(j + 1) * LANES]
        for h in (2 * j, 2 * j + 1):
            diag_max = jnp.maximum(diag_max, _fold8(_dot(kp, q_parts[h]) + bias_ref[0, h], jnp.max))
    neg_shift = -jnp.max(diag_max, axis=0, keepdims=True)
    all_tiles(functools.partial(sweep_single, neg_shift=neg_shift))

    def out_of_range(h):
        tot = jnp.sum(l_ref[h], axis=0, keepdims=True)
        ok = jnp.logical_and(tot >= ROWSUM_MIN, tot <= ROWSUM_MAX)
        return jnp.max(jnp.where(ok, 0.0, 1.0))

    @pl.when(functools.reduce(jnp.maximum, [out_of_range(h) for h in range(A_HEADS)]) > 0.0)
    def _():
        reset(NEG)
        all_tiles(sweep_max)
        settle_rows(m_ref)
        all_tiles(sweep_sum)

    for j in range(A_HEADS // 2):
        outs = [acc_ref[h] / jnp.sum(l_ref[h], axis=0, keepdims=True) for h in (2 * j, 2 * j + 1)]
        o_ref[:, j * LANES:(j + 1) * LANES] = jnp.concatenate(outs, axis=0).T.astype(o_ref.dtype)


def _dsa(qt, qit, wit, k, vt, kiki, near_bias, bsz, seq):
    nq = seq // TQ
    ksel = min(TOPK_MAX, seq // 4)
    qcol = lambda b, i: (0, b * nq + i)
    return pl.pallas_call(
        functools.partial(_dsa_kernel, ksel=ksel),
        grid=(bsz, nq),
        in_specs=[
            pl.BlockSpec((A_WIDTH, TQ), qcol),
            pl.BlockSpec((A_WIDTH, TQ), qcol),
            pl.BlockSpec((WI_ROWS, TQ), qcol),
            pl.BlockSpec((seq, A_WIDTH), lambda b, i: (b, 0)),
            pl.BlockSpec((nq, A_WIDTH, TQ), lambda b, i: (b, 0, 0)),
            pl.BlockSpec((seq, LANES), lambda b, i: (b, 0)),
            pl.BlockSpec((2, A_HEADS, TQ, TQ), lambda b, i: (0, 0, 0, 0)),
        ],
        out_specs=pl.BlockSpec((TQ, A_WIDTH), lambda b, i: (b * nq + i, 0)),
        out_shape=jax.ShapeDtypeStruct((bsz * seq, A_WIDTH), BF16),
        scratch_shapes=[pltpu.VMEM((nq, TQ, TQ), F32), pltpu.VMEM((A_HEADS, SUBLANES, TQ), F32),
                        pltpu.VMEM((A_HEADS, SUBLANES, TQ), F32), pltpu.VMEM((A_HEADS, A_HEAD_DIM, TQ), F32),
                        pltpu.VMEM((2, A_HEADS, TQ, TQ), BF16)],
        compiler_params=_cparams(("arbitrary", "arbitrary")), name="dsa_attention",
    )(qt, qit, wit, k, vt, kiki, near_bias)


RC = 64
RTS = 256


RNP = 4


def _rwkv_kernel(r_ref, k_ref, v_ref, wa_ref, gd_ref, rp_ref, kp_ref, vp_ref, wap_ref, gdp_ref,
                 par_ref, muw_ref, mug_ref, wdec_ref, wic_ref, wg_ref, o_ref, z_ref):
    t = pl.program_id(2)
    ts = r_ref.shape[0]

    @pl.when(t == 0)
    def _():
        z_ref[...] = jnp.zeros(z_ref.shape, F32)

    first = jnp.where(t == 0, 0.0, 1.0)

    def shifted(ref, pref, mu):
        x = ref[...]
        row = lax.broadcasted_iota(jnp.int32, x.shape, 0)
        prev = jnp.where(row == 0, pref[7:8, :] * first, pltpu.roll(x, 1, 0))
        return x + (prev - x) * mu

    par = par_ref[...]
    r_all = shifted(r_ref, rp_ref, par[8:9])
    k_all = shifted(k_ref, kp_ref, par[9:10])
    v_all = shifted(v_ref, vp_ref, par[10:11])
    wa = shifted(wa_ref, wap_ref, muw_ref[...])
    gd = shifted(gd_ref, gdp_ref, mug_ref[...])

    lane = lax.broadcasted_iota(jnp.int32, (LANES, LANES), 1)
    rowl = lax.broadcasted_iota(jnp.int32, (LANES, LANES), 0)
    same_head = (lane // B_HEAD_DIM) == (rowl // B_HEAD_DIM)
    gones = jnp.where(same_head, 1.0, 0.0)
    gones16 = gones.astype(BF16)
    strict = jnp.logical_and(same_head, rowl > lane)
    incl = jnp.logical_and(same_head, rowl >= lane)
    eye = jnp.where(lane == rowl, 1.0, 0.0)
    head0 = lax.broadcasted_iota(jnp.int32, (RC, LANES), 1) < B_HEAD_DIM

    def head_sum(x):
        hi, lo = _split2(x)
        return _dot(hi, gones16) + _dot(lo, gones16)

    def stack_heads(x):
        return jnp.concatenate([jnp.where(head0, x, 0.0), jnp.where(head0, 0.0, x)], axis=0)

    def fold_heads(x):
        return x[:RC] + x[RC:]

    def twice(x):
        return jnp.concatenate([x, x], axis=0)

    rt = lax.broadcasted_iota(jnp.int32, (ts, ts), 0)
    ct = lax.broadcasted_iota(jnp.int32, (ts, ts), 1)
    tri16 = jnp.where(jnp.logical_and((rt // RC) == (ct // RC), rt >= ct), 1.0, 0.0).astype(BF16)
    th_hi, th_lo = _split2(jnp.tanh(wa))
    wa16 = wa.astype(BF16)
    sg16 = jax.nn.sigmoid(gd).astype(BF16)
    nchunk = ts // RC
    npair = r_ref.shape[1] // LANES

    pairs = []
    for q in range(npair):
        ls = slice(q * LANES, (q + 1) * LANES)
        w0, a0, k_k, k_a, r_k, gn_g, gn_b = (par[n:n + 1, ls] for n in range(7))
        r, k, v = r_all[:, ls], k_all[:, ls], v_all[:, ls]
        wd_hi, wd_lo = _split2(wdec_ref[:, ls])
        z = w0 + (_dot(th_hi, wd_hi) + _dot(th_hi, wd_lo) + _dot(th_lo, wd_hi))
        w_log = -(jnp.maximum(-z, 0.0) + jnp.log1p(jnp.exp(-jnp.abs(z)))) - 0.5
        e = jnp.exp(w_log)
        a = jax.nn.sigmoid(a0 + _dot(wa16, wic_ref[:, ls].astype(BF16)))
        g = _dot(sg16, wg_ref[:, ls].astype(BF16))
        kkr = k * k_k
        kk = kkr / jnp.maximum(jnp.sqrt(head_sum(kkr * kkr)), 1e-12)
        k2 = k * (1.0 + (a - 1.0) * k_a)
        bonus = head_sum(r * k2 * r_k) * v
        kka = kk * a
        e_hi, e_mid, e_lo = _split3(e)
        cum = _dot(tri16, e_hi) + _dot(tri16, e_mid) + _dot(tri16, e_lo)
        pairs.append(dict(r_hat=r * jnp.exp(-cum), a_hat=-kk * jnp.exp(e - cum), cum=cum, kka=kka, k2=k2, v=v,
                          b_til=(kka * jnp.exp(cum)).astype(BF16), k_til=(k2 * jnp.exp(cum)).astype(BF16),
                          v16=v.astype(BF16), bonus=bonus, g=g, gn_g=gn_g, gn_b=gn_b))

    items = [(q, c) for q in range(npair) for c in range(nchunk)]
    n = range(len(items))
    sl = [slice(c * RC, (c + 1) * RC) for _, c in items]
    pq = [pairs[q] for q, _ in items]
    tot = [pq[i]["cum"][sl[i].stop - 1:sl[i].stop] for i in n]
    rem = [jnp.exp(pq[i]["cum"][sl[i]] - tot[i]) for i in n]
    a_s = [stack_heads(pq[i]["a_hat"][sl[i]]).astype(BF16) for i in n]
    r_s = [stack_heads(pq[i]["r_hat"][sl[i]]).astype(BF16) for i in n]
    v_s = [stack_heads(pq[i]["v"][sl[i]]).astype(BF16) for i in n]
    ar = [jnp.concatenate([a_s[i], r_s[i]], axis=0) for i in n]
    bk = [jnp.concatenate([twice(pq[i]["b_til"][sl[i]]), twice(pq[i]["k_til"][sl[i]])], axis=0) for i in n]
    m_all = [_dot_nt(ar[i], bk[i]) for i in n]
    l_ab = [jnp.where(strict, m_all[i][:LANES, :LANES], 0.0) for i in n]
    l_ak = [jnp.where(strict, m_all[i][:LANES, LANES:], 0.0).astype(BF16) for i in n]
    m_rb = [jnp.where(incl, m_all[i][LANES:, :LANES], 0.0).astype(BF16) for i in n]
    m_rk = [jnp.where(incl, m_all[i][LANES:, LANES:], 0.0).astype(BF16) for i in n]
    inv = [eye + l_ab[i] for i in n]
    pw = [l_ab[i].astype(BF16) for i in n]
    for _ in range(5):
        pw = [_dot(pw[i], pw[i]).astype(BF16) for i in n]
        inv = [inv[i] + _dot(inv[i].astype(BF16), pw[i]) for i in n]
    lv = [_dot(l_ak[i], v_s[i]).astype(BF16) for i in n]
    x_av = [_dot(inv[i].astype(BF16), jnp.concatenate([a_s[i], lv[i]], axis=1)) for i in n]
    y_av = [_dot(m_rb[i], x_av[i].astype(BF16)) for i in n]
    y_kv = [_dot(m_rk[i], v_s[i]) for i in n]
    r_p = [pq[i]["r_hat"][sl[i]] + fold_heads(y_av[i][:, :LANES]) for i in n]
    y0 = [fold_heads(y_av[i][:, LANES:] + y_kv[i]) for i in n]
    gh = [_dot((pq[i]["kka"][sl[i]] * rem[i]).T.astype(BF16), fold_heads(x_av[i]).astype(BF16)) for i in n]
    kv = [_dot((pq[i]["k2"][sl[i]] * rem[i]).T.astype(BF16), pq[i]["v16"][sl[i]]) for i in n]
    g_mat = [eye * jnp.exp(-tot[i]) + gones * gh[i][:, :LANES] for i in n]
    h_mat = [gones * (gh[i][:, LANES:] + kv[i]) for i in n]
    rg = [jnp.concatenate([r_p[i], g_mat[i]], axis=0).astype(BF16) for i in n]

    zc = [z_ref[q] for q in range(npair)]
    ys = [[] for _ in range(npair)]
    for c in range(nchunk):
        for q in range(npair):
            i = q * nchunk + c
            yz = _dot(rg[i], zc[q].astype(BF16))
            ys[q].append(yz[:RC] + y0[i])
            zc[q] = yz[RC:] + h_mat[i]
    for q in range(npair):
        z_ref[q] = zc[q]
        p = pairs[q]
        y = jnp.concatenate(ys[q], axis=0)
        mean = head_sum(y) * (1.0 / B_HEAD_DIM)
        yc = y - mean
        var = head_sum(yc * yc) * (1.0 / B_HEAD_DIM)
        yn = yc * lax.rsqrt(var + GN_EPS) * p["gn_g"] + p["gn_b"]
        o_ref[:, q * LANES:(q + 1) * LANES] = ((yn + p["bonus"]) * p["g"]).astype(o_ref.dtype)


def _rwkv(hb, par, mu_wa, mu_gd, wdec, wic, wg, bsz, seq):
    nt = seq // RTS
    width = RNP * LANES
    ngroup = B_WIDTH // width
    cb = B_WIDTH // width
    small = (3 * B_WIDTH) // LANES

    def main(shape_w, colblk):
        return pl.BlockSpec((RTS, shape_w), lambda b, p, t: (b * nt + t, colblk(p)))

    def prev(shape_w, colblk):
        return pl.BlockSpec((8, shape_w), lambda b, p, t: (jnp.maximum((b * nt + t) * (RTS // 8) - 1, 0), colblk(p)))

    wide = [lambda p: p, lambda p: cb + p, lambda p: 2 * cb + p]
    narrow = [lambda p: small, lambda p: small + 1]
    in_specs = ([main(width, c) for c in wide] + [main(LANES, c) for c in narrow]
                + [prev(width, c) for c in wide] + [prev(LANES, c) for c in narrow] + [
        pl.BlockSpec((16, width), lambda b, p, t: (0, p)),
        pl.BlockSpec((1, LANES), lambda b, p, t: (0, 0)),
        pl.BlockSpec((1, LANES), lambda b, p, t: (0, 0)),
        pl.BlockSpec((LANES, width), lambda b, p, t: (0, p)),
        pl.BlockSpec((LANES, width), lambda b, p, t: (0, p)),
        pl.BlockSpec((LANES, width), lambda b, p, t: (0, p)),
    ])
    return pl.pallas_call(
        _rwkv_kernel,
        grid=(bsz, ngroup, nt),
        in_specs=in_specs,
        out_specs=pl.BlockSpec((RTS, width), lambda b, p, t: (b * nt + t, p)),
        out_shape=jax.ShapeDtypeStruct((bsz * seq, B_WIDTH), BF16),
        scratch_shapes=[pltpu.VMEM((RNP, LANES, LANES), F32)],
        compiler_params=_cparams(("arbitrary", "arbitrary", "arbitrary")), name="rwkv7",
    )(*([hb] * 10), par, mu_wa, mu_gd, wdec, wic, wg)


def _pack_rwkv_params(mu, w0, w_decay, a0, w_iclr, k_k, k_a, r_k, gn_g, gn_b):
    bw = B_WIDTH
    zero = jnp.zeros((bw,), F32)
    rows = [w0, a0, k_k, k_a, r_k.reshape(bw), gn_g, gn_b, zero,
            mu[:bw], mu[bw:2 * bw], mu[2 * bw:3 * bw], zero, zero, zero, zero, zero]
    par = jnp.stack(rows)
    mu_wa = mu[3 * bw:3 * bw + LANES][None]
    mu_gd = mu[3 * bw + LANES:][None]
    zpad = jnp.zeros((DECAY_LORA, bw), F32)
    wdec = jnp.concatenate([w_decay, zpad], axis=0)
    wic = jnp.concatenate([zpad, w_iclr], axis=0)
    return par, mu_wa, mu_gd, wdec, wic


EXPERTS_PER_STEP = 2


def _xor_partner_rows(x, row, bit):
    up = pltpu.roll(x, bit, 0)
    down = pltpu.roll(x, x.shape[0] - bit, 0)
    return jnp.where((row & bit) != 0, up, down)


def _moe_gates_t(x_hi, x_lo, wrt, br):
    w_hi, w_lo = _split2(wrt)
    logits = _dot_nt(w_hi, x_hi) + _dot_nt(w_lo, x_hi) + _dot_nt(w_hi, x_lo)
    row = lax.broadcasted_iota(jnp.int32, logits.shape, 0)
    rowf = row.astype(F32)
    big = float(N_EXPERTS)
    scores = jax.nn.sigmoid(logits)
    sel = scores + br
    p1 = _xor_partner_rows(sel, row, 1)
    hi1, lo1 = jnp.maximum(sel, p1), jnp.minimum(sel, p1)
    hi2, lo2 = _xor_partner_rows(hi1, row, 2), _xor_partner_rows(lo1, row, 2)
    gscore = jnp.maximum(hi1, hi2) + jnp.maximum(jnp.minimum(hi1, hi2), jnp.maximum(lo1, lo2))
    gbest = jnp.max(gscore, axis=0, keepdims=True)
    first = jnp.min(jnp.where(gscore == gbest, rowf, big), axis=0, keepdims=True)
    in_group = jnp.floor(rowf * (1.0 / EXPERTS_PER_GROUP)) == jnp.floor(first * (1.0 / EXPERTS_PER_GROUP))
    masked = jnp.where(in_group, sel, NEG)
    m1 = jnp.max(masked, axis=0, keepdims=True)
    i1 = jnp.min(jnp.where(masked == m1, rowf, big), axis=0, keepdims=True)
    pick1 = rowf == i1
    masked2 = jnp.where(pick1, NEG, masked)
    m2 = jnp.max(masked2, axis=0, keepdims=True)
    i2 = jnp.min(jnp.where(masked2 == m2, rowf, big), axis=0, keepdims=True)
    pick2 = rowf == i2
    s1 = jnp.sum(jnp.where(pick1, scores, 0.0), axis=0, keepdims=True)
    s2 = jnp.sum(jnp.where(pick2, scores, 0.0), axis=0, keepdims=True)
    tot = s1 + s2
    return jnp.where(pick1, s1 / tot, 0.0) + jnp.where(pick2, s2 / tot, 0.0)


def _mix_moe_ple_kernel(x_ref, a_ref, bo_ref, woa_ref, wob_ref, gm_ref, bm_ref, p_ref, wrt_ref, br_ref, wgu_ref, wd_ref,
                        g_ref, b_ref, wpg_ref, wpp_ref, g2_ref, b2_ref, o_ref, gate_ref, acc_ref, xb_ref, x1_ref):
    e = pl.program_id(1)

    @pl.when(e == 0)
    def _():
        mix = _dot(a_ref[...], woa_ref[...]) + _dot(bo_ref[...], wob_ref[...])
        x1 = _ln(ALPHA * x_ref[...] + mix, gm_ref[...], bm_ref[...])
        x1_ref[...] = x1
        x_hi, x_lo = _split2(x1)
        gates_t = _moe_gates_t(x_hi, x_lo, wrt_ref[...], br_ref[:, :1])
        pad = jnp.zeros((LANES - gates_t.shape[0], gates_t.shape[1]), F32)
        gate_ref[...] = jnp.concatenate([gates_t, pad], axis=0).T
        xb_ref[...] = x_hi
        acc_ref[...] = jnp.zeros(acc_ref.shape, F32)

    gates = gate_ref[...]
    lane = lax.broadcasted_iota(jnp.int32, gates.shape, 1)
    xb = xb_ref[...]
    hidden = []
    for n in range(EXPERTS_PER_STEP):
        gcol = jnp.sum(jnp.where(lane == e * EXPERTS_PER_STEP + n, gates, 0.0), axis=1, keepdims=True)
        hu = _dot(xb, wgu_ref[n])
        hg, up = hu[:, :D_EXPERT], hu[:, D_EXPERT:]
        hidden.append(((hg * jax.nn.sigmoid(hg)) * up * gcol).astype(BF16))
    wd = wd_ref[...]
    acc_ref[...] += _dot(jnp.concatenate(hidden, axis=1), wd.reshape(wd.shape[0] * wd.shape[1], wd.shape[2]))

    @pl.when(e == pl.num_programs(1) - 1)
    def _():
        x2 = _ln(ALPHA * x1_ref[...] + acc_ref[...], g_ref[...], b_ref[...])
        gate = jax.nn.sigmoid(_dot(x2.astype(BF16), wpg_ref[...]))
        pe = _dot(p_ref[...].astype(BF16), wpp_ref[...])
        o_ref[...] = _ln(ALPHA * x2 + gate * pe, g2_ref[...], b2_ref[...])


def _mix_moe_ple(x, a_out, b_out, w_out, gm, bm, p, w_router, b_router, w_gate, w_up, w_down, g, b,
                 w_ple_gate, w_ple, g2, b2, tm=1024):
    t, d = x.shape
    ne = w_gate.shape[0]
    woa = w_out[:A_WIDTH].astype(BF16)
    wob = w_out[A_WIDTH:].astype(BF16)
    wgu = jnp.concatenate([w_gate, w_up], axis=2).astype(BF16)
    wd = w_down.astype(BF16)
    br = jnp.broadcast_to(b_router[:, None], (ne, LANES))
    row = lambda i, e: (i, 0)
    const = lambda i, e: (0, 0)
    vec = pl.BlockSpec((1, d), const)
    return pl.pallas_call(
        _mix_moe_ple_kernel, grid=(t // tm, ne // EXPERTS_PER_STEP),
        in_specs=[pl.BlockSpec((tm, d), row), pl.BlockSpec((tm, A_WIDTH), row), pl.BlockSpec((tm, B_WIDTH), row),
                  pl.BlockSpec((A_WIDTH, d), const), pl.BlockSpec((B_WIDTH, d), const), vec, vec,
                  pl.BlockSpec((tm, PLE_DIM), row),
                  pl.BlockSpec((ne, d), const), pl.BlockSpec((ne, LANES), const),
                  pl.BlockSpec((EXPERTS_PER_STEP, d, 2 * D_EXPERT), lambda i, e: (e, 0, 0)),
                  pl.BlockSpec((EXPERTS_PER_STEP, D_EXPERT, d), lambda i, e: (e, 0, 0)),
                  vec, vec, pl.BlockSpec((d, d), const), pl.BlockSpec((PLE_DIM, d), const), vec, vec],
        out_specs=pl.BlockSpec((tm, d), row), out_shape=jax.ShapeDtypeStruct((t, d), F32),
        scratch_shapes=[pltpu.VMEM((tm, LANES), F32), pltpu.VMEM((tm, d), F32), pltpu.VMEM((tm, d), BF16),
                        pltpu.VMEM((tm, d), F32)],
        compiler_params=_cparams(("arbitrary", "arbitrary")), name="mix_moe_ple_ln",
    )(x, a_out, b_out, woa, wob, gm, bm, p, w_router.T, br, wgu, wd, g, b,
      w_ple_gate.astype(BF16), w_ple.astype(BF16), g2, b2)


def kernel(x, p, ln_in_g, ln_in_b, w_in, w_out, mu_shift, w0, w_decay, a0, w_iclr, w_gate_up, k_k, k_a, r_k,
           gn_g, gn_b, rel_bias, w_router, b_router, w_exp_gate, w_exp_up, w_exp_down, w_ple, w_ple_gate,
           ln_mix_g, ln_mix_b, ln_ffn_g, ln_ffn_b, ln_ple_g, ln_ple_b):
    bsz, seq, d = x.shape
    t = bsz * seq
    depth = w_in.shape[0]
    xf = x.reshape(t, d)
    near_bias = _near_bias(rel_bias)
    for i in range(depth):
        w_rm, w_fm = _pack_w_in(w_in[i])
        outs = _inproj(xf, ln_in_g[None], ln_in_b[None], w_rm, w_fm, apply_ln=(i == 0))
        k_rm, kiki, hb, qt, qit, vt, wit = outs[:7]
        if i == 0:
            xf = outs[7]
        a_out = _dsa(qt, qit, wit, k_rm, vt, kiki, near_bias, bsz, seq)
        par, mu_wa, mu_gd, wdec, wic = _pack_rwkv_params(mu_shift[i], w0[i], w_decay[i], a0[i], w_iclr[i],
                                                         k_k[i], k_a[i], r_k[i], gn_g[i], gn_b[i])
        b_out = _rwkv(hb, par, mu_wa, mu_gd, wdec, wic, w_gate_up[i], bsz, seq)
        xf = _mix_moe_ple(xf, a_out, b_out, w_out[i], ln_mix_g[i][None], ln_mix_b[i][None],
                          p[i].reshape(t, PLE_DIM), w_router, b_router, w_exp_gate[i], w_exp_up[i], w_exp_down[i],
                          ln_ffn_g[i][None], ln_ffn_b[i][None], w_ple_gate[i], w_ple[i],
                          ln_ple_g[i][None], ln_ple_b[i][None])
    return xf.reshape(bsz, seq, d)
```

```python
import functools
import math

import numpy as np
import jax
import jax.numpy as jnp
from jax import lax
from jax.experimental import pallas as pl
from jax.experimental.pallas import tpu as pltpu

D_MODEL = 1024
DEPTH = 2
CHUNK = 64
A_HEAD_DIM = 64
A_WIDTH = D_MODEL // 2
A_HEADS = A_WIDTH // A_HEAD_DIM
IDX_HEADS = 8
IDX_DIM = 64
TOPK_MAX = 256
N_BUCKETS = 32
MAX_DISTANCE = 128
B_HEAD_DIM = 64
B_WIDTH = D_MODEL - A_WIDTH
B_HEADS = B_WIDTH // B_HEAD_DIM
DECAY_LORA = 64
ICLR_LORA = 64
GATE_LORA = 128
A_SIZES = (A_WIDTH, A_WIDTH, A_WIDTH, IDX_HEADS * IDX_DIM, IDX_DIM, IDX_HEADS)
B_SIZES = (B_WIDTH, B_WIDTH, B_WIDTH, DECAY_LORA, ICLR_LORA, GATE_LORA)
A_COLS = sum(A_SIZES)
B_COLS = sum(B_SIZES)
N_EXPERTS = 16
N_GROUPS = 4
EXPERTS_PER_GROUP = N_EXPERTS // N_GROUPS
D_EXPERT = 256
PLE_DIM = 256
ALPHA = (2 * DEPTH) ** 0.25
LN_EPS = 1e-5
GN_EPS = 64e-5
NEG = -1e30

LANES = 128
VMEM_LIMIT = 56 * 1024 * 1024
F32 = jnp.float32
BF16 = jnp.bfloat16
HI = lax.Precision.HIGHEST
NT_DIMS = (((1,), (1,)), ((), ()))


def _cparams(sem):
    return pltpu.CompilerParams(dimension_semantics=sem, vmem_limit_bytes=VMEM_LIMIT)


def _ln(x, g, b):
    mu = jnp.mean(x, axis=-1, keepdims=True)
    xc = x - mu
    var = jnp.mean(xc * xc, axis=-1, keepdims=True)
    return xc * lax.rsqrt(var + LN_EPS) * g + b


def _dot(a, b):
    return jnp.dot(a, b, preferred_element_type=F32)


def _dot_hi(a, b):
    return jnp.dot(a, b, preferred_element_type=F32, precision=HI)


def _dot_nt(a, b):
    return lax.dot_general(a, b, NT_DIMS, preferred_element_type=F32)


def _dot_nt_hi(a, b):
    return lax.dot_general(a, b, NT_DIMS, preferred_element_type=F32, precision=HI)


def _split2(x):
    hi = x.astype(BF16)
    return hi, (x - hi.astype(F32)).astype(BF16)


def _split3(x):
    hi = x.astype(BF16)
    rest = x - hi.astype(F32)
    mid = rest.astype(BF16)
    return hi, mid, (rest - mid.astype(F32)).astype(BF16)


IN_NCHUNK = 512


WI_ROWS = 16
LOG2E = math.log2(math.e)
Q_SCALE = A_HEAD_DIM ** -0.5 * LOG2E


def _inproj_kernel(x_ref, g_ref, b_ref, wr_ref, wf_ref, *out_refs, apply_ln):
    x = x_ref[...]
    if apply_ln:
        x = _ln(x, g_ref[...], b_ref[...])
        out_refs[7][...] = x
    xb = x.astype(BF16)
    k_ref, kk_ref, hb_ref, qt_ref, qit_ref, vt_ref, wit_ref = out_refs[:7]
    aw = A_WIDTH
    k_ref[...] = _dot(xb, wr_ref[:, :aw]).astype(BF16)
    kk_ref[...] = _dot(xb, wr_ref[:, aw:aw + LANES]).astype(BF16)
    nb = hb_ref.shape[1]
    for c0 in range(0, nb, IN_NCHUNK):
        c1 = min(c0 + IN_NCHUNK, nb)
        hb_ref[:, c0:c1] = _dot(xb, wr_ref[:, aw + LANES + c0:aw + LANES + c1])
    qt_ref[...] = (_dot_nt(wf_ref[:aw], xb) * Q_SCALE).astype(BF16)
    qit_ref[...] = _dot_nt(wf_ref[aw:2 * aw], xb).astype(BF16)
    vt = _dot_nt(wf_ref[2 * aw:3 * aw], xb).astype(BF16)
    for n in range(vt_ref.shape[0]):
        vt_ref[n] = vt[:, n * TQ:(n + 1) * TQ]
    wit_ref[...] = _dot_nt(wf_ref[3 * aw:], xb)


def _inproj(x, g, b, w_rm, w_fm, apply_ln, tm=512):
    t, d = x.shape
    aw = A_WIDTH
    nb = w_rm.shape[1] - aw - LANES
    row = lambda i: (i, 0)
    col = lambda i: (0, i)
    const = lambda i: (0, 0)
    out_shape = [jax.ShapeDtypeStruct((t, aw), BF16), jax.ShapeDtypeStruct((t, LANES), BF16),
                 jax.ShapeDtypeStruct((t, nb), F32), jax.ShapeDtypeStruct((aw, t), BF16),
                 jax.ShapeDtypeStruct((aw, t), BF16), jax.ShapeDtypeStruct((t // TQ, aw, TQ), BF16),
                 jax.ShapeDtypeStruct((WI_ROWS, t), F32)]
    out_specs = [pl.BlockSpec((tm, aw), row), pl.BlockSpec((tm, LANES), row), pl.BlockSpec((tm, nb), row),
                 pl.BlockSpec((aw, tm), col), pl.BlockSpec((aw, tm), col),
                 pl.BlockSpec((tm // TQ, aw, TQ), lambda i: (i, 0, 0)), pl.BlockSpec((WI_ROWS, tm), col)]
    if apply_ln:
        out_shape.append(jax.ShapeDtypeStruct((t, d), F32))
        out_specs.append(pl.BlockSpec((tm, d), row))
    return pl.pallas_call(
        functools.partial(_inproj_kernel, apply_ln=apply_ln),
        grid=(t // tm,),
        in_specs=[pl.BlockSpec((tm, d), row), pl.BlockSpec((1, d), const), pl.BlockSpec((1, d), const),
                  pl.BlockSpec(w_rm.shape, const), pl.BlockSpec(w_fm.shape, const)],
        out_specs=out_specs, out_shape=out_shape,
        compiler_params=_cparams(("arbitrary",)), name="inproj_ln" if apply_ln else "inproj",
    )(x, g, b, w_rm, w_fm)


def _pack_w_in(w):
    d = w.shape[0]
    aw = A_WIDTH
    cuts = np.cumsum(A_SIZES)
    wq, wk, wv, wqi = w[:, :aw], w[:, aw:2 * aw], w[:, 2 * aw:3 * aw], w[:, cuts[2]:cuts[3]]
    wki = w[:, cuts[3]:cuts[4]]
    wwi = w[:, cuts[4]:cuts[5]]
    w_rm = jnp.concatenate([wk, wki, wki, w[:, A_COLS:]], axis=1).astype(BF16)
    pad = jnp.zeros((d, WI_ROWS - IDX_HEADS), w.dtype)
    w_fm = jnp.concatenate([wq, wqi, wv, wwi, pad], axis=1).T.astype(BF16)
    return w_rm, w_fm


TQ = 256
ROW_GROUP = 32
BIGF = 3.0e38


def _t5_bucket_np(rel):
    nb = N_BUCKETS // 2
    max_exact = nb // 2
    ret = np.where(rel > 0, nb, 0)
    n = np.abs(rel)
    nf = np.maximum(n, 1).astype(np.float64)
    large = max_exact + np.floor(np.log(nf / max_exact) / math.log(MAX_DISTANCE / max_exact)
                                 * (nb - max_exact) + 1e-9).astype(np.int64)
    large = np.minimum(large, nb - 1)
    return ret + np.where(n < max_exact, n, large)


def _near_bias(rel_bias):
    qpos = np.arange(TQ)[None, :]
    kpos = np.arange(TQ)[:, None]
    bk = np.stack([_t5_bucket_np(kpos - qpos), _t5_bucket_np(kpos - TQ - qpos)])
    far = N_BUCKETS // 2 - 1
    tab = (rel_bias - rel_bias[far][None, :]) * LOG2E
    onehot = jnp.asarray(bk[..., None] == np.arange(N_BUCKETS)).astype(F32)
    return jnp.einsum('nkqb,bh->nhkq', onehot, tab.astype(F32), precision=HI)


BISECT_STEPS = 16
SUBLANES = 8
ROWSUM_MIN = 2.0 ** -60
ROWSUM_MAX = 2.0 ** 100
SNAP_STEPS = 3


def _paired_loop(n, body, carry):
    def pair(j, c):
        return body(2 * j + 1, 1, body(2 * j, 0, c))

    carry = lax.fori_loop(0, lax.shift_right_logical(n, 1), pair, carry)
    return lax.cond((n & 1) == 1, lambda c: body(n - 1, 0, c), lambda c: c, carry)


def _fold8(x, op):
    return op(x.reshape(x.shape[0] // SUBLANES, SUBLANES, x.shape[1]), axis=0)


def _dsa_kernel(qt_ref, qit_ref, wit_ref, k_ref, vt_ref, kk_ref, bias_ref, o_ref,
                s_ref, m_ref, l_ref, acc_ref, p_ref, *, ksel):
    i = pl.program_id(1)
    nt = i + 1
    tq = qt_ref.shape[1]
    kself = float(ksel)
    idx_scale = (IDX_HEADS ** -0.5) * (IDX_DIM ** -0.5)

    sub = lax.broadcasted_iota(jnp.int32, (LANES, tq), 0)
    lower = sub < (LANES // 2)
    zero_b = jnp.zeros((LANES, tq), BF16)

    def head_parts(ref):
        parts = []
        for h in range(A_HEADS):
            j, s = divmod(h, 2)
            blk = ref[j * LANES:(j + 1) * LANES, :]
            parts.append(jnp.where(lower if s == 0 else jnp.logical_not(lower), blk, zero_b))
        return parts

    def key_rows(kt):
        return pl.ds(pl.multiple_of(kt * tq, tq), tq)

    qi_parts = head_parts(qit_ref)
    wi = wit_ref[...] * idx_scale
    wrows = [wi[h:h + 1, :] for h in range(IDX_HEADS)]
    krow = lax.broadcasted_iota(jnp.int32, (tq, tq), 0)
    qcol = lax.broadcasted_iota(jnp.int32, (tq, tq), 1)
    adm_diag = (krow // CHUNK) <= (qcol // CHUNK)

    def score_tile(kt):
        kk = kk_ref[key_rows(kt), :]
        acc = jnp.zeros((tq, tq), F32)
        for h in range(IDX_HEADS):
            acc = acc + wrows[h] * jnp.maximum(_dot(kk, qi_parts[h]), 0.0)
        return acc

    def p1_body(kt, _, carry):
        sc = score_tile(kt)
        s_ref[kt] = sc
        return jnp.minimum(carry[0], _fold8(sc, jnp.min)), jnp.maximum(carry[1], _fold8(sc, jnp.max))

    mn8, mx8 = _paired_loop(i, p1_body, (jnp.full((SUBLANES, tq), BIGF, F32), jnp.full((SUBLANES, tq), -BIGF, F32)))
    sc = score_tile(i)
    s_ref[i] = jnp.where(adm_diag, sc, NEG)
    mn8 = jnp.minimum(mn8, _fold8(jnp.where(adm_diag, sc, BIGF), jnp.min))
    mx8 = jnp.maximum(mx8, _fold8(jnp.where(adm_diag, sc, -BIGF), jnp.max))
    rowmin = jnp.min(mn8, axis=0, keepdims=True)
    rowmax = jnp.max(mx8, axis=0, keepdims=True)

    def reduce_tiles(fn, inits, ops):
        def body(kt, _, carry):
            return fn(s_ref[kt], kt, carry)

        carry = _paired_loop(nt, body, tuple(jnp.full((SUBLANES, tq), v, F32) for v in inits))
        return [op(c, axis=0, keepdims=True) for op, c in zip(ops, carry)]

    qpos = i * tq + lax.broadcasted_iota(jnp.int32, (1, tq), 1)
    n_adm = (qpos // CHUNK + 1) * CHUNK
    trivial = n_adm <= ksel
    lo0 = jnp.where(trivial, 0.1 * NEG, rowmin)
    hi0 = jnp.where(trivial, 0.1 * NEG, rowmax)

    def count_ge(mid):
        def cnt(s, kt, carry):
            return (carry[0] + _fold8(jnp.where(s >= mid, 1.0, 0.0), jnp.sum),)

        return reduce_tiles(cnt, (0.0,), (jnp.sum,))[0]

    def a_body(_, st):
        lo, hi, clo = st
        mid = 0.5 * (lo + hi)
        c = count_ge(mid)
        ok = c >= kself
        return jnp.where(ok, mid, lo), jnp.where(ok, hi, mid), jnp.where(ok, c, clo)

    lo1, hi1, clo1 = lax.fori_loop(0, BISECT_STEPS, a_body, (lo0, hi0, n_adm.astype(F32)))
    hi1 = jnp.where(jnp.logical_or(clo1 == kself, trivial), lo1, hi1)

    def b_cond(st):
        lo, hi = st
        return jnp.max(jnp.where(lo < hi, 1.0, 0.0)) > 0.0

    def b_body(st):
        lo, hi = st
        mid = 0.5 * (lo + hi)
        mid = jnp.where(mid > lo, mid, hi)

        def cnt(s, kt, carry):
            c, vge, vlt = carry
            ge = s >= mid
            return (c + _fold8(jnp.where(ge, 1.0, 0.0), jnp.sum),
                    jnp.minimum(vge, _fold8(jnp.where(ge, s, BIGF), jnp.min)),
                    jnp.maximum(vlt, _fold8(jnp.where(ge, -BIGF, s), jnp.max)))

        c, vge, vlt = reduce_tiles(cnt, (0.0, BIGF, -BIGF), (jnp.sum, jnp.min, jnp.max))
        ok = c >= kself
        move_lo = jnp.logical_and(ok, jnp.logical_not(trivial))
        move_hi = jnp.logical_and(jnp.logical_not(ok), jnp.logical_not(trivial))
        return jnp.where(move_lo, vge, lo), jnp.where(move_hi, vlt, hi)

    snapped = lax.fori_loop(0, SNAP_STEPS, lambda _, st: b_body(st), (lo1, hi1))
    thr, _ = lax.while_loop(b_cond, b_body, snapped)

    def count_ties(s, kt, carry):
        return (carry[0] + _fold8(jnp.where(s >= thr, 1.0, 0.0), jnp.sum),
                carry[1] + _fold8(jnp.where(s > thr, 1.0, 0.0), jnp.sum))

    c_ge, c_gt = reduce_tiles(count_ties, (0.0, 0.0), (jnp.sum, jnp.sum))
    need = kself - c_gt
    tied = c_ge > kself
    s_total = float(s_ref.shape[0] * tq)
    jlo0 = jnp.where(tied, 0.0, s_total - 1.0)
    jhi0 = jnp.where(tied, (nt * tq).astype(F32), s_total)
    kidx = krow.astype(F32)

    def tie_cond(st):
        jlo, jhi = st
        return jnp.max(jhi - jlo) > 1.0

    def tie_body(st):
        jlo, jhi = st
        mid = jnp.floor(0.5 * (jlo + jhi))

        def cnt(s, kt, carry):
            idx = kidx + (kt * tq).astype(F32)
            return (carry[0] + _fold8(jnp.where(jnp.logical_and(s == thr, idx < mid), 1.0, 0.0), jnp.sum),)

        (c,) = reduce_tiles(cnt, (0.0,), (jnp.sum,))
        ok = c >= need
        done = (jhi - jlo) <= 1.0
        return (jnp.where(jnp.logical_or(ok, done), jlo, mid),
                jnp.where(jnp.logical_and(ok, jnp.logical_not(done)), mid, jhi))

    _, jcut = lax.while_loop(tie_cond, tie_body, (jlo0, jhi0))

    q_parts = head_parts(qt_ref)

    def reset(m_init):
        l_ref[...] = jnp.zeros(l_ref.shape, F32)
        acc_ref[...] = jnp.zeros(acc_ref.shape, F32)
        if m_init is not None:
            m_ref[...] = jnp.full(m_ref.shape, m_init, F32)

    def logits(kt, near, mb):
        for j in range(A_HEADS // 2):
            kp = k_ref[key_rows(kt), j * LANES:(j + 1) * LANES]
            for s_half in range(2):
                h = 2 * j + s_half
                lg = _dot(kp, q_parts[h]) + mb
                if near is not None:
                    lg = lg + bias_ref[near, h]
                yield h, lg

    def accumulate(kt, slot, shifted):
        for h, d in shifted:
            p = jnp.exp2(d)
            l_ref[h] += _fold8(p, jnp.sum)
            p_ref[slot, h] = p.astype(BF16)
        for h in range(A_HEADS):
            acc_ref[h] += _dot(vt_ref[kt, h * A_HEAD_DIM:(h + 1) * A_HEAD_DIM, :], p_ref[slot, h])

    def sweep_single(kt, near, slot, neg_shift):
        s = s_ref[kt]
        idx = kidx + (kt * tq).astype(F32)
        sel = jnp.logical_or(s > thr, jnp.logical_and(s == thr, idx < jcut))
        s_ref[kt] = jnp.where(sel, 0.0, NEG)
        accumulate(kt, slot, logits(kt, near, jnp.where(sel, neg_shift, NEG)))

    def sweep_max(kt, near, slot):
        for h, lg in logits(kt, near, s_ref[kt]):
            m_ref[h] = jnp.maximum(m_ref[h], _fold8(lg, jnp.max))

    def sweep_sum(kt, near, slot):
        accumulate(kt, slot, ((h, lg - m_ref[h][:1]) for h, lg in logits(kt, near, s_ref[kt])))

    def all_tiles(sweep):
        def body(kt, slot, c):
            sweep(kt, None, slot)
            return c

        _paired_loop(jnp.maximum(i - 1, 0), body, 0)

        @pl.when(i >= 1)
        def _():
            sweep(i - 1, 1, 1)

        sweep(i, 0, 0)

    def settle_rows(ref):
        for h in range(A_HEADS):
            ref[h] = jnp.broadcast_to(jnp.max(ref[h], axis=0, keepdims=True), (SUBLANES, tq))

    reset(None)
    diag_max = jnp.full((SUBLANES, tq), NEG, F32)
    for j in range(A_HEADS // 2):
        kp = k_ref[key_rows(i), j * LANES:(j + 1) * LANES]
        for h in (2 * j, 2 * j + 1):
            diag_max = jnp.maximum(diag_max, _fold8(_dot(kp, q_parts[h]) + bias_ref[0, h], jnp.max))
    neg_shift = -jnp.max(diag_max, axis=0, keepdims=True)
    all_tiles(functools.partial(sweep_single, neg_shift=neg_shift))

    def out_of_range(h):
        tot = jnp.sum(l_ref[h], axis=0, keepdims=True)
        ok = jnp.logical_and(tot >= ROWSUM_MIN, tot <= ROWSUM_MAX)
        return jnp.max(jnp.where(ok, 0.0, 1.0))

    @pl.when(functools.reduce(jnp.maximum, [out_of_range(h) for h in range(A_HEADS)]) > 0.0)
    def _():
        reset(NEG)
        all_tiles(sweep_max)
        settle_rows(m_ref)
        all_tiles(sweep_sum)

    for j in range(A_HEADS // 2):
        outs = [acc_ref[h] / jnp.sum(l_ref[h], axis=0, keepdims=True) for h in (2 * j, 2 * j + 1)]
        o_ref[:, j * LANES:(j + 1) * LANES] = jnp.concatenate(outs, axis=0).T.astype(o_ref.dtype)


def _dsa(qt, qit, wit, k, vt, kiki, near_bias, bsz, seq):
    nq = seq // TQ
    ksel = min(TOPK_MAX, seq // 4)
    qcol = lambda b, i: (0, b * nq + i)
    return pl.pallas_call(
        functools.partial(_dsa_kernel, ksel=ksel),
        grid=(bsz, nq),
        in_specs=[
            pl.BlockSpec((A_WIDTH, TQ), qcol),
            pl.BlockSpec((A_WIDTH, TQ), qcol),
            pl.BlockSpec((WI_ROWS, TQ), qcol),
            pl.BlockSpec((seq, A_WIDTH), lambda b, i: (b, 0)),
            pl.BlockSpec((nq, A_WIDTH, TQ), lambda b, i: (b, 0, 0)),
            pl.BlockSpec((seq, LANES), lambda b, i: (b, 0)),
            pl.BlockSpec((2, A_HEADS, TQ, TQ), lambda b, i: (0, 0, 0, 0)),
        ],
        out_specs=pl.BlockSpec((TQ, A_WIDTH), lambda b, i: (b * nq + i, 0)),
        out_shape=jax.ShapeDtypeStruct((bsz * seq, A_WIDTH), BF16),
        scratch_shapes=[pltpu.VMEM((nq, TQ, TQ), F32), pltpu.VMEM((A_HEADS, SUBLANES, TQ), F32),
                        pltpu.VMEM((A_HEADS, SUBLANES, TQ), F32), pltpu.VMEM((A_HEADS, A_HEAD_DIM, TQ), F32),
                        pltpu.VMEM((2, A_HEADS, TQ, TQ), BF16)],
        compiler_params=_cparams(("arbitrary", "arbitrary")), name="dsa_attention",
    )(qt, qit, wit, k, vt, kiki, near_bias)


RC = 64
RTS = 256


RNP = 4


def _rwkv_kernel(r_ref, k_ref, v_ref, wa_ref, gd_ref, rp_ref, kp_ref, vp_ref, wap_ref, gdp_ref,
                 par_ref, muw_ref, mug_ref, wdec_ref, wic_ref, wg_ref, o_ref, z_ref):
    t = pl.program_id(2)
    ts = r_ref.shape[0]

    @pl.when(t == 0)
    def _():
        z_ref[...] = jnp.zeros(z_ref.shape, F32)

    first = jnp.where(t == 0, 0.0, 1.0)

    def shifted(ref, pref, mu):
        x = ref[...]
        row = lax.broadcasted_iota(jnp.int32, x.shape, 0)
        prev = jnp.where(row == 0, pref[7:8, :] * first, pltpu.roll(x, 1, 0))
        return x + (prev - x) * mu

    par = par_ref[...]
    r_all = shifted(r_ref, rp_ref, par[8:9])
    k_all = shifted(k_ref, kp_ref, par[9:10])
    v_all = shifted(v_ref, vp_ref, par[10:11])
    wa = shifted(wa_ref, wap_ref, muw_ref[...])
    gd = shifted(gd_ref, gdp_ref, mug_ref[...])

    lane = lax.broadcasted_iota(jnp.int32, (LANES, LANES), 1)
    rowl = lax.broadcasted_iota(jnp.int32, (LANES, LANES), 0)
    same_head = (lane // B_HEAD_DIM) == (rowl // B_HEAD_DIM)
    gones = jnp.where(same_head, 1.0, 0.0)
    gones16 = gones.astype(BF16)
    strict = jnp.logical_and(same_head, rowl > lane)
    incl = jnp.logical_and(same_head, rowl >= lane)
    eye = jnp.where(lane == rowl, 1.0, 0.0)
    head0 = lax.broadcasted_iota(jnp.int32, (RC, LANES), 1) < B_HEAD_DIM

    def head_sum(x):
        hi, lo = _split2(x)
        return _dot(hi, gones16) + _dot(lo, gones16)

    def stack_heads(x):
        return jnp.concatenate([jnp.where(head0, x, 0.0), jnp.where(head0, 0.0, x)], axis=0)

    def fold_heads(x):
        return x[:RC] + x[RC:]

    def twice(x):
        return jnp.concatenate([x, x], axis=0)

    rt = lax.broadcasted_iota(jnp.int32, (ts, ts), 0)
    ct = lax.broadcasted_iota(jnp.int32, (ts, ts), 1)
    tri16 = jnp.where(jnp.logical_and((rt // RC) == (ct // RC), rt >= ct), 1.0, 0.0).astype(BF16)
    th_hi, th_lo = _split2(jnp.tanh(wa))
    wa16 = wa.astype(BF16)
    sg16 = jax.nn.sigmoid(gd).astype(BF16)
    nchunk = ts // RC
    npair = r_ref.shape[1] // LANES

    pairs = []
    for q in range(npair):
        ls = slice(q * LANES, (q + 1) * LANES)
        w0, a0, k_k, k_a, r_k, gn_g, gn_b = (par[n:n + 1, ls] for n in range(7))
        r, k, v = r_all[:, ls], k_all[:, ls], v_all[:, ls]
        wd_hi, wd_lo = _split2(wdec_ref[:, ls])
        z = w0 + (_dot(th_hi, wd_hi) + _dot(th_hi, wd_lo) + _dot(th_lo, wd_hi))
        w_log = -(jnp.maximum(-z, 0.0) + jnp.log1p(jnp.exp(-jnp.abs(z)))) - 0.5
        e = jnp.exp(w_log)
        a = jax.nn.sigmoid(a0 + _dot(wa16, wic_ref[:, ls].astype(BF16)))
        g = _dot(sg16, wg_ref[:, ls].astype(BF16))
        kkr = k * k_k
        kk = kkr / jnp.maximum(jnp.sqrt(head_sum(kkr * kkr)), 1e-12)
        k2 = k * (1.0 + (a - 1.0) * k_a)
        bonus = head_sum(r * k2 * r_k) * v
        kka = kk * a
        e_hi, e_mid, e_lo = _split3(e)
        cum = _dot(tri16, e_hi) + _dot(tri16, e_mid) + _dot(tri16, e_lo)
        pairs.append(dict(r_hat=r * jnp.exp(-cum), a_hat=-kk * jnp.exp(e - cum), cum=cum, kka=kka, k2=k2, v=v,
                          b_til=(kka * jnp.exp(cum)).astype(BF16), k_til=(k2 * jnp.exp(cum)).astype(BF16),
                          v16=v.astype(BF16), bonus=bonus, g=g, gn_g=gn_g, gn_b=gn_b))

    items = [(q, c) for q in range(npair) for c in range(nchunk)]
    n = range(len(items))
    sl = [slice(c * RC, (c + 1) * RC) for _, c in items]
    pq = [pairs[q] for q, _ in items]
    tot = [pq[i]["cum"][sl[i].stop - 1:sl[i].stop] for i in n]
    rem = [jnp.exp(pq[i]["cum"][sl[i]] - tot[i]) for i in n]
    a_s = [stack_heads(pq[i]["a_hat"][sl[i]]).astype(BF16) for i in n]
    r_s = [stack_heads(pq[i]["r_hat"][sl[i]]).astype(BF16) for i in n]
    v_s = [stack_heads(pq[i]["v"][sl[i]]).astype(BF16) for i in n]
    ar = [jnp.concatenate([a_s[i], r_s[i]], axis=0) for i in n]
    bk = [jnp.concatenate([twice(pq[i]["b_til"][sl[i]]), twice(pq[i]["k_til"][sl[i]])], axis=0) for i in n]
    m_all = [_dot_nt(ar[i], bk[i]) for i in n]
    l_ab = [jnp.where(strict, m_all[i][:LANES, :LANES], 0.0) for i in n]
    l_ak = [jnp.where(strict, m_all[i][:LANES, LANES:], 0.0).astype(BF16) for i in n]
    m_rb = [jnp.where(incl, m_all[i][LANES:, :LANES], 0.0).astype(BF16) for i in n]
    m_rk = [jnp.where(incl, m_all[i][LANES:, LANES:], 0.0).astype(BF16) for i in n]
    inv = [eye + l_ab[i] for i in n]
    pw = [l_ab[i].astype(BF16) for i in n]
    for _ in range(5):
        pw = [_dot(pw[i], pw[i]).astype(BF16) for i in n]
        inv = [inv[i] + _dot(inv[i].astype(BF16), pw[i]) for i in n]
    lv = [_dot(l_ak[i], v_s[i]).astype(BF16) for i in n]
    x_av = [_dot(inv[i].astype(BF16), jnp.concatenate([a_s[i], lv[i]], axis=1)) for i in n]
    y_av = [_dot(m_rb[i], x_av[i].astype(BF16)) for i in n]
    y_kv = [_dot(m_rk[i], v_s[i]) for i in n]
    r_p = [pq[i]["r_hat"][sl[i]] + fold_heads(y_av[i][:, :LANES]) for i in n]
    y0 = [fold_heads(y_av[i][:, LANES:] + y_kv[i]) for i in n]
    gh = [_dot((pq[i]["kka"][sl[i]] * rem[i]).T.astype(BF16), fold_heads(x_av[i]).astype(BF16)) for i in n]
    kv = [_dot((pq[i]["k2"][sl[i]] * rem[i]).T.astype(BF16), pq[i]["v16"][sl[i]]) for i in n]
    g_mat = [eye * jnp.exp(-tot[i]) + gones * gh[i][:, :LANES] for i in n]
    h_mat = [gones * (gh[i][:, LANES:] + kv[i]) for i in n]
    rg = [jnp.concatenate([r_p[i], g_mat[i]], axis=0).astype(BF16) for i in n]

    zc = [z_ref[q] for q in range(npair)]
    ys = [[] for _ in range(npair)]
    for c in range(nchunk):
        for q in range(npair):
            i = q * nchunk + c
            yz = _dot(rg[i], zc[q].astype(BF16))
            ys[q].append(yz[:RC] + y0[i])
            zc[q] = yz[RC:] + h_mat[i]
    for q in range(npair):
        z_ref[q] = zc[q]
        p = pairs[q]
        y = jnp.concatenate(ys[q], axis=0)
        mean = head_sum(y) * (1.0 / B_HEAD_DIM)
        yc = y - mean
        var = head_sum(yc * yc) * (1.0 / B_HEAD_DIM)
        yn = yc * lax.rsqrt(var + GN_EPS) * p["gn_g"] + p["gn_b"]
        o_ref[:, q * LANES:(q + 1) * LANES] = ((yn + p["bonus"]) * p["g"]).astype(o_ref.dtype)


def _rwkv(hb, par, mu_wa, mu_gd, wdec, wic, wg, bsz, seq):
    nt = seq // RTS
    width = RNP * LANES
    ngroup = B_WIDTH // width
    cb = B_WIDTH // width
    small = (3 * B_WIDTH) // LANES

    def main(shape_w, colblk):
        return pl.BlockSpec((RTS, shape_w), lambda b, p, t: (b * nt + t, colblk(p)))

    def prev(shape_w, colblk):
        return pl.BlockSpec((8, shape_w), lambda b, p, t: (jnp.maximum((b * nt + t) * (RTS // 8) - 1, 0), colblk(p)))

    wide = [lambda p: p, lambda p: cb + p, lambda p: 2 * cb + p]
    narrow = [lambda p: small, lambda p: small + 1]
    in_specs = ([main(width, c) for c in wide] + [main(LANES, c) for c in narrow]
                + [prev(width, c) for c in wide] + [prev(LANES, c) for c in narrow] + [
        pl.BlockSpec((16, width), lambda b, p, t: (0, p)),
        pl.BlockSpec((1, LANES), lambda b, p, t: (0, 0)),
        pl.BlockSpec((1, LANES), lambda b, p, t: (0, 0)),
        pl.BlockSpec((LANES, width), lambda b, p, t: (0, p)),
        pl.BlockSpec((LANES, width), lambda b, p, t: (0, p)),
        pl.BlockSpec((LANES, width), lambda b, p, t: (0, p)),
    ])
    return pl.pallas_call(
        _rwkv_kernel,
        grid=(bsz, ngroup, nt),
        in_specs=in_specs,
        out_specs=pl.BlockSpec((RTS, width), lambda b, p, t: (b * nt + t, p)),
        out_shape=jax.ShapeDtypeStruct((bsz * seq, B_WIDTH), BF16),
        scratch_shapes=[pltpu.VMEM((RNP, LANES, LANES), F32)],
        compiler_params=_cparams(("arbitrary", "arbitrary", "arbitrary")), name="rwkv7",
    )(*([hb] * 10), par, mu_wa, mu_gd, wdec, wic, wg)


def _pack_rwkv_params(mu, w0, w_decay, a0, w_iclr, k_k, k_a, r_k, gn_g, gn_b):
    bw = B_WIDTH
    zero = jnp.zeros((bw,), F32)
    rows = [w0, a0, k_k, k_a, r_k.reshape(bw), gn_g, gn_b, zero,
            mu[:bw], mu[bw:2 * bw], mu[2 * bw:3 * bw], zero, zero, zero, zero, zero]
    par = jnp.stack(rows)
    mu_wa = mu[3 * bw:3 * bw + LANES][None]
    mu_gd = mu[3 * bw + LANES:][None]
    zpad = jnp.zeros((DECAY_LORA, bw), F32)
    wdec = jnp.concatenate([w_decay, zpad], axis=0)
    wic = jnp.concatenate([zpad, w_iclr], axis=0)
    return par, mu_wa, mu_gd, wdec, wic


EXPERTS_PER_STEP = 2


def _xor_partner_rows(x, row, bit):
    up = pltpu.roll(x, bit, 0)
    down = pltpu.roll(x, x.shape[0] - bit, 0)
    return jnp.where((row & bit) != 0, up, down)


def _moe_gates_t(x_hi, x_lo, wrt, br):
    w_hi, w_lo = _split2(wrt)
    logits = _dot_nt(w_hi, x_hi) + _dot_nt(w_lo, x_hi) + _dot_nt(w_hi, x_lo)
    row = lax.broadcasted_iota(jnp.int32, logits.shape, 0)
    rowf = row.astype(F32)
    big = float(N_EXPERTS)
    scores = jax.nn.sigmoid(logits)
    sel = scores + br
    p1 = _xor_partner_rows(sel, row, 1)
    hi1, lo1 = jnp.maximum(sel, p1), jnp.minimum(sel, p1)
    hi2, lo2 = _xor_partner_rows(hi1, row, 2), _xor_partner_rows(lo1, row, 2)
    gscore = jnp.maximum(hi1, hi2) + jnp.maximum(jnp.minimum(hi1, hi2), jnp.maximum(lo1, lo2))
    gbest = jnp.max(gscore, axis=0, keepdims=True)
    first = jnp.min(jnp.where(gscore == gbest, rowf, big), axis=0, keepdims=True)
    in_group = jnp.floor(rowf * (1.0 / EXPERTS_PER_GROUP)) == jnp.floor(first * (1.0 / EXPERTS_PER_GROUP))
    masked = jnp.where(in_group, sel, NEG)
    m1 = jnp.max(masked, axis=0, keepdims=True)
    i1 = jnp.min(jnp.where(masked == m1, rowf, big), axis=0, keepdims=True)
    pick1 = rowf == i1
    masked2 = jnp.where(pick1, NEG, masked)
    m2 = jnp.max(masked2, axis=0, keepdims=True)
    i2 = jnp.min(jnp.where(masked2 == m2, rowf, big), axis=0, keepdims=True)
    pick2 = rowf == i2
    s1 = jnp.sum(jnp.where(pick1, scores, 0.0), axis=0, keepdims=True)
    s2 = jnp.sum(jnp.where(pick2, scores, 0.0), axis=0, keepdims=True)
    tot = s1 + s2
    return jnp.where(pick1, s1 / tot, 0.0) + jnp.where(pick2, s2 / tot, 0.0)


def _mix_moe_ple_kernel(x_ref, a_ref, bo_ref, woa_ref, wob_ref, gm_ref, bm_ref, p_ref, wrt_ref, br_ref, wgu_ref, wd_ref,
                        g_ref, b_ref, wpg_ref, wpp_ref, g2_ref, b2_ref, o_ref, gate_ref, acc_ref, xb_ref, x1_ref):
    e = pl.program_id(1)

    @pl.when(e == 0)
    def _():
        mix = _dot(a_ref[...], woa_ref[...]) + _dot(bo_ref[...], wob_ref[...])
        x1 = _ln(ALPHA * x_ref[...] + mix, gm_ref[...], bm_ref[...])
        x1_ref[...] = x1
        x_hi, x_lo = _split2(x1)
        gates_t = _moe_gates_t(x_hi, x_lo, wrt_ref[...], br_ref[:, :1])
        pad = jnp.zeros((LANES - gates_t.shape[0], gates_t.shape[1]), F32)
        gate_ref[...] = jnp.concatenate([gates_t, pad], axis=0).T
        xb_ref[...] = x_hi
        acc_ref[...] = jnp.zeros(acc_ref.shape, F32)

    gates = gate_ref[...]
    lane = lax.broadcasted_iota(jnp.int32, gates.shape, 1)
    xb = xb_ref[...]
    hidden = []
    for n in range(EXPERTS_PER_STEP):
        gcol = jnp.sum(jnp.where(lane == e * EXPERTS_PER_STEP + n, gates, 0.0), axis=1, keepdims=True)
        hu = _dot(xb, wgu_ref[n])
        hg, up = hu[:, :D_EXPERT], hu[:, D_EXPERT:]
        hidden.append(((hg * jax.nn.sigmoid(hg)) * up * gcol).astype(BF16))
    wd = wd_ref[...]
    acc_ref[...] += _dot(jnp.concatenate(hidden, axis=1), wd.reshape(wd.shape[0] * wd.shape[1], wd.shape[2]))

    @pl.when(e == pl.num_programs(1) - 1)
    def _():
        x2 = _ln(ALPHA * x1_ref[...] + acc_ref[...], g_ref[...], b_ref[...])
        gate = jax.nn.sigmoid(_dot(x2.astype(BF16), wpg_ref[...]))
        pe = _dot(p_ref[...].astype(BF16), wpp_ref[...])
        o_ref[...] = _ln(ALPHA * x2 + gate * pe, g2_ref[...], b2_ref[...])


def _mix_moe_ple(x, a_out, b_out, w_out, gm, bm, p, w_router, b_router, w_gate, w_up, w_down, g, b,
                 w_ple_gate, w_ple, g2, b2, tm=1024):
    t, d = x.shape
    ne = w_gate.shape[0]
    woa = w_out[:A_WIDTH].astype(BF16)
    wob = w_out[A_WIDTH:].astype(BF16)
    wgu = jnp.concatenate([w_gate, w_up], axis=2).astype(BF16)
    wd = w_down.astype(BF16)
    br = jnp.broadcast_to(b_router[:, None], (ne, LANES))
    row = lambda i, e: (i, 0)
    const = lambda i, e: (0, 0)
    vec = pl.BlockSpec((1, d), const)
    return pl.pallas_call(
        _mix_moe_ple_kernel, grid=(t // tm, ne // EXPERTS_PER_STEP),
        in_specs=[pl.BlockSpec((tm, d), row), pl.BlockSpec((tm, A_WIDTH), row), pl.BlockSpec((tm, B_WIDTH), row),
                  pl.BlockSpec((A_WIDTH, d), const), pl.BlockSpec((B_WIDTH, d), const), vec, vec,
                  pl.BlockSpec((tm, PLE_DIM), row),
                  pl.BlockSpec((ne, d), const), pl.BlockSpec((ne, LANES), const),
                  pl.BlockSpec((EXPERTS_PER_STEP, d, 2 * D_EXPERT), lambda i, e: (e, 0, 0)),
                  pl.BlockSpec((EXPERTS_PER_STEP, D_EXPERT, d), lambda i, e: (e, 0, 0)),
                  vec, vec, pl.BlockSpec((d, d), const), pl.BlockSpec((PLE_DIM, d), const), vec, vec],
        out_specs=pl.BlockSpec((tm, d), row), out_shape=jax.ShapeDtypeStruct((t, d), F32),
        scratch_shapes=[pltpu.VMEM((tm, LANES), F32), pltpu.VMEM((tm, d), F32), pltpu.VMEM((tm, d), BF16),
                        pltpu.VMEM((tm, d), F32)],
        compiler_params=_cparams(("arbitrary", "arbitrary")), name="mix_moe_ple_ln",
    )(x, a_out, b_out, woa, wob, gm, bm, p, w_router.T, br, wgu, wd, g, b,
      w_ple_gate.astype(BF16), w_ple.astype(BF16), g2, b2)


def kernel(x, p, ln_in_g, ln_in_b, w_in, w_out, mu_shift, w0, w_decay, a0, w_iclr, w_gate_up, k_k, k_a, r_k,
           gn_g, gn_b, rel_bias, w_router, b_router, w_exp_gate, w_exp_up, w_exp_down, w_ple, w_ple_gate,
           ln_mix_g, ln_mix_b, ln_ffn_g, ln_ffn_b, ln_ple_g, ln_ple_b):
    bsz, seq, d = x.shape
    t = bsz * seq
    depth = w_in.shape[0]
    xf = x.reshape(t, d)
    near_bias = _near_bias(rel_bias)
    for i in range(depth):
        w_rm, w_fm = _pack_w_in(w_in[i])
        outs = _inproj(xf, ln_in_g[None], ln_in_b[None], w_rm, w_fm, apply_ln=(i == 0))
        k_rm, kiki, hb, qt, qit, vt, wit = outs[:7]
        if i == 0:
            xf = outs[7]
        a_out = _dsa(qt, qit, wit, k_rm, vt, kiki, near_bias, bsz, seq)
        par, mu_wa, mu_gd, wdec, wic = _pack_rwkv_params(mu_shift[i], w0[i], w_decay[i], a0[i], w_iclr[i],
                                                         k_k[i], k_a[i], r_k[i], gn_g[i], gn_b[i])
        b_out = _rwkv(hb, par, mu_wa, mu_gd, wdec, wic, w_gate_up[i], bsz, seq)
        xf = _mix_moe_ple(xf, a_out, b_out, w_out[i], ln_mix_g[i][None], ln_mix_b[i][None],
                          p[i].reshape(t, PLE_DIM), w_router, b_router, w_exp_gate[i], w_exp_up[i], w_exp_down[i],
                          ln_ffn_g[i][None], ln_ffn_b[i][None], w_ple_gate[i], w_ple[i],
                          ln_ple_g[i][None], ln_ple_b[i][None])
    return xf.reshape(bsz, seq, d)
```

```python
import functools
import math

import numpy as np
import jax
import jax.numpy as jnp
from jax import lax
from jax.experimental import pallas as pl
from jax.experimental.pallas import tpu as pltpu

D_MODEL = 1024
DEPTH = 2
CHUNK = 64
A_HEAD_DIM = 64
A_WIDTH = D_MODEL // 2
A_HEADS = A_WIDTH // A_HEAD_DIM
IDX_HEADS = 8
IDX_DIM = 64
TOPK_MAX = 256
N_BUCKETS = 32
MAX_DISTANCE = 128
B_HEAD_DIM = 64
B_WIDTH = D_MODEL - A_WIDTH
B_HEADS = B_WIDTH // B_HEAD_DIM
DECAY_LORA = 64
ICLR_LORA = 64
GATE_LORA = 128
A_SIZES = (A_WIDTH, A_WIDTH, A_WIDTH, IDX_HEADS * IDX_DIM, IDX_DIM, IDX_HEADS)
B_SIZES = (B_WIDTH, B_WIDTH, B_WIDTH, DECAY_LORA, ICLR_LORA, GATE_LORA)
A_COLS = sum(A_SIZES)
B_COLS = sum(B_SIZES)
N_EXPERTS = 16
N_GROUPS = 4
EXPERTS_PER_GROUP = N_EXPERTS // N_GROUPS
D_EXPERT = 256
PLE_DIM = 256
ALPHA = (2 * DEPTH) ** 0.25
LN_EPS = 1e-5
GN_EPS = 64e-5
NEG = -1e30

LANES = 128
VMEM_LIMIT = 56 * 1024 * 1024
F32 = jnp.float32
BF16 = jnp.bfloat16
HI = lax.Precision.HIGHEST
NT_DIMS = (((1,), (1,)), ((), ()))


def _cparams(sem):
    return pltpu.CompilerParams(dimension_semantics=sem, vmem_limit_bytes=VMEM_LIMIT)


def _ln(x, g, b):
    mu = jnp.mean(x, axis=-1, keepdims=True)
    xc = x - mu
    var = jnp.mean(xc * xc, axis=-1, keepdims=True)
    return xc * lax.rsqrt(var + LN_EPS) * g + b


def _dot(a, b):
    return jnp.dot(a, b, preferred_element_type=F32)


def _dot_hi(a, b):
    return jnp.dot(a, b, preferred_element_type=F32, precision=HI)


def _dot_nt(a, b):
    return lax.dot_general(a, b, NT_DIMS, preferred_element_type=F32)


def _dot_nt_hi(a, b):
    return lax.dot_general(a, b, NT_DIMS, preferred_element_type=F32, precision=HI)


def _split2(x):
    hi = x.astype(BF16)
    return hi, (x - hi.astype(F32)).astype(BF16)


def _split3(x):
    hi = x.astype(BF16)
    rest = x - hi.astype(F32)
    mid = rest.astype(BF16)
    return hi, mid, (rest - mid.astype(F32)).astype(BF16)


IN_NCHUNK = 512


WI_ROWS = 16
LOG2E = math.log2(math.e)
Q_SCALE = A_HEAD_DIM ** -0.5 * LOG2E


def _inproj_kernel(x_ref, g_ref, b_ref, wr_ref, wf_ref, *out_refs, apply_ln):
    x = x_ref[...]
    if apply_ln:
        x = _ln(x, g_ref[...], b_ref[...])
        out_refs[7][...] = x
    xb = x.astype(BF16)
    k_ref, kk_ref, hb_ref, qt_ref, qit_ref, vt_ref, wit_ref = out_refs[:7]
    aw = A_WIDTH
    k_ref[...] = _dot(xb, wr_ref[:, :aw]).astype(BF16)
    kk_ref[...] = _dot(xb, wr_ref[:, aw:aw + LANES]).astype(BF16)
    nb = hb_ref.shape[1]
    for c0 in range(0, nb, IN_NCHUNK):
        c1 = min(c0 + IN_NCHUNK, nb)
        hb_ref[:, c0:c1] = _dot(xb, wr_ref[:, aw + LANES + c0:aw + LANES + c1])
    qt_ref[...] = (_dot_nt(wf_ref[:aw], xb) * Q_SCALE).astype(BF16)
    qit_ref[...] = _dot_nt(wf_ref[aw:2 * aw], xb).astype(BF16)
    vt = _dot_nt(wf_ref[2 * aw:3 * aw], xb).astype(BF16)
    for n in range(vt_ref.shape[0]):
        vt_ref[n] = vt[:, n * TQ:(n + 1) * TQ]
    wit_ref[...] = _dot_nt(wf_ref[3 * aw:], xb)


def _inproj(x, g, b, w_rm, w_fm, apply_ln, tm=512):
    t, d = x.shape
    aw = A_WIDTH
    nb = w_rm.shape[1] - aw - LANES
    row = lambda i: (i, 0)
    col = lambda i: (0, i)
    const = lambda i: (0, 0)
    out_shape = [jax.ShapeDtypeStruct((t, aw), BF16), jax.ShapeDtypeStruct((t, LANES), BF16),
                 jax.ShapeDtypeStruct((t, nb), F32), jax.ShapeDtypeStruct((aw, t), BF16),
                 jax.ShapeDtypeStruct((aw, t), BF16), jax.ShapeDtypeStruct((t // TQ, aw, TQ), BF16),
                 jax.ShapeDtypeStruct((WI_ROWS, t), F32)]
    out_specs = [pl.BlockSpec((tm, aw), row), pl.BlockSpec((tm, LANES), row), pl.BlockSpec((tm, nb), row),
                 pl.BlockSpec((aw, tm), col), pl.BlockSpec((aw, tm), col),
                 pl.BlockSpec((tm // TQ, aw, TQ), lambda i: (i, 0, 0)), pl.BlockSpec((WI_ROWS, tm), col)]
    if apply_ln:
        out_shape.append(jax.ShapeDtypeStruct((t, d), F32))
        out_specs.append(pl.BlockSpec((tm, d), row))
    return pl.pallas_call(
        functools.partial(_inproj_kernel, apply_ln=apply_ln),
        grid=(t // tm,),
        in_specs=[pl.BlockSpec((tm, d), row), pl.BlockSpec((1, d), const), pl.BlockSpec((1, d), const),
                  pl.BlockSpec(w_rm.shape, const), pl.BlockSpec(w_fm.shape, const)],
        out_specs=out_specs, out_shape=out_shape,
        compiler_params=_cparams(("arbitrary",)), name="inproj_ln" if apply_ln else "inproj",
    )(x, g, b, w_rm, w_fm)


def _pack_w_in(w):
    d = w.shape[0]
    aw = A_WIDTH
    cuts = np.cumsum(A_SIZES)
    wq, wk, wv, wqi = w[:, :aw], w[:, aw:2 * aw], w[:, 2 * aw:3 * aw], w[:, cuts[2]:cuts[3]]
    wki = w[:, cuts[3]:cuts[4]]
    wwi = w[:, cuts[4]:cuts[5]]
    w_rm = jnp.concatenate([wk, wki, wki, w[:, A_COLS:]], axis=1).astype(BF16)
    pad = jnp.zeros((d, WI_ROWS - IDX_HEADS), w.dtype)
    w_fm = jnp.concatenate([wq, wqi, wv, wwi, pad], axis=1).T.astype(BF16)
    return w_rm, w_fm


TQ = 256
ROW_GROUP = 32
BIGF = 3.0e38


def _t5_bucket_np(rel):
    nb = N_BUCKETS // 2
    max_exact = nb // 2
    ret = np.where(rel > 0, nb, 0)
    n = np.abs(rel)
    nf = np.maximum(n, 1).astype(np.float64)
    large = max_exact + np.floor(np.log(nf / max_exact) / math.log(MAX_DISTANCE / max_exact)
                                 * (nb - max_exact) + 1e-9).astype(np.int64)
    large = np.minimum(large, nb - 1)
    return ret + np.where(n < max_exact, n, large)


def _near_bias(rel_bias):
    period = 2 * TQ
    m = np.arange(period)
    rel = np.stack([m - (TQ - 1), m - (TQ - 1) - TQ])
    far = N_BUCKETS // 2 - 1
    tab = (rel_bias - rel_bias[far][None, :]) * LOG2E
    onehot = jnp.asarray(_t5_bucket_np(rel)[..., None] == np.arange(N_BUCKETS)).astype(F32)
    u = jnp.einsum('nmb,bh->nhm', onehot, tab.astype(F32), precision=HI)
    long = jnp.tile(u, (1, 1, TQ + 1))[..., :TQ * (period + 1)]
    return long.reshape(2, A_HEADS, TQ, period + 1)[..., :TQ][..., ::-1]


BISECT_STEPS = 12
SUBLANES = 8
ROWSUM_MIN = 2.0 ** -60
ROWSUM_MAX = 2.0 ** 100
SNAP_STEPS = 3


def _paired_loop(n, body, carry):
    def pair(j, c):
        return body(2 * j + 1, 1, body(2 * j, 0, c))

    carry = lax.fori_loop(0, lax.shift_right_logical(n, 1), pair, carry)
    return lax.cond((n & 1) == 1, lambda c: body(n - 1, 0, c), lambda c: c, carry)


def _fold8(x, op):
    return op(x.reshape(x.shape[0] // SUBLANES, SUBLANES, x.shape[1]), axis=0)


def _dsa_kernel(qt_ref, qit_ref, wit_ref, k_ref, vt_ref, kk_ref, bias_ref, o_ref,
                s_ref, m_ref, l_ref, acc_ref, p_ref, *, ksel):
    i = pl.program_id(1)
    nt = i + 1
    tq = qt_ref.shape[1]
    kself = float(ksel)
    idx_scale = (IDX_HEADS ** -0.5) * (IDX_DIM ** -0.5)

    sub = lax.broadcasted_iota(jnp.int32, (LANES, tq), 0)
    lower = sub < (LANES // 2)
    zero_b = jnp.zeros((LANES, tq), BF16)

    def head_parts(ref):
        parts = []
        for h in range(A_HEADS):
            j, s = divmod(h, 2)
            blk = ref[j * LANES:(j + 1) * LANES, :]
            parts.append(jnp.where(lower if s == 0 else jnp.logical_not(lower), blk, zero_b))
        return parts

    def key_rows(kt):
        return pl.ds(pl.multiple_of(kt * tq, tq), tq)

    qi_parts = head_parts(qit_ref)
    wi = wit_ref[...] * idx_scale
    wrows = [wi[h:h + 1, :] for h in range(IDX_HEADS)]
    krow = lax.broadcasted_iota(jnp.int32, (tq, tq), 0)
    qcol = lax.broadcasted_iota(jnp.int32, (tq, tq), 1)
    adm_diag = (krow // CHUNK) <= (qcol // CHUNK)

    def score_tile(kt):
        kk = kk_ref[key_rows(kt), :]
        acc = jnp.zeros((tq, tq), F32)
        for h in range(IDX_HEADS):
            acc = acc + wrows[h] * jnp.maximum(_dot(kk, qi_parts[h]), 0.0)
        return acc

    def p1_body(kt, _, carry):
        sc = score_tile(kt)
        s_ref[kt] = sc
        return jnp.minimum(carry[0], _fold8(sc, jnp.min)), jnp.maximum(carry[1], _fold8(sc, jnp.max))

    mn8, mx8 = _paired_loop(i, p1_body, (jnp.full((SUBLANES, tq), BIGF, F32), jnp.full((SUBLANES, tq), -BIGF, F32)))
    sc = score_tile(i)
    s_ref[i] = jnp.where(adm_diag, sc, NEG)
    mn8 = jnp.minimum(mn8, _fold8(jnp.where(adm_diag, sc, BIGF), jnp.min))
    mx8 = jnp.maximum(mx8, _fold8(jnp.where(adm_diag, sc, -BIGF), jnp.max))
    rowmin = jnp.min(mn8, axis=0, keepdims=True)
    rowmax = jnp.max(mx8, axis=0, keepdims=True)

    def reduce_tiles(fn, inits, ops):
        def body(kt, _, carry):
            return fn(s_ref[kt], kt, carry)

        carry = _paired_loop(nt, body, tuple(jnp.full((SUBLANES, tq), v, F32) for v in inits))
        return [op(c, axis=0, keepdims=True) for op, c in zip(ops, carry)]

    qpos = i * tq + lax.broadcasted_iota(jnp.int32, (1, tq), 1)
    n_adm = (qpos // CHUNK + 1) * CHUNK
    trivial = n_adm <= ksel
    lo0 = jnp.where(trivial, 0.1 * NEG, rowmin)
    hi0 = jnp.where(trivial, 0.1 * NEG, rowmax)

    def count_ge(mid):
        def cnt(s, kt, carry):
            return (carry[0] + _fold8(jnp.where(s >= mid, 1.0, 0.0), jnp.sum),)

        return reduce_tiles(cnt, (0.0,), (jnp.sum,))[0]

    def a_body(_, st):
        lo, hi, clo = st
        mid = 0.5 * (lo + hi)
        c = count_ge(mid)
        ok = c >= kself
        return jnp.where(ok, mid, lo), jnp.where(ok, hi, mid), jnp.where(ok, c, clo)

    lo1, hi1, clo1 = lax.fori_loop(0, BISECT_STEPS, a_body, (lo0, hi0, n_adm.astype(F32)))
    hi1 = jnp.where(jnp.logical_or(clo1 == kself, trivial), lo1, hi1)

    def b_cond(st):
        lo, hi = st
        return jnp.max(jnp.where(lo < hi, 1.0, 0.0)) > 0.0

    def b_body(st):
        lo, hi = st
        mid = 0.5 * (lo + hi)
        mid = jnp.where(mid > lo, mid, hi)

        def cnt(s, kt, carry):
            c, vge, vlt = carry
            ge = s >= mid
            return (c + _fold8(jnp.where(ge, 1.0, 0.0), jnp.sum),
                    jnp.minimum(vge, _fold8(jnp.where(ge, s, BIGF), jnp.min)),
                    jnp.maximum(vlt, _fold8(jnp.where(ge, -BIGF, s), jnp.max)))

        c, vge, vlt = reduce_tiles(cnt, (0.0, BIGF, -BIGF), (jnp.sum, jnp.min, jnp.max))
        ok = c >= kself
        move_lo = jnp.logical_and(ok, jnp.logical_not(trivial))
        move_hi = jnp.logical_and(jnp.logical_not(ok), jnp.logical_not(trivial))
        return jnp.where(move_lo, vge, lo), jnp.where(move_hi, vlt, hi)

    snapped = lax.fori_loop(0, SNAP_STEPS, lambda _, st: b_body(st), (lo1, hi1))
    thr, _ = lax.while_loop(b_cond, b_body, snapped)

    def count_ties(s, kt, carry):
        return (carry[0] + _fold8(jnp.where(s >= thr, 1.0, 0.0), jnp.sum),
                carry[1] + _fold8(jnp.where(s > thr, 1.0, 0.0), jnp.sum))

    c_ge, c_gt = reduce_tiles(count_ties, (0.0, 0.0), (jnp.sum, jnp.sum))
    need = kself - c_gt
    tied = c_ge > kself
    s_total = float(s_ref.shape[0] * tq)
    jlo0 = jnp.where(tied, 0.0, s_total - 1.0)
    jhi0 = jnp.where(tied, (nt * tq).astype(F32), s_total)
    kidx = krow.astype(F32)

    def tie_cond(st):
        jlo, jhi = st
        return jnp.max(jhi - jlo) > 1.0

    def tie_body(st):
        jlo, jhi = st
        mid = jnp.floor(0.5 * (jlo + jhi))

        def cnt(s, kt, carry):
            idx = kidx + (kt * tq).astype(F32)
            return (carry[0] + _fold8(jnp.where(jnp.logical_and(s == thr, idx < mid), 1.0, 0.0), jnp.sum),)

        (c,) = reduce_tiles(cnt, (0.0,), (jnp.sum,))
        ok = c >= need
        done = (jhi - jlo) <= 1.0
        return (jnp.where(jnp.logical_or(ok, done), jlo, mid),
                jnp.where(jnp.logical_and(ok, jnp.logical_not(done)), mid, jhi))

    _, jcut = lax.while_loop(tie_cond, tie_body, (jlo0, jhi0))

    q_parts = head_parts(qt_ref)

    def reset(m_init):
        l_ref[...] = jnp.zeros(l_ref.shape, F32)
        acc_ref[...] = jnp.zeros(acc_ref.shape, F32)
        if m_init is not None:
            m_ref[...] = jnp.full(m_ref.shape, m_init, F32)

    def logits(kt, near, mb):
        for j in range(A_HEADS // 2):
            kp = k_ref[key_rows(kt), j * LANES:(j + 1) * LANES]
            for s_half in range(2):
                h = 2 * j + s_half
                lg = _dot(kp, q_parts[h]) + mb
                if near is not None:
                    lg = lg + bias_ref[near, h]
                yield h, lg

    def accumulate(kt, slot, shifted):
        for h, d in shifted:
            p = jnp.exp2(d)
            l_ref[h] += _fold8(p, jnp.sum)
            p_ref[slot, h] = p.astype(BF16)
        for h in range(A_HEADS):
            acc_ref[h] += _dot(vt_ref[kt, h * A_HEAD_DIM:(h + 1) * A_HEAD_DIM, :], p_ref[slot, h])

    def sweep_single(kt, near, slot, neg_shift):
        s = s_ref[kt]
        idx = kidx + (kt * tq).astype(F32)
        sel = jnp.logical_or(s > thr, jnp.logical_and(s == thr, idx < jcut))
        s_ref[kt] = jnp.where(sel, 0.0, NEG)
        accumulate(kt, slot, logits(kt, near, jnp.where(sel, neg_shift, NEG)))

    def sweep_max(kt, near, slot):
        for h, lg in logits(kt, near, s_ref[kt]):
            m_ref[h] = jnp.maximum(m_ref[h], _fold8(lg, jnp.max))

    def sweep_sum(kt, near, slot):
        accumulate(kt, slot, ((h, lg - m_ref[h][:1]) for h, lg in logits(kt, near, s_ref[kt])))

    def all_tiles(sweep):
        def body(kt, slot, c):
            sweep(kt, None, slot)
            return c

        _paired_loop(jnp.maximum(i - 1, 0), body, 0)

        @pl.when(i >= 1)
        def _():
            sweep(i - 1, 1, 1)

        sweep(i, 0, 0)

    def settle_rows(ref):
        for h in range(A_HEADS):
            ref[h] = jnp.broadcast_to(jnp.max(ref[h], axis=0, keepdims=True), (SUBLANES, tq))

    reset(None)
    diag_max = jnp.full((SUBLANES, tq), NEG, F32)
    for j in range(A_HEADS // 2):
        kp = k_ref[key_rows(i), j * LANES:(j + 1) * LANES]
        for h in (2 * j, 2 * j + 1):
            diag_max = jnp.maximum(diag_max, _fold8(_dot(kp, q_parts[h]) + bias_ref[0, h], jnp.max))
    neg_shift = -jnp.max(diag_max, axis=0, keepdims=True)
    all_tiles(functools.partial(sweep_single, neg_shift=neg_shift))

    def out_of_range(h):
        tot = jnp.sum(l_ref[h], axis=0, keepdims=True)
        ok = jnp.logical_and(tot >= ROWSUM_MIN, tot <= ROWSUM_MAX)
        return jnp.max(jnp.where(ok, 0.0, 1.0))

    @pl.when(functools.reduce(jnp.maximum, [out_of_range(h) for h in range(A_HEADS)]) > 0.0)
    def _():
        reset(NEG)
        all_tiles(sweep_max)
        settle_rows(m_ref)
        all_tiles(sweep_sum)

    for j in range(A_HEADS // 2):
        outs = [acc_ref[h] / jnp.sum(l_ref[h], axis=0, keepdims=True) for h in (2 * j, 2 * j + 1)]
        o_ref[:, j * LANES:(j + 1) * LANES] = jnp.concatenate(outs, axis=0).T.astype(o_ref.dtype)


def _dsa(qt, qit, wit, k, vt, kiki, near_bias, bsz, seq):
    nq = seq // TQ
    ksel = min(TOPK_MAX, seq // 4)
    qcol = lambda b, i: (0, b * nq + i)
    return pl.pallas_call(
        functools.partial(_dsa_kernel, ksel=ksel),
        grid=(bsz, nq),
        in_specs=[
            pl.BlockSpec((A_WIDTH, TQ), qcol),
            pl.BlockSpec((A_WIDTH, TQ), qcol),
            pl.BlockSpec((WI_ROWS, TQ), qcol),
            pl.BlockSpec((seq, A_WIDTH), lambda b, i: (b, 0)),
            pl.BlockSpec((nq, A_WIDTH, TQ), lambda b, i: (b, 0, 0)),
            pl.BlockSpec((seq, LANES), lambda b, i: (b, 0)),
            pl.BlockSpec((2, A_HEADS, TQ, TQ), lambda b, i: (0, 0, 0, 0)),
        ],
        out_specs=pl.BlockSpec((TQ, A_WIDTH), lambda b, i: (b * nq + i, 0)),
        out_shape=jax.ShapeDtypeStruct((bsz * seq, A_WIDTH), BF16),
        scratch_shapes=[pltpu.VMEM((nq, TQ, TQ), F32), pltpu.VMEM((A_HEADS, SUBLANES, TQ), F32),
                        pltpu.VMEM((A_HEADS, SUBLANES, TQ), F32), pltpu.VMEM((A_HEADS, A_HEAD_DIM, TQ), F32),
                        pltpu.VMEM((2, A_HEADS, TQ, TQ), BF16)],
        compiler_params=_cparams(("arbitrary", "arbitrary")), name="dsa_attention",
    )(qt, qit, wit, k, vt, kiki, near_bias)


RC = 64
RTS = 256


RNP = 4


def _rwkv_kernel(r_ref, k_ref, v_ref, wa_ref, gd_ref, rp_ref, kp_ref, vp_ref, wap_ref, gdp_ref,
                 par_ref, muw_ref, mug_ref, wdec_ref, wic_ref, wg_ref, o_ref, z_ref):
    t = pl.program_id(2)
    ts = r_ref.shape[0]

    @pl.when(t == 0)
    def _():
        z_ref[...] = jnp.zeros(z_ref.shape, F32)

    first = jnp.where(t == 0, 0.0, 1.0)

    def shifted(ref, pref, mu):
        x = ref[...]
        row = lax.broadcasted_iota(jnp.int32, x.shape, 0)
        prev = jnp.where(row == 0, pref[7:8, :] * first, pltpu.roll(x, 1, 0))
        return x + (prev - x) * mu

    par = par_ref[...]
    r_all = shifted(r_ref, rp_ref, par[8:9])
    k_all = shifted(k_ref, kp_ref, par[9:10])
    v_all = shifted(v_ref, vp_ref, par[10:11])
    wa = shifted(wa_ref, wap_ref, muw_ref[...])
    gd = shifted(gd_ref, gdp_ref, mug_ref[...])

    lane = lax.broadcasted_iota(jnp.int32, (LANES, LANES), 1)
    rowl = lax.broadcasted_iota(jnp.int32, (LANES, LANES), 0)
    same_head = (lane // B_HEAD_DIM) == (rowl // B_HEAD_DIM)
    gones = jnp.where(same_head, 1.0, 0.0)
    gones16 = gones.astype(BF16)
    strict = jnp.logical_and(same_head, rowl > lane)
    incl = jnp.logical_and(same_head, rowl >= lane)
    eye = jnp.where(lane == rowl, 1.0, 0.0)
    head0 = lax.broadcasted_iota(jnp.int32, (RC, LANES), 1) < B_HEAD_DIM

    def head_sum(x):
        hi, lo = _split2(x)
        return _dot(hi, gones16) + _dot(lo, gones16)

    def stack_heads(x):
        return jnp.concatenate([jnp.where(head0, x, 0.0), jnp.where(head0, 0.0, x)], axis=0)

    def fold_heads(x):
        return x[:RC] + x[RC:]

    def twice(x):
        return jnp.concatenate([x, x], axis=0)

    rt = lax.broadcasted_iota(jnp.int32, (ts, ts), 0)
    ct = lax.broadcasted_iota(jnp.int32, (ts, ts), 1)
    tri16 = jnp.where(jnp.logical_and((rt // RC) == (ct // RC), rt >= ct), 1.0, 0.0).astype(BF16)
    th_hi, th_lo = _split2(jnp.tanh(wa))
    wa16 = wa.astype(BF16)
    sg16 = jax.nn.sigmoid(gd).astype(BF16)
    nchunk = ts // RC
    npair = r_ref.shape[1] // LANES

    pairs = []
    for q in range(npair):
        ls = slice(q * LANES, (q + 1) * LANES)
        w0, a0, k_k, k_a, r_k, gn_g, gn_b = (par[n:n + 1, ls] for n in range(7))
        r, k, v = r_all[:, ls], k_all[:, ls], v_all[:, ls]
        wd_hi, wd_lo = _split2(wdec_ref[:, ls])
        z = w0 + (_dot(th_hi, wd_hi) + _dot(th_hi, wd_lo) + _dot(th_lo, wd_hi))
        w_log = -(jnp.maximum(-z, 0.0) + jnp.log1p(jnp.exp(-jnp.abs(z)))) - 0.5
        e = jnp.exp(w_log)
        a = jax.nn.sigmoid(a0 + _dot(wa16, wic_ref[:, ls].astype(BF16)))
        g = _dot(sg16, wg_ref[:, ls].astype(BF16))
        kkr = k * k_k
        kk = kkr / jnp.maximum(jnp.sqrt(head_sum(kkr * kkr)), 1e-12)
        k2 = k * (1.0 + (a - 1.0) * k_a)
        bonus = head_sum(r * k2 * r_k) * v
        kka = kk * a
        e_hi, e_mid, e_lo = _split3(e)
        cum = _dot(tri16, e_hi) + _dot(tri16, e_mid) + _dot(tri16, e_lo)
        pairs.append(dict(r_hat=r * jnp.exp(-cum), a_hat=-kk * jnp.exp(e - cum), cum=cum, kka=kka, k2=k2, v=v,
                          b_til=(kka * jnp.exp(cum)).astype(BF16), k_til=(k2 * jnp.exp(cum)).astype(BF16),
                          v16=v.astype(BF16), bonus=bonus, g=g, gn_g=gn_g, gn_b=gn_b))

    items = [(q, c) for q in range(npair) for c in range(nchunk)]
    n = range(len(items))
    sl = [slice(c * RC, (c + 1) * RC) for _, c in items]
    pq = [pairs[q] for q, _ in items]
    tot = [pq[i]["cum"][sl[i].stop - 1:sl[i].stop] for i in n]
    rem = [jnp.exp(pq[i]["cum"][sl[i]] - tot[i]) for i in n]
    a_s = [stack_heads(pq[i]["a_hat"][sl[i]]).astype(BF16) for i in n]
    r_s = [stack_heads(pq[i]["r_hat"][sl[i]]).astype(BF16) for i in n]
    v_s = [stack_heads(pq[i]["v"][sl[i]]).astype(BF16) for i in n]
    ar = [jnp.concatenate([a_s[i], r_s[i]], axis=0) for i in n]
    bk = [jnp.concatenate([twice(pq[i]["b_til"][sl[i]]), twice(pq[i]["k_til"][sl[i]])], axis=0) for i in n]
    m_all = [_dot_nt(ar[i], bk[i]) for i in n]
    l_ab = [jnp.where(strict, m_all[i][:LANES, :LANES], 0.0) for i in n]
    l_ak = [jnp.where(strict, m_all[i][:LANES, LANES:], 0.0).astype(BF16) for i in n]
    m_rb = [jnp.where(incl, m_all[i][LANES:, :LANES], 0.0).astype(BF16) for i in n]
    m_rk = [jnp.where(incl, m_all[i][LANES:, LANES:], 0.0).astype(BF16) for i in n]
    inv = [eye + l_ab[i] for i in n]
    pw = [l_ab[i].astype(BF16) for i in n]
    for _ in range(5):
        pw = [_dot(pw[i], pw[i]).astype(BF16) for i in n]
        inv = [inv[i] + _dot(inv[i].astype(BF16), pw[i]) for i in n]
    lv = [_dot(l_ak[i], v_s[i]).astype(BF16) for i in n]
    x_av = [_dot(inv[i].astype(BF16), jnp.concatenate([a_s[i], lv[i]], axis=1)) for i in n]
    y_av = [_dot(m_rb[i], x_av[i].astype(BF16)) for i in n]
    y_kv = [_dot(m_rk[i], v_s[i]) for i in n]
    r_p = [pq[i]["r_hat"][sl[i]] + fold_heads(y_av[i][:, :LANES]) for i in n]
    y0 = [fold_heads(y_av[i][:, LANES:] + y_kv[i]) for i in n]
    gh = [_dot((pq[i]["kka"][sl[i]] * rem[i]).T.astype(BF16), fold_heads(x_av[i]).astype(BF16)) for i in n]
    kv = [_dot((pq[i]["k2"][sl[i]] * rem[i]).T.astype(BF16), pq[i]["v16"][sl[i]]) for i in n]
    g_mat = [eye * jnp.exp(-tot[i]) + gones * gh[i][:, :LANES] for i in n]
    h_mat = [gones * (gh[i][:, LANES:] + kv[i]) for i in n]
    rg = [jnp.concatenate([r_p[i], g_mat[i]], axis=0).astype(BF16) for i in n]

    zc = [z_ref[q] for q in range(npair)]
    ys = [[] for _ in range(npair)]
    for c in range(nchunk):
        for q in range(npair):
            i = q * nchunk + c
            yz = _dot(rg[i], zc[q].astype(BF16))
            ys[q].append(yz[:RC] + y0[i])
            zc[q] = yz[RC:] + h_mat[i]
    for q in range(npair):
        z_ref[q] = zc[q]
        p = pairs[q]
        y = jnp.concatenate(ys[q], axis=0)
        mean = head_sum(y) * (1.0 / B_HEAD_DIM)
        yc = y - mean
        var = head_sum(yc * yc) * (1.0 / B_HEAD_DIM)
        yn = yc * lax.rsqrt(var + GN_EPS) * p["gn_g"] + p["gn_b"]
        o_ref[:, q * LANES:(q + 1) * LANES] = ((yn + p["bonus"]) * p["g"]).astype(o_ref.dtype)


def _rwkv(hb, par, mu_wa, mu_gd, wdec, wic, wg, bsz, seq):
    nt = seq // RTS
    width = RNP * LANES
    ngroup = B_WIDTH // width
    cb = B_WIDTH // width
    small = (3 * B_WIDTH) // LANES

    def main(shape_w, colblk):
        return pl.BlockSpec((RTS, shape_w), lambda b, p, t: (b * nt + t, colblk(p)))

    def prev(shape_w, colblk):
        return pl.BlockSpec((8, shape_w), lambda b, p, t: (jnp.maximum((b * nt + t) * (RTS // 8) - 1, 0), colblk(p)))

    wide = [lambda p: p, lambda p: cb + p, lambda p: 2 * cb + p]
    narrow = [lambda p: small, lambda p: small + 1]
    in_specs = ([main(width, c) for c in wide] + [main(LANES, c) for c in narrow]
                + [prev(width, c) for c in wide] + [prev(LANES, c) for c in narrow] + [
        pl.BlockSpec((16, width), lambda b, p, t: (0, p)),
        pl.BlockSpec((1, LANES), lambda b, p, t: (0, 0)),
        pl.BlockSpec((1, LANES), lambda b, p, t: (0, 0)),
        pl.BlockSpec((LANES, width), lambda b, p, t: (0, p)),
        pl.BlockSpec((LANES, width), lambda b, p, t: (0, p)),
        pl.BlockSpec((LANES, width), lambda b, p, t: (0, p)),
    ])
    return pl.pallas_call(
        _rwkv_kernel,
        grid=(bsz, ngroup, nt),
        in_specs=in_specs,
        out_specs=pl.BlockSpec((RTS, width), lambda b, p, t: (b * nt + t, p)),
        out_shape=jax.ShapeDtypeStruct((bsz * seq, B_WIDTH), BF16),
        scratch_shapes=[pltpu.VMEM((RNP, LANES, LANES), F32)],
        compiler_params=_cparams(("arbitrary", "arbitrary", "arbitrary")), name="rwkv7",
    )(*([hb] * 10), par, mu_wa, mu_gd, wdec, wic, wg)


def _pack_rwkv_params(mu, w0, w_decay, a0, w_iclr, k_k, k_a, r_k, gn_g, gn_b):
    bw = B_WIDTH
    zero = jnp.zeros((bw,), F32)
    rows = [w0, a0, k_k, k_a, r_k.reshape(bw), gn_g, gn_b, zero,
            mu[:bw], mu[bw:2 * bw], mu[2 * bw:3 * bw], zero, zero, zero, zero, zero]
    par = jnp.stack(rows)
    mu_wa = mu[3 * bw:3 * bw + LANES][None]
    mu_gd = mu[3 * bw + LANES:][None]
    zpad = jnp.zeros((DECAY_LORA, bw), F32)
    wdec = jnp.concatenate([w_decay, zpad], axis=0)
    wic = jnp.concatenate([zpad, w_iclr], axis=0)
    return par, mu_wa, mu_gd, wdec, wic


EXPERTS_PER_STEP = 2


def _xor_partner_rows(x, row, bit):
    up = pltpu.roll(x, bit, 0)
    down = pltpu.roll(x, x.shape[0] - bit, 0)
    return jnp.where((row & bit) != 0, up, down)


def _moe_gates_t(x_hi, x_lo, wrt, br):
    w_hi, w_lo = _split2(wrt)
    logits = _dot_nt(w_hi, x_hi) + _dot_nt(w_lo, x_hi) + _dot_nt(w_hi, x_lo)
    row = lax.broadcasted_iota(jnp.int32, logits.shape, 0)
    rowf = row.astype(F32)
    big = float(N_EXPERTS)
    scores = jax.nn.sigmoid(logits)
    sel = scores + br
    p1 = _xor_partner_rows(sel, row, 1)
    hi1, lo1 = jnp.maximum(sel, p1), jnp.minimum(sel, p1)
    hi2, lo2 = _xor_partner_rows(hi1, row, 2), _xor_partner_rows(lo1, row, 2)
    gscore = jnp.maximum(hi1, hi2) + jnp.maximum(jnp.minimum(hi1, hi2), jnp.maximum(lo1, lo2))
    gbest = jnp.max(gscore, axis=0, keepdims=True)
    first = jnp.min(jnp.where(gscore == gbest, rowf, big), axis=0, keepdims=True)
    in_group = jnp.floor(rowf * (1.0 / EXPERTS_PER_GROUP)) == jnp.floor(first * (1.0 / EXPERTS_PER_GROUP))
    masked = jnp.where(in_group, sel, NEG)
    m1 = jnp.max(masked, axis=0, keepdims=True)
    i1 = jnp.min(jnp.where(masked == m1, rowf, big), axis=0, keepdims=True)
    pick1 = rowf == i1
    masked2 = jnp.where(pick1, NEG, masked)
    m2 = jnp.max(masked2, axis=0, keepdims=True)
    i2 = jnp.min(jnp.where(masked2 == m2, rowf, big), axis=0, keepdims=True)
    pick2 = rowf == i2
    s1 = jnp.sum(jnp.where(pick1, scores, 0.0), axis=0, keepdims=True)
    s2 = jnp.sum(jnp.where(pick2, scores, 0.0), axis=0, keepdims=True)
    tot = s1 + s2
    return jnp.where(pick1, s1 / tot, 0.0) + jnp.where(pick2, s2 / tot, 0.0)


def _mix_moe_ple_kernel(x_ref, a_ref, bo_ref, woa_ref, wob_ref, gm_ref, bm_ref, p_ref, wrt_ref, br_ref, wgu_ref, wd_ref,
                        g_ref, b_ref, wpg_ref, wpp_ref, g2_ref, b2_ref, o_ref, gate_ref, acc_ref, xb_ref, x1_ref):
    e = pl.program_id(1)

    @pl.when(e == 0)
    def _():
        mix = _dot(a_ref[...], woa_ref[...]) + _dot(bo_ref[...], wob_ref[...])
        x1 = _ln(ALPHA * x_ref[...] + mix, gm_ref[...], bm_ref[...])
        x1_ref[...] = x1
        x_hi, x_lo = _split2(x1)
        gates_t = _moe_gates_t(x_hi, x_lo, wrt_ref[...], br_ref[:, :1])
        pad = jnp.zeros((LANES - gates_t.shape[0], gates_t.shape[1]), F32)
        gate_ref[...] = jnp.concatenate([gates_t, pad], axis=0).T
        xb_ref[...] = x_hi
        acc_ref[...] = jnp.zeros(acc_ref.shape, F32)

    gates = gate_ref[...]
    lane = lax.broadcasted_iota(jnp.int32, gates.shape, 1)
    xb = xb_ref[...]
    hidden = []
    for n in range(EXPERTS_PER_STEP):
        gcol = jnp.sum(jnp.where(lane == e * EXPERTS_PER_STEP + n, gates, 0.0), axis=1, keepdims=True)
        hu = _dot(xb, wgu_ref[n])
        hg, up = hu[:, :D_EXPERT], hu[:, D_EXPERT:]
        hidden.append(((hg * jax.nn.sigmoid(hg)) * up * gcol).astype(BF16))
    wd = wd_ref[...]
    acc_ref[...] += _dot(jnp.concatenate(hidden, axis=1), wd.reshape(wd.shape[0] * wd.shape[1], wd.shape[2]))

    @pl.when(e == pl.num_programs(1) - 1)
    def _():
        x2 = _ln(ALPHA * x1_ref[...] + acc_ref[...], g_ref[...], b_ref[...])
        gate = jax.nn.sigmoid(_dot(x2.astype(BF16), wpg_ref[...]))
        pe = _dot(p_ref[...].astype(BF16), wpp_ref[...])
        o_ref[...] = _ln(ALPHA * x2 + gate * pe, g2_ref[...], b2_ref[...])


def _mix_moe_ple(x, a_out, b_out, w_out, gm, bm, p, w_router, b_router, w_gate, w_up, w_down, g, b,
                 w_ple_gate, w_ple, g2, b2, tm=1024):
    t, d = x.shape
    ne = w_gate.shape[0]
    woa = w_out[:A_WIDTH].astype(BF16)
    wob = w_out[A_WIDTH:].astype(BF16)
    wgu = jnp.concatenate([w_gate, w_up], axis=2).astype(BF16)
    wd = w_down.astype(BF16)
    br = jnp.broadcast_to(b_router[:, None], (ne, LANES))
    row = lambda i, e: (i, 0)
    const = lambda i, e: (0, 0)
    vec = pl.BlockSpec((1, d), const)
    return pl.pallas_call(
        _mix_moe_ple_kernel, grid=(t // tm, ne // EXPERTS_PER_STEP),
        in_specs=[pl.BlockSpec((tm, d), row), pl.BlockSpec((tm, A_WIDTH), row), pl.BlockSpec((tm, B_WIDTH), row),
                  pl.BlockSpec((A_WIDTH, d), const), pl.BlockSpec((B_WIDTH, d), const), vec, vec,
                  pl.BlockSpec((tm, PLE_DIM), row),
                  pl.BlockSpec((ne, d), const), pl.BlockSpec((ne, LANES), const),
                  pl.BlockSpec((EXPERTS_PER_STEP, d, 2 * D_EXPERT), lambda i, e: (e, 0, 0)),
                  pl.BlockSpec((EXPERTS_PER_STEP, D_EXPERT, d), lambda i, e: (e, 0, 0)),
                  vec, vec, pl.BlockSpec((d, d), const), pl.BlockSpec((PLE_DIM, d), const), vec, vec],
        out_specs=pl.BlockSpec((tm, d), row), out_shape=jax.ShapeDtypeStruct((t, d), F32),
        scratch_shapes=[pltpu.VMEM((tm, LANES), F32), pltpu.VMEM((tm, d), F32), pltpu.VMEM((tm, d), BF16),
                        pltpu.VMEM((tm, d), F32)],
        compiler_params=_cparams(("arbitrary", "arbitrary")), name="mix_moe_ple_ln",
    )(x, a_out, b_out, woa, wob, gm, bm, p, w_router.T, br, wgu, wd, g, b,
      w_ple_gate.astype(BF16), w_ple.astype(BF16), g2, b2)


def kernel(x, p, ln_in_g, ln_in_b, w_in, w_out, mu_shift, w0, w_decay, a0, w_iclr, w_gate_up, k_k, k_a, r_k,
           gn_g, gn_b, rel_bias, w_router, b_router, w_exp_gate, w_exp_up, w_exp_down, w_ple, w_ple_gate,
           ln_mix_g, ln_mix_b, ln_ffn_g, ln_ffn_b, ln_ple_g, ln_ple_b):
    bsz, seq, d = x.shape
    t = bsz * seq
    depth = w_in.shape[0]
    xf = x.reshape(t, d)
    near_bias = _near_bias(rel_bias)
    for i in range(depth):
        w_rm, w_fm = _pack_w_in(w_in[i])
        outs = _inproj(xf, ln_in_g[None], ln_in_b[None], w_rm, w_fm, apply_ln=(i == 0))
        k_rm, kiki, hb, qt, qit, vt, wit = outs[:7]
        if i == 0:
            xf = outs[7]
        a_out = _dsa(qt, qit, wit, k_rm, vt, kiki, near_bias, bsz, seq)
        par, mu_wa, mu_gd, wdec, wic = _pack_rwkv_params(mu_shift[i], w0[i], w_decay[i], a0[i], w_iclr[i],
                                                         k_k[i], k_a[i], r_k[i], gn_g[i], gn_b[i])
        b_out = _rwkv(hb, par, mu_wa, mu_gd, wdec, wic, w_gate_up[i], bsz, seq)
        xf = _mix_moe_ple(xf, a_out, b_out, w_out[i], ln_mix_g[i][None], ln_mix_b[i][None],
                          p[i].reshape(t, PLE_DIM), w_router, b_router, w_exp_gate[i], w_exp_up[i], w_exp_down[i],
                          ln_ffn_g[i][None], ln_ffn_b[i][None], w_ple_gate[i], w_ple[i],
                          ln_ple_g[i][None], ln_ple_b[i][None])
    return xf.reshape(bsz, seq, d)
```

```python
import functools
import math

import numpy as np
import jax
import jax.numpy as jnp
from jax import lax
from jax.experimental import pallas as pl
from jax.experimental.pallas import tpu as pltpu

D_MODEL = 1024
DEPTH = 2
CHUNK = 64
A_HEAD_DIM = 64
A_WIDTH = D_MODEL // 2
A_HEADS = A_WIDTH // A_HEAD_DIM
IDX_HEADS = 8
IDX_DIM = 64
TOPK_MAX = 256
N_BUCKETS = 32
MAX_DISTANCE = 128
B_HEAD_DIM = 64
B_WIDTH = D_MODEL - A_WIDTH
B_HEADS = B_WIDTH // B_HEAD_DIM
DECAY_LORA = 64
ICLR_LORA = 64
A_SIZES = (A_WIDTH, A_WIDTH, A_WIDTH, IDX_HEADS * IDX_DIM, IDX_DIM, IDX_HEADS)
A_COLS = sum(A_SIZES)
N_EXPERTS = 16
N_GROUPS = 4
EXPERTS_PER_GROUP = N_EXPERTS // N_GROUPS
D_EXPERT = 256
PLE_DIM = 256
ALPHA = (2 * DEPTH) ** 0.25
LN_EPS = 1e-5
GN_EPS = 64e-5
NEG = -1e30

LANES = 128
VMEM_LIMIT = 56 * 1024 * 1024
F32 = jnp.float32
BF16 = jnp.bfloat16
HI = lax.Precision.HIGHEST
NT_DIMS = (((1,), (1,)), ((), ()))


def _cparams(sem):
    return pltpu.CompilerParams(dimension_semantics=sem, vmem_limit_bytes=VMEM_LIMIT)


def _ln(x, g, b):
    mu = jnp.mean(x, axis=-1, keepdims=True)
    xc = x - mu
    var = jnp.mean(xc * xc, axis=-1, keepdims=True)
    return xc * lax.rsqrt(var + LN_EPS) * g + b


def _dot(a, b):
    return jnp.dot(a, b, preferred_element_type=F32)


def _dot_nt(a, b):
    return lax.dot_general(a, b, NT_DIMS, preferred_element_type=F32)


def _split2(x):
    hi = x.astype(BF16)
    return hi, (x - hi.astype(F32)).astype(BF16)


def _split3(x):
    hi = x.astype(BF16)
    rest = x - hi.astype(F32)
    mid = rest.astype(BF16)
    return hi, mid, (rest - mid.astype(F32)).astype(BF16)


IN_NCHUNK = 512


WI_ROWS = 16
LOG2E = math.log2(math.e)
Q_SCALE = A_HEAD_DIM ** -0.5 * LOG2E


def _inproj_kernel(x_ref, g_ref, b_ref, wr_ref, wf_ref, *out_refs, apply_ln):
    x = x_ref[...]
    if apply_ln:
        x = _ln(x, g_ref[...], b_ref[...])
        out_refs[7][...] = x
    xb = x.astype(BF16)
    k_ref, kk_ref, hb_ref, qt_ref, qit_ref, vt_ref, wit_ref = out_refs[:7]
    aw = A_WIDTH
    k_ref[...] = _dot(xb, wr_ref[:, :aw]).astype(BF16)
    kk_ref[...] = _dot(xb, wr_ref[:, aw:aw + LANES]).astype(BF16)
    nb = hb_ref.shape[1]
    for c0 in range(0, nb, IN_NCHUNK):
        c1 = min(c0 + IN_NCHUNK, nb)
        hb_ref[:, c0:c1] = _dot(xb, wr_ref[:, aw + LANES + c0:aw + LANES + c1])
    qt_ref[...] = (_dot_nt(wf_ref[:aw], xb) * Q_SCALE).astype(BF16)
    qit_ref[...] = _dot_nt(wf_ref[aw:2 * aw], xb).astype(BF16)
    vt = _dot_nt(wf_ref[2 * aw:3 * aw], xb).astype(BF16)
    for n in range(vt_ref.shape[0]):
        vt_ref[n] = vt[:, n * TQ:(n + 1) * TQ]
    wit_ref[...] = _dot_nt(wf_ref[3 * aw:], xb)


def _inproj(x, g, b, w_rm, w_fm, apply_ln, tm=512):
    t, d = x.shape
    aw = A_WIDTH
    nb = w_rm.shape[1] - aw - LANES
    row = lambda i: (i, 0)
    col = lambda i: (0, i)
    const = lambda i: (0, 0)
    out_shape = [jax.ShapeDtypeStruct((t, aw), BF16), jax.ShapeDtypeStruct((t, LANES), BF16),
                 jax.ShapeDtypeStruct((t, nb), F32), jax.ShapeDtypeStruct((aw, t), BF16),
                 jax.ShapeDtypeStruct((aw, t), BF16), jax.ShapeDtypeStruct((t // TQ, aw, TQ), BF16),
                 jax.ShapeDtypeStruct((WI_ROWS, t), F32)]
    out_specs = [pl.BlockSpec((tm, aw), row), pl.BlockSpec((tm, LANES), row), pl.BlockSpec((tm, nb), row),
                 pl.BlockSpec((aw, tm), col), pl.BlockSpec((aw, tm), col),
                 pl.BlockSpec((tm // TQ, aw, TQ), lambda i: (i, 0, 0)), pl.BlockSpec((WI_ROWS, tm), col)]
    if apply_ln:
        out_shape.append(jax.ShapeDtypeStruct((t, d), F32))
        out_specs.append(pl.BlockSpec((tm, d), row))
    return pl.pallas_call(
        functools.partial(_inproj_kernel, apply_ln=apply_ln),
        grid=(t // tm,),
        in_specs=[pl.BlockSpec((tm, d), row), pl.BlockSpec((1, d), const), pl.BlockSpec((1, d), const),
                  pl.BlockSpec(w_rm.shape, const), pl.BlockSpec(w_fm.shape, const)],
        out_specs=out_specs, out_shape=out_shape,
        compiler_params=_cparams(("arbitrary",)), name="inproj_ln" if apply_ln else "inproj",
    )(x, g, b, w_rm, w_fm)


def _pack_w_in(w):
    d = w.shape[0]
    aw = A_WIDTH
    cuts = np.cumsum(A_SIZES)
    wq, wk, wv, wqi = w[:, :aw], w[:, aw:2 * aw], w[:, 2 * aw:3 * aw], w[:, cuts[2]:cuts[3]]
    wki = w[:, cuts[3]:cuts[4]]
    wwi = w[:, cuts[4]:cuts[5]]
    w_rm = jnp.concatenate([wk, wki, wki, w[:, A_COLS:]], axis=1).astype(BF16)
    pad = jnp.zeros((d, WI_ROWS - IDX_HEADS), w.dtype)
    w_fm = jnp.concatenate([wq, wqi, wv, wwi, pad], axis=1).T.astype(BF16)
    return w_rm, w_fm


TQ = 256
BIGF = 3.0e38


def _t5_bucket_np(rel):
    nb = N_BUCKETS // 2
    max_exact = nb // 2
    ret = np.where(rel > 0, nb, 0)
    n = np.abs(rel)
    nf = np.maximum(n, 1).astype(np.float64)
    large = max_exact + np.floor(np.log(nf / max_exact) / math.log(MAX_DISTANCE / max_exact)
                                 * (nb - max_exact) + 1e-9).astype(np.int64)
    large = np.minimum(large, nb - 1)
    return ret + np.where(n < max_exact, n, large)


def _near_bias(rel_bias):
    qpos = np.arange(TQ)[None, :]
    kpos = np.arange(TQ)[:, None]
    bk = np.stack([_t5_bucket_np(kpos - qpos), _t5_bucket_np(kpos - TQ - qpos)])
    far = N_BUCKETS // 2 - 1
    tab = (rel_bias - rel_bias[far][None, :]) * LOG2E
    onehot = jnp.asarray(bk[..., None] == np.arange(N_BUCKETS)).astype(F32)
    return jnp.einsum('nkqb,bh->nhkq', onehot, tab.astype(F32), precision=HI)


BISECT_STEPS = 12
SUBLANES = 8
ROWSUM_MIN = 2.0 ** -60
ROWSUM_MAX = 2.0 ** 100
SNAP_STEPS = 3


def _paired_loop(n, body, carry):
    def pair(j, c):
        return body(2 * j + 1, 1, body(2 * j, 0, c))

    carry = lax.fori_loop(0, lax.shift_right_logical(n, 1), pair, carry)
    return lax.cond((n & 1) == 1, lambda c: body(n - 1, 0, c), lambda c: c, carry)


def _fold8(x, op):
    return op(x.reshape(x.shape[0] // SUBLANES, SUBLANES, x.shape[1]), axis=0)


def _dsa_kernel(qt_ref, qit_ref, wit_ref, k_ref, vt_ref, kk_ref, bias_ref, o_ref,
                s_ref, m_ref, l_ref, acc_ref, p_ref, *, ksel):
    i = pl.program_id(1)
    nt = i + 1
    tq = qt_ref.shape[1]
    kself = float(ksel)
    idx_scale = (IDX_HEADS ** -0.5) * (IDX_DIM ** -0.5)

    sub = lax.broadcasted_iota(jnp.int32, (LANES, tq), 0)
    lower = sub < (LANES // 2)
    zero_b = jnp.zeros((LANES, tq), BF16)

    def head_parts(ref):
        parts = []
        for h in range(A_HEADS):
            j, s = divmod(h, 2)
            blk = ref[j * LANES:(j + 1) * LANES, :]
            parts.append(jnp.where(lower if s == 0 else jnp.logical_not(lower), blk, zero_b))
        return parts

    def key_rows(kt):
        return pl.ds(pl.multiple_of(kt * tq, tq), tq)

    qi_parts = head_parts(qit_ref)
    wi = wit_ref[...] * idx_scale
    wrows = [wi[h:h + 1, :] for h in range(IDX_HEADS)]
    krow = lax.broadcasted_iota(jnp.int32, (tq, tq), 0)
    qcol = lax.broadcasted_iota(jnp.int32, (tq, tq), 1)
    adm_diag = (krow // CHUNK) <= (qcol // CHUNK)

    def score_tile(kt):
        kk = kk_ref[key_rows(kt), :]
        acc = jnp.zeros((tq, tq), F32)
        for h in range(IDX_HEADS):
            acc = acc + wrows[h] * jnp.maximum(_dot(kk, qi_parts[h]), 0.0)
        return acc

    def p1_body(kt, _, carry):
        sc = score_tile(kt)
        s_ref[kt] = sc
        return jnp.minimum(carry[0], _fold8(sc, jnp.min)), jnp.maximum(carry[1], _fold8(sc, jnp.max))

    mn8, mx8 = _paired_loop(i, p1_body, (jnp.full((SUBLANES, tq), BIGF, F32), jnp.full((SUBLANES, tq), -BIGF, F32)))
    sc = score_tile(i)
    s_ref[i] = jnp.where(adm_diag, sc, NEG)
    mn8 = jnp.minimum(mn8, _fold8(jnp.where(adm_diag, sc, BIGF), jnp.min))
    mx8 = jnp.maximum(mx8, _fold8(jnp.where(adm_diag, sc, -BIGF), jnp.max))
    rowmin = jnp.min(mn8, axis=0, keepdims=True)
    rowmax = jnp.max(mx8, axis=0, keepdims=True)

    def reduce_tiles(fn, inits, ops):
        def body(kt, _, carry):
            return fn(s_ref[kt], kt, carry)

        carry = _paired_loop(nt, body, tuple(jnp.full((SUBLANES, tq), v, F32) for v in inits))
        return [op(c, axis=0, keepdims=True) for op, c in zip(ops, carry)]

    qpos = i * tq + lax.broadcasted_iota(jnp.int32, (1, tq), 1)
    n_adm = (qpos // CHUNK + 1) * CHUNK
    trivial = n_adm <= ksel
    lo0 = jnp.where(trivial, 0.1 * NEG, rowmin)
    hi0 = jnp.where(trivial, 0.1 * NEG, rowmax)

    def count_ge(mid):
        def cnt(s, kt, carry):
            return (carry[0] + _fold8(jnp.where(s >= mid, 1.0, 0.0), jnp.sum),)

        return reduce_tiles(cnt, (0.0,), (jnp.sum,))[0]

    def a_body(_, st):
        lo, hi, clo = st
        mid = 0.5 * (lo + hi)
        c = count_ge(mid)
        ok = c >= kself
        return jnp.where(ok, mid, lo), jnp.where(ok, hi, mid), jnp.where(ok, c, clo)

    lo1, hi1, clo1 = lax.fori_loop(0, BISECT_STEPS, a_body, (lo0, hi0, n_adm.astype(F32)))
    hi1 = jnp.where(jnp.logical_or(clo1 == kself, trivial), lo1, hi1)

    def b_cond(st):
        lo, hi = st
        return jnp.max(jnp.where(lo < hi, 1.0, 0.0)) > 0.0

    def b_body(st):
        lo, hi = st
        mid = 0.5 * (lo + hi)
        mid = jnp.where(mid > lo, mid, hi)

        def cnt(s, kt, carry):
            c, vge, vlt = carry
            ge = s >= mid
            return (c + _fold8(jnp.where(ge, 1.0, 0.0), jnp.sum),
                    jnp.minimum(vge, _fold8(jnp.where(ge, s, BIGF), jnp.min)),
                    jnp.maximum(vlt, _fold8(jnp.where(ge, -BIGF, s), jnp.max)))

        c, vge, vlt = reduce_tiles(cnt, (0.0, BIGF, -BIGF), (jnp.sum, jnp.min, jnp.max))
        ok = c >= kself
        move_lo = jnp.logical_and(ok, jnp.logical_not(trivial))
        move_hi = jnp.logical_and(jnp.logical_not(ok), jnp.logical_not(trivial))
        return jnp.where(move_lo, vge, lo), jnp.where(move_hi, vlt, hi)

    snapped = lax.fori_loop(0, SNAP_STEPS, lambda _, st: b_body(st), (lo1, hi1))
    thr, _ = lax.while_loop(b_cond, b_body, snapped)

    def count_ties(s, kt, carry):
        return (carry[0] + _fold8(jnp.where(s >= thr, 1.0, 0.0), jnp.sum),
                carry[1] + _fold8(jnp.where(s > thr, 1.0, 0.0), jnp.sum))

    c_ge, c_gt = reduce_tiles(count_ties, (0.0, 0.0), (jnp.sum, jnp.sum))
    need = kself - c_gt
    tied = c_ge > kself
    s_total = float(s_ref.shape[0] * tq)
    jlo0 = jnp.where(tied, 0.0, s_total - 1.0)
    jhi0 = jnp.where(tied, (nt * tq).astype(F32), s_total)
    kidx = krow.astype(F32)

    def tie_cond(st):
        jlo, jhi = st
        return jnp.max(jhi - jlo) > 1.0

    def tie_body(st):
        jlo, jhi = st
        mid = jnp.floor(0.5 * (jlo + jhi))

        def cnt(s, kt, carry):
            idx = kidx + (kt * tq).astype(F32)
            return (carry[0] + _fold8(jnp.where(jnp.logical_and(s == thr, idx < mid), 1.0, 0.0), jnp.sum),)

        (c,) = reduce_tiles(cnt, (0.0,), (jnp.sum,))
        ok = c >= need
        done = (jhi - jlo) <= 1.0
        return (jnp.where(jnp.logical_or(ok, done), jlo, mid),
                jnp.where(jnp.logical_and(ok, jnp.logical_not(done)), mid, jhi))

    _, jcut = lax.while_loop(tie_cond, tie_body, (jlo0, jhi0))

    q_parts = head_parts(qt_ref)

    def reset(m_init):
        l_ref[...] = jnp.zeros(l_ref.shape, F32)
        acc_ref[...] = jnp.zeros(acc_ref.shape, F32)
        if m_init is not None:
            m_ref[...] = jnp.full(m_ref.shape, m_init, F32)

    def logits(kt, near, mb):
        for j in range(A_HEADS // 2):
            kp = k_ref[key_rows(kt), j * LANES:(j + 1) * LANES]
            for s_half in range(2):
                h = 2 * j + s_half
                lg = _dot(kp, q_parts[h]) + mb
                if near is not None:
                    lg = lg + bias_ref[near, h]
                yield h, lg

    def accumulate(kt, slot, shifted):
        for h, d in shifted:
            p = jnp.exp2(d)
            l_ref[h] += _fold8(p, jnp.sum)
            p_ref[slot, h] = p.astype(BF16)
        for h in range(A_HEADS):
            acc_ref[h] += _dot(vt_ref[kt, h * A_HEAD_DIM:(h + 1) * A_HEAD_DIM, :], p_ref[slot, h])

    def sweep_single(kt, near, slot, neg_shift):
        s = s_ref[kt]
        idx = kidx + (kt * tq).astype(F32)
        sel = jnp.logical_or(s > thr, jnp.logical_and(s == thr, idx < jcut))
        s_ref[kt] = jnp.where(sel, 0.0, NEG)
        accumulate(kt, slot, logits(kt, near, jnp.where(sel, neg_shift, NEG)))

    def sweep_max(kt, near, slot):
        for h, lg in logits(kt, near, s_ref[kt]):
            m_ref[h] = jnp.maximum(m_ref[h], _fold8(lg, jnp.max))

    def sweep_sum(kt, near, slot):
        accumulate(kt, slot, ((h, lg - m_ref[h][:1]) for h, lg in logits(kt, near, s_ref[kt])))

    def all_tiles(sweep):
        def body(kt, slot, c):
            sweep(kt, None, slot)
            return c

        _paired_loop(jnp.maximum(i - 1, 0), body, 0)

        @pl.when(i >= 1)
        def _():
            sweep(i - 1, 1, 1)

        sweep(i, 0, 0)

    def settle_rows(ref):
        for h in range(A_HEADS):
            ref[h] = jnp.broadcast_to(jnp.max(ref[h], axis=0, keepdims=True), (SUBLANES, tq))

    reset(None)
    diag_max = jnp.full((SUBLANES, tq), NEG, F32)
    for j in range(A_HEADS // 2):
        kp = k_ref[key_rows(i), j * LANES:(j + 1) * LANES]
        for h in (2 * j, 2 * j + 1):
            diag_max = jnp.maximum(diag_max, _fold8(_dot(kp, q_parts[h]) + bias_ref[0, h], jnp.max))
    neg_shift = -jnp.max(diag_max, axis=0, keepdims=True)
    all_tiles(functools.partial(sweep_single, neg_shift=neg_shift))

    def out_of_range(h):
        tot = jnp.sum(l_ref[h], axis=0, keepdims=True)
        ok = jnp.logical_and(tot >= ROWSUM_MIN, tot <= ROWSUM_MAX)
        return jnp.max(jnp.where(ok, 0.0, 1.0))

    @pl.when(functools.reduce(jnp.maximum, [out_of_range(h) for h in range(A_HEADS)]) > 0.0)
    def _():
        reset(NEG)
        all_tiles(sweep_max)
        settle_rows(m_ref)
        all_tiles(sweep_sum)

    for j in range(A_HEADS // 2):
        outs = [acc_ref[h] / jnp.sum(l_ref[h], axis=0, keepdims=True) for h in (2 * j, 2 * j + 1)]
        o_ref[:, j * LANES:(j + 1) * LANES] = jnp.concatenate(outs, axis=0).T.astype(o_ref.dtype)


def _dsa(qt, qit, wit, k, vt, kiki, near_bias, bsz, seq):
    nq = seq // TQ
    ksel = min(TOPK_MAX, seq // 4)
    qcol = lambda b, i: (0, b * nq + i)
    return pl.pallas_call(
        functools.partial(_dsa_kernel, ksel=ksel),
        grid=(bsz, nq),
        in_specs=[
            pl.BlockSpec((A_WIDTH, TQ), qcol),
            pl.BlockSpec((A_WIDTH, TQ), qcol),
            pl.BlockSpec((WI_ROWS, TQ), qcol),
            pl.BlockSpec((seq, A_WIDTH), lambda b, i: (b, 0)),
            pl.BlockSpec((nq, A_WIDTH, TQ), lambda b, i: (b, 0, 0)),
            pl.BlockSpec((seq, LANES), lambda b, i: (b, 0)),
            pl.BlockSpec((2, A_HEADS, TQ, TQ), lambda b, i: (0, 0, 0, 0)),
        ],
        out_specs=pl.BlockSpec((TQ, A_WIDTH), lambda b, i: (b * nq + i, 0)),
        out_shape=jax.ShapeDtypeStruct((bsz * seq, A_WIDTH), BF16),
        scratch_shapes=[pltpu.VMEM((nq, TQ, TQ), F32), pltpu.VMEM((A_HEADS, SUBLANES, TQ), F32),
                        pltpu.VMEM((A_HEADS, SUBLANES, TQ), F32), pltpu.VMEM((A_HEADS, A_HEAD_DIM, TQ), F32),
                        pltpu.VMEM((2, A_HEADS, TQ, TQ), BF16)],
        compiler_params=_cparams(("arbitrary", "arbitrary")), name="dsa_attention",
    )(qt, qit, wit, k, vt, kiki, near_bias)


RC = 64
RTS = 256


RNP = 4


def _rwkv_kernel(r_ref, k_ref, v_ref, wa_ref, gd_ref, rp_ref, kp_ref, vp_ref, wap_ref, gdp_ref,
                 par_ref, muw_ref, mug_ref, wdec_ref, wic_ref, wg_ref, o_ref, z_ref):
    t = pl.program_id(2)
    ts = r_ref.shape[0]

    @pl.when(t == 0)
    def _():
        z_ref[...] = jnp.zeros(z_ref.shape, F32)

    first = jnp.where(t == 0, 0.0, 1.0)

    def shifted(ref, pref, mu):
        x = ref[...]
        row = lax.broadcasted_iota(jnp.int32, x.shape, 0)
        prev = jnp.where(row == 0, pref[7:8, :] * first, pltpu.roll(x, 1, 0))
        return x + (prev - x) * mu

    par = par_ref[...]
    r_all = shifted(r_ref, rp_ref, par[8:9])
    k_all = shifted(k_ref, kp_ref, par[9:10])
    v_all = shifted(v_ref, vp_ref, par[10:11])
    wa = shifted(wa_ref, wap_ref, muw_ref[...])
    gd = shifted(gd_ref, gdp_ref, mug_ref[...])

    lane = lax.broadcasted_iota(jnp.int32, (LANES, LANES), 1)
    rowl = lax.broadcasted_iota(jnp.int32, (LANES, LANES), 0)
    same_head = (lane // B_HEAD_DIM) == (rowl // B_HEAD_DIM)
    gones = jnp.where(same_head, 1.0, 0.0)
    gones16 = gones.astype(BF16)
    strict = jnp.logical_and(same_head, rowl > lane)
    incl = jnp.logical_and(same_head, rowl >= lane)
    eye = jnp.where(lane == rowl, 1.0, 0.0)
    head0 = lax.broadcasted_iota(jnp.int32, (RC, LANES), 1) < B_HEAD_DIM

    def head_sum(x):
        hi, lo = _split2(x)
        return _dot(hi, gones16) + _dot(lo, gones16)

    def stack_heads(x):
        return jnp.concatenate([jnp.where(head0, x, 0.0), jnp.where(head0, 0.0, x)], axis=0)

    def fold_heads(x):
        return x[:RC] + x[RC:]

    def twice(x):
        return jnp.concatenate([x, x], axis=0)

    rt = lax.broadcasted_iota(jnp.int32, (ts, ts), 0)
    ct = lax.broadcasted_iota(jnp.int32, (ts, ts), 1)
    tri16 = jnp.where(jnp.logical_and((rt // RC) == (ct // RC), rt >= ct), 1.0, 0.0).astype(BF16)
    th_hi, th_lo = _split2(jnp.tanh(wa))
    wa16 = wa.astype(BF16)
    sg16 = jax.nn.sigmoid(gd).astype(BF16)
    nchunk = ts // RC
    npair = r_ref.shape[1] // LANES

    pairs = []
    for q in range(npair):
        ls = slice(q * LANES, (q + 1) * LANES)
        w0, a0, k_k, k_a, r_k, gn_g, gn_b = (par[n:n + 1, ls] for n in range(7))
        r, k, v = r_all[:, ls], k_all[:, ls], v_all[:, ls]
        wd_hi, wd_lo = _split2(wdec_ref[:, ls])
        z = w0 + (_dot(th_hi, wd_hi) + _dot(th_hi, wd_lo) + _dot(th_lo, wd_hi))
        w_log = -(jnp.maximum(-z, 0.0) + jnp.log1p(jnp.exp(-jnp.abs(z)))) - 0.5
        e = jnp.exp(w_log)
        a = jax.nn.sigmoid(a0 + _dot(wa16, wic_ref[:, ls].astype(BF16)))
        g = _dot(sg16, wg_ref[:, ls].astype(BF16))
        kkr = k * k_k
        kk = kkr / jnp.maximum(jnp.sqrt(head_sum(kkr * kkr)), 1e-12)
        k2 = k * (1.0 + (a - 1.0) * k_a)
        bonus = head_sum(r * k2 * r_k) * v
        kka = kk * a
        e_hi, e_mid, e_lo = _split3(e)
        cum = _dot(tri16, e_hi) + _dot(tri16, e_mid) + _dot(tri16, e_lo)
        pairs.append(dict(r_hat=r * jnp.exp(-cum), a_hat=-kk * jnp.exp(e - cum), cum=cum, kka=kka, k2=k2, v=v,
                          b_til=(kka * jnp.exp(cum)).astype(BF16), k_til=(k2 * jnp.exp(cum)).astype(BF16),
                          v16=v.astype(BF16), bonus=bonus, g=g, gn_g=gn_g, gn_b=gn_b))

    items = [(q, c) for q in range(npair) for c in range(nchunk)]
    n = range(len(items))
    sl = [slice(c * RC, (c + 1) * RC) for _, c in items]
    pq = [pairs[q] for q, _ in items]
    tot = [pq[i]["cum"][sl[i].stop - 1:sl[i].stop] for i in n]
    rem = [jnp.exp(pq[i]["cum"][sl[i]] - tot[i]) for i in n]
    a_s = [stack_heads(pq[i]["a_hat"][sl[i]]).astype(BF16) for i in n]
    r_s = [stack_heads(pq[i]["r_hat"][sl[i]]).astype(BF16) for i in n]
    v_s = [stack_heads(pq[i]["v"][sl[i]]).astype(BF16) for i in n]
    ar = [jnp.concatenate([a_s[i], r_s[i]], axis=0) for i in n]
    bk = [jnp.concatenate([twice(pq[i]["b_til"][sl[i]]), twice(pq[i]["k_til"][sl[i]])], axis=0) for i in n]
    m_all = [_dot_nt(ar[i], bk[i]) for i in n]
    l_ab = [jnp.where(strict, m_all[i][:LANES, :LANES], 0.0) for i in n]
    l_ak = [jnp.where(strict, m_all[i][:LANES, LANES:], 0.0).astype(BF16) for i in n]
    m_rb = [jnp.where(incl, m_all[i][LANES:, :LANES], 0.0).astype(BF16) for i in n]
    m_rk = [jnp.where(incl, m_all[i][LANES:, LANES:], 0.0).astype(BF16) for i in n]
    inv = [eye + l_ab[i] for i in n]
    pw = [l_ab[i].astype(BF16) for i in n]
    for _ in range(5):
        pw = [_dot(pw[i], pw[i]).astype(BF16) for i in n]
        inv = [inv[i] + _dot(inv[i].astype(BF16), pw[i]) for i in n]
    lv = [_dot(l_ak[i], v_s[i]).astype(BF16) for i in n]
    x_av = [_dot(inv[i].astype(BF16), jnp.concatenate([a_s[i], lv[i]], axis=1)) for i in n]
    y_av = [_dot(m_rb[i], x_av[i].astype(BF16)) for i in n]
    y_kv = [_dot(m_rk[i], v_s[i]) for i in n]
    r_p = [pq[i]["r_hat"][sl[i]] + fold_heads(y_av[i][:, :LANES]) for i in n]
    y0 = [fold_heads(y_av[i][:, LANES:] + y_kv[i]) for i in n]
    gh = [_dot((pq[i]["kka"][sl[i]] * rem[i]).T.astype(BF16), fold_heads(x_av[i]).astype(BF16)) for i in n]
    kv = [_dot((pq[i]["k2"][sl[i]] * rem[i]).T.astype(BF16), pq[i]["v16"][sl[i]]) for i in n]
    g_mat = [eye * jnp.exp(-tot[i]) + gones * gh[i][:, :LANES] for i in n]
    h_mat = [gones * (gh[i][:, LANES:] + kv[i]) for i in n]
    rg = [jnp.concatenate([r_p[i], g_mat[i]], axis=0).astype(BF16) for i in n]

    zc = [z_ref[q] for q in range(npair)]
    ys = [[] for _ in range(npair)]
    for c in range(nchunk):
        for q in range(npair):
            i = q * nchunk + c
            yz = _dot(rg[i], zc[q].astype(BF16))
            ys[q].append(yz[:RC] + y0[i])
            zc[q] = yz[RC:] + h_mat[i]
    for q in range(npair):
        z_ref[q] = zc[q]
        p = pairs[q]
        y = jnp.concatenate(ys[q], axis=0)
        mean = head_sum(y) * (1.0 / B_HEAD_DIM)
        yc = y - mean
        var = head_sum(yc * yc) * (1.0 / B_HEAD_DIM)
        yn = yc * lax.rsqrt(var + GN_EPS) * p["gn_g"] + p["gn_b"]
        o_ref[:, q * LANES:(q + 1) * LANES] = ((yn + p["bonus"]) * p["g"]).astype(o_ref.dtype)


def _rwkv(hb, par, mu_wa, mu_gd, wdec, wic, wg, bsz, seq):
    nt = seq // RTS
    width = RNP * LANES
    ngroup = B_WIDTH // width
    cb = B_WIDTH // width
    small = (3 * B_WIDTH) // LANES

    def main(shape_w, colblk):
        return pl.BlockSpec((RTS, shape_w), lambda b, p, t: (b * nt + t, colblk(p)))

    def prev(shape_w, colblk):
        return pl.BlockSpec((8, shape_w), lambda b, p, t: (jnp.maximum((b * nt + t) * (RTS // 8) - 1, 0), colblk(p)))

    wide = [lambda p: p, lambda p: cb + p, lambda p: 2 * cb + p]
    narrow = [lambda p: small, lambda p: small + 1]
    in_specs = ([main(width, c) for c in wide] + [main(LANES, c) for c in narrow]
                + [prev(width, c) for c in wide] + [prev(LANES, c) for c in narrow] + [
        pl.BlockSpec((16, width), lambda b, p, t: (0, p)),
        pl.BlockSpec((1, LANES), lambda b, p, t: (0, 0)),
        pl.BlockSpec((1, LANES), lambda b, p, t: (0, 0)),
        pl.BlockSpec((LANES, width), lambda b, p, t: (0, p)),
        pl.BlockSpec((LANES, width), lambda b, p, t: (0, p)),
        pl.BlockSpec((LANES, width), lambda b, p, t: (0, p)),
    ])
    return pl.pallas_call(
        _rwkv_kernel,
        grid=(bsz, ngroup, nt),
        in_specs=in_specs,
        out_specs=pl.BlockSpec((RTS, width), lambda b, p, t: (b * nt + t, p)),
        out_shape=jax.ShapeDtypeStruct((bsz * seq, B_WIDTH), BF16),
        scratch_shapes=[pltpu.VMEM((RNP, LANES, LANES), F32)],
        compiler_params=_cparams(("arbitrary", "arbitrary", "arbitrary")), name="rwkv7",
    )(*([hb] * 10), par, mu_wa, mu_gd, wdec, wic, wg)


def _pack_rwkv_params(mu, w0, w_decay, a0, w_iclr, k_k, k_a, r_k, gn_g, gn_b):
    bw = B_WIDTH
    zero = jnp.zeros((bw,), F32)
    rows = [w0, a0, k_k, k_a, r_k.reshape(bw), gn_g, gn_b, zero,
            mu[:bw], mu[bw:2 * bw], mu[2 * bw:3 * bw], zero, zero, zero, zero, zero]
    par = jnp.stack(rows)
    mu_wa = mu[3 * bw:3 * bw + LANES][None]
    mu_gd = mu[3 * bw + LANES:][None]
    zpad = jnp.zeros((DECAY_LORA, bw), F32)
    wdec = jnp.concatenate([w_decay, zpad], axis=0)
    wic = jnp.concatenate([zpad, w_iclr], axis=0)
    return par, mu_wa, mu_gd, wdec, wic


EXPERTS_PER_STEP = 2


def _xor_partner_rows(x, row, bit):
    up = pltpu.roll(x, bit, 0)
    down = pltpu.roll(x, x.shape[0] - bit, 0)
    return jnp.where((row & bit) != 0, up, down)


def _moe_gates_t(x_hi, x_lo, wrt, br):
    w_hi, w_lo = _split2(wrt)
    logits = _dot_nt(w_hi, x_hi) + _dot_nt(w_lo, x_hi) + _dot_nt(w_hi, x_lo)
    row = lax.broadcasted_iota(jnp.int32, logits.shape, 0)
    rowf = row.astype(F32)
    big = float(N_EXPERTS)
    scores = jax.nn.sigmoid(logits)
    sel = scores + br
    p1 = _xor_partner_rows(sel, row, 1)
    hi1, lo1 = jnp.maximum(sel, p1), jnp.minimum(sel, p1)
    hi2, lo2 = _xor_partner_rows(hi1, row, 2), _xor_partner_rows(lo1, row, 2)
    gscore = jnp.maximum(hi1, hi2) + jnp.maximum(jnp.minimum(hi1, hi2), jnp.maximum(lo1, lo2))
    gbest = jnp.max(gscore, axis=0, keepdims=True)
    first = jnp.min(jnp.where(gscore == gbest, rowf, big), axis=0, keepdims=True)
    in_group = jnp.floor(rowf * (1.0 / EXPERTS_PER_GROUP)) == jnp.floor(first * (1.0 / EXPERTS_PER_GROUP))
    masked = jnp.where(in_group, sel, NEG)
    m1 = jnp.max(masked, axis=0, keepdims=True)
    i1 = jnp.min(jnp.where(masked == m1, rowf, big), axis=0, keepdims=True)
    pick1 = rowf == i1
    masked2 = jnp.where(pick1, NEG, masked)
    m2 = jnp.max(masked2, axis=0, keepdims=True)
    i2 = jnp.min(jnp.where(masked2 == m2, rowf, big), axis=0, keepdims=True)
    pick2 = rowf == i2
    s1 = jnp.sum(jnp.where(pick1, scores, 0.0), axis=0, keepdims=True)
    s2 = jnp.sum(jnp.where(pick2, scores, 0.0), axis=0, keepdims=True)
    tot = s1 + s2
    return jnp.where(pick1, s1 / tot, 0.0) + jnp.where(pick2, s2 / tot, 0.0)


def _mix_moe_ple_kernel(x_ref, a_ref, bo_ref, woa_ref, wob_ref, gm_ref, bm_ref, p_ref, wrt_ref, br_ref, wgu_ref, wd_ref,
                        g_ref, b_ref, wpg_ref, wpp_ref, g2_ref, b2_ref, o_ref, gate_ref, acc_ref, xb_ref, x1_ref):
    e = pl.program_id(1)

    @pl.when(e == 0)
    def _():
        mix = _dot(a_ref[...], woa_ref[...]) + _dot(bo_ref[...], wob_ref[...])
        x1 = _ln(ALPHA * x_ref[...] + mix, gm_ref[...], bm_ref[...])
        x1_ref[...] = x1
        x_hi, x_lo = _split2(x1)
        gates_t = _moe_gates_t(x_hi, x_lo, wrt_ref[...], br_ref[:, :1])
        pad = jnp.zeros((LANES - gates_t.shape[0], gates_t.shape[1]), F32)
        gate_ref[...] = jnp.concatenate([gates_t, pad], axis=0).T
        xb_ref[...] = x_hi
        acc_ref[...] = jnp.zeros(acc_ref.shape, F32)

    gates = gate_ref[...]
    lane = lax.broadcasted_iota(jnp.int32, gates.shape, 1)
    xb = xb_ref[...]
    hidden = []
    for n in range(EXPERTS_PER_STEP):
        gcol = jnp.sum(jnp.where(lane == e * EXPERTS_PER_STEP + n, gates, 0.0), axis=1, keepdims=True)
        hu = _dot(xb, wgu_ref[n])
        hg, up = hu[:, :D_EXPERT], hu[:, D_EXPERT:]
        hidden.append(((hg * jax.nn.sigmoid(hg)) * up * gcol).astype(BF16))
    wd = wd_ref[...]
    acc_ref[...] += _dot(jnp.concatenate(hidden, axis=1), wd.reshape(wd.shape[0] * wd.shape[1], wd.shape[2]))

    @pl.when(e == pl.num_programs(1) - 1)
    def _():
        x2 = _ln(ALPHA * x1_ref[...] + acc_ref[...], g_ref[...], b_ref[...])
        gate = jax.nn.sigmoid(_dot(x2.astype(BF16), wpg_ref[...]))
        pe = _dot(p_ref[...].astype(BF16), wpp_ref[...])
        o_ref[...] = _ln(ALPHA * x2 + gate * pe, g2_ref[...], b2_ref[...])


def _mix_moe_ple(x, a_out, b_out, w_out, gm, bm, p, w_router, b_router, w_gate, w_up, w_down, g, b,
                 w_ple_gate, w_ple, g2, b2, tm=1024):
    t, d = x.shape
    ne = w_gate.shape[0]
    woa = w_out[:A_WIDTH].astype(BF16)
    wob = w_out[A_WIDTH:].astype(BF16)
    wgu = jnp.concatenate([w_gate, w_up], axis=2).astype(BF16)
    wd = w_down.astype(BF16)
    br = jnp.broadcast_to(b_router[:, None], (ne, LANES))
    row = lambda i, e: (i, 0)
    const = lambda i, e: (0, 0)
    vec = pl.BlockSpec((1, d), const)
    return pl.pallas_call(
        _mix_moe_ple_kernel, grid=(t // tm, ne // EXPERTS_PER_STEP),
        in_specs=[pl.BlockSpec((tm, d), row), pl.BlockSpec((tm, A_WIDTH), row), pl.BlockSpec((tm, B_WIDTH), row),
                  pl.BlockSpec((A_WIDTH, d), const), pl.BlockSpec((B_WIDTH, d), const), vec, vec,
                  pl.BlockSpec((tm, PLE_DIM), row),
                  pl.BlockSpec((ne, d), const), pl.BlockSpec((ne, LANES), const),
                  pl.BlockSpec((EXPERTS_PER_STEP, d, 2 * D_EXPERT), lambda i, e: (e, 0, 0)),
                  pl.BlockSpec((EXPERTS_PER_STEP, D_EXPERT, d), lambda i, e: (e, 0, 0)),
                  vec, vec, pl.BlockSpec((d, d), const), pl.BlockSpec((PLE_DIM, d), const), vec, vec],
        out_specs=pl.BlockSpec((tm, d), row), out_shape=jax.ShapeDtypeStruct((t, d), F32),
        scratch_shapes=[pltpu.VMEM((tm, LANES), F32), pltpu.VMEM((tm, d), F32), pltpu.VMEM((tm, d), BF16),
                        pltpu.VMEM((tm, d), F32)],
        compiler_params=_cparams(("arbitrary", "arbitrary")), name="mix_moe_ple_ln",
    )(x, a_out, b_out, woa, wob, gm, bm, p, w_router.T, br, wgu, wd, g, b,
      w_ple_gate.astype(BF16), w_ple.astype(BF16), g2, b2)


def kernel(x, p, ln_in_g, ln_in_b, w_in, w_out, mu_shift, w0, w_decay, a0, w_iclr, w_gate_up, k_k, k_a, r_k,
           gn_g, gn_b, rel_bias, w_router, b_router, w_exp_gate, w_exp_up, w_exp_down, w_ple, w_ple_gate,
           ln_mix_g, ln_mix_b, ln_ffn_g, ln_ffn_b, ln_ple_g, ln_ple_b):
    bsz, seq, d = x.shape
    t = bsz * seq
    depth = w_in.shape[0]
    xf = x.reshape(t, d)
    near_bias = _near_bias(rel_bias)
    for i in range(depth):
        w_rm, w_fm = _pack_w_in(w_in[i])
        outs = _inproj(xf, ln_in_g[None], ln_in_b[None], w_rm, w_fm, apply_ln=(i == 0))
        k_rm, kiki, hb, qt, qit, vt, wit = outs[:7]
        if i == 0:
            xf = outs[7]
        a_out = _dsa(qt, qit, wit, k_rm, vt, kiki, near_bias, bsz, seq)
        par, mu_wa, mu_gd, wdec, wic = _pack_rwkv_params(mu_shift[i], w0[i], w_decay[i], a0[i], w_iclr[i],
                                                         k_k[i], k_a[i], r_k[i], gn_g[i], gn_b[i])
        b_out = _rwkv(hb, par, mu_wa, mu_gd, wdec, wic, w_gate_up[i], bsz, seq)
        xf = _mix_moe_ple(xf, a_out, b_out, w_out[i], ln_mix_g[i][None], ln_mix_b[i][None],
                          p[i].reshape(t, PLE_DIM), w_router, b_router, w_exp_gate[i], w_exp_up[i], w_exp_down[i],
                          ln_ffn_g[i][None], ln_ffn_b[i][None], w_ple_gate[i], w_ple[i],
                          ln_ple_g[i][None], ln_ple_b[i][None])
    return xf.reshape(bsz, seq, d)
```
